```python
import jax, jax.numpy as jnp
from jax import lax
import numpy as np

D_MODEL = 2048
BATCH = 16
SEQ = 2048
DEPTH = 4

CTX_LEN = 256
GRID_W = 64
RMS_EPS = 1e-6
N_MOD = 6
ROPE_THETA = 10000.0
GLA_HEADS = 4
GLA_DK = D_MODEL // 16
GLA_DV = D_MODEL // 8
GLA_KEY_WIDTH = GLA_HEADS * GLA_DK
GLA_VAL_WIDTH = GLA_HEADS * GLA_DV
GLA_RANK = 16
GLA_TAU = 16.0
GLA_CHUNK = 64
LRU_WIDTH = D_MODEL // 2
LRU_BLOCKS = 8
LRU_BLOCK_DIM = LRU_WIDTH // LRU_BLOCKS
LRU_CONV = 4
LRU_C = 8.0
REC_SPLITS = [GLA_KEY_WIDTH, 2 * GLA_KEY_WIDTH, 2 * GLA_KEY_WIDTH + GLA_VAL_WIDTH, 2 * GLA_KEY_WIDTH + 2 * GLA_VAL_WIDTH, 2 * GLA_KEY_WIDTH + 2 * GLA_VAL_WIDTH + GLA_RANK, 2 * GLA_KEY_WIDTH + 2 * GLA_VAL_WIDTH + 2 * GLA_RANK, 2 * GLA_KEY_WIDTH + 2 * GLA_VAL_WIDTH + 2 * GLA_RANK + LRU_WIDTH]
REC_IN = 2 * GLA_KEY_WIDTH + 2 * GLA_VAL_WIDTH + 2 * GLA_RANK + 2 * LRU_WIDTH
NA_HEADS = 16
NA_HEAD_DIM = D_MODEL // NA_HEADS
NA_KH = 8
NA_KW = 16
NA_QBLOCK_W = 16
NA_BAND_W = 32
N_EXPERTS = 16
EC_CAPACITY_FACTOR = 2
EXPERT_FF = D_MODEL // 2
N_REC_LAYERS = (DEPTH + 1) // 2
N_NA_LAYERS = DEPTH // 2

kernel_name = "hybrid_gla_rglru_natten_ec_dit"


def rms_norm(x, g):
    xf = x.astype(jnp.float32)
    y = xf * lax.rsqrt(jnp.mean(xf * xf, axis=-1, keepdims=True) + RMS_EPS)
    return (y * g.astype(jnp.float32)).astype(x.dtype)


def modulate(h, shift, scale):
    return h * (1 + scale) + shift


def axial_rope(n_tok, head_dim):
    t = jnp.arange(n_tok)
    rows = (t // GRID_W).astype(jnp.float32)
    cols = (t % GRID_W).astype(jnp.float32)
    n_freq = head_dim // 4
    inv = ROPE_THETA ** (-jnp.arange(n_freq, dtype=jnp.float32) / n_freq)
    ang = jnp.concatenate([rows[:, None] * inv, cols[:, None] * inv], axis=-1)
    return jnp.cos(ang), jnp.sin(ang)


def apply_rope(t, cos, sin):
    tp = t.astype(jnp.float32).reshape(*t.shape[:-1], -1, 2)
    t1, t2 = tp[..., 0], tp[..., 1]
    out = jnp.stack([t1 * cos - t2 * sin, t1 * sin + t2 * cos], axis=-1)
    return out.reshape(t.shape).astype(t.dtype)


def gla_chunked(q, k, v, log_a, s0):
    B, H, T, DK = q.shape
    DV = v.shape[-1]
    n = T // GLA_CHUNK

    def chunks(t):
        return t.astype(jnp.float32).reshape(B, H, n, GLA_CHUNK, t.shape[-1])

    q, k, v, la = chunks(q), chunks(k), chunks(v), chunks(log_a)
    b = jnp.cumsum(la, axis=3)
    b_end = b[:, :, :, -1:, :]
    q_dec = q * jnp.exp(b)
    k_inv = k * jnp.exp(-b)
    k_end = k * jnp.exp(b_end - b)
    lower = np.tril(np.ones((GLA_CHUNK, GLA_CHUNK), dtype=bool))
    scores = jnp.where(lower, jnp.einsum('bhnid,bhnjd->bhnij', q_dec, k_inv), 0.0)
    o_intra = jnp.einsum('bhnij,bhnjv->bhniv', scores, v)

    def step(state, xs):
        q_c, k_c, v_c, dec = xs
        o_c = jnp.einsum('bhid,bhdv->bhiv', q_c, state)
        state = state * dec[..., None] + jnp.einsum('bhjd,bhjv->bhdv', k_c, v_c)
        return state, o_c

    xs = (jnp.moveaxis(q_dec, 2, 0), jnp.moveaxis(k_end, 2, 0), jnp.moveaxis(v, 2, 0),
          jnp.moveaxis(jnp.exp(b_end[:, :, :, 0, :]), 2, 0))
    s_fin, o_inter = lax.scan(step, s0.astype(jnp.float32), xs)
    o = o_intra + jnp.moveaxis(o_inter, 0, 2)
    return o.reshape(B, H, T, DV), s_fin


def gla_bidir(q, k, v, la_f, la_b, s_f, s_b):
    o_f, fin_f = gla_chunked(q, k, v, la_f, s_f)
    o_b, fin_b = gla_chunked(q[:, :, ::-1], k[:, :, ::-1], v[:, :, ::-1], la_b[:, :, ::-1], s_b)
    return o_f + o_b[:, :, ::-1], fin_f, fin_b


def gla_inputs(p, alpha_up, alpha_b, cos, sin):
    q, k, v, _, za_f, za_b = p[:6]
    B, T, _ = q.shape

    def heads(t, d):
        return t.reshape(B, T, GLA_HEADS, d).transpose(0, 2, 1, 3)

    q = heads(q, GLA_DK) * GLA_DK ** -0.5
    k = heads(k, GLA_DK)
    v = heads(v, GLA_DV)
    if cos is not None:
        q, k = apply_rope(q, cos, sin), apply_rope(k, cos, sin)
    la_f = heads(jax.nn.log_sigmoid((za_f @ alpha_up[0] + alpha_b[0]).astype(jnp.float32)) / GLA_TAU, GLA_DK)
    la_b = heads(jax.nn.log_sigmoid((za_b @ alpha_up[1] + alpha_b[1]).astype(jnp.float32)) / GLA_TAU, GLA_DK)
    return q, k, v, la_f, la_b


def gla_output(o, g, gla_g):
    B, H, T, DV = o.shape
    o = o * lax.rsqrt(jnp.mean(o * o, axis=-1, keepdims=True) + RMS_EPS) * gla_g.astype(jnp.float32)
    o = o.transpose(0, 2, 1, 3).reshape(B, T, H * DV)
    return o * jax.nn.silu(g.astype(jnp.float32))


def depthwise_conv_centred(x, w, b):
    K, W = w.shape
    lo = (K - 1) // 2
    y = lax.conv_general_dilated(x, w.astype(x.dtype)[:, None, :], window_strides=(1,), padding=[(lo, K - 1 - lo)],
                                 dimension_numbers=('NWC', 'WIO', 'NWC'), feature_group_count=W)
    return y + b


def block_diag_linear(x, w, b):
    xb = x.reshape(*x.shape[:-1], LRU_BLOCKS, LRU_BLOCK_DIM)
    return jnp.einsum('...nd,nde->...ne', xb, w).reshape(x.shape) + b


def linear_scan(a, bx, h0, reverse):
    if reverse:
        a, bx = a[:, ::-1], bx[:, ::-1]
    bx = bx.at[:, 0].add(a[:, 0] * h0)

    def comb(left, right):
        return left[0] * right[0], right[0] * left[1] + right[1]

    _, h = lax.associative_scan(comb, (a, bx), axis=1)
    h_fin = h[:, -1]
    if reverse:
        h = h[:, ::-1]
    return h, h_fin


def rglru_direction(xc, w_a, b_a, w_i, b_i, lam, h0, reverse):
    r = jax.nn.sigmoid(block_diag_linear(xc, w_a, b_a))
    i = jax.nn.sigmoid(block_diag_linear(xc, w_i, b_i))
    log_a = -LRU_C * r * jax.nn.softplus(-lam.astype(jnp.float32))
    a = jnp.exp(log_a)
    bx = jnp.sqrt(-jnp.expm1(2.0 * log_a)) * (i * xc)
    return linear_scan(a, bx, h0, reverse)


def rglru_bidir(xc, w_a, b_a, w_i, b_i, lam, h_f, h_b):
    o_f, fin_f = rglru_direction(xc, w_a[0], b_a[0], w_i[0], b_i[0], lam[0], h_f, False)
    o_b, fin_b = rglru_direction(xc, w_a[1], b_a[1], w_i[1], b_i[1], lam[1], h_b, True)
    return o_f + o_b, fin_f, fin_b


def rec_mixer(h_ctx, h_lat, w_in, alpha_up, alpha_b, gla_g, conv_w, conv_b, w_a, b_a, w_i, b_i, lam, cos, sin, need_ctx_out):
    p_ctx = jnp.split(h_ctx @ w_in, REC_SPLITS, axis=-1)
    p_lat = jnp.split(h_lat @ w_in, REC_SPLITS, axis=-1)
    B = h_lat.shape[0]
    qc, kc, vc, lac_f, lac_b = gla_inputs(p_ctx, alpha_up, alpha_b, None, None)
    ql, kl, vl, lal_f, lal_b = gla_inputs(p_lat, alpha_up, alpha_b, cos, sin)
    s0 = jnp.zeros((B, GLA_HEADS, GLA_DK, GLA_DV), jnp.float32)
    oc, sc_f, sc_b = gla_bidir(qc, kc, vc, lac_f, lac_b, s0, s0)
    ol, _, _ = gla_bidir(ql, kl, vl, lal_f, lal_b, sc_f, sc_b)
    xc_ctx = depthwise_conv_centred(p_ctx[6].astype(jnp.float32), conv_w, conv_b)
    xc_lat = depthwise_conv_centred(p_lat[6].astype(jnp.float32), conv_w, conv_b)
    h0 = jnp.zeros((B, LRU_WIDTH), jnp.float32)
    hc, hc_f, hc_b = rglru_bidir(xc_ctx, w_a, b_a, w_i, b_i, lam, h0, h0)
    hl, _, _ = rglru_bidir(xc_lat, w_a, b_a, w_i, b_i, lam, hc_f, hc_b)
    y_lat = jnp.concatenate([gla_output(ol, p_lat[3], gla_g), hl * jax.nn.gelu(p_lat[7].astype(jnp.float32))], axis=-1).astype(h_lat.dtype)
    if not need_ctx_out:
        return None, y_lat
    y_ctx = jnp.concatenate([gla_output(oc, p_ctx[3], gla_g), hc * jax.nn.gelu(p_ctx[7].astype(jnp.float32))], axis=-1).astype(h_ctx.dtype)
    return y_ctx, y_lat


def na_mixer(h_ctx, h_lat, w_qkv, rpb, need_ctx_out):
    B, T, _ = h_lat.shape
    rows = T // GRID_W
    kh = min(NA_KH, rows)

    def split_heads(h):
        qkv = (h @ w_qkv).reshape(B, h.shape[1], 3, NA_HEADS, NA_HEAD_DIM)
        return qkv[:, :, 0].transpose(0, 2, 1, 3), qkv[:, :, 1].transpose(0, 2, 1, 3), qkv[:, :, 2].transpose(0, 2, 1, 3)

    q_l, k_l, v_l = split_heads(h_lat)
    q_c, k_c, v_c = split_heads(h_ctx)
    scale = NA_HEAD_DIM ** -0.5
    k_g = k_l.reshape(B, NA_HEADS, rows, GRID_W, NA_HEAD_DIM)
    v_g = v_l.reshape(B, NA_HEADS, rows, GRID_W, NA_HEAD_DIM)
    q_rows = jnp.moveaxis(q_l.reshape(B, NA_HEADS, rows, GRID_W, NA_HEAD_DIM), 2, 0)
    r = np.arange(rows)
    row_start = np.clip(r - kh // 2, 0, rows - kh)
    dr = row_start[:, None] + np.arange(kh)[None, :] - r[:, None]
    n_qb = GRID_W // NA_QBLOCK_W
    qcol = np.arange(GRID_W).reshape(n_qb, NA_QBLOCK_W)
    band_start = np.clip(np.arange(n_qb) * NA_QBLOCK_W - NA_KW // 2, 0, GRID_W - NA_BAND_W)
    band_cols = band_start[:, None] + np.arange(NA_BAND_W)[None, :]
    col_start = np.clip(qcol - NA_KW // 2, 0, GRID_W - NA_KW)
    col_valid = (band_cols[:, None, :] >= col_start[..., None]) & (band_cols[:, None, :] < col_start[..., None] + NA_KW)
    dc_idx = np.clip(band_cols[:, None, :] - qcol[..., None], -(NA_KW - 1), NA_KW - 1) + NA_KW - 1
    key_mask = np.broadcast_to(col_valid[:, :, None, :], (n_qb, NA_QBLOCK_W, kh, NA_BAND_W)).reshape(n_qb, NA_QBLOCK_W, kh * NA_BAND_W)
    rpb_c = rpb[:, :, dc_idx]
    n_loc = kh * NA_BAND_W

    def band(t):
        t = t[:, :, :, band_cols]
        return t.transpose(0, 1, 3, 2, 4, 5).reshape(B, NA_HEADS, n_qb, n_loc, NA_HEAD_DIM)

    def row_block(args):
        q_r, rs, dr_r = args
        kb = band(lax.dynamic_slice_in_dim(k_g, rs, kh, axis=2))
        vb = band(lax.dynamic_slice_in_dim(v_g, rs, kh, axis=2))
        qb = q_r.reshape(B, NA_HEADS, n_qb, NA_QBLOCK_W, NA_HEAD_DIM)
        bias = rpb_c[:, dr_r + NA_KH - 1].transpose(0, 2, 3, 1, 4).reshape(NA_HEADS, n_qb, NA_QBLOCK_W, n_loc)
        bias = jnp.where(key_mask, bias.astype(jnp.float32), -jnp.inf)
        s_loc = jnp.einsum('bhjqd,bhjkd->bhjqk', qb, kb).astype(jnp.float32) * scale + bias
        s_ctx = jnp.einsum('bhjqd,bhld->bhjql', qb, k_c).astype(jnp.float32) * scale
        p = jax.nn.softmax(jnp.concatenate([s_loc, s_ctx], axis=-1), axis=-1).astype(v_g.dtype)
        o = jnp.einsum('bhjqk,bhjkd->bhjqd', p[..., :n_loc], vb) + jnp.einsum('bhjql,bhld->bhjqd', p[..., n_loc:], v_c)
        return o.reshape(B, NA_HEADS, GRID_W, NA_HEAD_DIM)

    o_rows = lax.map(row_block, (q_rows, jnp.asarray(row_start, jnp.int32), jnp.asarray(dr, jnp.int32)))
    y_lat = o_rows.transpose(1, 0, 3, 2, 4).reshape(B, T, D_MODEL)
    if not need_ctx_out:
        return None, y_lat
    s = jnp.einsum('bhqd,bhkd->bhqk', q_c, k_c).astype(jnp.float32) * scale
    o_c = jnp.einsum('bhqk,bhkd->bhqd', jax.nn.softmax(s, axis=-1).astype(v_c.dtype), v_c)
    y_ctx = o_c.transpose(0, 2, 1, 3).reshape(B, h_ctx.shape[1], D_MODEL)
    return y_ctx, y_lat


def expert_choice_ffn(h, w_r, w_g, w_u, w_d):
    B, N, _ = h.shape
    cap = EC_CAPACITY_FACTOR * N // N_EXPERTS
    aff = jax.nn.softmax((h @ w_r).astype(jnp.float32), axis=-1)
    gate, idx = lax.top_k(jnp.swapaxes(aff, 1, 2), cap)
    bidx = jnp.arange(B)[:, None, None]
    xs = h[bidx, idx]
    hid = jax.nn.silu(jnp.einsum('becd,edf->becf', xs, w_g)) * jnp.einsum('becd,edf->becf', xs, w_u)
    y = jnp.einsum('becf,efd->becd', hid, w_d) * gate[..., None].astype(h.dtype)
    return jnp.zeros_like(h).at[bidx, idx].add(y)


def setup_inputs(seed: int = 0) -> dict:
    key = jax.random.key(seed)
    ks = iter(jax.random.split(key, 40))
    f32 = jnp.float32

    def nrm(shape, scale):
        return scale * jax.random.normal(next(ks), shape, f32)

    D, E, F = D_MODEL, N_EXPERTS, EXPERT_FF
    L, R, A = DEPTH, N_REC_LAYERS, N_NA_LAYERS
    u = jax.random.uniform(next(ks), (R, 2, LRU_WIDTH), f32, 0.9, 0.999)
    a0 = u ** (1.0 / LRU_C)
    lam = jnp.log(a0) - jnp.log1p(-a0)
    return {
        'x': nrm((BATCH, SEQ, D), 1.0),
        'c': nrm((BATCH, D), 1.0),
        'ctx': nrm((BATCH, CTX_LEN, D), 1.0),
        'c_ctx': nrm((D,), 1.0),
        'w_mod': nrm((L, D, N_MOD * D), 0.5 * D ** -0.5),
        'b_mod': nrm((L, N_MOD * D), 0.01),
        'norm_mix_g': 1.0 + nrm((L, D), 0.05),
        'norm_ffn_g': 1.0 + nrm((L, D), 0.05),
        'w_out': nrm((L, D, D), D ** -0.5),
        'router_w': nrm((L, D, E), D ** -0.5),
        'exp_w_gate': nrm((L, E, D, F), D ** -0.5),
        'exp_w_up': nrm((L, E, D, F), D ** -0.5),
        'exp_w_down': nrm((L, E, F, D), F ** -0.5),
        'rec_w_in': nrm((R, D, REC_IN), D ** -0.5),
        'gla_alpha_up': nrm((R, 2, GLA_RANK, GLA_KEY_WIDTH), GLA_RANK ** -0.5),
        'gla_alpha_b': nrm((R, 2, GLA_KEY_WIDTH), 0.1),
        'gla_norm_g': 1.0 + nrm((R, GLA_DV), 0.05),
        'lru_conv_w': nrm((R, LRU_CONV, LRU_WIDTH), LRU_CONV ** -0.5),
        'lru_conv_b': nrm((R, LRU_WIDTH), 0.01),
        'lru_w_a': nrm((R, 2, LRU_BLOCKS, LRU_BLOCK_DIM, LRU_BLOCK_DIM), LRU_BLOCK_DIM ** -0.5),
        'lru_b_a': nrm((R, 2, LRU_WIDTH), 0.01),
        'lru_w_i': nrm((R, 2, LRU_BLOCKS, LRU_BLOCK_DIM, LRU_BLOCK_DIM), LRU_BLOCK_DIM ** -0.5),
        'lru_b_i': nrm((R, 2, LRU_WIDTH), 0.01),
        'lru_lambda': lam,
        'na_w_qkv': nrm((A, D, 3 * D), D ** -0.5),
        'na_rpb': nrm((A, NA_HEADS, 2 * NA_KH - 1, 2 * NA_KW - 1), 0.02),
        'norm_f_g': 1.0 + nrm((D,), 0.05),
    }


def reference(x, c, ctx, c_ctx, w_mod, b_mod, norm_mix_g, norm_ffn_g, w_out, router_w, exp_w_gate, exp_w_up, exp_w_down,
              rec_w_in, gla_alpha_up, gla_alpha_b, gla_norm_g, lru_conv_w, lru_conv_b, lru_w_a, lru_b_a, lru_w_i, lru_b_i,
              lru_lambda, na_w_qkv, na_rpb, norm_f_g):
    n_tok = x.shape[1]
    cos, sin = axial_rope(n_tok, GLA_DK)
    c_act = jax.nn.silu(c)
    cc_act = jax.nn.silu(c_ctx)
    for l in range(DEPTH):
        last = l == DEPTH - 1
        i = l // 2
        m_lat = jnp.split((c_act @ w_mod[l] + b_mod[l])[:, None, :], N_MOD, axis=-1)
        m_ctx = jnp.split((cc_act @ w_mod[l] + b_mod[l])[None, None, :], N_MOD, axis=-1)
        h_lat = modulate(rms_norm(x, norm_mix_g[l]), m_lat[0], m_lat[1])
        h_ctx = modulate(rms_norm(ctx, norm_mix_g[l]), m_ctx[0], m_ctx[1])
        if l % 2 == 0:
            y_ctx, y_lat = rec_mixer(h_ctx, h_lat, rec_w_in[i], gla_alpha_up[i], gla_alpha_b[i], gla_norm_g[i],
                                     lru_conv_w[i], lru_conv_b[i], lru_w_a[i], lru_b_a[i], lru_w_i[i], lru_b_i[i],
                                     lru_lambda[i], cos, sin, not last)
        else:
            y_ctx, y_lat = na_mixer(h_ctx, h_lat, na_w_qkv[i], na_rpb[i], not last)
        x = x + m_lat[2] * (y_lat @ w_out[l])
        x = x + m_lat[5] * expert_choice_ffn(modulate(rms_norm(x, norm_ffn_g[l]), m_lat[3], m_lat[4]),
                                             router_w[l], exp_w_gate[l], exp_w_up[l], exp_w_down[l])
        if not last:
            ctx = ctx + m_ctx[2] * (y_ctx @ w_out[l])
            ctx = ctx + m_ctx[5] * expert_choice_ffn(modulate(rms_norm(ctx, norm_ffn_g[l]), m_ctx[3], m_ctx[4]),
                                                     router_w[l], exp_w_gate[l], exp_w_up[l], exp_w_down[l])
    return rms_norm(x, norm_f_g)
```

```python
import functools

import numpy as np
import jax
import jax.numpy as jnp
from jax import lax
from jax.experimental import pallas as pl
from jax.experimental.pallas import tpu as pltpu

D_MODEL = 2048
DEPTH = 4
GRID_W = 64
RMS_EPS = 1e-6
N_MOD = 6
ROPE_THETA = 10000.0
GLA_HEADS = 4
GLA_DK = D_MODEL // 16
GLA_DV = D_MODEL // 8
GLA_KEY_WIDTH = GLA_HEADS * GLA_DK
GLA_VAL_WIDTH = GLA_HEADS * GLA_DV
GLA_RANK = 16
GLA_TAU = 16.0
GLA_CHUNK = 64
LRU_WIDTH = D_MODEL // 2
LRU_BLOCKS = 8
LRU_BLOCK_DIM = LRU_WIDTH // LRU_BLOCKS
LRU_CONV = 4
LRU_C = 8.0
NA_HEADS = 16
NA_HEAD_DIM = D_MODEL // NA_HEADS
NA_KH = 8
NA_KW = 16
N_EXPERTS = 16
EC_CAPACITY_FACTOR = 2
EXPERT_FF = D_MODEL // 2

F32 = jnp.float32
BF16 = jnp.bfloat16
HIGHEST = lax.Precision.HIGHEST
LANES = 128
NEG_BIG = -1e30
NA_QROWS = 4
NA_KROWS = NA_QROWS + NA_KH - 1
PREFIX_CHUNK = 256

_NT = (((1,), (1,)), ((), ()))
_TN = (((0,), (0,)), ((), ()))


def _params(sem, vmem_mib=48):
    return pltpu.CompilerParams(dimension_semantics=sem, vmem_limit_bytes=vmem_mib * 2 ** 20)


def _sigmoid(z):
    return 1.0 / (1.0 + jnp.exp(-z))


def _silu(z):
    return z * _sigmoid(z)


def _mod_kernel(c_ref, w_ref, b_ref, o_ref):
    a = _silu(c_ref[...]).astype(BF16)
    o_ref[0] = jnp.dot(a, w_ref[0].astype(BF16), preferred_element_type=F32) + b_ref[0]


def _modulation(cc, w_mod, b_mod):
    R, D = cc.shape
    L, _, N = w_mod.shape
    tn = 1024
    return pl.pallas_call(
        _mod_kernel,
        grid=(L, N // tn),
        in_specs=[pl.BlockSpec((R, D), lambda l, j: (0, 0)),
                  pl.BlockSpec((1, D, tn), lambda l, j: (l, 0, j)),
                  pl.BlockSpec((1, 1, tn), lambda l, j: (l, 0, j))],
        out_specs=pl.BlockSpec((1, R, tn), lambda l, j: (l, 0, j)),
        out_shape=jax.ShapeDtypeStruct((L, R, N), F32),
        compiler_params=_params(("arbitrary", "arbitrary")),
        name="modulation",
    )(cc, w_mod, b_mod.reshape(L, 1, N))


def _norm_kernel(*refs, modulated, with_router, shift_row, scale_row):
    it = iter(refs)
    x_ref, g_ref = next(it), next(it)
    m_ref = next(it) if modulated else None
    wr_ref = next(it) if with_router else None
    h_ref = next(it)
    lg_ref = next(it) if with_router else None
    x = x_ref[0]
    y = x * lax.rsqrt(jnp.mean(x * x, axis=-1, keepdims=True) + RMS_EPS) * g_ref[...]
    if modulated:
        y = y * (1.0 + m_ref[0, scale_row:scale_row + 1, :]) + m_ref[0, shift_row:shift_row + 1, :]
    h_ref[0] = y.astype(h_ref.dtype)
    if with_router:
        lg_ref[0] = lax.dot_general(wr_ref[...], y, _NT, precision=HIGHEST, preferred_element_type=F32)


def _norm(x, g, mod=None, shift_row=0, scale_row=1, router_wt=None, out_dtype=BF16):
    B, T, D = x.shape
    tt = 256
    modulated, with_router = mod is not None, router_wt is not None
    in_specs = [pl.BlockSpec((1, tt, D), lambda b, t: (b, t, 0)), pl.BlockSpec((1, D), lambda b, t: (0, 0))]
    args = [x, g.reshape(1, D)]
    if modulated:
        per_sample = mod.shape[0] > 1
        in_specs.append(pl.BlockSpec((1, N_MOD, D), (lambda b, t: (b, 0, 0)) if per_sample else (lambda b, t: (0, 0, 0))))
        args.append(mod)
    out_specs = [pl.BlockSpec((1, tt, D), lambda b, t: (b, t, 0))]
    out_shape = [jax.ShapeDtypeStruct((B, T, D), out_dtype)]
    if with_router:
        E = router_wt.shape[0]
        in_specs.append(pl.BlockSpec((E, D), lambda b, t: (0, 0)))
        args.append(router_wt)
        out_specs.append(pl.BlockSpec((1, E, tt), lambda b, t: (b, 0, t)))
        out_shape.append(jax.ShapeDtypeStruct((B, E, T), F32))
    res = pl.pallas_call(
        functools.partial(_norm_kernel, modulated=modulated, with_router=with_router,
                          shift_row=shift_row, scale_row=scale_row),
        grid=(B, T // tt), in_specs=in_specs, out_specs=out_specs, out_shape=out_shape,
        compiler_params=_params(("arbitrary", "arbitrary")),
        name="rmsnorm",
    )(*args)
    return res if with_router else res[0]


def _proj_kernel(a_ref, w_ref, o_ref):
    o_ref[...] = jnp.dot(a_ref[...], w_ref[...], preferred_element_type=F32).astype(o_ref.dtype)


def _proj(a, w, out_dtype, tn):
    B, T, K = a.shape
    N = w.shape[1]
    M = B * T
    tm = min(M, 512)
    out = pl.pallas_call(
        _proj_kernel,
        grid=(N // tn, M // tm),
        in_specs=[pl.BlockSpec((tm, K), lambda j, i: (i, 0)), pl.BlockSpec((K, tn), lambda j, i: (0, j))],
        out_specs=pl.BlockSpec((tm, tn), lambda j, i: (i, j)),
        out_shape=jax.ShapeDtypeStruct((M, N), out_dtype),
        compiler_params=_params(("arbitrary", "arbitrary")),
        name="proj",
    )(a.reshape(M, K), w)
    return out.reshape(B, T, N)


def _oproj_kernel(*refs, n_in, gate_row):
    a_refs, w_refs = refs[:n_in], refs[n_in:2 * n_in]
    x_ref, m_ref, o_ref = refs[2 * n_in:]
    acc = jnp.dot(a_refs[0][0], w_refs[0][...], preferred_element_type=F32)
    for a_ref, w_ref in zip(a_refs[1:], w_refs[1:]):
        acc = acc + jnp.dot(a_ref[0], w_ref[...], preferred_element_type=F32)
    o_ref[0] = x_ref[0] + m_ref[0, gate_row:gate_row + 1, :] * acc


def _oproj(ys, ws, x, mod, gate_row):
    B, T, N = x.shape
    tm, tn = min(T, 512), 1024
    per_sample = mod.shape[0] > 1
    in_specs = [pl.BlockSpec((1, tm, y.shape[2]), lambda j, b, t: (b, t, 0)) for y in ys]
    in_specs += [pl.BlockSpec((w.shape[0], tn), lambda j, b, t: (0, j)) for w in ws]
    in_specs += [pl.BlockSpec((1, tm, tn), lambda j, b, t: (b, t, j)),
                 pl.BlockSpec((1, N_MOD, tn), (lambda j, b, t: (b, 0, j)) if per_sample else (lambda j, b, t: (0, 0, j)))]
    return pl.pallas_call(
        functools.partial(_oproj_kernel, n_in=len(ys), gate_row=gate_row),
        grid=(N // tn, B, T // tm), in_specs=in_specs,
        out_specs=pl.BlockSpec((1, tm, tn), lambda j, b, t: (b, t, j)),
        out_shape=jax.ShapeDtypeStruct((B, T, N), F32),
        compiler_params=_params(("arbitrary", "arbitrary", "arbitrary")),
        name="oproj",
    )(*ys, *ws, x, mod)


def _log_sigmoid(z):
    return jnp.minimum(z, 0.0) - jnp.log(1.0 + jnp.exp(-jnp.abs(z)))


def _gla_kernel(*refs, n_chunks, use_rope, has_state):
    it = iter(refs)
    q_ref, k_ref, v_ref, g_ref, za_ref = (next(it) for _ in range(5))
    cos_ref, sin_ref = (next(it), next(it)) if use_rope else (None, None)
    aup_ref, ab_ref, gg_ref = next(it), next(it), next(it)
    s0f_ref, s0b_ref = (next(it), next(it)) if has_state else (None, None)
    y_ref, sf_ref, sb_ref = next(it), next(it), next(it)
    la_scr, qk_scr, o_scr, st_scr = next(it), next(it), next(it), next(it)
    C = GLA_CHUNK

    za = za_ref[0]
    for d in range(2):
        z = jnp.dot(za, aup_ref[d], precision=HIGHEST, preferred_element_type=F32) + ab_ref[d]
        la_scr[d] = _log_sigmoid(z) * (1.0 / GLA_TAU)

    q = q_ref[0] * (GLA_DK ** -0.5)
    k = k_ref[0]
    if use_rope:
        cos, sin = cos_ref[...], sin_ref[...]
        even = (lax.broadcasted_iota(jnp.int32, q.shape, 1) % 2) == 0

        def rope(t):
            swapped = jnp.where(even, pltpu.roll(t, LANES - 1, 1), pltpu.roll(t, 1, 1))
            return t * cos + swapped * sin
        q, k = rope(q), rope(k)
    qk_scr[0] = q
    qk_scr[1] = k

    if has_state:
        st_scr[0] = s0f_ref[0, 0]
        st_scr[1] = s0b_ref[0, 0]
    else:
        st_scr[...] = jnp.zeros(st_scr.shape, F32)

    row = lax.broadcasted_iota(jnp.int32, (C, C), 0)
    col = lax.broadcasted_iota(jnp.int32, (C, C), 1)
    keep = (row >= col, row <= col)
    tri = tuple(jnp.where(m, 1.0, 0.0).astype(F32) for m in keep)

    def chunk(c, d):
        r0 = pl.multiple_of(c * C, C)
        qc = qk_scr[0, pl.ds(r0, C), :]
        kc = qk_scr[1, pl.ds(r0, C), :]
        vb = v_ref[0, pl.ds(r0, C), :].astype(BF16)
        la = la_scr[d, pl.ds(r0, C), :]
        b = jnp.dot(tri[d], la, precision=HIGHEST, preferred_element_type=F32)
        tot = jnp.sum(la, axis=0, keepdims=True)
        qd = (qc * jnp.exp(b)).astype(BF16)
        ki = (kc * jnp.exp(-b)).astype(BF16)
        ke = (kc * jnp.exp(tot - b)).astype(BF16)
        s = lax.dot_general(qd, ki, _NT, preferred_element_type=F32)
        s = jnp.where(keep[d], s, 0.0).astype(BF16)
        st = st_scr[d]
        o = jnp.dot(s, vb, preferred_element_type=F32)
        o = o + lax.dot_general(qd, st.astype(BF16), _NT, preferred_element_type=F32)
        st_scr[d] = st * jnp.exp(tot) + lax.dot_general(vb, ke, _TN, preferred_element_type=F32)
        return r0, o

    half = n_chunks // 2

    def first_half(i, carry):
        r0, o = chunk(i, 0)
        o_scr[pl.ds(r0, C), :] = o
        r0, o = chunk(n_chunks - 1 - i, 1)
        o_scr[pl.ds(r0, C), :] = o
        return carry

    def second_half(i, carry):
        r0, o = chunk(i, 0)
        o_scr[pl.ds(r0, C), :] += o
        r0, o = chunk(n_chunks - 1 - i, 1)
        o_scr[pl.ds(r0, C), :] += o
        return carry

    lax.fori_loop(0, half, first_half, 0)
    lax.fori_loop(half, n_chunks, second_half, 0)
    sf_ref[0, 0] = st_scr[0]
    sb_ref[0, 0] = st_scr[1]

    o = o_scr[...]
    o = o * lax.rsqrt(jnp.mean(o * o, axis=-1, keepdims=True) + RMS_EPS) * gg_ref[...]
    y_ref[0] = (o * _silu(g_ref[0])).astype(y_ref.dtype)


def _gla(p_main, p_za, aup, ab, gg, rope_tabs, states):
    B, T, _ = p_main.shape
    H, DK, DV = GLA_HEADS, GLA_DK, GLA_DV
    use_rope, has_state = rope_tabs is not None, states is not None
    in_specs = [pl.BlockSpec((1, T, DK), lambda b, h: (b, 0, h)),
                pl.BlockSpec((1, T, DK), lambda b, h: (b, 0, H + h)),
                pl.BlockSpec((1, T, DV), lambda b, h: (b, 0, H + h)),
                pl.BlockSpec((1, T, DV), lambda b, h: (b, 0, 2 * H + h)),
                pl.BlockSpec((1, T, LANES), lambda b, h: (b, 0, 0))]
    args = [p_main, p_main, p_main, p_main, p_za]
    if use_rope:
        in_specs += [pl.BlockSpec((T, DK), lambda b, h: (0, 0))] * 2
        args += list(rope_tabs)
    in_specs += [pl.BlockSpec((2, LANES, DK), lambda b, h: (0, 0, h)),
                 pl.BlockSpec((2, 1, DK), lambda b, h: (0, 0, h)),
                 pl.BlockSpec((1, DV), lambda b, h: (0, 0))]
    args += [aup, ab, gg]
    st_spec = pl.BlockSpec((1, 1, DV, DK), lambda b, h: (b, h, 0, 0))
    if has_state:
        in_specs += [st_spec, st_spec]
        args += list(states)
    st_shape = jax.ShapeDtypeStruct((B, H, DV, DK), F32)
    y, sf, sb = pl.pallas_call(
        functools.partial(_gla_kernel, n_chunks=T // GLA_CHUNK, use_rope=use_rope, has_state=has_state),
        grid=(B, H), in_specs=in_specs,
        out_specs=[pl.BlockSpec((1, T, DV), lambda b, h: (b, 0, h)), st_spec, st_spec],
        out_shape=[jax.ShapeDtypeStruct((B, T, H * DV), BF16), st_shape, st_shape],
        scratch_shapes=[pltpu.VMEM((2, T, DK), F32), pltpu.VMEM((2, T, DK), F32),
                        pltpu.VMEM((T, DV), F32), pltpu.VMEM((2, DV, DK), F32)],
        compiler_params=_params(("arbitrary", "arbitrary")),
        name="gla",
    )(*args)
    return y, (sf, sb)


def _gelu_tanh(z):
    return 0.5 * z * (1.0 + jnp.tanh(np.sqrt(2.0 / np.pi) * (z + 0.044715 * (z * z * z))))


def _lru_kernel(x_ref, gt_ref, cw_ref, cb_ref, wa_ref, ba_ref, wi_ref, bi_ref, lam_ref, h0f_ref, h0b_ref,
                y_ref, hf_ref, hb_ref, *, T):
    x = x_ref[0]
    row = lax.broadcasted_iota(jnp.int32, x.shape, 0)

    def from_earlier(v, d, fill):
        return jnp.where(row >= d, pltpu.roll(v, d, 0), fill)

    def from_later(v, d, fill):
        return jnp.where(row < T - d, pltpu.roll(v, T - d, 0), fill)

    xc = (cw_ref[0:1, :] * from_earlier(x, 1, 0.0) + cw_ref[1:2, :] * x
          + cw_ref[2:3, :] * from_later(x, 1, 0.0) + cw_ref[3:4, :] * from_later(x, 2, 0.0) + cb_ref[...])
    xcb = xc.astype(BF16)
    total = None
    for d, (h0_ref, fin_ref) in enumerate(((h0f_ref, hf_ref), (h0b_ref, hb_ref))):
        r = _sigmoid(jnp.dot(xcb, wa_ref[d, 0], preferred_element_type=F32) + ba_ref[d])
        gate_i = _sigmoid(jnp.dot(xcb, wi_ref[d, 0], preferred_element_type=F32) + bi_ref[d])
        neg_lam = -lam_ref[d]
        softplus = jnp.maximum(neg_lam, 0.0) + jnp.log(1.0 + jnp.exp(-jnp.abs(neg_lam)))
        log_a = (-LRU_C) * r * softplus
        a = jnp.exp(log_a)
        h = jnp.sqrt(1.0 - jnp.exp(2.0 * log_a)) * (gate_i * xc)
        shift = from_earlier if d == 0 else from_later
        first = 0 if d == 0 else T - 1
        h = jnp.where(row == first, h + a * h0_ref[0], h)
        step = 1
        while step < T:
            h = a * shift(h, step, 0.0) + h
            if step * 2 < T:
                a = a * shift(a, step, 1.0)
            step *= 2
        last = T - 1 if d == 0 else 0
        fin_ref[0] = h[last:last + 1, :]
        total = h if total is None else total + h
    y_ref[0] = (total * _gelu_tanh(gt_ref[0])).astype(y_ref.dtype)


def _lru(p_main, cw, cb, wa, ba, wi, bi, lam, states):
    B, T, _ = p_main.shape
    W, NB, BD = LRU_WIDTH, LRU_BLOCKS, LRU_BLOCK_DIM
    x_blk0 = (2 * GLA_KEY_WIDTH + 2 * GLA_VAL_WIDTH) // BD
    vec = lambda n: pl.BlockSpec((n, 1, BD), lambda b, j: (0, 0, j))
    mat = pl.BlockSpec((2, 1, BD, BD), lambda b, j: (0, j, 0, 0))
    st_spec = pl.BlockSpec((1, 1, BD), lambda b, j: (b, 0, j))
    st_shape = jax.ShapeDtypeStruct((B, 1, W), F32)
    y, hf, hb = pl.pallas_call(
        functools.partial(_lru_kernel, T=T),
        grid=(B, NB),
        in_specs=[pl.BlockSpec((1, T, BD), lambda b, j: (b, 0, x_blk0 + j)),
                  pl.BlockSpec((1, T, BD), lambda b, j: (b, 0, x_blk0 + NB + j)),
                  pl.BlockSpec((LRU_CONV, BD), lambda b, j: (0, j)),
                  pl.BlockSpec((1, BD), lambda b, j: (0, j)),
                  mat, vec(2), mat, vec(2), vec(2), st_spec, st_spec],
        out_specs=[pl.BlockSpec((1, T, BD), lambda b, j: (b, 0, j)), st_spec, st_spec],
        out_shape=[jax.ShapeDtypeStruct((B, T, W), BF16), st_shape, st_shape],
        compiler_params=_params(("arbitrary", "arbitrary")),
        name="rglru",
    )(p_main, p_main, cw, cb.reshape(1, W), wa, ba.reshape(2, 1, W), wi, bi.reshape(2, 1, W),
      lam.reshape(2, 1, W), *states)
    return y, (hf, hb)


def _na_tile(i, n_tiles, rows):
    r0 = i * NA_QROWS
    ks = min(max(r0 - NA_KH // 2, 0), rows - NA_KROWS)
    cls = 0 if i == 0 else (2 if i == n_tiles - 1 else 1)
    return r0, ks, cls


def _na_bias_tables(rpb, rows):
    n_tiles = rows // NA_QROWS
    tabs = []
    for i in (0, 1, n_tiles - 1):
        r0, ks, _ = _na_tile(i, n_tiles, rows)
        qr = (r0 + np.arange(NA_QROWS))[:, None, None, None]
        qc = np.arange(GRID_W)[None, :, None, None]
        kr = (ks + np.arange(NA_KROWS))[None, None, :, None]
        kc = np.arange(GRID_W)[None, None, None, :]
        row_start = np.clip(qr - NA_KH // 2, 0, rows - NA_KH)
        col_start = np.clip(qc - NA_KW // 2, 0, GRID_W - NA_KW)
        valid = (kr >= row_start) & (kr < row_start + NA_KH) & (kc >= col_start) & (kc < col_start + NA_KW)
        dr = np.clip(kr - qr, -(NA_KH - 1), NA_KH - 1) + NA_KH - 1
        dc = np.clip(kc - qc, -(NA_KW - 1), NA_KW - 1) + NA_KW - 1
        shape = (NA_QROWS, GRID_W, NA_KROWS, GRID_W)
        dr, dc, valid = (np.broadcast_to(t, shape).reshape(NA_QROWS * GRID_W, NA_KROWS * GRID_W) for t in (dr, dc, valid))
        tabs.append(jnp.where(valid[None], rpb[:, dr, dc].astype(F32), NEG_BIG))
    return jnp.stack(tabs, axis=1)


def _na_kernel(*refs, rows, need_ctx):
    q_ref, k_ref, v_ref, qc_ref, kc_ref, vc_ref, tab_ref = refs[:7]
    y_ref = refs[7]
    scale = NA_HEAD_DIM ** -0.5
    kc, vc = kc_ref[0], vc_ref[0]
    n_tiles = rows // NA_QROWS
    nq, nk = NA_QROWS * GRID_W, NA_KROWS * GRID_W
    for i in range(n_tiles):
        r0, ks, cls = _na_tile(i, n_tiles, rows)
        q = q_ref[0, r0 * GRID_W:r0 * GRID_W + nq, :]
        kt = k_ref[0, ks * GRID_W:ks * GRID_W + nk, :]
        vt = v_ref[0, ks * GRID_W:ks * GRID_W + nk, :]
        s_loc = lax.dot_general(q, kt, _NT, preferred_element_type=F32) * scale + tab_ref[0, cls]
        s_ctx = lax.dot_general(q, kc, _NT, preferred_element_type=F32) * scale
        m = jnp.maximum(jnp.max(s_loc, axis=-1, keepdims=True), jnp.max(s_ctx, axis=-1, keepdims=True))
        p_loc, p_ctx = jnp.exp(s_loc - m), jnp.exp(s_ctx - m)
        denom = jnp.sum(p_loc, axis=-1, keepdims=True) + jnp.sum(p_ctx, axis=-1, keepdims=True)
        o = (jnp.dot(p_loc.astype(BF16), vt, preferred_element_type=F32)
             + jnp.dot(p_ctx.astype(BF16), vc, preferred_element_type=F32))
        y_ref[0, r0 * GRID_W:r0 * GRID_W + nq, :] = (o / denom).astype(y_ref.dtype)
    if need_ctx:
        yc_ref = refs[8]
        s = lax.dot_general(qc_ref[0], kc, _NT, preferred_element_type=F32) * scale
        p = jnp.exp(s - jnp.max(s, axis=-1, keepdims=True))
        o = jnp.dot(p.astype(BF16), vc, preferred_element_type=F32) / jnp.sum(p, axis=-1, keepdims=True)
        yc_ref[0] = o.astype(yc_ref.dtype)


def _na(qkv_lat, qkv_ctx, tabs, need_ctx):
    B, T, _ = qkv_lat.shape
    Tc = qkv_ctx.shape[1]
    H, HD = NA_HEADS, NA_HEAD_DIM
    blk = lambda t, off: pl.BlockSpec((1, t, HD), lambda h, b: (b, 0, off + h))
    out_specs = [pl.BlockSpec((1, T, HD), lambda h, b: (b, 0, h))]
    out_shape = [jax.ShapeDtypeStruct((B, T, H * HD), BF16)]
    if need_ctx:
        out_specs.append(pl.BlockSpec((1, Tc, HD), lambda h, b: (b, 0, h)))
        out_shape.append(jax.ShapeDtypeStruct((B, Tc, H * HD), BF16))
    res = pl.pallas_call(
        functools.partial(_na_kernel, rows=T // GRID_W, need_ctx=need_ctx),
        grid=(H, B),
        in_specs=[blk(T, 0), blk(T, H), blk(T, 2 * H), blk(Tc, 0), blk(Tc, H), blk(Tc, 2 * H),
                  pl.BlockSpec((1,) + tabs.shape[1:], lambda h, b: (h, 0, 0, 0))],
        out_specs=out_specs, out_shape=out_shape,
        compiler_params=_params(("arbitrary", "arbitrary")),
        name="natten",
    )(qkv_lat, qkv_lat, qkv_lat, qkv_ctx, qkv_ctx, qkv_ctx, tabs)
    return (res[1] if need_ctx else None), res[0]


def _route_kernel(lg_ref, pos_ref, aff_ref, *, cap):
    lg = lg_ref[0]
    E, N = lg.shape
    ex = jnp.exp(lg - jnp.max(lg, axis=0, keepdims=True))
    aff = ex / jnp.sum(ex, axis=0, keepdims=True)
    aff_ref[0] = aff
    bits = pltpu.bitcast(aff, jnp.int32)
    thr = jnp.zeros((E, 1), jnp.int32)
    for bit in range(29, -1, -1):
        cand = thr | (1 << bit)
        cnt = jnp.sum(jnp.where(bits >= cand, 1.0, 0.0), axis=1, keepdims=True)
        thr = jnp.where(cnt >= cap, cand, thr)
    above = bits > thr
    tied = bits == thr
    need = cap - jnp.sum(jnp.where(above, 1.0, 0.0), axis=1, keepdims=True)
    pc = min(PREFIX_CHUNK, N)
    before = jnp.where(lax.broadcasted_iota(jnp.int32, (pc, pc), 0) < lax.broadcasted_iota(jnp.int32, (pc, pc), 1),
                       1.0, 0.0).astype(BF16)
    tied_seen = jnp.zeros((E, 1), F32)
    sel_seen = jnp.zeros((E, 1), F32)
    for c in range(N // pc):
        sl = slice(c * pc, (c + 1) * pc)
        tied_c = jnp.where(tied[:, sl], 1.0, 0.0)
        tied_before = jnp.dot(tied_c.astype(BF16), before, preferred_element_type=F32) + tied_seen
        sel_c = jnp.where(above[:, sl], 1.0, jnp.where(tied_before < need, tied_c, 0.0))
        sel_before = jnp.dot(sel_c.astype(BF16), before, preferred_element_type=F32) + sel_seen
        pos_ref[0, :, sl] = jnp.where(sel_c > 0.0, sel_before, -1.0)
        tied_seen = tied_seen + jnp.sum(tied_c, axis=1, keepdims=True)
        sel_seen = sel_seen + jnp.sum(sel_c, axis=1, keepdims=True)


def _route(logits_t, cap):
    B, E, N = logits_t.shape
    spec = pl.BlockSpec((1, E, N), lambda b: (b, 0, 0))
    return pl.pallas_call(
        functools.partial(_route_kernel, cap=cap),
        grid=(B,), in_specs=[spec], out_specs=[spec, spec],
        out_shape=[jax.ShapeDtypeStruct((B, E, N), F32)] * 2,
        compiler_params=_params(("arbitrary",)),
        name="route",
    )(logits_t)


def _gather_kernel(pos_ref, aff_ref, h_ref, xs_ref, gate_ref, *, cap):
    pos = pos_ref[0, 0]
    N = pos.shape[1]
    hit = lax.broadcasted_iota(jnp.int32, (cap, N), 0).astype(F32) == pos
    onehot = jnp.where(hit, 1.0, 0.0).astype(BF16)
    xs_ref[0, 0] = jnp.dot(onehot, h_ref[0], preferred_element_type=F32).astype(xs_ref.dtype)
    gate_ref[0, 0] = jnp.sum(jnp.where(hit, aff_ref[0, 0], 0.0), axis=1, keepdims=True)


def _gather(pos, aff, h, cap):
    B, E, N = pos.shape
    D = h.shape[2]
    row = pl.BlockSpec((1, 1, 1, N), lambda b, e: (b, e, 0, 0))
    return pl.pallas_call(
        functools.partial(_gather_kernel, cap=cap),
        grid=(B, E),
        in_specs=[row, row, pl.BlockSpec((1, N, D), lambda b, e: (b, 0, 0))],
        out_specs=[pl.BlockSpec((1, 1, cap, D), lambda b, e: (b, e, 0, 0)),
                   pl.BlockSpec((1, 1, cap, 1), lambda b, e: (b, e, 0, 0))],
        out_shape=[jax.ShapeDtypeStruct((B, E, cap, D), BF16), jax.ShapeDtypeStruct((B, E, cap, 1), F32)],
        compiler_params=_params(("arbitrary", "arbitrary")),
        name="moe_gather",
    )(pos.reshape(B, E, 1, N), aff.reshape(B, E, 1, N), h)


def _ffn_kernel(xs_ref, gate_ref, wg_ref, wu_ref, wd_ref, y_ref):
    xs = xs_ref[0, 0]
    hid = _silu(jnp.dot(xs, wg_ref[0], preferred_element_type=F32)) * jnp.dot(xs, wu_ref[0], preferred_element_type=F32)
    y = jnp.dot(hid.astype(BF16), wd_ref[0], preferred_element_type=F32) * gate_ref[0, 0]
    y_ref[0, 0] = y.astype(y_ref.dtype)


def _ffn(xs, gate, wg, wu, wd):
    B, E, cap, D = xs.shape
    F = wg.shape[2]
    tok = lambda n: pl.BlockSpec((1, 1, cap, n), lambda e, b: (b, e, 0, 0))
    return pl.pallas_call(
        _ffn_kernel,
        grid=(E, B),
        in_specs=[tok(D), tok(1), pl.BlockSpec((1, D, F), lambda e, b: (e, 0, 0)),
                  pl.BlockSpec((1, D, F), lambda e, b: (e, 0, 0)), pl.BlockSpec((1, F, D), lambda e, b: (e, 0, 0))],
        out_specs=tok(D),
        out_shape=jax.ShapeDtypeStruct((B, E, cap, D), BF16),
        compiler_params=_params(("arbitrary", "arbitrary")),
        name="moe_ffn",
    )(xs, gate, wg, wu, wd)


def _scatter_kernel(pos_ref, y_ref, x_ref, m_ref, o_ref, acc_ref, *, cap, gate_row):
    e = pl.program_id(2)
    pos_t = pos_ref[0]
    lane = lax.broadcasted_iota(jnp.int32, pos_t.shape, 1)
    slot = jnp.sum(jnp.where(lane == e, pos_t, 0.0), axis=1, keepdims=True)
    hit = slot == lax.broadcasted_iota(jnp.int32, (pos_t.shape[0], cap), 1).astype(F32)
    part = jnp.dot(jnp.where(hit, 1.0, 0.0).astype(BF16), y_ref[0, 0], preferred_element_type=F32)

    @pl.when(e == 0)
    def _():
        acc_ref[...] = part

    @pl.when(e > 0)
    def _():
        acc_ref[...] += part

    @pl.when(e == pl.num_programs(2) - 1)
    def _():
        o_ref[0] = x_ref[0] + m_ref[0, gate_row:gate_row + 1, :] * acc_ref[...]


def _scatter(pos, y, x, mod, gate_row):
    B, E, cap, D = y.shape
    N = x.shape[1]
    tn = min(N, 512)
    per_sample = mod.shape[0] > 1
    return pl.pallas_call(
        functools.partial(_scatter_kernel, cap=cap, gate_row=gate_row),
        grid=(B, N // tn, E),
        in_specs=[pl.BlockSpec((1, tn, E), lambda b, t, e: (b, t, 0)),
                  pl.BlockSpec((1, 1, cap, D), lambda b, t, e: (b, e, 0, 0)),
                  pl.BlockSpec((1, tn, D), lambda b, t, e: (b, t, 0)),
                  pl.BlockSpec((1, N_MOD, D), (lambda b, t, e: (b, 0, 0)) if per_sample else (lambda b, t, e: (0, 0, 0)))],
        out_specs=pl.BlockSpec((1, tn, D), lambda b, t, e: (b, t, 0)),
        out_shape=jax.ShapeDtypeStruct((B, N, D), F32),
        scratch_shapes=[pltpu.VMEM((tn, D), F32)],
        compiler_params=_params(("arbitrary", "arbitrary", "arbitrary")),
        name="moe_scatter",
    )(jnp.transpose(pos, (0, 2, 1)), y, x, mod)


def _moe(x, g, mod, router_wt, wg, wu, wd):
    N = x.shape[1]
    cap = EC_CAPACITY_FACTOR * N // N_EXPERTS
    h, logits_t = _norm(x, g, mod, shift_row=3, scale_row=4, router_wt=router_wt)
    pos, aff = _route(logits_t, cap)
    xs, gate = _gather(pos, aff, h, cap)
    y = _ffn(xs, gate, wg, wu, wd)
    return _scatter(pos, y, x, mod, gate_row=5)


def _rope_tables(n_tok):
    t = jnp.arange(n_tok)
    rows = (t // GRID_W).astype(F32)
    cols = (t % GRID_W).astype(F32)
    n_freq = GLA_DK // 4
    inv = ROPE_THETA ** (-jnp.arange(n_freq, dtype=F32) / n_freq)
    ang = jnp.concatenate([rows[:, None] * inv, cols[:, None] * inv], axis=-1)
    cos = jnp.repeat(jnp.cos(ang), 2, axis=-1)
    sin = jnp.repeat(jnp.sin(ang), 2, axis=-1) * jnp.tile(jnp.asarray([-1.0, 1.0], F32), GLA_DK // 2)
    return cos, sin


def _rec_mixer(h_ctx, h_lat, w_in, alpha_up, alpha_b, gla_g, conv_w, conv_b, w_a, b_a, w_i, b_i, lam, rope_tabs):
    za0 = 2 * GLA_KEY_WIDTH + 2 * GLA_VAL_WIDTH
    za1 = za0 + 2 * GLA_RANK
    w_main = jnp.concatenate([w_in[:, :za0], w_in[:, za1:]], axis=1).astype(BF16)
    w_za = jnp.pad(w_in[:, za0:za1], ((0, 0), (0, LANES - 2 * GLA_RANK))).astype(BF16)
    aup = jnp.zeros((2, LANES, GLA_KEY_WIDTH), F32)
    aup = aup.at[0, :GLA_RANK].set(alpha_up[0]).at[1, GLA_RANK:2 * GLA_RANK].set(alpha_up[1])
    ab = alpha_b.reshape(2, 1, GLA_KEY_WIDTH)
    gg = gla_g.reshape(1, GLA_DV)
    wa, wi = w_a.astype(BF16), w_i.astype(BF16)
    outs = []
    gla_state, lru_state = None, None
    for h, tabs in ((h_ctx, None), (h_lat, rope_tabs)):
        B = h.shape[0]
        p_main = _proj(h, w_main, F32, tn=1280)
        p_za = _proj(h, w_za, F32, tn=LANES)
        y_gla, gla_state = _gla(p_main, p_za, aup, ab, gg, tabs, gla_state)
        if lru_state is None:
            lru_state = (jnp.zeros((B, 1, LRU_WIDTH), F32),) * 2
        y_lru, lru_state = _lru(p_main, conv_w, conv_b, wa, b_a, wi, b_i, lam, lru_state)
        outs.append([y_gla, y_lru])
    return outs


def kernel(x, c, ctx, c_ctx, w_mod, b_mod, norm_mix_g, norm_ffn_g, w_out, router_w, exp_w_gate, exp_w_up, exp_w_down, rec_w_in, gla_alpha_up, gla_alpha_b, gla_norm_g, lru_conv_w, lru_conv_b, lru_w_a, lru_b_a, lru_w_i, lru_b_i, lru_lambda, na_w_qkv, na_rpb, norm_f_g):
    B, T, D = x.shape
    pad_rows = (-(B + 1)) % 8
    cc = jnp.concatenate([c, c_ctx[None, :], jnp.zeros((pad_rows, D), F32)], axis=0)
    m_all = _modulation(cc, w_mod, b_mod)
    rope_tabs = _rope_tables(T)
    for l in range(DEPTH):
        last = l == DEPTH - 1
        i = l // 2
        m_lat = m_all[l, :B].reshape(B, N_MOD, D)
        m_ctx = m_all[l, B:B + 1].reshape(1, N_MOD, D)
        h_lat = _norm(x, norm_mix_g[l], m_lat)
        h_ctx = _norm(ctx, norm_mix_g[l], m_ctx)
        w_o = w_out[l].astype(BF16)
        if l % 2 == 0:
            ys_ctx, ys_lat = _rec_mixer(h_ctx, h_lat, rec_w_in[i], gla_alpha_up[i], gla_alpha_b[i], gla_norm_g[i],
                                        lru_conv_w[i], lru_conv_b[i], lru_w_a[i], lru_b_a[i], lru_w_i[i], lru_b_i[i],
                                        lru_lambda[i], rope_tabs)
            ws = [w_o[:GLA_VAL_WIDTH], w_o[GLA_VAL_WIDTH:]]
        else:
            w_qkv = na_w_qkv[i].astype(BF16)
            qkv_lat = _proj(h_lat, w_qkv, BF16, tn=1536)
            qkv_ctx = _proj(h_ctx, w_qkv, BF16, tn=1536)
            y_ctx, y_lat = _na(qkv_lat, qkv_ctx, _na_bias_tables(na_rpb[i], T // GRID_W), not last)
            ys_ctx, ys_lat, ws = [y_ctx], [y_lat], [w_o]
        moe_w = (router_w[l].T, exp_w_gate[l].astype(BF16), exp_w_up[l].astype(BF16), exp_w_down[l].astype(BF16))
        x = _oproj(ys_lat, ws, x, m_lat, gate_row=2)
        x = _moe(x, norm_ffn_g[l], m_lat, *moe_w)
        if not last:
            ctx = _oproj(ys_ctx, ws, ctx, m_ctx, gate_row=2)
            ctx = _moe(ctx, norm_ffn_g[l], m_ctx, *moe_w)
    return _norm(x, norm_f_g, out_dtype=F32)
```

```python
import functools

import numpy as np
import jax
import jax.numpy as jnp
from jax import lax
from jax.experimental import pallas as pl
from jax.experimental.pallas import tpu as pltpu

D_MODEL = 2048
DEPTH = 4
GRID_W = 64
RMS_EPS = 1e-6
N_MOD = 6
ROPE_THETA = 10000.0
GLA_HEADS = 4
GLA_DK = D_MODEL // 16
GLA_DV = D_MODEL // 8
GLA_KEY_WIDTH = GLA_HEADS * GLA_DK
GLA_VAL_WIDTH = GLA_HEADS * GLA_DV
GLA_RANK = 16
GLA_TAU = 16.0
GLA_CHUNK = 64
LRU_WIDTH = D_MODEL // 2
LRU_BLOCKS = 8
LRU_BLOCK_DIM = LRU_WIDTH // LRU_BLOCKS
LRU_CONV = 4
LRU_C = 8.0
NA_HEADS = 16
NA_HEAD_DIM = D_MODEL // NA_HEADS
NA_KH = 8
NA_KW = 16
N_EXPERTS = 16
EC_CAPACITY_FACTOR = 2
EXPERT_FF = D_MODEL // 2

F32 = jnp.float32
BF16 = jnp.bfloat16
HIGHEST = lax.Precision.HIGHEST
LANES = 128
NEG_BIG = -1e30
NA_QROWS = 4
NA_KROWS = NA_QROWS + NA_KH - 1
PREFIX_CHUNK = 256
SLOT_WINDOW = 64
SLOT_ALIGN = 16

_NT = (((1,), (1,)), ((), ()))
_TN = (((0,), (0,)), ((), ()))


def _params(sem, vmem_mib=48):
    return pltpu.CompilerParams(dimension_semantics=sem, vmem_limit_bytes=vmem_mib * 2 ** 20)


def _sigmoid(z):
    return 1.0 / (1.0 + jnp.exp(-z))


def _silu(z):
    return z * _sigmoid(z)


def _mod_kernel(c_ref, w_ref, b_ref, o_ref):
    a = _silu(c_ref[...]).astype(BF16)
    o_ref[0] = jnp.dot(a, w_ref[0].astype(BF16), preferred_element_type=F32) + b_ref[0]


def _modulation(cc, w_mod, b_mod):
    R, D = cc.shape
    L, _, N = w_mod.shape
    tn = 1024
    return pl.pallas_call(
        _mod_kernel,
        grid=(L, N // tn),
        in_specs=[pl.BlockSpec((R, D), lambda l, j: (0, 0)),
                  pl.BlockSpec((1, D, tn), lambda l, j: (l, 0, j)),
                  pl.BlockSpec((1, 1, tn), lambda l, j: (l, 0, j))],
        out_specs=pl.BlockSpec((1, R, tn), lambda l, j: (l, 0, j)),
        out_shape=jax.ShapeDtypeStruct((L, R, N), F32),
        compiler_params=_params(("arbitrary", "arbitrary")),
        name="modulation",
    )(cc, w_mod, b_mod.reshape(L, 1, N))


def _norm_kernel(*refs, modulated, with_router, shift_row, scale_row):
    it = iter(refs)
    x_ref, g_ref = next(it), next(it)
    m_ref = next(it) if modulated else None
    wr_ref = next(it) if with_router else None
    h_ref = next(it)
    lg_ref = next(it) if with_router else None
    x = x_ref[0]
    y = x * lax.rsqrt(jnp.mean(x * x, axis=-1, keepdims=True) + RMS_EPS) * g_ref[...]
    if modulated:
        y = y * (1.0 + m_ref[0, scale_row:scale_row + 1, :]) + m_ref[0, shift_row:shift_row + 1, :]
    h_ref[0] = y.astype(h_ref.dtype)
    if with_router:
        lg_ref[0] = lax.dot_general(wr_ref[...], y, _NT, precision=HIGHEST, preferred_element_type=F32)


def _norm(x, g, mod=None, shift_row=0, scale_row=1, router_wt=None, out_dtype=BF16):
    B, T, D = x.shape
    tt = 256
    modulated, with_router = mod is not None, router_wt is not None
    in_specs = [pl.BlockSpec((1, tt, D), lambda b, t: (b, t, 0)), pl.BlockSpec((1, D), lambda b, t: (0, 0))]
    args = [x, g.reshape(1, D)]
    if modulated:
        per_sample = mod.shape[0] > 1
        in_specs.append(pl.BlockSpec((1, N_MOD, D), (lambda b, t: (b, 0, 0)) if per_sample else (lambda b, t: (0, 0, 0))))
        args.append(mod)
    out_specs = [pl.BlockSpec((1, tt, D), lambda b, t: (b, t, 0))]
    out_shape = [jax.ShapeDtypeStruct((B, T, D), out_dtype)]
    if with_router:
        E = router_wt.shape[0]
        in_specs.append(pl.BlockSpec((E, D), lambda b, t: (0, 0)))
        args.append(router_wt)
        out_specs.append(pl.BlockSpec((1, E, tt), lambda b, t: (b, 0, t)))
        out_shape.append(jax.ShapeDtypeStruct((B, E, T), F32))
    res = pl.pallas_call(
        functools.partial(_norm_kernel, modulated=modulated, with_router=with_router,
                          shift_row=shift_row, scale_row=scale_row),
        grid=(B, T // tt), in_specs=in_specs, out_specs=out_specs, out_shape=out_shape,
        compiler_params=_params(("arbitrary", "arbitrary")),
        name="rmsnorm",
    )(*args)
    return res if with_router else res[0]


def _proj_kernel(a_ref, w_ref, o_ref):
    o_ref[...] = jnp.dot(a_ref[...], w_ref[...], preferred_element_type=F32).astype(o_ref.dtype)


def _proj(a, w, out_dtype, tn):
    B, T, K = a.shape
    N = w.shape[1]
    M = B * T
    tm = min(M, 512)
    out = pl.pallas_call(
        _proj_kernel,
        grid=(N // tn, M // tm),
        in_specs=[pl.BlockSpec((tm, K), lambda j, i: (i, 0)), pl.BlockSpec((K, tn), lambda j, i: (0, j))],
        out_specs=pl.BlockSpec((tm, tn), lambda j, i: (i, j)),
        out_shape=jax.ShapeDtypeStruct((M, N), out_dtype),
        compiler_params=_params(("arbitrary", "arbitrary")),
        name="proj",
    )(a.reshape(M, K), w)
    return out.reshape(B, T, N)


def _oproj_kernel(*refs, n_in, gate_row):
    a_refs, w_refs = refs[:n_in], refs[n_in:2 * n_in]
    x_ref, m_ref, o_ref = refs[2 * n_in:]
    acc = jnp.dot(a_refs[0][0], w_refs[0][...], preferred_element_type=F32)
    for a_ref, w_ref in zip(a_refs[1:], w_refs[1:]):
        acc = acc + jnp.dot(a_ref[0], w_ref[...], preferred_element_type=F32)
    o_ref[0] = x_ref[0] + m_ref[0, gate_row:gate_row + 1, :] * acc


def _oproj(ys, ws, x, mod, gate_row):
    B, T, N = x.shape
    tm, tn = min(T, 512), 1024
    per_sample = mod.shape[0] > 1
    in_specs = [pl.BlockSpec((1, tm, y.shape[2]), lambda j, b, t: (b, t, 0)) for y in ys]
    in_specs += [pl.BlockSpec((w.shape[0], tn), lambda j, b, t: (0, j)) for w in ws]
    in_specs += [pl.BlockSpec((1, tm, tn), lambda j, b, t: (b, t, j)),
                 pl.BlockSpec((1, N_MOD, tn), (lambda j, b, t: (b, 0, j)) if per_sample else (lambda j, b, t: (0, 0, j)))]
    return pl.pallas_call(
        functools.partial(_oproj_kernel, n_in=len(ys), gate_row=gate_row),
        grid=(N // tn, B, T // tm), in_specs=in_specs,
        out_specs=pl.BlockSpec((1, tm, tn), lambda j, b, t: (b, t, j)),
        out_shape=jax.ShapeDtypeStruct((B, T, N), F32),
        compiler_params=_params(("arbitrary", "arbitrary", "arbitrary")),
        name="oproj",
    )(*ys, *ws, x, mod)


def _log_sigmoid(z):
    return jnp.minimum(z, 0.0) - jnp.log(1.0 + jnp.exp(-jnp.abs(z)))


def _gla_kernel(*refs, n_chunks, use_rope, has_state):
    it = iter(refs)
    q_ref, k_ref, v_ref, g_ref, za_ref = (next(it) for _ in range(5))
    cos_ref, sin_ref = (next(it), next(it)) if use_rope else (None, None)
    aup_ref, ab_ref, gg_ref = next(it), next(it), next(it)
    s0f_ref, s0b_ref = (next(it), next(it)) if has_state else (None, None)
    y_ref, sf_ref, sb_ref = next(it), next(it), next(it)
    la_scr, qk_scr, o_scr, st_scr = next(it), next(it), next(it), next(it)
    C = GLA_CHUNK

    za = za_ref[0]
    for d in range(2):
        z = jnp.dot(za, aup_ref[d], precision=HIGHEST, preferred_element_type=F32) + ab_ref[d]
        la_scr[d] = _log_sigmoid(z) * (1.0 / GLA_TAU)

    q = q_ref[0] * (GLA_DK ** -0.5)
    k = k_ref[0]
    if use_rope:
        cos, sin = cos_ref[...], sin_ref[...]
        even = (lax.broadcasted_iota(jnp.int32, q.shape, 1) % 2) == 0

        def rope(t):
            swapped = jnp.where(even, pltpu.roll(t, LANES - 1, 1), pltpu.roll(t, 1, 1))
            return t * cos + swapped * sin
        q, k = rope(q), rope(k)
    qk_scr[0] = q
    qk_scr[1] = k

    if has_state:
        st_scr[0] = s0f_ref[0, 0]
        st_scr[1] = s0b_ref[0, 0]
    else:
        st_scr[...] = jnp.zeros(st_scr.shape, F32)

    row = lax.broadcasted_iota(jnp.int32, (C, C), 0)
    col = lax.broadcasted_iota(jnp.int32, (C, C), 1)
    keep = (row >= col, row <= col)
    tri = tuple(jnp.where(m, 1.0, 0.0).astype(F32) for m in keep)

    def chunk(c, d):
        r0 = pl.multiple_of(c * C, C)
        qc = qk_scr[0, pl.ds(r0, C), :]
        kc = qk_scr[1, pl.ds(r0, C), :]
        vb = v_ref[0, pl.ds(r0, C), :].astype(BF16)
        la = la_scr[d, pl.ds(r0, C), :]
        b = jnp.dot(tri[d], la, precision=HIGHEST, preferred_element_type=F32)
        tot = jnp.sum(la, axis=0, keepdims=True)
        qd = (qc * jnp.exp(b)).astype(BF16)
        ki = (kc * jnp.exp(-b)).astype(BF16)
        ke = (kc * jnp.exp(tot - b)).astype(BF16)
        s = lax.dot_general(qd, ki, _NT, preferred_element_type=F32)
        s = jnp.where(keep[d], s, 0.0).astype(BF16)
        st = st_scr[d]
        o = jnp.dot(s, vb, preferred_element_type=F32)
        o = o + lax.dot_general(qd, st.astype(BF16), _NT, preferred_element_type=F32)
        st_scr[d] = st * jnp.exp(tot) + lax.dot_general(vb, ke, _TN, preferred_element_type=F32)
        return r0, o

    half = n_chunks // 2

    def first_half(i, carry):
        r0, o = chunk(i, 0)
        o_scr[pl.ds(r0, C), :] = o
        r0, o = chunk(n_chunks - 1 - i, 1)
        o_scr[pl.ds(r0, C), :] = o
        return carry

    def second_half(i, carry):
        r0, o = chunk(i, 0)
        o_scr[pl.ds(r0, C), :] += o
        r0, o = chunk(n_chunks - 1 - i, 1)
        o_scr[pl.ds(r0, C), :] += o
        return carry

    lax.fori_loop(0, half, first_half, 0)
    lax.fori_loop(half, n_chunks, second_half, 0)
    sf_ref[0, 0] = st_scr[0]
    sb_ref[0, 0] = st_scr[1]

    o = o_scr[...]
    o = o * lax.rsqrt(jnp.mean(o * o, axis=-1, keepdims=True) + RMS_EPS) * gg_ref[...]
    y_ref[0] = (o * _silu(g_ref[0])).astype(y_ref.dtype)


def _gla(p_main, p_za, aup, ab, gg, rope_tabs, states):
    B, T, _ = p_main.shape
    H, DK, DV = GLA_HEADS, GLA_DK, GLA_DV
    use_rope, has_state = rope_tabs is not None, states is not None
    in_specs = [pl.BlockSpec((1, T, DK), lambda b, h: (b, 0, h)),
                pl.BlockSpec((1, T, DK), lambda b, h: (b, 0, H + h)),
                pl.BlockSpec((1, T, DV), lambda b, h: (b, 0, H + h)),
                pl.BlockSpec((1, T, DV), lambda b, h: (b, 0, 2 * H + h)),
                pl.BlockSpec((1, T, LANES), lambda b, h: (b, 0, 0))]
    args = [p_main, p_main, p_main, p_main, p_za]
    if use_rope:
        in_specs += [pl.BlockSpec((T, DK), lambda b, h: (0, 0))] * 2
        args += list(rope_tabs)
    in_specs += [pl.BlockSpec((2, LANES, DK), lambda b, h: (0, 0, h)),
                 pl.BlockSpec((2, 1, DK), lambda b, h: (0, 0, h)),
                 pl.BlockSpec((1, DV), lambda b, h: (0, 0))]
    args += [aup, ab, gg]
    st_spec = pl.BlockSpec((1, 1, DV, DK), lambda b, h: (b, h, 0, 0))
    if has_state:
        in_specs += [st_spec, st_spec]
        args += list(states)
    st_shape = jax.ShapeDtypeStruct((B, H, DV, DK), F32)
    y, sf, sb = pl.pallas_call(
        functools.partial(_gla_kernel, n_chunks=T // GLA_CHUNK, use_rope=use_rope, has_state=has_state),
        grid=(B, H), in_specs=in_specs,
        out_specs=[pl.BlockSpec((1, T, DV), lambda b, h: (b, 0, h)), st_spec, st_spec],
        out_shape=[jax.ShapeDtypeStruct((B, T, H * DV), BF16), st_shape, st_shape],
        scratch_shapes=[pltpu.VMEM((2, T, DK), F32), pltpu.VMEM((2, T, DK), F32),
                        pltpu.VMEM((T, DV), F32), pltpu.VMEM((2, DV, DK), F32)],
        compiler_params=_params(("arbitrary", "arbitrary")),
        name="gla",
    )(*args)
    return y, (sf, sb)


def _gelu_tanh(z):
    return 0.5 * z * (1.0 + jnp.tanh(np.sqrt(2.0 / np.pi) * (z + 0.044715 * (z * z * z))))


def _lru_kernel(x_ref, gt_ref, cw_ref, cb_ref, wa_ref, ba_ref, wi_ref, bi_ref, lam_ref, h0f_ref, h0b_ref,
                y_ref, hf_ref, hb_ref, *, T):
    x = x_ref[0]
    row = lax.broadcasted_iota(jnp.int32, x.shape, 0)

    def from_earlier(v, d, fill):
        return jnp.where(row >= d, pltpu.roll(v, d, 0), fill)

    def from_later(v, d, fill):
        return jnp.where(row < T - d, pltpu.roll(v, T - d, 0), fill)

    xc = (cw_ref[0:1, :] * from_earlier(x, 1, 0.0) + cw_ref[1:2, :] * x
          + cw_ref[2:3, :] * from_later(x, 1, 0.0) + cw_ref[3:4, :] * from_later(x, 2, 0.0) + cb_ref[...])
    xcb = xc.astype(BF16)
    total = None
    for d, (h0_ref, fin_ref) in enumerate(((h0f_ref, hf_ref), (h0b_ref, hb_ref))):
        r = _sigmoid(jnp.dot(xcb, wa_ref[d, 0], preferred_element_type=F32) + ba_ref[d])
        gate_i = _sigmoid(jnp.dot(xcb, wi_ref[d, 0], preferred_element_type=F32) + bi_ref[d])
        neg_lam = -lam_ref[d]
        softplus = jnp.maximum(neg_lam, 0.0) + jnp.log(1.0 + jnp.exp(-jnp.abs(neg_lam)))
        log_a = (-LRU_C) * r * softplus
        a = jnp.exp(log_a)
        h = jnp.sqrt(1.0 - jnp.exp(2.0 * log_a)) * (gate_i * xc)
        shift = from_earlier if d == 0 else from_later
        first = 0 if d == 0 else T - 1
        h = jnp.where(row == first, h + a * h0_ref[0], h)
        step = 1
        while step < T:
            h = a * shift(h, step, 0.0) + h
            if step * 2 < T:
                a = a * shift(a, step, 1.0)
            step *= 2
        last = T - 1 if d == 0 else 0
        fin_ref[0] = h[last:last + 1, :]
        total = h if total is None else total + h
    y_ref[0] = (total * _gelu_tanh(gt_ref[0])).astype(y_ref.dtype)


def _lru(p_main, cw, cb, wa, ba, wi, bi, lam, states):
    B, T, _ = p_main.shape
    W, NB, BD = LRU_WIDTH, LRU_BLOCKS, LRU_BLOCK_DIM
    x_blk0 = (2 * GLA_KEY_WIDTH + 2 * GLA_VAL_WIDTH) // BD
    vec = lambda n: pl.BlockSpec((n, 1, BD), lambda b, j: (0, 0, j))
    mat = pl.BlockSpec((2, 1, BD, BD), lambda b, j: (0, j, 0, 0))
    st_spec = pl.BlockSpec((1, 1, BD), lambda b, j: (b, 0, j))
    st_shape = jax.ShapeDtypeStruct((B, 1, W), F32)
    y, hf, hb = pl.pallas_call(
        functools.partial(_lru_kernel, T=T),
        grid=(B, NB),
        in_specs=[pl.BlockSpec((1, T, BD), lambda b, j: (b, 0, x_blk0 + j)),
                  pl.BlockSpec((1, T, BD), lambda b, j: (b, 0, x_blk0 + NB + j)),
                  pl.BlockSpec((LRU_CONV, BD), lambda b, j: (0, j)),
                  pl.BlockSpec((1, BD), lambda b, j: (0, j)),
                  mat, vec(2), mat, vec(2), vec(2), st_spec, st_spec],
        out_specs=[pl.BlockSpec((1, T, BD), lambda b, j: (b, 0, j)), st_spec, st_spec],
        out_shape=[jax.ShapeDtypeStruct((B, T, W), BF16), st_shape, st_shape],
        compiler_params=_params(("arbitrary", "arbitrary")),
        name="rglru",
    )(p_main, p_main, cw, cb.reshape(1, W), wa, ba.reshape(2, 1, W), wi, bi.reshape(2, 1, W),
      lam.reshape(2, 1, W), *states)
    return y, (hf, hb)


def _na_tile(i, n_tiles, rows):
    r0 = i * NA_QROWS
    ks = min(max(r0 - NA_KH // 2, 0), rows - NA_KROWS)
    cls = 0 if i == 0 else (2 if i == n_tiles - 1 else 1)
    return r0, ks, cls


def _na_bias_tables(rpb, rows):
    H = rpb.shape[0]
    qc = np.arange(GRID_W)[:, None]
    kc = np.arange(GRID_W)[None, :]
    col_start = np.clip(qc - NA_KW // 2, 0, GRID_W - NA_KW)
    col_valid = (kc >= col_start) & (kc < col_start + NA_KW)
    dc = np.clip(kc - qc, -(NA_KW - 1), NA_KW - 1) + NA_KW - 1
    pick = (dc[None] == np.arange(2 * NA_KW - 1)[:, None, None]).astype(np.float32)
    band = jnp.einsum('hrd,dqk->hrqk', rpb.astype(F32), jnp.asarray(pick), precision=HIGHEST)
    band = jnp.where(col_valid, band, NEG_BIG)
    masked = jnp.full((H, GRID_W, GRID_W), NEG_BIG, F32)
    n_tiles = rows // NA_QROWS
    tabs = []
    for i in (0, 1, n_tiles - 1):
        r0, ks, _ = _na_tile(i, n_tiles, rows)
        per_qrow = []
        for qr in range(r0, r0 + NA_QROWS):
            row_start = min(max(qr - NA_KH // 2, 0), rows - NA_KH)
            blocks = [band[:, kr - qr + NA_KH - 1] if row_start <= kr < row_start + NA_KH else masked
                      for kr in range(ks, ks + NA_KROWS)]
            per_qrow.append(jnp.stack(blocks, axis=2).reshape(H, GRID_W, NA_KROWS * GRID_W))
        tabs.append(jnp.concatenate(per_qrow, axis=1))
    return jnp.stack(tabs, axis=1)


def _na_kernel(*refs, rows, need_ctx):
    q_ref, k_ref, v_ref, qc_ref, kc_ref, vc_ref, tab_ref = refs[:7]
    y_ref = refs[7]
    scale = NA_HEAD_DIM ** -0.5
    kc, vc = kc_ref[0], vc_ref[0]
    n_tiles = rows // NA_QROWS
    nq, nk = NA_QROWS * GRID_W, NA_KROWS * GRID_W
    for i in range(n_tiles):
        r0, ks, cls = _na_tile(i, n_tiles, rows)
        q = q_ref[0, r0 * GRID_W:r0 * GRID_W + nq, :]
        kt = k_ref[0, ks * GRID_W:ks * GRID_W + nk, :]
        vt = v_ref[0, ks * GRID_W:ks * GRID_W + nk, :]
        s_loc = lax.dot_general(q, kt, _NT, preferred_element_type=F32) * scale + tab_ref[0, cls]
        s_ctx = lax.dot_general(q, kc, _NT, preferred_element_type=F32) * scale
        m = jnp.maximum(jnp.max(s_loc, axis=-1, keepdims=True), jnp.max(s_ctx, axis=-1, keepdims=True))
        p_loc, p_ctx = jnp.exp(s_loc - m), jnp.exp(s_ctx - m)
        denom = jnp.sum(p_loc, axis=-1, keepdims=True) + jnp.sum(p_ctx, axis=-1, keepdims=True)
        o = (jnp.dot(p_loc.astype(BF16), vt, preferred_element_type=F32)
             + jnp.dot(p_ctx.astype(BF16), vc, preferred_element_type=F32))
        y_ref[0, r0 * GRID_W:r0 * GRID_W + nq, :] = (o / denom).astype(y_ref.dtype)
    if need_ctx:
        yc_ref = refs[8]
        s = lax.dot_general(qc_ref[0], kc, _NT, preferred_element_type=F32) * scale
        p = jnp.exp(s - jnp.max(s, axis=-1, keepdims=True))
        o = jnp.dot(p.astype(BF16), vc, preferred_element_type=F32) / jnp.sum(p, axis=-1, keepdims=True)
        yc_ref[0] = o.astype(yc_ref.dtype)


def _na(qkv_lat, qkv_ctx, tabs, need_ctx):
    B, T, _ = qkv_lat.shape
    Tc = qkv_ctx.shape[1]
    H, HD = NA_HEADS, NA_HEAD_DIM
    blk = lambda t, off: pl.BlockSpec((1, t, HD), lambda h, b: (b, 0, off + h))
    out_specs = [pl.BlockSpec((1, T, HD), lambda h, b: (b, 0, h))]
    out_shape = [jax.ShapeDtypeStruct((B, T, H * HD), BF16)]
    if need_ctx:
        out_specs.append(pl.BlockSpec((1, Tc, HD), lambda h, b: (b, 0, h)))
        out_shape.append(jax.ShapeDtypeStruct((B, Tc, H * HD), BF16))
    res = pl.pallas_call(
        functools.partial(_na_kernel, rows=T // GRID_W, need_ctx=need_ctx),
        grid=(H, B),
        in_specs=[blk(T, 0), blk(T, H), blk(T, 2 * H), blk(Tc, 0), blk(Tc, H), blk(Tc, 2 * H),
                  pl.BlockSpec((1,) + tabs.shape[1:], lambda h, b: (h, 0, 0, 0))],
        out_specs=out_specs, out_shape=out_shape,
        compiler_params=_params(("arbitrary", "arbitrary")),
        name="natten",
    )(qkv_lat, qkv_lat, qkv_lat, qkv_ctx, qkv_ctx, qkv_ctx, tabs)
    return (res[1] if need_ctx else None), res[0]


def _route_kernel(lg_ref, pos_ref, aff_ref, lo_ref, *, cap):
    lg = lg_ref[0]
    E, N = lg.shape
    ex = jnp.exp(lg - jnp.max(lg, axis=0, keepdims=True))
    aff = ex / jnp.sum(ex, axis=0, keepdims=True)
    aff_ref[0] = aff
    bits = pltpu.bitcast(aff, jnp.int32)
    thr = jnp.zeros((E, 1), jnp.int32)
    for bit in range(29, -1, -1):
        cand = thr | (1 << bit)
        cnt = jnp.sum(jnp.where(bits >= cand, 1.0, 0.0), axis=1, keepdims=True)
        thr = jnp.where(cnt >= cap, cand, thr)
    above = bits > thr
    tied = bits == thr
    need = cap - jnp.sum(jnp.where(above, 1.0, 0.0), axis=1, keepdims=True)
    pc = min(PREFIX_CHUNK, N)
    before = jnp.where(lax.broadcasted_iota(jnp.int32, (pc, pc), 0) < lax.broadcasted_iota(jnp.int32, (pc, pc), 1),
                       1.0, 0.0).astype(BF16)
    tied_seen = jnp.zeros((E, 1), F32)
    sel_seen = jnp.zeros((E, 1), F32)
    tile_lane = lax.broadcasted_iota(jnp.int32, (E, LANES), 1)
    lo = jnp.where(tile_lane == N // pc, float(cap), 0.0)
    for c in range(N // pc):
        sl = slice(c * pc, (c + 1) * pc)
        lo = jnp.where(tile_lane == c, sel_seen, lo)
        tied_c = jnp.where(tied[:, sl], 1.0, 0.0)
        tied_before = jnp.dot(tied_c.astype(BF16), before, preferred_element_type=F32) + tied_seen
        sel_c = jnp.where(above[:, sl], 1.0, jnp.where(tied_before < need, tied_c, 0.0))
        sel_before = jnp.dot(sel_c.astype(BF16), before, preferred_element_type=F32) + sel_seen
        pos_ref[0, :, sl] = jnp.where(sel_c > 0.0, sel_before, -1.0)
        tied_seen = tied_seen + jnp.sum(tied_c, axis=1, keepdims=True)
        sel_seen = sel_seen + jnp.sum(sel_c, axis=1, keepdims=True)
    lo_ref[0] = lo.astype(jnp.int32)


def _route(logits_t, cap):
    B, E, N = logits_t.shape
    spec = pl.BlockSpec((1, E, N), lambda b: (b, 0, 0))
    lo_spec = pl.BlockSpec((1, E, LANES), lambda b: (b, 0, 0))
    pos, aff, lo = pl.pallas_call(
        functools.partial(_route_kernel, cap=cap),
        grid=(B,), in_specs=[spec], out_specs=[spec, spec, lo_spec],
        out_shape=[jax.ShapeDtypeStruct((B, E, N), F32)] * 2 + [jax.ShapeDtypeStruct((B, E, LANES), jnp.int32)],
        compiler_params=_params(("arbitrary",)),
        name="route",
    )(logits_t)
    n_bounds = N // min(PREFIX_CHUNK, N) + 1
    return pos, aff, lo[:, :, :n_bounds].reshape(-1)


def _window_count(lo_ref, base, experts, tile, n_bounds, win):
    starts, n_win = [], 0
    for e in experts:
        first = lo_ref[(base + e) * n_bounds + tile]
        end = lo_ref[(base + e) * n_bounds + tile + 1]
        start = (first // SLOT_ALIGN) * SLOT_ALIGN
        starts.append(start)
        n_win = jnp.maximum(n_win, (end - start + win - 1) // win)
    return starts, n_win


def _gather_kernel(lo_ref, pos_ref, h_ref, xs_ref, *, cap, win, tn):
    EG, N = pos_ref.shape[1], pos_ref.shape[2]
    n_bounds = N // tn + 1
    base = pl.program_id(0) * (EG * pl.num_programs(1)) + pl.program_id(1) * EG
    xs_ref[...] = jnp.zeros(xs_ref.shape, xs_ref.dtype)
    slot_iota = lax.broadcasted_iota(jnp.int32, (win, tn), 0)
    for j in range(N // tn):
        h_tile = h_ref[0, j * tn:(j + 1) * tn, :]
        starts, n_win = _window_count(lo_ref, base, range(EG), j, n_bounds, win)

        def window(w, carry):
            pieces, offs = [], []
            for e in range(EG):
                first = starts[e] + w * win
                off = jnp.minimum(first, cap - win)
                prow = pos_ref[0, e:e + 1, j * tn:(j + 1) * tn]
                hit = ((slot_iota + off).astype(F32) == prow) & (prow >= first.astype(F32))
                pieces.append(jnp.where(hit, 1.0, 0.0).astype(BF16))
                offs.append(pl.multiple_of(off, SLOT_ALIGN))
            got = jnp.dot(jnp.concatenate(pieces, axis=0), h_tile, preferred_element_type=F32)
            for e in range(EG):
                xs_ref[0, e, pl.ds(offs[e], win), :] += got[e * win:(e + 1) * win].astype(xs_ref.dtype)
            return carry

        lax.fori_loop(0, n_win, window, 0)


def _gather(lo, pos, h, cap):
    B, E, N = pos.shape
    D = h.shape[2]
    EG = 8
    tn, win = min(PREFIX_CHUNK, N), min(SLOT_WINDOW, cap)
    return pl.pallas_call(
        functools.partial(_gather_kernel, cap=cap, win=win, tn=tn),
        grid_spec=pltpu.PrefetchScalarGridSpec(
            num_scalar_prefetch=1, grid=(B, E // EG),
            in_specs=[pl.BlockSpec((1, EG, N), lambda b, g, lo: (b, g, 0)),
                      pl.BlockSpec((1, N, D), lambda b, g, lo: (b, 0, 0))],
            out_specs=pl.BlockSpec((1, EG, cap, D), lambda b, g, lo: (b, g, 0, 0))),
        out_shape=jax.ShapeDtypeStruct((B, E, cap, D), BF16),
        compiler_params=_params(("arbitrary", "arbitrary")),
        name="moe_gather",
    )(lo, pos, h)


def _ffn_kernel(xs_ref, wg_ref, wu_ref, wd_ref, y_ref):
    xs = xs_ref[0, 0]
    hid = _silu(jnp.dot(xs, wg_ref[0], preferred_element_type=F32)) * jnp.dot(xs, wu_ref[0], preferred_element_type=F32)
    y_ref[0, 0] = jnp.dot(hid.astype(BF16), wd_ref[0], preferred_element_type=F32).astype(y_ref.dtype)


def _ffn(xs, wg, wu, wd):
    B, E, cap, D = xs.shape
    F = wg.shape[2]
    tok = pl.BlockSpec((1, 1, cap, D), lambda e, b: (b, e, 0, 0))
    return pl.pallas_call(
        _ffn_kernel,
        grid=(E, B),
        in_specs=[tok, pl.BlockSpec((1, D, F), lambda e, b: (e, 0, 0)),
                  pl.BlockSpec((1, D, F), lambda e, b: (e, 0, 0)), pl.BlockSpec((1, F, D), lambda e, b: (e, 0, 0))],
        out_specs=tok,
        out_shape=jax.ShapeDtypeStruct((B, E, cap, D), BF16),
        compiler_params=_params(("arbitrary", "arbitrary")),
        name="moe_ffn",
    )(xs, wg, wu, wd)


def _combine_kernel(lo_ref, pos_ref, aff_ref, y_ref, x_ref, m_ref, o_ref, ystack_ref, acc_ref, *, cap, win, gate_row):
    E = y_ref.shape[1]
    tn = pos_ref.shape[1]
    per_lane_block = LANES // win
    n_bounds = pl.num_programs(1) + 1
    j = pl.program_id(1)
    starts, n_win = _window_count(lo_ref, pl.program_id(0) * E, range(E), j, n_bounds, win)
    pos_t, aff_t = pos_ref[0], aff_ref[0]
    lane = lax.broadcasted_iota(jnp.int32, (tn, LANES), 1)
    acc_ref[...] = jnp.zeros(acc_ref.shape, F32)

    def window(w, carry):
        blocks_hi, blocks_lo = [], []
        for blk in range(E // per_lane_block):
            gates = jnp.zeros((tn, LANES), F32)
            for q in range(per_lane_block):
                e = blk * per_lane_block + q
                first = starts[e] + w * win
                off = jnp.minimum(first, cap - win)
                ystack_ref[e * win:(e + 1) * win, :] = y_ref[0, e, pl.ds(pl.multiple_of(off, SLOT_ALIGN), win), :]
                slot = pos_t[:, e:e + 1]
                hit = (slot == (lane + (off - q * win)).astype(F32)) & (slot >= first.astype(F32))
                if per_lane_block > 1:
                    hit = hit & (lane >= q * win) & (lane < (q + 1) * win)
                gates = jnp.where(hit, aff_t[:, e:e + 1], gates)
            hi = gates.astype(BF16)
            blocks_hi.append(hi)
            blocks_lo.append((gates - hi.astype(F32)).astype(BF16))
        ys = ystack_ref[...]
        acc_ref[...] += (jnp.dot(jnp.concatenate(blocks_hi, axis=1), ys, preferred_element_type=F32)
                         + jnp.dot(jnp.concatenate(blocks_lo, axis=1), ys, preferred_element_type=F32))
        return carry

    lax.fori_loop(0, n_win, window, 0)
    o_ref[0] = x_ref[0] + m_ref[0, gate_row:gate_row + 1, :] * acc_ref[...]


def _combine(lo, pos, aff, y, x, mod, gate_row):
    B, E, cap, D = y.shape
    N = x.shape[1]
    tn, win = min(PREFIX_CHUNK, N), min(SLOT_WINDOW, cap)
    per_sample = mod.shape[0] > 1
    tok = pl.BlockSpec((1, tn, E), lambda b, t, lo: (b, t, 0))
    return pl.pallas_call(
        functools.partial(_combine_kernel, cap=cap, win=win, gate_row=gate_row),
        grid_spec=pltpu.PrefetchScalarGridSpec(
            num_scalar_prefetch=1, grid=(B, N // tn),
            in_specs=[tok, tok,
                      pl.BlockSpec((1, E, cap, D), lambda b, t, lo: (b, 0, 0, 0)),
                      pl.BlockSpec((1, tn, D), lambda b, t, lo: (b, t, 0)),
                      pl.BlockSpec((1, N_MOD, D), (lambda b, t, lo: (b, 0, 0)) if per_sample else (lambda b, t, lo: (0, 0, 0)))],
            out_specs=pl.BlockSpec((1, tn, D), lambda b, t, lo: (b, t, 0)),
            scratch_shapes=[pltpu.VMEM((E * win, D), BF16), pltpu.VMEM((tn, D), F32)]),
        out_shape=jax.ShapeDtypeStruct((B, N, D), F32),
        compiler_params=_params(("arbitrary", "arbitrary"), vmem_mib=56),
        name="moe_combine",
    )(lo, jnp.transpose(pos, (0, 2, 1)), jnp.transpose(aff, (0, 2, 1)), y, x, mod)


def _moe(x, g, mod, router_wt, wg, wu, wd):
    N = x.shape[1]
    cap = EC_CAPACITY_FACTOR * N // N_EXPERTS
    h, logits_t = _norm(x, g, mod, shift_row=3, scale_row=4, router_wt=router_wt)
    pos, aff, lo = _route(logits_t, cap)
    y = _ffn(_gather(lo, pos, h, cap), wg, wu, wd)
    return _combine(lo, pos, aff, y, x, mod, gate_row=5)


def _rope_tables(n_tok):
    t = jnp.arange(n_tok)
    rows = (t // GRID_W).astype(F32)
    cols = (t % GRID_W).astype(F32)
    n_freq = GLA_DK // 4
    inv = ROPE_THETA ** (-jnp.arange(n_freq, dtype=F32) / n_freq)
    ang = jnp.concatenate([rows[:, None] * inv, cols[:, None] * inv], axis=-1)
    cos = jnp.repeat(jnp.cos(ang), 2, axis=-1)
    sin = jnp.repeat(jnp.sin(ang), 2, axis=-1) * jnp.tile(jnp.asarray([-1.0, 1.0], F32), GLA_DK // 2)
    return cos, sin


def _rec_mixer(h_ctx, h_lat, w_in, alpha_up, alpha_b, gla_g, conv_w, conv_b, w_a, b_a, w_i, b_i, lam, rope_tabs):
    za0 = 2 * GLA_KEY_WIDTH + 2 * GLA_VAL_WIDTH
    za1 = za0 + 2 * GLA_RANK
    w_main = jnp.concatenate([w_in[:, :za0], w_in[:, za1:]], axis=1).astype(BF16)
    w_za = jnp.pad(w_in[:, za0:za1], ((0, 0), (0, LANES - 2 * GLA_RANK))).astype(BF16)
    aup = jnp.zeros((2, LANES, GLA_KEY_WIDTH), F32)
    aup = aup.at[0, :GLA_RANK].set(alpha_up[0]).at[1, GLA_RANK:2 * GLA_RANK].set(alpha_up[1])
    ab = alpha_b.reshape(2, 1, GLA_KEY_WIDTH)
    gg = gla_g.reshape(1, GLA_DV)
    wa, wi = w_a.astype(BF16), w_i.astype(BF16)
    outs = []
    gla_state, lru_state = None, None
    for h, tabs in ((h_ctx, None), (h_lat, rope_tabs)):
        B = h.shape[0]
        p_main = _proj(h, w_main, F32, tn=1280)
        p_za = _proj(h, w_za, F32, tn=LANES)
        y_gla, gla_state = _gla(p_main, p_za, aup, ab, gg, tabs, gla_state)
        if lru_state is None:
            lru_state = (jnp.zeros((B, 1, LRU_WIDTH), F32),) * 2
        y_lru, lru_state = _lru(p_main, conv_w, conv_b, wa, b_a, wi, b_i, lam, lru_state)
        outs.append([y_gla, y_lru])
    return outs


def kernel(x, c, ctx, c_ctx, w_mod, b_mod, norm_mix_g, norm_ffn_g, w_out, router_w, exp_w_gate, exp_w_up, exp_w_down, rec_w_in, gla_alpha_up, gla_alpha_b, gla_norm_g, lru_conv_w, lru_conv_b, lru_w_a, lru_b_a, lru_w_i, lru_b_i, lru_lambda, na_w_qkv, na_rpb, norm_f_g):
    B, T, D = x.shape
    pad_rows = (-(B + 1)) % 8
    cc = jnp.concatenate([c, c_ctx[None, :], jnp.zeros((pad_rows, D), F32)], axis=0)
    m_all = _modulation(cc, w_mod, b_mod)
    rope_tabs = _rope_tables(T)
    for l in range(DEPTH):
        last = l == DEPTH - 1
        i = l // 2
        m_lat = m_all[l, :B].reshape(B, N_MOD, D)
        m_ctx = m_all[l, B:B + 1].reshape(1, N_MOD, D)
        h_lat = _norm(x, norm_mix_g[l], m_lat)
        h_ctx = _norm(ctx, norm_mix_g[l], m_ctx)
        w_o = w_out[l].astype(BF16)
        if l % 2 == 0:
            ys_ctx, ys_lat = _rec_mixer(h_ctx, h_lat, rec_w_in[i], gla_alpha_up[i], gla_alpha_b[i], gla_norm_g[i],
                                        lru_conv_w[i], lru_conv_b[i], lru_w_a[i], lru_b_a[i], lru_w_i[i], lru_b_i[i],
                                        lru_lambda[i], rope_tabs)
            ws = [w_o[:GLA_VAL_WIDTH], w_o[GLA_VAL_WIDTH:]]
        else:
            w_qkv = na_w_qkv[i].astype(BF16)
            qkv_lat = _proj(h_lat, w_qkv, BF16, tn=1536)
            qkv_ctx = _proj(h_ctx, w_qkv, BF16, tn=1536)
            y_ctx, y_lat = _na(qkv_lat, qkv_ctx, _na_bias_tables(na_rpb[i], T // GRID_W), not last)
            ys_ctx, ys_lat, ws = [y_ctx], [y_lat], [w_o]
        moe_w = (router_w[l].T, exp_w_gate[l].astype(BF16), exp_w_up[l].astype(BF16), exp_w_down[l].astype(BF16))
        x = _oproj(ys_lat, ws, x, m_lat, gate_row=2)
        x = _moe(x, norm_ffn_g[l], m_lat, *moe_w)
        if not last:
            ctx = _oproj(ys_ctx, ws, ctx, m_ctx, gate_row=2)
            ctx = _moe(ctx, norm_ffn_g[l], m_ctx, *moe_w)
    return _norm(x, norm_f_g, out_dtype=F32)
```

```python
import functools

import numpy as np
import jax
import jax.numpy as jnp
from jax import lax
from jax.experimental import pallas as pl
from jax.experimental.pallas import tpu as pltpu

D_MODEL = 2048
DEPTH = 4
GRID_W = 64
RMS_EPS = 1e-6
N_MOD = 6
ROPE_THETA = 10000.0
GLA_HEADS = 4
GLA_DK = D_MODEL // 16
GLA_DV = D_MODEL // 8
GLA_KEY_WIDTH = GLA_HEADS * GLA_DK
GLA_VAL_WIDTH = GLA_HEADS * GLA_DV
GLA_RANK = 16
GLA_TAU = 16.0
GLA_CHUNK = 64
LRU_WIDTH = D_MODEL // 2
LRU_BLOCKS = 8
LRU_BLOCK_DIM = LRU_WIDTH // LRU_BLOCKS
LRU_CONV = 4
LRU_C = 8.0
NA_HEADS = 16
NA_HEAD_DIM = D_MODEL // NA_HEADS
NA_KH = 8
NA_KW = 16
N_EXPERTS = 16
EC_CAPACITY_FACTOR = 2
EXPERT_FF = D_MODEL // 2

F32 = jnp.float32
BF16 = jnp.bfloat16
HIGHEST = lax.Precision.HIGHEST
LANES = 128
NEG_BIG = -1e30
LOG2_E = float(np.log2(np.e))
NA_QROWS = 4
NA_KROWS = NA_QROWS + NA_KH - 1
PREFIX_CHUNK = 256
SLOT_WINDOW = 64
LRU_SEGMENTS = 8
LRU_PITCH_PAD = 8
FFN_ROWS = 512
SLOT_ALIGN = 16

_NT = (((1,), (1,)), ((), ()))
_TN = (((0,), (0,)), ((), ()))


def _params(sem, vmem_mib=48):
    return pltpu.CompilerParams(dimension_semantics=sem, vmem_limit_bytes=vmem_mib * 2 ** 20)


def _sigmoid(z):
    return 1.0 / (1.0 + jnp.exp(-z))


def _silu(z):
    return z * _sigmoid(z)


def _mod_kernel(c_ref, w_ref, b_ref, o_ref):
    a = _silu(c_ref[...]).astype(BF16)
    o_ref[0] = jnp.dot(a, w_ref[0].astype(BF16), preferred_element_type=F32) + b_ref[0]


def _modulation(cc, w_mod, b_mod):
    R, D = cc.shape
    L, _, N = w_mod.shape
    tn = 1024
    return pl.pallas_call(
        _mod_kernel,
        grid=(L, N // tn),
        in_specs=[pl.BlockSpec((R, D), lambda l, j: (0, 0)),
                  pl.BlockSpec((1, D, tn), lambda l, j: (l, 0, j)),
                  pl.BlockSpec((1, 1, tn), lambda l, j: (l, 0, j))],
        out_specs=pl.BlockSpec((1, R, tn), lambda l, j: (l, 0, j)),
        out_shape=jax.ShapeDtypeStruct((L, R, N), F32),
        compiler_params=_params(("arbitrary", "arbitrary")),
        name="modulation",
    )(cc, w_mod, b_mod.reshape(L, 1, N))


def _norm_kernel(*refs, modulated, with_router, shift_row, scale_row):
    it = iter(refs)
    x_ref, g_ref = next(it), next(it)
    m_ref = next(it) if modulated else None
    wr_ref = next(it) if with_router else None
    h_ref = next(it)
    lg_ref = next(it) if with_router else None
    x = x_ref[0]
    y = x * lax.rsqrt(jnp.mean(x * x, axis=-1, keepdims=True) + RMS_EPS) * g_ref[...]
    if modulated:
        y = y * (1.0 + m_ref[0, scale_row:scale_row + 1, :]) + m_ref[0, shift_row:shift_row + 1, :]
    h_ref[0] = y.astype(h_ref.dtype)
    if with_router:
        lg_ref[0] = lax.dot_general(wr_ref[...], y, _NT, precision=HIGHEST, preferred_element_type=F32)


def _norm(x, g, mod=None, shift_row=0, scale_row=1, router_wt=None, out_dtype=BF16):
    B, T, D = x.shape
    tt = min(T, 512)
    modulated, with_router = mod is not None, router_wt is not None
    in_specs = [pl.BlockSpec((1, tt, D), lambda b, t: (b, t, 0)), pl.BlockSpec((1, D), lambda b, t: (0, 0))]
    args = [x, g.reshape(1, D)]
    if modulated:
        per_sample = mod.shape[0] > 1
        in_specs.append(pl.BlockSpec((1, N_MOD, D), (lambda b, t: (b, 0, 0)) if per_sample else (lambda b, t: (0, 0, 0))))
        args.append(mod)
    out_specs = [pl.BlockSpec((1, tt, D), lambda b, t: (b, t, 0))]
    out_shape = [jax.ShapeDtypeStruct((B, T, D), out_dtype)]
    if with_router:
        E = router_wt.shape[0]
        in_specs.append(pl.BlockSpec((E, D), lambda b, t: (0, 0)))
        args.append(router_wt)
        out_specs.append(pl.BlockSpec((1, E, tt), lambda b, t: (b, 0, t)))
        out_shape.append(jax.ShapeDtypeStruct((B, E, T), F32))
    res = pl.pallas_call(
        functools.partial(_norm_kernel, modulated=modulated, with_router=with_router,
                          shift_row=shift_row, scale_row=scale_row),
        grid=(B, T // tt), in_specs=in_specs, out_specs=out_specs, out_shape=out_shape,
        compiler_params=_params(("arbitrary", "arbitrary")),
        name="rmsnorm",
    )(*args)
    return res if with_router else res[0]


def _proj_kernel(a_ref, w_ref, o_ref):
    o_ref[...] = jnp.dot(a_ref[...], w_ref[...], preferred_element_type=F32).astype(o_ref.dtype)


def _proj(a, w, out_dtype, tn):
    B, T, K = a.shape
    N = w.shape[1]
    M = B * T
    tm = min(M, 512)
    out = pl.pallas_call(
        _proj_kernel,
        grid=(N // tn, M // tm),
        in_specs=[pl.BlockSpec((tm, K), lambda j, i: (i, 0)), pl.BlockSpec((K, tn), lambda j, i: (0, j))],
        out_specs=pl.BlockSpec((tm, tn), lambda j, i: (i, j)),
        out_shape=jax.ShapeDtypeStruct((M, N), out_dtype),
        compiler_params=_params(("arbitrary", "arbitrary")),
        name="proj",
    )(a.reshape(M, K), w)
    return out.reshape(B, T, N)


def _oproj_kernel(*refs, n_in, gate_row):
    a_refs, w_refs = refs[:n_in], refs[n_in:2 * n_in]
    x_ref, m_ref, o_ref = refs[2 * n_in:]
    acc = jnp.dot(a_refs[0][0], w_refs[0][...], preferred_element_type=F32)
    for a_ref, w_ref in zip(a_refs[1:], w_refs[1:]):
        acc = acc + jnp.dot(a_ref[0], w_ref[...], preferred_element_type=F32)
    o_ref[0] = x_ref[0] + m_ref[0, gate_row:gate_row + 1, :] * acc


def _oproj(ys, ws, x, mod, gate_row):
    B, T, N = x.shape
    tm, tn = min(T, 512), 1024
    per_sample = mod.shape[0] > 1
    in_specs = [pl.BlockSpec((1, tm, y.shape[2]), lambda j, b, t: (b, t, 0)) for y in ys]
    in_specs += [pl.BlockSpec((w.shape[0], tn), lambda j, b, t: (0, j)) for w in ws]
    in_specs += [pl.BlockSpec((1, tm, tn), lambda j, b, t: (b, t, j)),
                 pl.BlockSpec((1, N_MOD, tn), (lambda j, b, t: (b, 0, j)) if per_sample else (lambda j, b, t: (0, 0, j)))]
    return pl.pallas_call(
        functools.partial(_oproj_kernel, n_in=len(ys), gate_row=gate_row),
        grid=(N // tn, B, T // tm), in_specs=in_specs,
        out_specs=pl.BlockSpec((1, tm, tn), lambda j, b, t: (b, t, j)),
        out_shape=jax.ShapeDtypeStruct((B, T, N), F32),
        compiler_params=_params(("arbitrary", "arbitrary", "arbitrary")),
        name="oproj",
    )(*ys, *ws, x, mod)


def _log_sigmoid(z):
    return jnp.minimum(z, 0.0) - jnp.log(1.0 + jnp.exp(-jnp.abs(z)))


def _gla_kernel(*refs, n_chunks, use_rope, has_state):
    it = iter(refs)
    q_ref, k_ref, v_ref, g_ref, za_ref = (next(it) for _ in range(5))
    cos_ref, sin_ref = (next(it), next(it)) if use_rope else (None, None)
    aup_ref, ab_ref, gg_ref = next(it), next(it), next(it)
    s0_ref = next(it) if has_state else None
    y_ref, sfin_ref = next(it), next(it)
    la_scr, qk_scr, qd_scr, ki_scr, kv_scr, dec_scr, sall_scr, st_scr = (next(it) for _ in range(8))
    C, DK = GLA_CHUNK, GLA_DK

    z = jnp.dot(za_ref[0], aup_ref[...], precision=HIGHEST, preferred_element_type=F32) + ab_ref[...]
    la_scr[...] = _log_sigmoid(z) * (1.0 / GLA_TAU)

    q = q_ref[0] * (GLA_DK ** -0.5)
    k = k_ref[0]
    if use_rope:
        cos, sin = cos_ref[...], sin_ref[...]
        even = (lax.broadcasted_iota(jnp.int32, q.shape, 1) % 2) == 0

        def rope(t):
            swapped = jnp.where(even, pltpu.roll(t, LANES - 1, 1), pltpu.roll(t, 1, 1))
            return t * cos + swapped * sin
        q, k = rope(q), rope(k)
    qk_scr[0] = q
    qk_scr[1] = k

    row = lax.broadcasted_iota(jnp.int32, (C, C), 0)
    col = lax.broadcasted_iota(jnp.int32, (C, C), 1)
    keep_f, keep_b = row >= col, row <= col
    tri = jnp.where(keep_f, 1.0, 0.0).astype(F32)

    def prepare(c, carry):
        rows = pl.ds(pl.multiple_of(c * C, C), C)
        la = la_scr[rows, :]
        prefix = jnp.dot(tri, la, precision=HIGHEST, preferred_element_type=F32)
        tot = jnp.sum(la, axis=0, keepdims=True)
        b_f = prefix[:, :DK]
        b_b = tot[:, DK:] - prefix[:, DK:] + la[:, DK:]
        dec = jnp.exp(tot)
        qc, kc = qk_scr[0, rows, :], qk_scr[1, rows, :]
        qd_scr[rows, :] = jnp.concatenate([qc * jnp.exp(b_f), qc * jnp.exp(b_b)], axis=1).astype(BF16)
        ki_f, ki_b = kc * jnp.exp(-b_f), kc * jnp.exp(-b_b)
        ki_scr[0, rows, :] = ki_f.astype(BF16)
        ki_scr[1, rows, :] = ki_b.astype(BF16)
        ke = jnp.concatenate([ki_f * dec[:, :DK], ki_b * dec[:, DK:]], axis=1).astype(BF16)
        kv_scr[c] = lax.dot_general(v_ref[0, rows, :].astype(BF16), ke, _TN, preferred_element_type=F32)
        dec_scr[c] = jnp.broadcast_to(dec, dec_scr.shape[1:])
        return carry

    lax.fori_loop(0, n_chunks, prepare, 0, unroll=2)

    st_scr[...] = s0_ref[0, 0] if has_state else jnp.zeros(st_scr.shape, F32)

    def recur(i, carry):
        cf, cb = i, n_chunks - 1 - i
        st = st_scr[...]
        sall_scr[cf, :, :DK] = st[:, :DK].astype(BF16)
        sall_scr[cb, :, DK:] = st[:, DK:].astype(BF16)
        st_scr[:, :DK] = st[:, :DK] * dec_scr[cf, 0:1, :DK] + kv_scr[cf, :, :DK]
        st_scr[:, DK:] = st[:, DK:] * dec_scr[cb, 0:1, DK:] + kv_scr[cb, :, DK:]
        return carry

    lax.fori_loop(0, n_chunks, recur, 0)
    sfin_ref[0, 0] = st_scr[...]

    def emit(c, carry):
        rows = pl.ds(pl.multiple_of(c * C, C), C)
        qd = qd_scr[rows, :]
        s = (jnp.where(keep_f, lax.dot_general(qd[:, :DK], ki_scr[0, rows, :], _NT, preferred_element_type=F32), 0.0)
             + jnp.where(keep_b, lax.dot_general(qd[:, DK:], ki_scr[1, rows, :], _NT, preferred_element_type=F32), 0.0))
        o = (jnp.dot(s.astype(BF16), v_ref[0, rows, :].astype(BF16), preferred_element_type=F32)
             + lax.dot_general(qd, sall_scr[c], _NT, preferred_element_type=F32))
        o = o * lax.rsqrt(jnp.mean(o * o, axis=-1, keepdims=True) + RMS_EPS) * gg_ref[...]
        y_ref[0, rows, :] = (o * _silu(g_ref[0, rows, :])).astype(y_ref.dtype)
        return carry

    lax.fori_loop(0, n_chunks, emit, 0, unroll=2)


def _gla(p_main, p_za, aup, ab, gg, rope_tabs, state):
    B, T, _ = p_main.shape
    H, DK, DV = GLA_HEADS, GLA_DK, GLA_DV
    use_rope, has_state = rope_tabs is not None, state is not None
    n_chunks = T // GLA_CHUNK
    in_specs = [pl.BlockSpec((1, T, DK), lambda b, h: (b, 0, h)),
                pl.BlockSpec((1, T, DK), lambda b, h: (b, 0, H + h)),
                pl.BlockSpec((1, T, DV), lambda b, h: (b, 0, H + h)),
                pl.BlockSpec((1, T, DV), lambda b, h: (b, 0, 2 * H + h)),
                pl.BlockSpec((1, T, LANES), lambda b, h: (b, 0, 0))]
    args = [p_main, p_main, p_main, p_main, p_za]
    if use_rope:
        in_specs += [pl.BlockSpec((T, DK), lambda b, h: (0, 0))] * 2
        args += list(rope_tabs)
    in_specs += [pl.BlockSpec((LANES, 2 * DK), lambda b, h: (0, h)),
                 pl.BlockSpec((1, 2 * DK), lambda b, h: (0, h)),
                 pl.BlockSpec((1, DV), lambda b, h: (0, 0))]
    args += [aup, ab, gg]
    st_spec = pl.BlockSpec((1, 1, DV, 2 * DK), lambda b, h: (b, h, 0, 0))
    if has_state:
        in_specs.append(st_spec)
        args.append(state)
    return pl.pallas_call(
        functools.partial(_gla_kernel, n_chunks=n_chunks, use_rope=use_rope, has_state=has_state),
        grid=(B, H), in_specs=in_specs,
        out_specs=[pl.BlockSpec((1, T, DV), lambda b, h: (b, 0, h)), st_spec],
        out_shape=[jax.ShapeDtypeStruct((B, T, H * DV), BF16), jax.ShapeDtypeStruct((B, H, DV, 2 * DK), F32)],
        scratch_shapes=[pltpu.VMEM((T, 2 * DK), F32), pltpu.VMEM((2, T, DK), F32),
                        pltpu.VMEM((T, 2 * DK), BF16), pltpu.VMEM((2, T, DK), BF16),
                        pltpu.VMEM((n_chunks, DV, 2 * DK), F32), pltpu.VMEM((n_chunks, 8, 2 * DK), F32),
                        pltpu.VMEM((n_chunks, DV, 2 * DK), BF16), pltpu.VMEM((DV, 2 * DK), F32)],
        compiler_params=_params(("arbitrary", "arbitrary")),
        name="gla",
    )(*args)


def _gelu_tanh(z):
    return 0.5 * z * (1.0 + jnp.tanh(np.sqrt(2.0 / np.pi) * (z + 0.044715 * (z * z * z))))


def _lru_kernel(x_ref, gt_ref, cw_ref, cb_ref, wa_ref, ba_ref, wi_ref, bi_ref, lam_ref, h0f_ref, h0b_ref,
                y_ref, hf_ref, hb_ref, a_scr, bx_scr, hl_scr, ap_scr, *, T):
    L = T // LRU_SEGMENTS
    P = L + LRU_PITCH_PAD
    x = x_ref[0]
    row = lax.broadcasted_iota(jnp.int32, x.shape, 0)

    def from_earlier(v, d, fill):
        return jnp.where(row >= d, pltpu.roll(v, d, 0), fill)

    def from_later(v, d, fill):
        return jnp.where(row < T - d, pltpu.roll(v, T - d, 0), fill)

    xc = (cw_ref[0:1, :] * from_earlier(x, 1, 0.0) + cw_ref[1:2, :] * x
          + cw_ref[2:3, :] * from_later(x, 1, 0.0) + cw_ref[3:4, :] * from_later(x, 2, 0.0) + cb_ref[...])
    xcb = xc.astype(BF16)

    def sigmoid(z):
        return 0.5 * jnp.tanh(0.5 * z) + 0.5

    for d in range(2):
        r = sigmoid(jnp.dot(xcb, wa_ref[d, 0], preferred_element_type=F32) + ba_ref[d])
        gate_i = sigmoid(jnp.dot(xcb, wi_ref[d, 0], preferred_element_type=F32) + bi_ref[d])
        neg_lam = -lam_ref[d]
        softplus = jnp.maximum(neg_lam, 0.0) + jnp.log(1.0 + jnp.exp(-jnp.abs(neg_lam)))
        a = jnp.exp((-LRU_C) * r * softplus)
        bx = jnp.sqrt(1.0 - a * a) * (gate_i * xc)
        for s in range(LRU_SEGMENTS):
            a_scr[d, s * P:s * P + L, :] = a[s * L:(s + 1) * L]
            bx_scr[d, s * P:s * P + L, :] = bx[s * L:(s + 1) * L]

    def step(i, carry):
        h_f, p_f, h_b, p_b = carry
        at_f = pl.ds(i, LRU_SEGMENTS, stride=P)
        at_b = pl.ds(L - 1 - i, LRU_SEGMENTS, stride=P)
        a_f, a_b = a_scr[0, at_f, :], a_scr[1, at_b, :]
        h_f = a_f * h_f + bx_scr[0, at_f, :]
        h_b = a_b * h_b + bx_scr[1, at_b, :]
        p_f, p_b = a_f * p_f, a_b * p_b
        hl_scr[0, at_f, :] = h_f
        hl_scr[1, at_b, :] = h_b
        ap_scr[0, at_f, :] = p_f
        ap_scr[1, at_b, :] = p_b
        return h_f, p_f, h_b, p_b

    zeros, ones = jnp.zeros((LRU_SEGMENTS, x.shape[1]), F32), jnp.ones((LRU_SEGMENTS, x.shape[1]), F32)
    h_f, p_f, h_b, p_b = lax.fori_loop(0, L, step, (zeros, ones, zeros, ones), unroll=8)

    enter_f, enter_b = [h0f_ref[0]], [h0b_ref[0]]
    for s in range(LRU_SEGMENTS):
        enter_f.append(h_f[s:s + 1] + p_f[s:s + 1] * enter_f[-1])
        t = LRU_SEGMENTS - 1 - s
        enter_b.append(h_b[t:t + 1] + p_b[t:t + 1] * enter_b[-1])
    hf_ref[0] = enter_f[-1]
    hb_ref[0] = enter_b[-1]
    for s in range(LRU_SEGMENTS):
        seg = slice(s * P, s * P + L)
        h = (hl_scr[0, seg, :] + ap_scr[0, seg, :] * enter_f[s]
             + hl_scr[1, seg, :] + ap_scr[1, seg, :] * enter_b[LRU_SEGMENTS - 1 - s])
        y_ref[0, s * L:(s + 1) * L, :] = (h * _gelu_tanh(gt_ref[0, s * L:(s + 1) * L, :])).astype(y_ref.dtype)


def _lru(p_main, cw, cb, wa, ba, wi, bi, lam, states):
    B, T, _ = p_main.shape
    W, NB, BD = LRU_WIDTH, LRU_BLOCKS, LRU_BLOCK_DIM
    x_blk0 = (2 * GLA_KEY_WIDTH + 2 * GLA_VAL_WIDTH) // BD
    vec = lambda n: pl.BlockSpec((n, 1, BD), lambda b, j: (0, 0, j))
    mat = pl.BlockSpec((2, 1, BD, BD), lambda b, j: (0, j, 0, 0))
    st_spec = pl.BlockSpec((1, 1, BD), lambda b, j: (b, 0, j))
    st_shape = jax.ShapeDtypeStruct((B, 1, W), F32)
    y, hf, hb = pl.pallas_call(
        functools.partial(_lru_kernel, T=T),
        grid=(B, NB),
        in_specs=[pl.BlockSpec((1, T, BD), lambda b, j: (b, 0, x_blk0 + j)),
                  pl.BlockSpec((1, T, BD), lambda b, j: (b, 0, x_blk0 + NB + j)),
                  pl.BlockSpec((LRU_CONV, BD), lambda b, j: (0, j)),
                  pl.BlockSpec((1, BD), lambda b, j: (0, j)),
                  mat, vec(2), mat, vec(2), vec(2), st_spec, st_spec],
        out_specs=[pl.BlockSpec((1, T, BD), lambda b, j: (b, 0, j)), st_spec, st_spec],
        out_shape=[jax.ShapeDtypeStruct((B, T, W), BF16), st_shape, st_shape],
        scratch_shapes=[pltpu.VMEM((2, LRU_SEGMENTS * (T // LRU_SEGMENTS + LRU_PITCH_PAD), BD), F32)] * 4,
        compiler_params=_params(("arbitrary", "arbitrary")),
        name="rglru",
    )(p_main, p_main, cw, cb.reshape(1, W), wa, ba.reshape(2, 1, W), wi, bi.reshape(2, 1, W),
      lam.reshape(2, 1, W), *states)
    return y, (hf, hb)


def _na_tile(i, n_tiles, rows):
    r0 = i * NA_QROWS
    ks = min(max(r0 - NA_KH // 2, 0), rows - NA_KROWS)
    cls = 0 if i == 0 else (2 if i == n_tiles - 1 else 1)
    return r0, ks, cls


def _na_bias_tables(rpb, rows):
    H = rpb.shape[0]
    qc = np.arange(GRID_W)[:, None]
    kc = np.arange(GRID_W)[None, :]
    col_start = np.clip(qc - NA_KW // 2, 0, GRID_W - NA_KW)
    col_valid = (kc >= col_start) & (kc < col_start + NA_KW)
    dc = np.clip(kc - qc, -(NA_KW - 1), NA_KW - 1) + NA_KW - 1
    pick = (dc[None] == np.arange(2 * NA_KW - 1)[:, None, None]).astype(np.float32)
    band = jnp.einsum('hrd,dqk->hrqk', rpb.astype(F32), jnp.asarray(pick), precision=HIGHEST)
    band = jnp.where(col_valid, band, NEG_BIG)
    masked = jnp.full((H, GRID_W, GRID_W), NEG_BIG, F32)
    n_tiles = rows // NA_QROWS
    tabs = []
    for i in (0, 1, n_tiles - 1):
        r0, ks, _ = _na_tile(i, n_tiles, rows)
        per_qrow = []
        for qr in range(r0, r0 + NA_QROWS):
            row_start = min(max(qr - NA_KH // 2, 0), rows - NA_KH)
            blocks = [band[:, kr - qr + NA_KH - 1] if row_start <= kr < row_start + NA_KH else masked
                      for kr in range(ks, ks + NA_KROWS)]
            per_qrow.append(jnp.stack(blocks, axis=2).reshape(H, GRID_W, NA_KROWS * GRID_W))
        tabs.append(jnp.concatenate(per_qrow, axis=1))
    return jnp.stack(tabs, axis=1) * LOG2_E


def _na_kernel(*refs, rows, need_ctx):
    q_ref, k_ref, v_ref, qc_ref, kc_ref, vc_ref, tab_ref = refs[:7]
    y_ref = refs[7]
    scale = NA_HEAD_DIM ** -0.5 * LOG2_E
    kc, vc = kc_ref[0], vc_ref[0]
    n_tiles = rows // NA_QROWS
    nq, nk = NA_QROWS * GRID_W, NA_KROWS * GRID_W
    for i in range(n_tiles):
        r0, ks, cls = _na_tile(i, n_tiles, rows)
        q = q_ref[0, r0 * GRID_W:r0 * GRID_W + nq, :]
        kt = k_ref[0, ks * GRID_W:ks * GRID_W + nk, :]
        vt = v_ref[0, ks * GRID_W:ks * GRID_W + nk, :]
        s_loc = lax.dot_general(q, kt, _NT, preferred_element_type=F32) * scale + tab_ref[0, cls]
        s_ctx = lax.dot_general(q, kc, _NT, preferred_element_type=F32) * scale
        m = jnp.maximum(jnp.max(s_loc, axis=-1, keepdims=True), jnp.max(s_ctx, axis=-1, keepdims=True))
        p_loc, p_ctx = jnp.exp2(s_loc - m), jnp.exp2(s_ctx - m)
        denom = jnp.sum(p_loc, axis=-1, keepdims=True) + jnp.sum(p_ctx, axis=-1, keepdims=True)
        o = (jnp.dot(p_loc.astype(BF16), vt, preferred_element_type=F32)
             + jnp.dot(p_ctx.astype(BF16), vc, preferred_element_type=F32))
        y_ref[0, r0 * GRID_W:r0 * GRID_W + nq, :] = (o / denom).astype(y_ref.dtype)
    if need_ctx:
        yc_ref = refs[8]
        s = lax.dot_general(qc_ref[0], kc, _NT, preferred_element_type=F32) * scale
        p = jnp.exp2(s - jnp.max(s, axis=-1, keepdims=True))
        o = jnp.dot(p.astype(BF16), vc, preferred_element_type=F32) / jnp.sum(p, axis=-1, keepdims=True)
        yc_ref[0] = o.astype(yc_ref.dtype)


def _na(qkv_lat, qkv_ctx, tabs, need_ctx):
    B, T, _ = qkv_lat.shape
    Tc = qkv_ctx.shape[1]
    H, HD = NA_HEADS, NA_HEAD_DIM
    blk = lambda t, off: pl.BlockSpec((1, t, HD), lambda h, b: (b, 0, off + h))
    out_specs = [pl.BlockSpec((1, T, HD), lambda h, b: (b, 0, h))]
    out_shape = [jax.ShapeDtypeStruct((B, T, H * HD), BF16)]
    if need_ctx:
        out_specs.append(pl.BlockSpec((1, Tc, HD), lambda h, b: (b, 0, h)))
        out_shape.append(jax.ShapeDtypeStruct((B, Tc, H * HD), BF16))
    res = pl.pallas_call(
        functools.partial(_na_kernel, rows=T // GRID_W, need_ctx=need_ctx),
        grid=(H, B),
        in_specs=[blk(T, 0), blk(T, H), blk(T, 2 * H), blk(Tc, 0), blk(Tc, H), blk(Tc, 2 * H),
                  pl.BlockSpec((1,) + tabs.shape[1:], lambda h, b: (h, 0, 0, 0))],
        out_specs=out_specs, out_shape=out_shape,
        compiler_params=_params(("arbitrary", "arbitrary")),
        name="natten",
    )(qkv_lat, qkv_lat, qkv_lat, qkv_ctx, qkv_ctx, qkv_ctx, tabs)
    return (res[1] if need_ctx else None), res[0]


def _route_kernel(lg_ref, pos_ref, aff_ref, lo_ref, *, cap):
    lg = lg_ref[0]
    E, N = lg.shape
    ex = jnp.exp(lg - jnp.max(lg, axis=0, keepdims=True))
    aff = ex / jnp.sum(ex, axis=0, keepdims=True)
    aff_ref[0] = aff
    bits = pltpu.bitcast(aff, jnp.int32)
    thr = jnp.zeros((E, 1), jnp.int32)
    for bit in range(29, -1, -1):
        cand = thr | (1 << bit)
        cnt = jnp.sum(jnp.where(bits >= cand, 1.0, 0.0), axis=1, keepdims=True)
        thr = jnp.where(cnt >= cap, cand, thr)
    above = bits > thr
    tied = bits == thr
    need = cap - jnp.sum(jnp.where(above, 1.0, 0.0), axis=1, keepdims=True)
    pc = min(PREFIX_CHUNK, N)
    before = jnp.where(lax.broadcasted_iota(jnp.int32, (pc, pc), 0) < lax.broadcasted_iota(jnp.int32, (pc, pc), 1),
                       1.0, 0.0).astype(BF16)
    tied_seen = jnp.zeros((E, 1), F32)
    sel_seen = jnp.zeros((E, 1), F32)
    tile_lane = lax.broadcasted_iota(jnp.int32, (E, LANES), 1)
    lo = jnp.where(tile_lane == N // pc, float(cap), 0.0)
    for c in range(N // pc):
        sl = slice(c * pc, (c + 1) * pc)
        lo = jnp.where(tile_lane == c, sel_seen, lo)
        tied_c = jnp.where(tied[:, sl], 1.0, 0.0)
        tied_before = jnp.dot(tied_c.astype(BF16), before, preferred_element_type=F32) + tied_seen
        sel_c = jnp.where(above[:, sl], 1.0, jnp.where(tied_before < need, tied_c, 0.0))
        sel_before = jnp.dot(sel_c.astype(BF16), before, preferred_element_type=F32) + sel_seen
        pos_ref[0, :, sl] = jnp.where(sel_c > 0.0, sel_before, -1.0)
        tied_seen = tied_seen + jnp.sum(tied_c, axis=1, keepdims=True)
        sel_seen = sel_seen + jnp.sum(sel_c, axis=1, keepdims=True)
    lo_ref[0] = lo.astype(jnp.int32)


def _route(logits_t, cap):
    B, E, N = logits_t.shape
    spec = pl.BlockSpec((1, E, N), lambda b: (b, 0, 0))
    lo_spec = pl.BlockSpec((1, E, LANES), lambda b: (b, 0, 0))
    pos, aff, lo = pl.pallas_call(
        functools.partial(_route_kernel, cap=cap),
        grid=(B,), in_specs=[spec], out_specs=[spec, spec, lo_spec],
        out_shape=[jax.ShapeDtypeStruct((B, E, N), F32)] * 2 + [jax.ShapeDtypeStruct((B, E, LANES), jnp.int32)],
        compiler_params=_params(("arbitrary",)),
        name="route",
    )(logits_t)
    n_bounds = N // min(PREFIX_CHUNK, N) + 1
    return pos, aff, lo[:, :, :n_bounds].reshape(-1)


def _window_count(lo_ref, base, experts, tile, n_bounds, win):
    starts, n_win = [], 0
    for e in experts:
        first = lo_ref[(base + e) * n_bounds + tile]
        end = lo_ref[(base + e) * n_bounds + tile + 1]
        start = (first // SLOT_ALIGN) * SLOT_ALIGN
        starts.append(start)
        n_win = jnp.maximum(n_win, (end - start + win - 1) // win)
    return starts, n_win


def _gather_kernel(lo_ref, pos_ref, h_ref, xs_ref, *, cap, win, tn):
    EG, N = pos_ref.shape[1], pos_ref.shape[2]
    n_bounds = N // tn + 1
    base = pl.program_id(0) * (EG * pl.num_programs(1)) + pl.program_id(1) * EG
    xs_ref[...] = jnp.zeros(xs_ref.shape, xs_ref.dtype)
    slot_iota = lax.broadcasted_iota(jnp.int32, (win, tn), 0)
    for j in range(N // tn):
        h_tile = h_ref[0, j * tn:(j + 1) * tn, :]
        starts, n_win = _window_count(lo_ref, base, range(EG), j, n_bounds, win)

        def window(w, carry):
            pieces, offs = [], []
            for e in range(EG):
                first = starts[e] + w * win
                off = jnp.minimum(first, cap - win)
                prow = pos_ref[0, e:e + 1, j * tn:(j + 1) * tn]
                hit = ((slot_iota + off).astype(F32) == prow) & (prow >= first.astype(F32))
                pieces.append(jnp.where(hit, 1.0, 0.0).astype(BF16))
                offs.append(pl.multiple_of(off, SLOT_ALIGN))
            got = jnp.dot(jnp.concatenate(pieces, axis=0), h_tile, preferred_element_type=F32)
            for e in range(EG):
                xs_ref[0, e, pl.ds(offs[e], win), :] += got[e * win:(e + 1) * win].astype(xs_ref.dtype)
            return carry

        lax.fori_loop(0, n_win, window, 0)


def _gather(lo, pos, h, cap):
    B, E, N = pos.shape
    D = h.shape[2]
    EG = 8
    tn, win = min(PREFIX_CHUNK, N), min(SLOT_WINDOW, cap)
    return pl.pallas_call(
        functools.partial(_gather_kernel, cap=cap, win=win, tn=tn),
        grid_spec=pltpu.PrefetchScalarGridSpec(
            num_scalar_prefetch=1, grid=(B, E // EG),
            in_specs=[pl.BlockSpec((1, EG, N), lambda b, g, lo: (b, g, 0)),
                      pl.BlockSpec((1, N, D), lambda b, g, lo: (b, 0, 0))],
            out_specs=pl.BlockSpec((1, EG, cap, D), lambda b, g, lo: (b, g, 0, 0))),
        out_shape=jax.ShapeDtypeStruct((B, E, cap, D), BF16),
        compiler_params=_params(("arbitrary", "arbitrary")),
        name="moe_gather",
    )(lo, pos, h)


def _ffn_kernel(xs_ref, wg_ref, wu_ref, wd_ref, y_ref):
    bb, _, cap, D = xs_ref.shape
    xs = xs_ref[:, 0].reshape(bb * cap, D)
    hid = _silu(jnp.dot(xs, wg_ref[0], preferred_element_type=F32)) * jnp.dot(xs, wu_ref[0], preferred_element_type=F32)
    y = jnp.dot(hid.astype(BF16), wd_ref[0], preferred_element_type=F32)
    y_ref[:, 0] = y.reshape(bb, cap, D).astype(y_ref.dtype)


def _ffn(xs, wg, wu, wd):
    B, E, cap, D = xs.shape
    F = wg.shape[2]
    bb = max(1, min(B, FFN_ROWS // cap))
    tok = pl.BlockSpec((bb, 1, cap, D), lambda e, b: (b, e, 0, 0))
    return pl.pallas_call(
        _ffn_kernel,
        grid=(E, B // bb),
        in_specs=[tok, pl.BlockSpec((1, D, F), lambda e, b: (e, 0, 0)),
                  pl.BlockSpec((1, D, F), lambda e, b: (e, 0, 0)), pl.BlockSpec((1, F, D), lambda e, b: (e, 0, 0))],
        out_specs=tok,
        out_shape=jax.ShapeDtypeStruct((B, E, cap, D), BF16),
        compiler_params=_params(("arbitrary", "arbitrary")),
        name="moe_ffn",
    )(xs, wg, wu, wd)


def _combine_kernel(lo_ref, pos_ref, aff_ref, y_ref, x_ref, m_ref, o_ref, ystack_ref, acc_ref, *, cap, win, gate_row):
    E = y_ref.shape[1]
    tn = pos_ref.shape[1]
    per_lane_block = LANES // win
    n_bounds = pl.num_programs(1) + 1
    j = pl.program_id(1)
    starts, n_win = _window_count(lo_ref, pl.program_id(0) * E, range(E), j, n_bounds, win)
    pos_t, aff_t = pos_ref[0], aff_ref[0]
    lane = lax.broadcasted_iota(jnp.int32, (tn, LANES), 1)
    acc_ref[...] = jnp.zeros(acc_ref.shape, F32)

    def window(w, carry):
        blocks_hi, blocks_lo = [], []
        for blk in range(E // per_lane_block):
            gates = jnp.zeros((tn, LANES), F32)
            for q in range(per_lane_block):
                e = blk * per_lane_block + q
                first = starts[e] + w * win
                off = jnp.minimum(first, cap - win)
                ystack_ref[e * win:(e + 1) * win, :] = y_ref[0, e, pl.ds(pl.multiple_of(off, SLOT_ALIGN), win), :]
                slot = pos_t[:, e:e + 1]
                hit = (slot == (lane + (off - q * win)).astype(F32)) & (slot >= first.astype(F32))
                if per_lane_block > 1:
                    hit = hit & (lane >= q * win) & (lane < (q + 1) * win)
                gates = jnp.where(hit, aff_t[:, e:e + 1], gates)
            hi = gates.astype(BF16)
            blocks_hi.append(hi)
            blocks_lo.append((gates - hi.astype(F32)).astype(BF16))
        ys = ystack_ref[...]
        acc_ref[...] += (jnp.dot(jnp.concatenate(blocks_hi, axis=1), ys, preferred_element_type=F32)
                         + jnp.dot(jnp.concatenate(blocks_lo, axis=1), ys, preferred_element_type=F32))
        return carry

    lax.fori_loop(0, n_win, window, 0)
    o_ref[0] = x_ref[0] + m_ref[0, gate_row:gate_row + 1, :] * acc_ref[...]


def _combine(lo, pos, aff, y, x, mod, gate_row):
    B, E, cap, D = y.shape
    N = x.shape[1]
    tn, win = min(PREFIX_CHUNK, N), min(SLOT_WINDOW, cap)
    per_sample = mod.shape[0] > 1
    tok = pl.BlockSpec((1, tn, E), lambda b, t, lo: (b, t, 0))
    return pl.pallas_call(
        functools.partial(_combine_kernel, cap=cap, win=win, gate_row=gate_row),
        grid_spec=pltpu.PrefetchScalarGridSpec(
            num_scalar_prefetch=1, grid=(B, N // tn),
            in_specs=[tok, tok,
                      pl.BlockSpec((1, E, cap, D), lambda b, t, lo: (b, 0, 0, 0)),
                      pl.BlockSpec((1, tn, D), lambda b, t, lo: (b, t, 0)),
                      pl.BlockSpec((1, N_MOD, D), (lambda b, t, lo: (b, 0, 0)) if per_sample else (lambda b, t, lo: (0, 0, 0)))],
            out_specs=pl.BlockSpec((1, tn, D), lambda b, t, lo: (b, t, 0)),
            scratch_shapes=[pltpu.VMEM((E * win, D), BF16), pltpu.VMEM((tn, D), F32)]),
        out_shape=jax.ShapeDtypeStruct((B, N, D), F32),
        compiler_params=_params(("arbitrary", "arbitrary"), vmem_mib=56),
        name="moe_combine",
    )(lo, jnp.transpose(pos, (0, 2, 1)), jnp.transpose(aff, (0, 2, 1)), y, x, mod)


def _moe(x, g, mod, router_wt, wg, wu, wd):
    N = x.shape[1]
    cap = EC_CAPACITY_FACTOR * N // N_EXPERTS
    h, logits_t = _norm(x, g, mod, shift_row=3, scale_row=4, router_wt=router_wt)
    pos, aff, lo = _route(logits_t, cap)
    y = _ffn(_gather(lo, pos, h, cap), wg, wu, wd)
    return _combine(lo, pos, aff, y, x, mod, gate_row=5)


def _rope_tables(n_tok):
    t = jnp.arange(n_tok)
    rows = (t // GRID_W).astype(F32)
    cols = (t % GRID_W).astype(F32)
    n_freq = GLA_DK // 4
    inv = ROPE_THETA ** (-jnp.arange(n_freq, dtype=F32) / n_freq)
    ang = jnp.concatenate([rows[:, None] * inv, cols[:, None] * inv], axis=-1)
    cos = jnp.repeat(jnp.cos(ang), 2, axis=-1)
    sin = jnp.repeat(jnp.sin(ang), 2, axis=-1) * jnp.tile(jnp.asarray([-1.0, 1.0], F32), GLA_DK // 2)
    return cos, sin


def _rec_mixer(h_ctx, h_lat, w_in, alpha_up, alpha_b, gla_g, conv_w, conv_b, w_a, b_a, w_i, b_i, lam, rope_tabs):
    za0 = 2 * GLA_KEY_WIDTH + 2 * GLA_VAL_WIDTH
    za1 = za0 + 2 * GLA_RANK
    w_main = jnp.concatenate([w_in[:, :za0], w_in[:, za1:]], axis=1).astype(BF16)
    w_za = jnp.pad(w_in[:, za0:za1], ((0, 0), (0, LANES - 2 * GLA_RANK))).astype(BF16)
    up = alpha_up.reshape(2, GLA_RANK, GLA_HEADS, 1, GLA_DK)
    zero = jnp.zeros_like(up[0])
    aup = jnp.concatenate([jnp.concatenate([up[0], zero], axis=2), jnp.concatenate([zero, up[1]], axis=2)], axis=0)
    aup = jnp.pad(aup.reshape(2 * GLA_RANK, 2 * GLA_KEY_WIDTH), ((0, LANES - 2 * GLA_RANK), (0, 0)))
    ab = jnp.transpose(alpha_b.reshape(2, GLA_HEADS, GLA_DK), (1, 0, 2)).reshape(1, 2 * GLA_KEY_WIDTH)
    gg = gla_g.reshape(1, GLA_DV)
    wa, wi = w_a.astype(BF16), w_i.astype(BF16)
    outs = []
    gla_state, lru_state = None, None
    for h, tabs in ((h_ctx, None), (h_lat, rope_tabs)):
        B = h.shape[0]
        p_main = _proj(h, w_main, F32, tn=1280)
        p_za = _proj(h, w_za, F32, tn=LANES)
        y_gla, gla_state = _gla(p_main, p_za, aup, ab, gg, tabs, gla_state)
        if lru_state is None:
            lru_state = (jnp.zeros((B, 1, LRU_WIDTH), F32),) * 2
        y_lru, lru_state = _lru(p_main, conv_w, conv_b, wa, b_a, wi, b_i, lam, lru_state)
        outs.append([y_gla, y_lru])
    return outs


def kernel(x, c, ctx, c_ctx, w_mod, b_mod, norm_mix_g, norm_ffn_g, w_out, router_w, exp_w_gate, exp_w_up, exp_w_down, rec_w_in, gla_alpha_up, gla_alpha_b, gla_norm_g, lru_conv_w, lru_conv_b, lru_w_a, lru_b_a, lru_w_i, lru_b_i, lru_lambda, na_w_qkv, na_rpb, norm_f_g):
    B, T, D = x.shape
    pad_rows = (-(B + 1)) % 8
    cc = jnp.concatenate([c, c_ctx[None, :], jnp.zeros((pad_rows, D), F32)], axis=0)
    m_all = _modulation(cc, w_mod, b_mod)
    rope_tabs = _rope_tables(T)
    for l in range(DEPTH):
        last = l == DEPTH - 1
        i = l // 2
        m_lat = m_all[l, :B].reshape(B, N_MOD, D)
        m_ctx = m_all[l, B:B + 1].reshape(1, N_MOD, D)
        h_lat = _norm(x, norm_mix_g[l], m_lat)
        h_ctx = _norm(ctx, norm_mix_g[l], m_ctx)
        w_o = w_out[l].astype(BF16)
        if l % 2 == 0:
            ys_ctx, ys_lat = _rec_mixer(h_ctx, h_lat, rec_w_in[i], gla_alpha_up[i], gla_alpha_b[i], gla_norm_g[i],
                                        lru_conv_w[i], lru_conv_b[i], lru_w_a[i], lru_b_a[i], lru_w_i[i], lru_b_i[i],
                                        lru_lambda[i], rope_tabs)
            ws = [w_o[:GLA_VAL_WIDTH], w_o[GLA_VAL_WIDTH:]]
        else:
            w_qkv = na_w_qkv[i].astype(BF16)
            qkv_lat = _proj(h_lat, w_qkv, BF16, tn=1536)
            qkv_ctx = _proj(h_ctx, w_qkv, BF16, tn=1536)
            y_ctx, y_lat = _na(qkv_lat, qkv_ctx, _na_bias_tables(na_rpb[i], T // GRID_W), not last)
            ys_ctx, ys_lat, ws = [y_ctx], [y_lat], [w_o]
        moe_w = (router_w[l].T, exp_w_gate[l].astype(BF16), exp_w_up[l].astype(BF16), exp_w_down[l].astype(BF16))
        x = _oproj(ys_lat, ws, x, m_lat, gate_row=2)
        x = _moe(x, norm_ffn_g[l], m_lat, *moe_w)
        if not last:
            ctx = _oproj(ys_ctx, ws, ctx, m_ctx, gate_row=2)
            ctx = _moe(ctx, norm_ffn_g[l], m_ctx, *moe_w)
    return _norm(x, norm_f_g, out_dtype=F32)
```

```python
import functools

import numpy as np
import jax
import jax.numpy as jnp
from jax import lax
from jax.experimental import pallas as pl
from jax.experimental.pallas import tpu as pltpu

D_MODEL = 2048
DEPTH = 4
GRID_W = 64
RMS_EPS = 1e-6
N_MOD = 6
ROPE_THETA = 10000.0
GLA_HEADS = 4
GLA_DK = D_MODEL // 16
GLA_DV = D_MODEL // 8
GLA_KEY_WIDTH = GLA_HEADS * GLA_DK
GLA_VAL_WIDTH = GLA_HEADS * GLA_DV
GLA_RANK = 16
GLA_TAU = 16.0
GLA_CHUNK = 64
LRU_WIDTH = D_MODEL // 2
LRU_BLOCKS = 8
LRU_BLOCK_DIM = LRU_WIDTH // LRU_BLOCKS
LRU_CONV = 4
LRU_C = 8.0
NA_HEADS = 16
NA_HEAD_DIM = D_MODEL // NA_HEADS
NA_KH = 8
NA_KW = 16
N_EXPERTS = 16
EC_CAPACITY_FACTOR = 2
EXPERT_FF = D_MODEL // 2

F32 = jnp.float32
BF16 = jnp.bfloat16
HIGHEST = lax.Precision.HIGHEST
LANES = 128
NEG_BIG = -1e30
LOG2_E = float(np.log2(np.e))
NA_QROWS = 4
NA_KROWS = NA_QROWS + NA_KH - 1
PREFIX_CHUNK = 256
SLOT_WINDOW = 64
GLA_PREP_ROWS = 512
LRU_SEGMENTS = 8
LRU_PITCH_PAD = 8
FFN_ROWS = 512
SLOT_ALIGN = 16

_NT = (((1,), (1,)), ((), ()))
_TN = (((0,), (0,)), ((), ()))


def _params(sem, vmem_mib=48):
    return pltpu.CompilerParams(dimension_semantics=sem, vmem_limit_bytes=vmem_mib * 2 ** 20)


def _sigmoid(z):
    return 1.0 / (1.0 + jnp.exp(-z))


def _silu(z):
    return z * _sigmoid(z)


def _mod_kernel(c_ref, w_ref, b_ref, o_ref):
    a = _silu(c_ref[...]).astype(BF16)
    o_ref[0] = jnp.dot(a, w_ref[0].astype(BF16), preferred_element_type=F32) + b_ref[0]


def _modulation(cc, w_mod, b_mod):
    R, D = cc.shape
    L, _, N = w_mod.shape
    tn = 1024
    return pl.pallas_call(
        _mod_kernel,
        grid=(L, N // tn),
        in_specs=[pl.BlockSpec((R, D), lambda l, j: (0, 0)),
                  pl.BlockSpec((1, D, tn), lambda l, j: (l, 0, j)),
                  pl.BlockSpec((1, 1, tn), lambda l, j: (l, 0, j))],
        out_specs=pl.BlockSpec((1, R, tn), lambda l, j: (l, 0, j)),
        out_shape=jax.ShapeDtypeStruct((L, R, N), F32),
        compiler_params=_params(("arbitrary", "arbitrary")),
        name="modulation",
    )(cc, w_mod, b_mod.reshape(L, 1, N))


def _norm_kernel(*refs, modulated, with_router, shift_row, scale_row):
    it = iter(refs)
    x_ref, g_ref = next(it), next(it)
    m_ref = next(it) if modulated else None
    wr_ref = next(it) if with_router else None
    h_ref = next(it)
    lg_ref = next(it) if with_router else None
    x = x_ref[0]
    y = x * lax.rsqrt(jnp.mean(x * x, axis=-1, keepdims=True) + RMS_EPS) * g_ref[...]
    if modulated:
        y = y * (1.0 + m_ref[0, scale_row:scale_row + 1, :]) + m_ref[0, shift_row:shift_row + 1, :]
    h_ref[0] = y.astype(h_ref.dtype)
    if with_router:
        lg_ref[0] = lax.dot_general(wr_ref[...], y, _NT, precision=HIGHEST, preferred_element_type=F32)


def _norm(x, g, mod=None, shift_row=0, scale_row=1, router_wt=None, out_dtype=BF16):
    B, T, D = x.shape
    tt = min(T, 512)
    modulated, with_router = mod is not None, router_wt is not None
    in_specs = [pl.BlockSpec((1, tt, D), lambda b, t: (b, t, 0)), pl.BlockSpec((1, D), lambda b, t: (0, 0))]
    args = [x, g.reshape(1, D)]
    if modulated:
        per_sample = mod.shape[0] > 1
        in_specs.append(pl.BlockSpec((1, N_MOD, D), (lambda b, t: (b, 0, 0)) if per_sample else (lambda b, t: (0, 0, 0))))
        args.append(mod)
    out_specs = [pl.BlockSpec((1, tt, D), lambda b, t: (b, t, 0))]
    out_shape = [jax.ShapeDtypeStruct((B, T, D), out_dtype)]
    if with_router:
        E = router_wt.shape[0]
        in_specs.append(pl.BlockSpec((E, D), lambda b, t: (0, 0)))
        args.append(router_wt)
        out_specs.append(pl.BlockSpec((1, E, tt), lambda b, t: (b, 0, t)))
        out_shape.append(jax.ShapeDtypeStruct((B, E, T), F32))
    res = pl.pallas_call(
        functools.partial(_norm_kernel, modulated=modulated, with_router=with_router,
                          shift_row=shift_row, scale_row=scale_row),
        grid=(B, T // tt), in_specs=in_specs, out_specs=out_specs, out_shape=out_shape,
        compiler_params=_params(("arbitrary", "arbitrary")),
        name="rmsnorm",
    )(*args)
    return res if with_router else res[0]


def _proj_kernel(a_ref, w_ref, o_ref):
    o_ref[...] = jnp.dot(a_ref[...], w_ref[...], preferred_element_type=F32).astype(o_ref.dtype)


def _proj(a, w, out_dtype, tn):
    B, T, K = a.shape
    N = w.shape[1]
    M = B * T
    tm = min(M, 512)
    out = pl.pallas_call(
        _proj_kernel,
        grid=(N // tn, M // tm),
        in_specs=[pl.BlockSpec((tm, K), lambda j, i: (i, 0)), pl.BlockSpec((K, tn), lambda j, i: (0, j))],
        out_specs=pl.BlockSpec((tm, tn), lambda j, i: (i, j)),
        out_shape=jax.ShapeDtypeStruct((M, N), out_dtype),
        compiler_params=_params(("arbitrary", "arbitrary")),
        name="proj",
    )(a.reshape(M, K), w)
    return out.reshape(B, T, N)


def _oproj_kernel(*refs, n_in, gate_row):
    a_refs, w_refs = refs[:n_in], refs[n_in:2 * n_in]
    x_ref, m_ref, o_ref = refs[2 * n_in:]
    acc = jnp.dot(a_refs[0][0], w_refs[0][...], preferred_element_type=F32)
    for a_ref, w_ref in zip(a_refs[1:], w_refs[1:]):
        acc = acc + jnp.dot(a_ref[0], w_ref[...], preferred_element_type=F32)
    o_ref[0] = x_ref[0] + m_ref[0, gate_row:gate_row + 1, :] * acc


def _oproj(ys, ws, x, mod, gate_row):
    B, T, N = x.shape
    tm, tn = min(T, 512), 1024
    per_sample = mod.shape[0] > 1
    in_specs = [pl.BlockSpec((1, tm, y.shape[2]), lambda j, b, t: (b, t, 0)) for y in ys]
    in_specs += [pl.BlockSpec((w.shape[0], tn), lambda j, b, t: (0, j)) for w in ws]
    in_specs += [pl.BlockSpec((1, tm, tn), lambda j, b, t: (b, t, j)),
                 pl.BlockSpec((1, N_MOD, tn), (lambda j, b, t: (b, 0, j)) if per_sample else (lambda j, b, t: (0, 0, j)))]
    return pl.pallas_call(
        functools.partial(_oproj_kernel, n_in=len(ys), gate_row=gate_row),
        grid=(N // tn, B, T // tm), in_specs=in_specs,
        out_specs=pl.BlockSpec((1, tm, tn), lambda j, b, t: (b, t, j)),
        out_shape=jax.ShapeDtypeStruct((B, T, N), F32),
        compiler_params=_params(("arbitrary", "arbitrary", "arbitrary")),
        name="oproj",
    )(*ys, *ws, x, mod)


def _log_sigmoid(z):
    return jnp.minimum(z, 0.0) - jnp.log(1.0 + jnp.exp(-jnp.abs(z)))


def _gla_kernel(*refs, n_chunks, use_rope, has_state):
    it = iter(refs)
    q_ref, k_ref, v_ref, g_ref, za_ref = (next(it) for _ in range(5))
    cos_ref, sin_ref = (next(it), next(it)) if use_rope else (None, None)
    aup_ref, ab_ref, gg_ref = next(it), next(it), next(it)
    s0_ref = next(it) if has_state else None
    y_ref, sfin_ref = next(it), next(it)
    qd_scr, ki_scr, ke_scr, kv_scr, dec_scr, sall_scr, st_scr = (next(it) for _ in range(7))
    C, DK = GLA_CHUNK, GLA_DK
    T = n_chunks * C
    RB = min(T, GLA_PREP_ROWS)
    cpb = RB // C

    def prepare(blk, carry):
        r0 = pl.multiple_of(blk * RB, RB)
        rows = pl.ds(r0, RB)
        z = jnp.dot(za_ref[0, rows, :].astype(BF16), aup_ref[...], preferred_element_type=F32) + ab_ref[...]
        la = _log_sigmoid(z) * (1.0 / GLA_TAU)
        in_chunk = lax.broadcasted_iota(jnp.int32, la.shape, 0) % C
        prefix = la
        step = 1
        while step < C:
            prefix = prefix + jnp.where(in_chunk >= step, pltpu.roll(prefix, step, 0), 0.0)
            step *= 2
        q = q_ref[0, rows, :] * (GLA_DK ** -0.5)
        k = k_ref[0, rows, :]
        if use_rope:
            cos, sin = cos_ref[rows, :], sin_ref[rows, :]
            even = (lax.broadcasted_iota(jnp.int32, q.shape, 1) % 2) == 0

            def rope(t):
                swapped = jnp.where(even, pltpu.roll(t, LANES - 1, 1), pltpu.roll(t, 1, 1))
                return t * cos + swapped * sin
            q, k = rope(q), rope(k)
        split = lambda t: t.reshape(cpb, C, t.shape[-1])
        la, prefix, q, k = split(la), split(prefix), split(q), split(k)
        tot = prefix[:, C - 1:C, :]
        b_f = prefix[..., :DK]
        b_b = tot[..., DK:] - prefix[..., DK:] + la[..., DK:]
        dec = jnp.exp(tot)
        c0 = pl.multiple_of(blk * cpb, cpb)
        qd_scr[pl.ds(c0, cpb)] = jnp.concatenate([q * jnp.exp(b_f), q * jnp.exp(b_b)], axis=2).astype(BF16)
        ki_f, ki_b = k * jnp.exp(-b_f), k * jnp.exp(-b_b)
        ki_scr[0, pl.ds(c0, cpb)] = ki_f.astype(BF16)
        ki_scr[1, pl.ds(c0, cpb)] = ki_b.astype(BF16)
        ke_scr[pl.ds(c0, cpb)] = jnp.concatenate([ki_f * dec[..., :DK], ki_b * dec[..., DK:]], axis=2).astype(BF16)
        dec_scr[pl.ds(c0, cpb)] = jnp.broadcast_to(dec, (cpb,) + dec_scr.shape[1:])
        return carry

    lax.fori_loop(0, T // RB, prepare, 0)

    def increment(c, carry):
        rows = pl.ds(pl.multiple_of(c * C, C), C)
        kv_scr[c] = lax.dot_general(v_ref[0, rows, :].astype(BF16), ke_scr[c], _TN, preferred_element_type=F32)
        return carry

    lax.fori_loop(0, n_chunks, increment, 0, unroll=4)

    row = lax.broadcasted_iota(jnp.int32, (C, C), 0)
    col = lax.broadcasted_iota(jnp.int32, (C, C), 1)
    keep_f, keep_b = row >= col, row <= col

    st_scr[...] = s0_ref[0, 0] if has_state else jnp.zeros(st_scr.shape, F32)

    def recur(i, carry):
        cf, cb = i, n_chunks - 1 - i
        st = st_scr[...]
        sall_scr[cf, :, :DK] = st[:, :DK].astype(BF16)
        sall_scr[cb, :, DK:] = st[:, DK:].astype(BF16)
        st_scr[:, :DK] = st[:, :DK] * dec_scr[cf, 0:1, :DK] + kv_scr[cf, :, :DK]
        st_scr[:, DK:] = st[:, DK:] * dec_scr[cb, 0:1, DK:] + kv_scr[cb, :, DK:]
        return carry

    lax.fori_loop(0, n_chunks, recur, 0)
    sfin_ref[0, 0] = st_scr[...]

    def emit(c, carry):
        rows = pl.ds(pl.multiple_of(c * C, C), C)
        qd = qd_scr[c]
        s = (jnp.where(keep_f, lax.dot_general(qd[:, :DK], ki_scr[0, c], _NT, preferred_element_type=F32), 0.0)
             + jnp.where(keep_b, lax.dot_general(qd[:, DK:], ki_scr[1, c], _NT, preferred_element_type=F32), 0.0))
        o = (jnp.dot(s.astype(BF16), v_ref[0, rows, :].astype(BF16), preferred_element_type=F32)
             + lax.dot_general(qd, sall_scr[c], _NT, preferred_element_type=F32))
        o = o * lax.rsqrt(jnp.mean(o * o, axis=-1, keepdims=True) + RMS_EPS) * gg_ref[...]
        y_ref[0, rows, :] = (o * _silu(g_ref[0, rows, :])).astype(y_ref.dtype)
        return carry

    lax.fori_loop(0, n_chunks, emit, 0, unroll=4)


def _gla(p_main, p_za, aup, ab, gg, rope_tabs, state):
    B, T, _ = p_main.shape
    H, DK, DV = GLA_HEADS, GLA_DK, GLA_DV
    use_rope, has_state = rope_tabs is not None, state is not None
    n_chunks = T // GLA_CHUNK
    in_specs = [pl.BlockSpec((1, T, DK), lambda b, h: (b, 0, h)),
                pl.BlockSpec((1, T, DK), lambda b, h: (b, 0, H + h)),
                pl.BlockSpec((1, T, DV), lambda b, h: (b, 0, H + h)),
                pl.BlockSpec((1, T, DV), lambda b, h: (b, 0, 2 * H + h)),
                pl.BlockSpec((1, T, LANES), lambda b, h: (b, 0, 0))]
    args = [p_main, p_main, p_main, p_main, p_za]
    if use_rope:
        in_specs += [pl.BlockSpec((T, DK), lambda b, h: (0, 0))] * 2
        args += list(rope_tabs)
    in_specs += [pl.BlockSpec((LANES, 2 * DK), lambda b, h: (0, h)),
                 pl.BlockSpec((1, 2 * DK), lambda b, h: (0, h)),
                 pl.BlockSpec((1, DV), lambda b, h: (0, 0))]
    args += [aup, ab, gg]
    st_spec = pl.BlockSpec((1, 1, DV, 2 * DK), lambda b, h: (b, h, 0, 0))
    if has_state:
        in_specs.append(st_spec)
        args.append(state)
    return pl.pallas_call(
        functools.partial(_gla_kernel, n_chunks=n_chunks, use_rope=use_rope, has_state=has_state),
        grid=(B, H), in_specs=in_specs,
        out_specs=[pl.BlockSpec((1, T, DV), lambda b, h: (b, 0, h)), st_spec],
        out_shape=[jax.ShapeDtypeStruct((B, T, H * DV), BF16), jax.ShapeDtypeStruct((B, H, DV, 2 * DK), F32)],
        scratch_shapes=[pltpu.VMEM((n_chunks, GLA_CHUNK, 2 * DK), BF16), pltpu.VMEM((2, n_chunks, GLA_CHUNK, DK), BF16),
                        pltpu.VMEM((n_chunks, GLA_CHUNK, 2 * DK), BF16),
                        pltpu.VMEM((n_chunks, DV, 2 * DK), F32), pltpu.VMEM((n_chunks, 8, 2 * DK), F32),
                        pltpu.VMEM((n_chunks, DV, 2 * DK), BF16), pltpu.VMEM((DV, 2 * DK), F32)],
        compiler_params=_params(("arbitrary", "arbitrary")),
        name="gla",
    )(*args)


def _gelu_tanh(z):
    return 0.5 * z * (1.0 + jnp.tanh(np.sqrt(2.0 / np.pi) * (z + 0.044715 * (z * z * z))))


def _lru_kernel(x_ref, gt_ref, cw_ref, cb_ref, wa_ref, ba_ref, wi_ref, bi_ref, lam_ref, h0f_ref, h0b_ref,
                y_ref, hf_ref, hb_ref, a_scr, bx_scr, hl_scr, ap_scr, *, T):
    L = T // LRU_SEGMENTS
    P = L + LRU_PITCH_PAD
    x = x_ref[0]
    row = lax.broadcasted_iota(jnp.int32, x.shape, 0)

    def from_earlier(v, d, fill):
        return jnp.where(row >= d, pltpu.roll(v, d, 0), fill)

    def from_later(v, d, fill):
        return jnp.where(row < T - d, pltpu.roll(v, T - d, 0), fill)

    xc = (cw_ref[0:1, :] * from_earlier(x, 1, 0.0) + cw_ref[1:2, :] * x
          + cw_ref[2:3, :] * from_later(x, 1, 0.0) + cw_ref[3:4, :] * from_later(x, 2, 0.0) + cb_ref[...])
    xcb = xc.astype(BF16)

    def sigmoid(z):
        return 0.5 * jnp.tanh(0.5 * z) + 0.5

    for d in range(2):
        r = sigmoid(jnp.dot(xcb, wa_ref[d, 0], preferred_element_type=F32) + ba_ref[d])
        gate_i = sigmoid(jnp.dot(xcb, wi_ref[d, 0], preferred_element_type=F32) + bi_ref[d])
        neg_lam = -lam_ref[d]
        softplus = jnp.maximum(neg_lam, 0.0) + jnp.log(1.0 + jnp.exp(-jnp.abs(neg_lam)))
        a = jnp.exp((-LRU_C) * r * softplus)
        bx = jnp.sqrt(1.0 - a * a) * (gate_i * xc)
        for s in range(LRU_SEGMENTS):
            a_scr[d, s * P:s * P + L, :] = a[s * L:(s + 1) * L]
            bx_scr[d, s * P:s * P + L, :] = bx[s * L:(s + 1) * L]

    def step(i, carry):
        h_f, p_f, h_b, p_b = carry
        at_f = pl.ds(i, LRU_SEGMENTS, stride=P)
        at_b = pl.ds(L - 1 - i, LRU_SEGMENTS, stride=P)
        a_f, a_b = a_scr[0, at_f, :], a_scr[1, at_b, :]
        h_f = a_f * h_f + bx_scr[0, at_f, :]
        h_b = a_b * h_b + bx_scr[1, at_b, :]
        p_f, p_b = a_f * p_f, a_b * p_b
        hl_scr[0, at_f, :] = h_f
        hl_scr[1, at_b, :] = h_b
        ap_scr[0, at_f, :] = p_f
        ap_scr[1, at_b, :] = p_b
        return h_f, p_f, h_b, p_b

    zeros, ones = jnp.zeros((LRU_SEGMENTS, x.shape[1]), F32), jnp.ones((LRU_SEGMENTS, x.shape[1]), F32)
    h_f, p_f, h_b, p_b = lax.fori_loop(0, L, step, (zeros, ones, zeros, ones), unroll=8)

    enter_f, enter_b = [h0f_ref[0]], [h0b_ref[0]]
    for s in range(LRU_SEGMENTS):
        enter_f.append(h_f[s:s + 1] + p_f[s:s + 1] * enter_f[-1])
        t = LRU_SEGMENTS - 1 - s
        enter_b.append(h_b[t:t + 1] + p_b[t:t + 1] * enter_b[-1])
    hf_ref[0] = enter_f[-1]
    hb_ref[0] = enter_b[-1]
    for s in range(LRU_SEGMENTS):
        seg = slice(s * P, s * P + L)
        h = (hl_scr[0, seg, :] + ap_scr[0, seg, :] * enter_f[s]
             + hl_scr[1, seg, :] + ap_scr[1, seg, :] * enter_b[LRU_SEGMENTS - 1 - s])
        y_ref[0, s * L:(s + 1) * L, :] = (h * _gelu_tanh(gt_ref[0, s * L:(s + 1) * L, :])).astype(y_ref.dtype)


def _lru(p_main, cw, cb, wa, ba, wi, bi, lam, states):
    B, T, _ = p_main.shape
    W, NB, BD = LRU_WIDTH, LRU_BLOCKS, LRU_BLOCK_DIM
    x_blk0 = (2 * GLA_KEY_WIDTH + 2 * GLA_VAL_WIDTH) // BD
    vec = lambda n: pl.BlockSpec((n, 1, BD), lambda b, j: (0, 0, j))
    mat = pl.BlockSpec((2, 1, BD, BD), lambda b, j: (0, j, 0, 0))
    st_spec = pl.BlockSpec((1, 1, BD), lambda b, j: (b, 0, j))
    st_shape = jax.ShapeDtypeStruct((B, 1, W), F32)
    y, hf, hb = pl.pallas_call(
        functools.partial(_lru_kernel, T=T),
        grid=(B, NB),
        in_specs=[pl.BlockSpec((1, T, BD), lambda b, j: (b, 0, x_blk0 + j)),
                  pl.BlockSpec((1, T, BD), lambda b, j: (b, 0, x_blk0 + NB + j)),
                  pl.BlockSpec((LRU_CONV, BD), lambda b, j: (0, j)),
                  pl.BlockSpec((1, BD), lambda b, j: (0, j)),
                  mat, vec(2), mat, vec(2), vec(2), st_spec, st_spec],
        out_specs=[pl.BlockSpec((1, T, BD), lambda b, j: (b, 0, j)), st_spec, st_spec],
        out_shape=[jax.ShapeDtypeStruct((B, T, W), BF16), st_shape, st_shape],
        scratch_shapes=[pltpu.VMEM((2, LRU_SEGMENTS * (T // LRU_SEGMENTS + LRU_PITCH_PAD), BD), F32)] * 4,
        compiler_params=_params(("arbitrary", "arbitrary")),
        name="rglru",
    )(p_main, p_main, cw, cb.reshape(1, W), wa, ba.reshape(2, 1, W), wi, bi.reshape(2, 1, W),
      lam.reshape(2, 1, W), *states)
    return y, (hf, hb)


def _na_tile(i, n_tiles, rows):
    r0 = i * NA_QROWS
    ks = min(max(r0 - NA_KH // 2, 0), rows - NA_KROWS)
    cls = 0 if i == 0 else (2 if i == n_tiles - 1 else 1)
    return r0, ks, cls


def _na_bias_tables(rpb, rows):
    H = rpb.shape[0]
    qc = np.arange(GRID_W)[:, None]
    kc = np.arange(GRID_W)[None, :]
    col_start = np.clip(qc - NA_KW // 2, 0, GRID_W - NA_KW)
    col_valid = (kc >= col_start) & (kc < col_start + NA_KW)
    dc = np.clip(kc - qc, -(NA_KW - 1), NA_KW - 1) + NA_KW - 1
    pick = (dc[None] == np.arange(2 * NA_KW - 1)[:, None, None]).astype(np.float32)
    band = jnp.einsum('hrd,dqk->hrqk', rpb.astype(F32), jnp.asarray(pick), precision=HIGHEST)
    band = jnp.where(col_valid, band, NEG_BIG)
    masked = jnp.full((H, GRID_W, GRID_W), NEG_BIG, F32)
    n_tiles = rows // NA_QROWS
    tabs = []
    for i in (0, 1, n_tiles - 1):
        r0, ks, _ = _na_tile(i, n_tiles, rows)
        per_qrow = []
        for qr in range(r0, r0 + NA_QROWS):
            row_start = min(max(qr - NA_KH // 2, 0), rows - NA_KH)
            blocks = [band[:, kr - qr + NA_KH - 1] if row_start <= kr < row_start + NA_KH else masked
                      for kr in range(ks, ks + NA_KROWS)]
            per_qrow.append(jnp.stack(blocks, axis=2).reshape(H, GRID_W, NA_KROWS * GRID_W))
        tabs.append(jnp.concatenate(per_qrow, axis=1))
    return jnp.stack(tabs, axis=1) * LOG2_E


def _na_kernel(*refs, rows, need_ctx):
    q_ref, k_ref, v_ref, qc_ref, kc_ref, vc_ref, tab_ref = refs[:7]
    y_ref = refs[7]
    scale = NA_HEAD_DIM ** -0.5 * LOG2_E
    kc, vc = kc_ref[0], vc_ref[0]
    n_tiles = rows // NA_QROWS
    nq, nk = NA_QROWS * GRID_W, NA_KROWS * GRID_W
    for i in range(n_tiles):
        r0, ks, cls = _na_tile(i, n_tiles, rows)
        q = q_ref[0, r0 * GRID_W:r0 * GRID_W + nq, :]
        kt = k_ref[0, ks * GRID_W:ks * GRID_W + nk, :]
        vt = v_ref[0, ks * GRID_W:ks * GRID_W + nk, :]
        s_loc = lax.dot_general(q, kt, _NT, preferred_element_type=F32) * scale + tab_ref[0, cls]
        s_ctx = lax.dot_general(q, kc, _NT, preferred_element_type=F32) * scale
        m = jnp.maximum(jnp.max(s_loc, axis=-1, keepdims=True), jnp.max(s_ctx, axis=-1, keepdims=True))
        p_loc, p_ctx = jnp.exp2(s_loc - m), jnp.exp2(s_ctx - m)
        denom = jnp.sum(p_loc, axis=-1, keepdims=True) + jnp.sum(p_ctx, axis=-1, keepdims=True)
        o = (jnp.dot(p_loc.astype(BF16), vt, preferred_element_type=F32)
             + jnp.dot(p_ctx.astype(BF16), vc, preferred_element_type=F32))
        y_ref[0, r0 * GRID_W:r0 * GRID_W + nq, :] = (o / denom).astype(y_ref.dtype)
    if need_ctx:
        yc_ref = refs[8]
        s = lax.dot_general(qc_ref[0], kc, _NT, preferred_element_type=F32) * scale
        p = jnp.exp2(s - jnp.max(s, axis=-1, keepdims=True))
        o = jnp.dot(p.astype(BF16), vc, preferred_element_type=F32) / jnp.sum(p, axis=-1, keepdims=True)
        yc_ref[0] = o.astype(yc_ref.dtype)


def _na(qkv_lat, qkv_ctx, tabs, need_ctx):
    B, T, _ = qkv_lat.shape
    Tc = qkv_ctx.shape[1]
    H, HD = NA_HEADS, NA_HEAD_DIM
    blk = lambda t, off: pl.BlockSpec((1, t, HD), lambda h, b: (b, 0, off + h))
    out_specs = [pl.BlockSpec((1, T, HD), lambda h, b: (b, 0, h))]
    out_shape = [jax.ShapeDtypeStruct((B, T, H * HD), BF16)]
    if need_ctx:
        out_specs.append(pl.BlockSpec((1, Tc, HD), lambda h, b: (b, 0, h)))
        out_shape.append(jax.ShapeDtypeStruct((B, Tc, H * HD), BF16))
    res = pl.pallas_call(
        functools.partial(_na_kernel, rows=T // GRID_W, need_ctx=need_ctx),
        grid=(H, B),
        in_specs=[blk(T, 0), blk(T, H), blk(T, 2 * H), blk(Tc, 0), blk(Tc, H), blk(Tc, 2 * H),
                  pl.BlockSpec((1,) + tabs.shape[1:], lambda h, b: (h, 0, 0, 0))],
        out_specs=out_specs, out_shape=out_shape,
        compiler_params=_params(("arbitrary", "arbitrary")),
        name="natten",
    )(qkv_lat, qkv_lat, qkv_lat, qkv_ctx, qkv_ctx, qkv_ctx, tabs)
    return (res[1] if need_ctx else None), res[0]


def _route_kernel(lg_ref, pos_ref, aff_ref, lo_ref, *, cap):
    lg = lg_ref[0]
    E, N = lg.shape
    ex = jnp.exp(lg - jnp.max(lg, axis=0, keepdims=True))
    aff = ex / jnp.sum(ex, axis=0, keepdims=True)
    aff_ref[0] = aff
    bits = pltpu.bitcast(aff, jnp.int32)
    thr = jnp.zeros((E, 1), jnp.int32)
    for bit in range(29, -1, -1):
        cand = thr | (1 << bit)
        cnt = jnp.sum(jnp.where(bits >= cand, 1.0, 0.0), axis=1, keepdims=True)
        thr = jnp.where(cnt >= cap, cand, thr)
    above = bits > thr
    tied = bits == thr
    need = cap - jnp.sum(jnp.where(above, 1.0, 0.0), axis=1, keepdims=True)
    pc = min(PREFIX_CHUNK, N)
    before = jnp.where(lax.broadcasted_iota(jnp.int32, (pc, pc), 0) < lax.broadcasted_iota(jnp.int32, (pc, pc), 1),
                       1.0, 0.0).astype(BF16)
    tied_seen = jnp.zeros((E, 1), F32)
    sel_seen = jnp.zeros((E, 1), F32)
    tile_lane = lax.broadcasted_iota(jnp.int32, (E, LANES), 1)
    lo = jnp.where(tile_lane == N // pc, float(cap), 0.0)
    for c in range(N // pc):
        sl = slice(c * pc, (c + 1) * pc)
        lo = jnp.where(tile_lane == c, sel_seen, lo)
        tied_c = jnp.where(tied[:, sl], 1.0, 0.0)
        tied_before = jnp.dot(tied_c.astype(BF16), before, preferred_element_type=F32) + tied_seen
        sel_c = jnp.where(above[:, sl], 1.0, jnp.where(tied_before < need, tied_c, 0.0))
        sel_before = jnp.dot(sel_c.astype(BF16), before, preferred_element_type=F32) + sel_seen
        pos_ref[0, :, sl] = jnp.where(sel_c > 0.0, sel_before, -1.0)
        tied_seen = tied_seen + jnp.sum(tied_c, axis=1, keepdims=True)
        sel_seen = sel_seen + jnp.sum(sel_c, axis=1, keepdims=True)
    lo_ref[0] = lo.astype(jnp.int32)


def _route(logits_t, cap):
    B, E, N = logits_t.shape
    spec = pl.BlockSpec((1, E, N), lambda b: (b, 0, 0))
    lo_spec = pl.BlockSpec((1, E, LANES), lambda b: (b, 0, 0))
    pos, aff, lo = pl.pallas_call(
        functools.partial(_route_kernel, cap=cap),
        grid=(B,), in_specs=[spec], out_specs=[spec, spec, lo_spec],
        out_shape=[jax.ShapeDtypeStruct((B, E, N), F32)] * 2 + [jax.ShapeDtypeStruct((B, E, LANES), jnp.int32)],
        compiler_params=_params(("arbitrary",)),
        name="route",
    )(logits_t)
    n_bounds = N // min(PREFIX_CHUNK, N) + 1
    return pos, aff, lo[:, :, :n_bounds].reshape(-1)


def _window_count(lo_ref, base, experts, tile, n_bounds, win):
    starts, n_win = [], 0
    for e in experts:
        first = lo_ref[(base + e) * n_bounds + tile]
        end = lo_ref[(base + e) * n_bounds + tile + 1]
        start = (first // SLOT_ALIGN) * SLOT_ALIGN
        starts.append(start)
        n_win = jnp.maximum(n_win, (end - start + win - 1) // win)
    return starts, n_win


def _gather_kernel(lo_ref, pos_ref, h_ref, xs_ref, *, cap, win, tn):
    EG, N = pos_ref.shape[1], pos_ref.shape[2]
    n_bounds = N // tn + 1
    base = pl.program_id(0) * (EG * pl.num_programs(1)) + pl.program_id(1) * EG
    xs_ref[...] = jnp.zeros(xs_ref.shape, xs_ref.dtype)
    slot_iota = lax.broadcasted_iota(jnp.int32, (win, tn), 0)
    for j in range(N // tn):
        h_tile = h_ref[0, j * tn:(j + 1) * tn, :]
        starts, n_win = _window_count(lo_ref, base, range(EG), j, n_bounds, win)

        def window(w, carry):
            pieces, offs = [], []
            for e in range(EG):
                first = starts[e] + w * win
                off = jnp.minimum(first, cap - win)
                prow = pos_ref[0, e:e + 1, j * tn:(j + 1) * tn]
                hit = ((slot_iota + off).astype(F32) == prow) & (prow >= first.astype(F32))
                pieces.append(jnp.where(hit, 1.0, 0.0).astype(BF16))
                offs.append(pl.multiple_of(off, SLOT_ALIGN))
            got = jnp.dot(jnp.concatenate(pieces, axis=0), h_tile, preferred_element_type=F32)
            for e in range(EG):
                xs_ref[0, e, pl.ds(offs[e], win), :] += got[e * win:(e + 1) * win].astype(xs_ref.dtype)
            return carry

        lax.fori_loop(0, n_win, window, 0)


def _gather(lo, pos, h, cap):
    B, E, N = pos.shape
    D = h.shape[2]
    EG = 8
    tn, win = min(PREFIX_CHUNK, N), min(SLOT_WINDOW, cap)
    return pl.pallas_call(
        functools.partial(_gather_kernel, cap=cap, win=win, tn=tn),
        grid_spec=pltpu.PrefetchScalarGridSpec(
            num_scalar_prefetch=1, grid=(B, E // EG),
            in_specs=[pl.BlockSpec((1, EG, N), lambda b, g, lo: (b, g, 0)),
                      pl.BlockSpec((1, N, D), lambda b, g, lo: (b, 0, 0))],
            out_specs=pl.BlockSpec((1, EG, cap, D), lambda b, g, lo: (b, g, 0, 0))),
        out_shape=jax.ShapeDtypeStruct((B, E, cap, D), BF16),
        compiler_params=_params(("arbitrary", "arbitrary")),
        name="moe_gather",
    )(lo, pos, h)


def _ffn_kernel(xs_ref, wg_ref, wu_ref, wd_ref, y_ref):
    bb, _, cap, D = xs_ref.shape
    xs = xs_ref[:, 0].reshape(bb * cap, D)
    hid = (_silu(jnp.dot(xs, wg_ref[0, 0], preferred_element_type=F32))
           * jnp.dot(xs, wu_ref[0, 0], preferred_element_type=F32))
    y = jnp.dot(hid.astype(BF16), wd_ref[0, 0], preferred_element_type=F32)
    y_ref[:, 0] = y.reshape(bb, cap, D).astype(y_ref.dtype)


def _ffn(xs, wg, wu, wd, layer):
    B, E, cap, D = xs.shape
    F = wg.shape[3]
    bb = max(1, min(B, FFN_ROWS // cap))
    tok = pl.BlockSpec((bb, 1, cap, D), lambda e, b: (b, e, 0, 0))
    return pl.pallas_call(
        _ffn_kernel,
        grid=(E, B // bb),
        in_specs=[tok, pl.BlockSpec((1, 1, D, F), lambda e, b: (layer, e, 0, 0)),
                  pl.BlockSpec((1, 1, D, F), lambda e, b: (layer, e, 0, 0)),
                  pl.BlockSpec((1, 1, F, D), lambda e, b: (layer, e, 0, 0))],
        out_specs=tok,
        out_shape=jax.ShapeDtypeStruct((B, E, cap, D), BF16),
        compiler_params=_params(("arbitrary", "arbitrary")),
        name="moe_ffn",
    )(xs, wg, wu, wd)


def _combine_kernel(lo_ref, pos_ref, aff_ref, y_ref, x_ref, m_ref, o_ref, ystack_ref, acc_ref, *, cap, win, gate_row):
    E = y_ref.shape[1]
    tn = pos_ref.shape[1]
    n_bounds = pl.num_programs(1) + 1
    j = pl.program_id(1)
    starts, n_win = _window_count(lo_ref, pl.program_id(0) * E, range(E), j, n_bounds, win)
    col = lax.broadcasted_iota(jnp.int32, (E, E * win), 1)
    spread = jnp.where(col // win == lax.broadcasted_iota(jnp.int32, (E, E * win), 0), 1.0, 0.0).astype(BF16)
    slot = jnp.dot(pos_ref[0].astype(BF16), spread, preferred_element_type=F32)
    gate = jnp.dot(aff_ref[0].astype(BF16), spread, preferred_element_type=F32)
    col1 = lax.broadcasted_iota(jnp.int32, (1, E * win), 1)
    col_expert, col_in_win = col1 // win, col1 % win
    acc_ref[...] = jnp.zeros(acc_ref.shape, F32)

    def window(w, carry):
        first_col = jnp.zeros((1, E * win), jnp.int32)
        want_col = jnp.zeros((1, E * win), jnp.int32)
        for e in range(E):
            first = starts[e] + w * win
            off = jnp.minimum(first, cap - win)
            ystack_ref[e * win:(e + 1) * win, :] = y_ref[0, e, pl.ds(pl.multiple_of(off, SLOT_ALIGN), win), :]
            first_col = jnp.where(col_expert == e, first, first_col)
            want_col = jnp.where(col_expert == e, off + col_in_win, want_col)
        hit = (slot == want_col.astype(F32)) & (slot >= first_col.astype(F32))
        acc_ref[...] += jnp.dot(jnp.where(hit, gate, 0.0).astype(BF16), ystack_ref[...], preferred_element_type=F32)
        return carry

    lax.fori_loop(0, n_win, window, 0)
    o_ref[0] = x_ref[0] + m_ref[0, gate_row:gate_row + 1, :] * acc_ref[...]


def _combine(lo, pos, aff, y, x, mod, gate_row):
    B, E, cap, D = y.shape
    N = x.shape[1]
    tn, win = min(PREFIX_CHUNK, N), min(SLOT_WINDOW, cap)
    per_sample = mod.shape[0] > 1
    tok = pl.BlockSpec((1, tn, E), lambda b, t, lo: (b, t, 0))
    return pl.pallas_call(
        functools.partial(_combine_kernel, cap=cap, win=win, gate_row=gate_row),
        grid_spec=pltpu.PrefetchScalarGridSpec(
            num_scalar_prefetch=1, grid=(B, N // tn),
            in_specs=[tok, tok,
                      pl.BlockSpec((1, E, cap, D), lambda b, t, lo: (b, 0, 0, 0)),
                      pl.BlockSpec((1, tn, D), lambda b, t, lo: (b, t, 0)),
                      pl.BlockSpec((1, N_MOD, D), (lambda b, t, lo: (b, 0, 0)) if per_sample else (lambda b, t, lo: (0, 0, 0)))],
            out_specs=pl.BlockSpec((1, tn, D), lambda b, t, lo: (b, t, 0)),
            scratch_shapes=[pltpu.VMEM((E * win, D), BF16), pltpu.VMEM((tn, D), F32)]),
        out_shape=jax.ShapeDtypeStruct((B, N, D), F32),
        compiler_params=_params(("arbitrary", "arbitrary"), vmem_mib=56),
        name="moe_combine",
    )(lo, jnp.transpose(pos, (0, 2, 1)), jnp.transpose(aff, (0, 2, 1)), y, x, mod)


def _moe(x, g, mod, router_wt, wg, wu, wd, layer):
    N = x.shape[1]
    cap = EC_CAPACITY_FACTOR * N // N_EXPERTS
    assert cap <= 256, "slot ids must stay exact in bf16"
    h, logits_t = _norm(x, g, mod, shift_row=3, scale_row=4, router_wt=router_wt)
    pos, aff, lo = _route(logits_t, cap)
    y = _ffn(_gather(lo, pos, h, cap), wg, wu, wd, layer)
    return _combine(lo, pos, aff, y, x, mod, gate_row=5)


def _rope_tables(n_tok):
    t = jnp.arange(n_tok)
    rows = (t // GRID_W).astype(F32)
    cols = (t % GRID_W).astype(F32)
    n_freq = GLA_DK // 4
    inv = ROPE_THETA ** (-jnp.arange(n_freq, dtype=F32) / n_freq)
    ang = jnp.concatenate([rows[:, None] * inv, cols[:, None] * inv], axis=-1)
    cos = jnp.repeat(jnp.cos(ang), 2, axis=-1)
    sin = jnp.repeat(jnp.sin(ang), 2, axis=-1) * jnp.tile(jnp.asarray([-1.0, 1.0], F32), GLA_DK // 2)
    return cos, sin


def _rec_mixer(h_ctx, h_lat, w_in, alpha_up, alpha_b, gla_g, conv_w, conv_b, w_a, b_a, w_i, b_i, lam, rope_tabs):
    za0 = 2 * GLA_KEY_WIDTH + 2 * GLA_VAL_WIDTH
    za1 = za0 + 2 * GLA_RANK
    w_main = jnp.concatenate([w_in[:, :za0], w_in[:, za1:]], axis=1).astype(BF16)
    w_za = jnp.pad(w_in[:, za0:za1], ((0, 0), (0, LANES - 2 * GLA_RANK))).astype(BF16)
    up = alpha_up.reshape(2, GLA_RANK, GLA_HEADS, 1, GLA_DK)
    zero = jnp.zeros_like(up[0])
    aup = jnp.concatenate([jnp.concatenate([up[0], zero], axis=2), jnp.concatenate([zero, up[1]], axis=2)], axis=0)
    aup = jnp.pad(aup.reshape(2 * GLA_RANK, 2 * GLA_KEY_WIDTH), ((0, LANES - 2 * GLA_RANK), (0, 0))).astype(BF16)
    ab = jnp.transpose(alpha_b.reshape(2, GLA_HEADS, GLA_DK), (1, 0, 2)).reshape(1, 2 * GLA_KEY_WIDTH)
    gg = gla_g.reshape(1, GLA_DV)
    wa, wi = w_a.astype(BF16), w_i.astype(BF16)
    outs = []
    gla_state, lru_state = None, None
    for h, tabs in ((h_ctx, None), (h_lat, rope_tabs)):
        B = h.shape[0]
        p_main = _proj(h, w_main, F32, tn=1280)
        p_za = _proj(h, w_za, F32, tn=LANES)
        y_gla, gla_state = _gla(p_main, p_za, aup, ab, gg, tabs, gla_state)
        if lru_state is None:
            lru_state = (jnp.zeros((B, 1, LRU_WIDTH), F32),) * 2
        y_lru, lru_state = _lru(p_main, conv_w, conv_b, wa, b_a, wi, b_i, lam, lru_state)
        outs.append([y_gla, y_lru])
    return outs


def kernel(x, c, ctx, c_ctx, w_mod, b_mod, norm_mix_g, norm_ffn_g, w_out, router_w, exp_w_gate, exp_w_up, exp_w_down, rec_w_in, gla_alpha_up, gla_alpha_b, gla_norm_g, lru_conv_w, lru_conv_b, lru_w_a, lru_b_a, lru_w_i, lru_b_i, lru_lambda, na_w_qkv, na_rpb, norm_f_g):
    B, T, D = x.shape
    pad_rows = (-(B + 1)) % 8
    cc = jnp.concatenate([c, c_ctx[None, :], jnp.zeros((pad_rows, D), F32)], axis=0)
    m_all = _modulation(cc, w_mod, b_mod)
    rope_tabs = _rope_tables(T)
    exp_wg, exp_wu, exp_wd = exp_w_gate.astype(BF16), exp_w_up.astype(BF16), exp_w_down.astype(BF16)
    for l in range(DEPTH):
        last = l == DEPTH - 1
        i = l // 2
        m_lat = m_all[l, :B].reshape(B, N_MOD, D)
        m_ctx = m_all[l, B:B + 1].reshape(1, N_MOD, D)
        h_lat = _norm(x, norm_mix_g[l], m_lat)
        h_ctx = _norm(ctx, norm_mix_g[l], m_ctx)
        w_o = w_out[l].astype(BF16)
        if l % 2 == 0:
            ys_ctx, ys_lat = _rec_mixer(h_ctx, h_lat, rec_w_in[i], gla_alpha_up[i], gla_alpha_b[i], gla_norm_g[i],
                                        lru_conv_w[i], lru_conv_b[i], lru_w_a[i], lru_b_a[i], lru_w_i[i], lru_b_i[i],
                                        lru_lambda[i], rope_tabs)
            ws = [w_o[:GLA_VAL_WIDTH], w_o[GLA_VAL_WIDTH:]]
        else:
            w_qkv = na_w_qkv[i].astype(BF16)
            qkv_lat = _proj(h_lat, w_qkv, BF16, tn=1536)
            qkv_ctx = _proj(h_ctx, w_qkv, BF16, tn=1536)
            y_ctx, y_lat = _na(qkv_lat, qkv_ctx, _na_bias_tables(na_rpb[i], T // GRID_W), not last)
            ys_ctx, ys_lat, ws = [y_ctx], [y_lat], [w_o]
        moe_w = (router_w[l].T, exp_wg, exp_wu, exp_wd, l)
        x = _oproj(ys_lat, ws, x, m_lat, gate_row=2)
        x = _moe(x, norm_ffn_g[l], m_lat, *moe_w)
        if not last:
            ctx = _oproj(ys_ctx, ws, ctx, m_ctx, gate_row=2)
            ctx = _moe(ctx, norm_ffn_g[l], m_ctx, *moe_w)
    return _norm(x, norm_f_g, out_dtype=F32)
```

```python
import functools

import numpy as np
import jax
import jax.numpy as jnp
from jax import lax
from jax.experimental import pallas as pl
from jax.experimental.pallas import tpu as pltpu

D_MODEL = 2048
DEPTH = 4
GRID_W = 64
RMS_EPS = 1e-6
N_MOD = 6
ROPE_THETA = 10000.0
GLA_HEADS = 4
GLA_DK = D_MODEL // 16
GLA_DV = D_MODEL // 8
GLA_KEY_WIDTH = GLA_HEADS * GLA_DK
GLA_VAL_WIDTH = GLA_HEADS * GLA_DV
GLA_RANK = 16
GLA_TAU = 16.0
GLA_CHUNK = 64
LRU_WIDTH = D_MODEL // 2
LRU_BLOCKS = 8
LRU_BLOCK_DIM = LRU_WIDTH // LRU_BLOCKS
LRU_CONV = 4
LRU_C = 8.0
NA_HEADS = 16
NA_HEAD_DIM = D_MODEL // NA_HEADS
NA_KH = 8
NA_KW = 16
N_EXPERTS = 16
EC_CAPACITY_FACTOR = 2
EXPERT_FF = D_MODEL // 2

F32 = jnp.float32
BF16 = jnp.bfloat16
HIGHEST = lax.Precision.HIGHEST
LANES = 128
NEG_BIG = -1e30
LOG2_E = float(np.log2(np.e))
NA_QROWS = 4
NA_KROWS = NA_QROWS + NA_KH - 1
PREFIX_CHUNK = 256
SLOT_WINDOW = 64
GLA_PREP_ROWS = 512
LRU_SEGMENTS = 8
LRU_PITCH_PAD = 8
FFN_ROWS = 512
SLOT_ALIGN = 16

_NT = (((1,), (1,)), ((), ()))
_TN = (((0,), (0,)), ((), ()))


def _params(sem, vmem_mib=48):
    return pltpu.CompilerParams(dimension_semantics=sem, vmem_limit_bytes=vmem_mib * 2 ** 20)


def _sigmoid(z):
    return 1.0 / (1.0 + jnp.exp(-z))


def _silu(z):
    return z * _sigmoid(z)


def _mod_kernel(c_ref, w_ref, b_ref, o_ref):
    a = _silu(c_ref[...]).astype(BF16)
    o_ref[0] = jnp.dot(a, w_ref[0].astype(BF16), preferred_element_type=F32) + b_ref[0]


def _modulation(cc, w_mod, b_mod):
    R, D = cc.shape
    L, _, N = w_mod.shape
    tn = 1024
    return pl.pallas_call(
        _mod_kernel,
        grid=(L, N // tn),
        in_specs=[pl.BlockSpec((R, D), lambda l, j: (0, 0)),
                  pl.BlockSpec((1, D, tn), lambda l, j: (l, 0, j)),
                  pl.BlockSpec((1, 1, tn), lambda l, j: (l, 0, j))],
        out_specs=pl.BlockSpec((1, R, tn), lambda l, j: (l, 0, j)),
        out_shape=jax.ShapeDtypeStruct((L, R, N), F32),
        compiler_params=_params(("arbitrary", "arbitrary")),
        name="modulation",
    )(cc, w_mod, b_mod.reshape(L, 1, N))


def _rms_modulate(x, g_ref, m_ref=None, shift_row=0, scale_row=1):
    y = x * lax.rsqrt(jnp.mean(x * x, axis=-1, keepdims=True) + RMS_EPS) * g_ref[...]
    if m_ref is not None:
        y = y * (1.0 + m_ref[0, scale_row:scale_row + 1, :]) + m_ref[0, shift_row:shift_row + 1, :]
    return y


def _norm_kernel(*refs, modulated, with_router, shift_row, scale_row):
    it = iter(refs)
    x_ref, g_ref = next(it), next(it)
    m_ref = next(it) if modulated else None
    wr_ref = next(it) if with_router else None
    h_ref = next(it)
    lg_ref = next(it) if with_router else None
    y = _rms_modulate(x_ref[0], g_ref, m_ref, shift_row, scale_row)
    h_ref[0] = y.astype(h_ref.dtype)
    if with_router:
        lg_ref[0] = lax.dot_general(wr_ref[...], y, _NT, precision=HIGHEST, preferred_element_type=F32)


def _norm(x, g, mod=None, shift_row=0, scale_row=1, router_wt=None, out_dtype=BF16):
    B, T, D = x.shape
    tt = min(T, 512)
    modulated, with_router = mod is not None, router_wt is not None
    in_specs = [pl.BlockSpec((1, tt, D), lambda b, t: (b, t, 0)), pl.BlockSpec((1, D), lambda b, t: (0, 0))]
    args = [x, g.reshape(1, D)]
    if modulated:
        per_sample = mod.shape[0] > 1
        in_specs.append(pl.BlockSpec((1, N_MOD, D), (lambda b, t: (b, 0, 0)) if per_sample else (lambda b, t: (0, 0, 0))))
        args.append(mod)
    out_specs = [pl.BlockSpec((1, tt, D), lambda b, t: (b, t, 0))]
    out_shape = [jax.ShapeDtypeStruct((B, T, D), out_dtype)]
    if with_router:
        E = router_wt.shape[0]
        in_specs.append(pl.BlockSpec((E, D), lambda b, t: (0, 0)))
        args.append(router_wt)
        out_specs.append(pl.BlockSpec((1, E, tt), lambda b, t: (b, 0, t)))
        out_shape.append(jax.ShapeDtypeStruct((B, E, T), F32))
    res = pl.pallas_call(
        functools.partial(_norm_kernel, modulated=modulated, with_router=with_router,
                          shift_row=shift_row, scale_row=scale_row),
        grid=(B, T // tt), in_specs=in_specs, out_specs=out_specs, out_shape=out_shape,
        compiler_params=_params(("arbitrary", "arbitrary")),
        name="rmsnorm",
    )(*args)
    return res if with_router else res[0]


def _proj_kernel(a_ref, w_ref, o_ref):
    o_ref[...] = jnp.dot(a_ref[...], w_ref[...], preferred_element_type=F32).astype(o_ref.dtype)


def _proj(a, w, out_dtype, tn):
    B, T, K = a.shape
    N = w.shape[1]
    M = B * T
    tm = min(M, 512)
    out = pl.pallas_call(
        _proj_kernel,
        grid=(N // tn, M // tm),
        in_specs=[pl.BlockSpec((tm, K), lambda j, i: (i, 0)), pl.BlockSpec((K, tn), lambda j, i: (0, j))],
        out_specs=pl.BlockSpec((tm, tn), lambda j, i: (i, j)),
        out_shape=jax.ShapeDtypeStruct((M, N), out_dtype),
        compiler_params=_params(("arbitrary", "arbitrary")),
        name="proj",
    )(a.reshape(M, K), w)
    return out.reshape(B, T, N)


def _oproj_kernel(*refs, n_in):
    a_refs, w_refs = refs[:n_in], refs[n_in:2 * n_in]
    x_ref, m_ref, g_ref, wr_ref, o_ref, h_ref, lg_ref = refs[2 * n_in:]
    acc = jnp.dot(a_refs[0][0], w_refs[0][...], preferred_element_type=F32)
    for a_ref, w_ref in zip(a_refs[1:], w_refs[1:]):
        acc = acc + jnp.dot(a_ref[0], w_ref[...], preferred_element_type=F32)
    x_new = x_ref[0] + m_ref[0, 2:3, :] * acc
    o_ref[0] = x_new
    y = _rms_modulate(x_new, g_ref, m_ref, shift_row=3, scale_row=4)
    h_ref[0] = y.astype(h_ref.dtype)
    lg_ref[0] = lax.dot_general(wr_ref[...], y, _NT, precision=HIGHEST, preferred_element_type=F32)


def _oproj(ys, ws, x, mod, g_ffn, router_wt):
    B, T, D = x.shape
    E = router_wt.shape[0]
    tm = min(T, 512)
    per_sample = mod.shape[0] > 1
    rows = lambda n: pl.BlockSpec((1, tm, n), lambda b, t: (b, t, 0))
    in_specs = [rows(y.shape[2]) for y in ys]
    in_specs += [pl.BlockSpec(w.shape, lambda b, t: (0, 0)) for w in ws]
    in_specs += [rows(D),
                 pl.BlockSpec((1, N_MOD, D), (lambda b, t: (b, 0, 0)) if per_sample else (lambda b, t: (0, 0, 0))),
                 pl.BlockSpec((1, D), lambda b, t: (0, 0)), pl.BlockSpec((E, D), lambda b, t: (0, 0))]
    return pl.pallas_call(
        functools.partial(_oproj_kernel, n_in=len(ys)),
        grid=(B, T // tm), in_specs=in_specs,
        out_specs=[rows(D), rows(D), pl.BlockSpec((1, E, tm), lambda b, t: (b, 0, t))],
        out_shape=[jax.ShapeDtypeStruct((B, T, D), F32), jax.ShapeDtypeStruct((B, T, D), BF16),
                   jax.ShapeDtypeStruct((B, E, T), F32)],
        compiler_params=_params(("arbitrary", "arbitrary"), vmem_mib=56),
        name="oproj",
    )(*ys, *ws, x, mod, g_ffn.reshape(1, D), router_wt)


def _log_sigmoid(z):
    return jnp.minimum(z, 0.0) - jnp.log(1.0 + jnp.exp(-jnp.abs(z)))


def _gla_kernel(*refs, n_chunks, use_rope, has_state):
    it = iter(refs)
    q_ref, k_ref, v_ref, g_ref, za_ref = (next(it) for _ in range(5))
    cos_ref, sin_ref = (next(it), next(it)) if use_rope else (None, None)
    aup_ref, ab_ref, gg_ref = next(it), next(it), next(it)
    s0_ref = next(it) if has_state else None
    y_ref, sfin_ref = next(it), next(it)
    qd_scr, ki_scr, ke_scr, kv_scr, dec_scr, sall_scr, st_scr = (next(it) for _ in range(7))
    C, DK = GLA_CHUNK, GLA_DK
    T = n_chunks * C
    RB = min(T, GLA_PREP_ROWS)
    cpb = RB // C

    def prepare(blk, carry):
        r0 = pl.multiple_of(blk * RB, RB)
        rows = pl.ds(r0, RB)
        z = jnp.dot(za_ref[0, rows, :].astype(BF16), aup_ref[...], preferred_element_type=F32) + ab_ref[...]
        la = _log_sigmoid(z) * (1.0 / GLA_TAU)
        in_chunk = lax.broadcasted_iota(jnp.int32, la.shape, 0) % C
        prefix = la
        step = 1
        while step < C:
            prefix = prefix + jnp.where(in_chunk >= step, pltpu.roll(prefix, step, 0), 0.0)
            step *= 2
        q = q_ref[0, rows, :] * (GLA_DK ** -0.5)
        k = k_ref[0, rows, :]
        if use_rope:
            cos, sin = cos_ref[rows, :], sin_ref[rows, :]
            even = (lax.broadcasted_iota(jnp.int32, q.shape, 1) % 2) == 0

            def rope(t):
                swapped = jnp.where(even, pltpu.roll(t, LANES - 1, 1), pltpu.roll(t, 1, 1))
                return t * cos + swapped * sin
            q, k = rope(q), rope(k)
        split = lambda t: t.reshape(cpb, C, t.shape[-1])
        la, prefix, q, k = split(la), split(prefix), split(q), split(k)
        tot = prefix[:, C - 1:C, :]
        b_f = prefix[..., :DK]
        b_b = tot[..., DK:] - prefix[..., DK:] + la[..., DK:]
        dec = jnp.exp(tot)
        c0 = pl.multiple_of(blk * cpb, cpb)
        qd_scr[pl.ds(c0, cpb)] = jnp.concatenate([q * jnp.exp(b_f), q * jnp.exp(b_b)], axis=2).astype(BF16)
        ki_f, ki_b = k * jnp.exp(-b_f), k * jnp.exp(-b_b)
        ki_scr[0, pl.ds(c0, cpb)] = ki_f.astype(BF16)
        ki_scr[1, pl.ds(c0, cpb)] = ki_b.astype(BF16)
        ke_scr[pl.ds(c0, cpb)] = jnp.concatenate([ki_f * dec[..., :DK], ki_b * dec[..., DK:]], axis=2).astype(BF16)
        dec_scr[pl.ds(c0, cpb)] = jnp.broadcast_to(dec, (cpb,) + dec_scr.shape[1:])
        return carry

    lax.fori_loop(0, T // RB, prepare, 0)

    def increment(c, carry):
        rows = pl.ds(pl.multiple_of(c * C, C), C)
        kv_scr[c] = lax.dot_general(v_ref[0, rows, :].astype(BF16), ke_scr[c], _TN, preferred_element_type=F32)
        return carry

    lax.fori_loop(0, n_chunks, increment, 0, unroll=4)

    row = lax.broadcasted_iota(jnp.int32, (C, C), 0)
    col = lax.broadcasted_iota(jnp.int32, (C, C), 1)
    keep_f, keep_b = row >= col, row <= col

    st_scr[...] = s0_ref[0, 0] if has_state else jnp.zeros(st_scr.shape, F32)

    def recur(i, carry):
        cf, cb = i, n_chunks - 1 - i
        st = st_scr[...]
        sall_scr[cf, :, :DK] = st[:, :DK].astype(BF16)
        sall_scr[cb, :, DK:] = st[:, DK:].astype(BF16)
        st_scr[:, :DK] = st[:, :DK] * dec_scr[cf, 0:1, :DK] + kv_scr[cf, :, :DK]
        st_scr[:, DK:] = st[:, DK:] * dec_scr[cb, 0:1, DK:] + kv_scr[cb, :, DK:]
        return carry

    lax.fori_loop(0, n_chunks, recur, 0)
    sfin_ref[0, 0] = st_scr[...]

    def emit(c, carry):
        rows = pl.ds(pl.multiple_of(c * C, C), C)
        qd = qd_scr[c]
        s = (jnp.where(keep_f, lax.dot_general(qd[:, :DK], ki_scr[0, c], _NT, preferred_element_type=F32), 0.0)
             + jnp.where(keep_b, lax.dot_general(qd[:, DK:], ki_scr[1, c], _NT, preferred_element_type=F32), 0.0))
        o = (jnp.dot(s.astype(BF16), v_ref[0, rows, :].astype(BF16), preferred_element_type=F32)
             + lax.dot_general(qd, sall_scr[c], _NT, preferred_element_type=F32))
        o = o * lax.rsqrt(jnp.mean(o * o, axis=-1, keepdims=True) + RMS_EPS) * gg_ref[...]
        y_ref[0, rows, :] = (o * _silu(g_ref[0, rows, :])).astype(y_ref.dtype)
        return carry

    lax.fori_loop(0, n_chunks, emit, 0, unroll=4)


def _gla(p_main, p_za, aup, ab, gg, rope_tabs, state):
    B, T, _ = p_main.shape
    H, DK, DV = GLA_HEADS, GLA_DK, GLA_DV
    use_rope, has_state = rope_tabs is not None, state is not None
    n_chunks = T // GLA_CHUNK
    in_specs = [pl.BlockSpec((1, T, DK), lambda b, h: (b, 0, h)),
                pl.BlockSpec((1, T, DK), lambda b, h: (b, 0, H + h)),
                pl.BlockSpec((1, T, DV), lambda b, h: (b, 0, H + h)),
                pl.BlockSpec((1, T, DV), lambda b, h: (b, 0, 2 * H + h)),
                pl.BlockSpec((1, T, LANES), lambda b, h: (b, 0, 0))]
    args = [p_main, p_main, p_main, p_main, p_za]
    if use_rope:
        in_specs += [pl.BlockSpec((T, DK), lambda b, h: (0, 0))] * 2
        args += list(rope_tabs)
    in_specs += [pl.BlockSpec((LANES, 2 * DK), lambda b, h: (0, h)),
                 pl.BlockSpec((1, 2 * DK), lambda b, h: (0, h)),
                 pl.BlockSpec((1, DV), lambda b, h: (0, 0))]
    args += [aup, ab, gg]
    st_spec = pl.BlockSpec((1, 1, DV, 2 * DK), lambda b, h: (b, h, 0, 0))
    if has_state:
        in_specs.append(st_spec)
        args.append(state)
    return pl.pallas_call(
        functools.partial(_gla_kernel, n_chunks=n_chunks, use_rope=use_rope, has_state=has_state),
        grid=(B, H), in_specs=in_specs,
        out_specs=[pl.BlockSpec((1, T, DV), lambda b, h: (b, 0, h)), st_spec],
        out_shape=[jax.ShapeDtypeStruct((B, T, H * DV), BF16), jax.ShapeDtypeStruct((B, H, DV, 2 * DK), F32)],
        scratch_shapes=[pltpu.VMEM((n_chunks, GLA_CHUNK, 2 * DK), BF16), pltpu.VMEM((2, n_chunks, GLA_CHUNK, DK), BF16),
                        pltpu.VMEM((n_chunks, GLA_CHUNK, 2 * DK), BF16),
                        pltpu.VMEM((n_chunks, DV, 2 * DK), F32), pltpu.VMEM((n_chunks, 8, 2 * DK), F32),
                        pltpu.VMEM((n_chunks, DV, 2 * DK), BF16), pltpu.VMEM((DV, 2 * DK), F32)],
        compiler_params=_params(("arbitrary", "arbitrary")),
        name="gla",
    )(*args)


def _gelu_tanh(z):
    return 0.5 * z * (1.0 + jnp.tanh(np.sqrt(2.0 / np.pi) * (z + 0.044715 * (z * z * z))))


def _lru_kernel(x_ref, gt_ref, cw_ref, cb_ref, wa_ref, ba_ref, wi_ref, bi_ref, lam_ref, h0f_ref, h0b_ref,
                y_ref, hf_ref, hb_ref, a_scr, bx_scr, hl_scr, ap_scr, *, T):
    L = T // LRU_SEGMENTS
    P = L + LRU_PITCH_PAD
    x = x_ref[0]
    row = lax.broadcasted_iota(jnp.int32, x.shape, 0)

    def from_earlier(v, d, fill):
        return jnp.where(row >= d, pltpu.roll(v, d, 0), fill)

    def from_later(v, d, fill):
        return jnp.where(row < T - d, pltpu.roll(v, T - d, 0), fill)

    xc = (cw_ref[0:1, :] * from_earlier(x, 1, 0.0) + cw_ref[1:2, :] * x
          + cw_ref[2:3, :] * from_later(x, 1, 0.0) + cw_ref[3:4, :] * from_later(x, 2, 0.0) + cb_ref[...])
    xcb = xc.astype(BF16)

    def sigmoid(z):
        return 0.5 * jnp.tanh(0.5 * z) + 0.5

    for d in range(2):
        r = sigmoid(jnp.dot(xcb, wa_ref[d, 0], preferred_element_type=F32) + ba_ref[d])
        gate_i = sigmoid(jnp.dot(xcb, wi_ref[d, 0], preferred_element_type=F32) + bi_ref[d])
        neg_lam = -lam_ref[d]
        softplus = jnp.maximum(neg_lam, 0.0) + jnp.log(1.0 + jnp.exp(-jnp.abs(neg_lam)))
        a = jnp.exp((-LRU_C) * r * softplus)
        bx = jnp.sqrt(1.0 - a * a) * (gate_i * xc)
        for s in range(LRU_SEGMENTS):
            a_scr[d, s * P:s * P + L, :] = a[s * L:(s + 1) * L]
            bx_scr[d, s * P:s * P + L, :] = bx[s * L:(s + 1) * L]

    def step(i, carry):
        h_f, p_f, h_b, p_b = carry
        at_f = pl.ds(i, LRU_SEGMENTS, stride=P)
        at_b = pl.ds(L - 1 - i, LRU_SEGMENTS, stride=P)
        a_f, a_b = a_scr[0, at_f, :], a_scr[1, at_b, :]
        h_f = a_f * h_f + bx_scr[0, at_f, :]
        h_b = a_b * h_b + bx_scr[1, at_b, :]
        p_f, p_b = a_f * p_f, a_b * p_b
        hl_scr[0, at_f, :] = h_f
        hl_scr[1, at_b, :] = h_b
        ap_scr[0, at_f, :] = p_f
        ap_scr[1, at_b, :] = p_b
        return h_f, p_f, h_b, p_b

    zeros, ones = jnp.zeros((LRU_SEGMENTS, x.shape[1]), F32), jnp.ones((LRU_SEGMENTS, x.shape[1]), F32)
    h_f, p_f, h_b, p_b = lax.fori_loop(0, L, step, (zeros, ones, zeros, ones), unroll=8)

    enter_f, enter_b = [h0f_ref[0]], [h0b_ref[0]]
    for s in range(LRU_SEGMENTS):
        enter_f.append(h_f[s:s + 1] + p_f[s:s + 1] * enter_f[-1])
        t = LRU_SEGMENTS - 1 - s
        enter_b.append(h_b[t:t + 1] + p_b[t:t + 1] * enter_b[-1])
    hf_ref[0] = enter_f[-1]
    hb_ref[0] = enter_b[-1]
    for s in range(LRU_SEGMENTS):
        seg = slice(s * P, s * P + L)
        h = (hl_scr[0, seg, :] + ap_scr[0, seg, :] * enter_f[s]
             + hl_scr[1, seg, :] + ap_scr[1, seg, :] * enter_b[LRU_SEGMENTS - 1 - s])
        y_ref[0, s * L:(s + 1) * L, :] = (h * _gelu_tanh(gt_ref[0, s * L:(s + 1) * L, :])).astype(y_ref.dtype)


def _lru(p_main, cw, cb, wa, ba, wi, bi, lam, states):
    B, T, _ = p_main.shape
    W, NB, BD = LRU_WIDTH, LRU_BLOCKS, LRU_BLOCK_DIM
    x_blk0 = (2 * GLA_KEY_WIDTH + 2 * GLA_VAL_WIDTH) // BD
    vec = lambda n: pl.BlockSpec((n, 1, BD), lambda b, j: (0, 0, j))
    mat = pl.BlockSpec((2, 1, BD, BD), lambda b, j: (0, j, 0, 0))
    st_spec = pl.BlockSpec((1, 1, BD), lambda b, j: (b, 0, j))
    st_shape = jax.ShapeDtypeStruct((B, 1, W), F32)
    y, hf, hb = pl.pallas_call(
        functools.partial(_lru_kernel, T=T),
        grid=(B, NB),
        in_specs=[pl.BlockSpec((1, T, BD), lambda b, j: (b, 0, x_blk0 + j)),
                  pl.BlockSpec((1, T, BD), lambda b, j: (b, 0, x_blk0 + NB + j)),
                  pl.BlockSpec((LRU_CONV, BD), lambda b, j: (0, j)),
                  pl.BlockSpec((1, BD), lambda b, j: (0, j)),
                  mat, vec(2), mat, vec(2), vec(2), st_spec, st_spec],
        out_specs=[pl.BlockSpec((1, T, BD), lambda b, j: (b, 0, j)), st_spec, st_spec],
        out_shape=[jax.ShapeDtypeStruct((B, T, W), BF16), st_shape, st_shape],
        scratch_shapes=[pltpu.VMEM((2, LRU_SEGMENTS * (T // LRU_SEGMENTS + LRU_PITCH_PAD), BD), F32)] * 4,
        compiler_params=_params(("arbitrary", "arbitrary")),
        name="rglru",
    )(p_main, p_main, cw, cb.reshape(1, W), wa, ba.reshape(2, 1, W), wi, bi.reshape(2, 1, W),
      lam.reshape(2, 1, W), *states)
    return y, (hf, hb)


def _na_tile(i, n_tiles, rows):
    r0 = i * NA_QROWS
    ks = min(max(r0 - NA_KH // 2, 0), rows - NA_KROWS)
    cls = 0 if i == 0 else (2 if i == n_tiles - 1 else 1)
    return r0, ks, cls


def _na_bias_tables(rpb, rows):
    H = rpb.shape[0]
    qc = np.arange(GRID_W)[:, None]
    kc = np.arange(GRID_W)[None, :]
    col_start = np.clip(qc - NA_KW // 2, 0, GRID_W - NA_KW)
    col_valid = (kc >= col_start) & (kc < col_start + NA_KW)
    dc = np.clip(kc - qc, -(NA_KW - 1), NA_KW - 1) + NA_KW - 1
    pick = (dc[None] == np.arange(2 * NA_KW - 1)[:, None, None]).astype(np.float32)
    band = jnp.einsum('hrd,dqk->hrqk', rpb.astype(F32), jnp.asarray(pick), precision=HIGHEST)
    band = jnp.where(col_valid, band, NEG_BIG)
    masked = jnp.full((H, GRID_W, GRID_W), NEG_BIG, F32)
    n_tiles = rows // NA_QROWS
    tabs = []
    for i in (0, 1, n_tiles - 1):
        r0, ks, _ = _na_tile(i, n_tiles, rows)
        per_qrow = []
        for qr in range(r0, r0 + NA_QROWS):
            row_start = min(max(qr - NA_KH // 2, 0), rows - NA_KH)
            blocks = [band[:, kr - qr + NA_KH - 1] if row_start <= kr < row_start + NA_KH else masked
                      for kr in range(ks, ks + NA_KROWS)]
            per_qrow.append(jnp.stack(blocks, axis=2).reshape(H, GRID_W, NA_KROWS * GRID_W))
        tabs.append(jnp.concatenate(per_qrow, axis=1))
    return jnp.stack(tabs, axis=1) * LOG2_E


def _na_kernel(*refs, rows, need_ctx):
    q_ref, k_ref, v_ref, qc_ref, kc_ref, vc_ref, tab_ref = refs[:7]
    y_ref = refs[7]
    scale = NA_HEAD_DIM ** -0.5 * LOG2_E
    kc, vc = kc_ref[0], vc_ref[0]
    n_tiles = rows // NA_QROWS
    nq, nk = NA_QROWS * GRID_W, NA_KROWS * GRID_W
    for i in range(n_tiles):
        r0, ks, cls = _na_tile(i, n_tiles, rows)
        q = q_ref[0, r0 * GRID_W:r0 * GRID_W + nq, :]
        kt = k_ref[0, ks * GRID_W:ks * GRID_W + nk, :]
        vt = v_ref[0, ks * GRID_W:ks * GRID_W + nk, :]
        s_loc = lax.dot_general(q, kt, _NT, preferred_element_type=F32) * scale + tab_ref[0, cls]
        s_ctx = lax.dot_general(q, kc, _NT, preferred_element_type=F32) * scale
        m = jnp.maximum(jnp.max(s_loc, axis=-1, keepdims=True), jnp.max(s_ctx, axis=-1, keepdims=True))
        p_loc, p_ctx = jnp.exp2(s_loc - m), jnp.exp2(s_ctx - m)
        denom = jnp.sum(p_loc, axis=-1, keepdims=True) + jnp.sum(p_ctx, axis=-1, keepdims=True)
        o = (jnp.dot(p_loc.astype(BF16), vt, preferred_element_type=F32)
             + jnp.dot(p_ctx.astype(BF16), vc, preferred_element_type=F32))
        y_ref[0, r0 * GRID_W:r0 * GRID_W + nq, :] = (o / denom).astype(y_ref.dtype)
    if need_ctx:
        yc_ref = refs[8]
        s = lax.dot_general(qc_ref[0], kc, _NT, preferred_element_type=F32) * scale
        p = jnp.exp2(s - jnp.max(s, axis=-1, keepdims=True))
        o = jnp.dot(p.astype(BF16), vc, preferred_element_type=F32) / jnp.sum(p, axis=-1, keepdims=True)
        yc_ref[0] = o.astype(yc_ref.dtype)


def _na(qkv_lat, qkv_ctx, tabs, need_ctx):
    B, T, _ = qkv_lat.shape
    Tc = qkv_ctx.shape[1]
    H, HD = NA_HEADS, NA_HEAD_DIM
    blk = lambda t, off: pl.BlockSpec((1, t, HD), lambda h, b: (b, 0, off + h))
    out_specs = [pl.BlockSpec((1, T, HD), lambda h, b: (b, 0, h))]
    out_shape = [jax.ShapeDtypeStruct((B, T, H * HD), BF16)]
    if need_ctx:
        out_specs.append(pl.BlockSpec((1, Tc, HD), lambda h, b: (b, 0, h)))
        out_shape.append(jax.ShapeDtypeStruct((B, Tc, H * HD), BF16))
    res = pl.pallas_call(
        functools.partial(_na_kernel, rows=T // GRID_W, need_ctx=need_ctx),
        grid=(H, B),
        in_specs=[blk(T, 0), blk(T, H), blk(T, 2 * H), blk(Tc, 0), blk(Tc, H), blk(Tc, 2 * H),
                  pl.BlockSpec((1,) + tabs.shape[1:], lambda h, b: (h, 0, 0, 0))],
        out_specs=out_specs, out_shape=out_shape,
        compiler_params=_params(("arbitrary", "arbitrary")),
        name="natten",
    )(qkv_lat, qkv_lat, qkv_lat, qkv_ctx, qkv_ctx, qkv_ctx, tabs)
    return (res[1] if need_ctx else None), res[0]


def _route_kernel(lg_ref, pos_ref, aff_ref, lo_ref, *, cap):
    lg = lg_ref[0]
    E, N = lg.shape
    ex = jnp.exp(lg - jnp.max(lg, axis=0, keepdims=True))
    aff = ex / jnp.sum(ex, axis=0, keepdims=True)
    aff_ref[0] = aff
    bits = pltpu.bitcast(aff, jnp.int32)
    thr = jnp.zeros((E, 1), jnp.int32)
    for bit in range(29, -1, -1):
        cand = thr | (1 << bit)
        cnt = jnp.sum(jnp.where(bits >= cand, 1.0, 0.0), axis=1, keepdims=True)
        thr = jnp.where(cnt >= cap, cand, thr)
    above = bits > thr
    tied = bits == thr
    need = cap - jnp.sum(jnp.where(above, 1.0, 0.0), axis=1, keepdims=True)
    pc = min(PREFIX_CHUNK, N)
    before = jnp.where(lax.broadcasted_iota(jnp.int32, (pc, pc), 0) < lax.broadcasted_iota(jnp.int32, (pc, pc), 1),
                       1.0, 0.0).astype(BF16)
    tied_seen = jnp.zeros((E, 1), F32)
    sel_seen = jnp.zeros((E, 1), F32)
    tile_lane = lax.broadcasted_iota(jnp.int32, (E, LANES), 1)
    lo = jnp.where(tile_lane == N // pc, float(cap), 0.0)
    for c in range(N // pc):
        sl = slice(c * pc, (c + 1) * pc)
        lo = jnp.where(tile_lane == c, sel_seen, lo)
        tied_c = jnp.where(tied[:, sl], 1.0, 0.0)
        tied_before = jnp.dot(tied_c.astype(BF16), before, preferred_element_type=F32) + tied_seen
        sel_c = jnp.where(above[:, sl], 1.0, jnp.where(tied_before < need, tied_c, 0.0))
        sel_before = jnp.dot(sel_c.astype(BF16), before, preferred_element_type=F32) + sel_seen
        pos_ref[0, :, sl] = jnp.where(sel_c > 0.0, sel_before, -1.0)
        tied_seen = tied_seen + jnp.sum(tied_c, axis=1, keepdims=True)
        sel_seen = sel_seen + jnp.sum(sel_c, axis=1, keepdims=True)
    lo_ref[0] = lo.astype(jnp.int32)


def _route(logits_t, cap):
    B, E, N = logits_t.shape
    spec = pl.BlockSpec((1, E, N), lambda b: (b, 0, 0))
    lo_spec = pl.BlockSpec((1, E, LANES), lambda b: (b, 0, 0))
    pos, aff, lo = pl.pallas_call(
        functools.partial(_route_kernel, cap=cap),
        grid=(B,), in_specs=[spec], out_specs=[spec, spec, lo_spec],
        out_shape=[jax.ShapeDtypeStruct((B, E, N), F32)] * 2 + [jax.ShapeDtypeStruct((B, E, LANES), jnp.int32)],
        compiler_params=_params(("arbitrary",)),
        name="route",
    )(logits_t)
    n_bounds = N // min(PREFIX_CHUNK, N) + 1
    return pos, aff, lo[:, :, :n_bounds].reshape(-1)


def _window_count(lo_ref, base, experts, tile, n_bounds, win):
    starts, n_win = [], 0
    for e in experts:
        first = lo_ref[(base + e) * n_bounds + tile]
        end = lo_ref[(base + e) * n_bounds + tile + 1]
        start = (first // SLOT_ALIGN) * SLOT_ALIGN
        starts.append(start)
        n_win = jnp.maximum(n_win, (end - start + win - 1) // win)
    return starts, n_win


def _gather_kernel(lo_ref, pos_ref, h_ref, xs_ref, *, cap, win, tn):
    EG, N = pos_ref.shape[1], pos_ref.shape[2]
    n_bounds = N // tn + 1
    base = pl.program_id(0) * (EG * pl.num_programs(1)) + pl.program_id(1) * EG
    xs_ref[...] = jnp.zeros(xs_ref.shape, xs_ref.dtype)
    slot_iota = lax.broadcasted_iota(jnp.int32, (win, tn), 0)
    for j in range(N // tn):
        h_tile = h_ref[0, j * tn:(j + 1) * tn, :]
        starts, n_win = _window_count(lo_ref, base, range(EG), j, n_bounds, win)

        def window(w, carry):
            pieces, offs = [], []
            for e in range(EG):
                first = starts[e] + w * win
                off = jnp.minimum(first, cap - win)
                prow = pos_ref[0, e:e + 1, j * tn:(j + 1) * tn]
                hit = ((slot_iota + off).astype(F32) == prow) & (prow >= first.astype(F32))
                pieces.append(jnp.where(hit, 1.0, 0.0).astype(BF16))
                offs.append(pl.multiple_of(off, SLOT_ALIGN))
            got = jnp.dot(jnp.concatenate(pieces, axis=0), h_tile, preferred_element_type=F32)
            for e in range(EG):
                xs_ref[0, e, pl.ds(offs[e], win), :] += got[e * win:(e + 1) * win].astype(xs_ref.dtype)
            return carry

        lax.fori_loop(0, n_win, window, 0)


def _gather(lo, pos, h, cap):
    B, E, N = pos.shape
    D = h.shape[2]
    EG = 8
    tn, win = min(PREFIX_CHUNK, N), min(SLOT_WINDOW, cap)
    return pl.pallas_call(
        functools.partial(_gather_kernel, cap=cap, win=win, tn=tn),
        grid_spec=pltpu.PrefetchScalarGridSpec(
            num_scalar_prefetch=1, grid=(B, E // EG),
            in_specs=[pl.BlockSpec((1, EG, N), lambda b, g, lo: (b, g, 0)),
                      pl.BlockSpec((1, N, D), lambda b, g, lo: (b, 0, 0))],
            out_specs=pl.BlockSpec((1, EG, cap, D), lambda b, g, lo: (b, g, 0, 0))),
        out_shape=jax.ShapeDtypeStruct((B, E, cap, D), BF16),
        compiler_params=_params(("arbitrary", "arbitrary")),
        name="moe_gather",
    )(lo, pos, h)


def _ffn_kernel(xs_ref, wg_ref, wu_ref, wd_ref, y_ref):
    bb, _, cap, D = xs_ref.shape
    xs = xs_ref[:, 0].reshape(bb * cap, D)
    hid = (_silu(jnp.dot(xs, wg_ref[0, 0], preferred_element_type=F32))
           * jnp.dot(xs, wu_ref[0, 0], preferred_element_type=F32))
    y = jnp.dot(hid.astype(BF16), wd_ref[0, 0], preferred_element_type=F32)
    y_ref[:, 0] = y.reshape(bb, cap, D).astype(y_ref.dtype)


def _ffn(xs, wg, wu, wd, layer):
    B, E, cap, D = xs.shape
    F = wg.shape[3]
    bb = max(1, min(B, FFN_ROWS // cap))
    tok = pl.BlockSpec((bb, 1, cap, D), lambda e, b: (b, e, 0, 0))
    return pl.pallas_call(
        _ffn_kernel,
        grid=(E, B // bb),
        in_specs=[tok, pl.BlockSpec((1, 1, D, F), lambda e, b: (layer, e, 0, 0)),
                  pl.BlockSpec((1, 1, D, F), lambda e, b: (layer, e, 0, 0)),
                  pl.BlockSpec((1, 1, F, D), lambda e, b: (layer, e, 0, 0))],
        out_specs=tok,
        out_shape=jax.ShapeDtypeStruct((B, E, cap, D), BF16),
        compiler_params=_params(("arbitrary", "arbitrary")),
        name="moe_ffn",
    )(xs, wg, wu, wd)


def _combine_kernel(*refs, cap, win, keep_x, next_modulated):
    it = iter(refs)
    lo_ref, pos_ref, aff_ref, y_ref, x_ref, m_ref, gn_ref = (next(it) for _ in range(7))
    mn_ref = next(it) if next_modulated else None
    o_ref = next(it) if keep_x else None
    hn_ref, ystack_ref, acc_ref = next(it), next(it), next(it)
    E = y_ref.shape[1]
    tn = pos_ref.shape[1]
    n_bounds = pl.num_programs(1) + 1
    j = pl.program_id(1)
    starts, n_win = _window_count(lo_ref, pl.program_id(0) * E, range(E), j, n_bounds, win)
    col = lax.broadcasted_iota(jnp.int32, (E, E * win), 1)
    spread = jnp.where(col // win == lax.broadcasted_iota(jnp.int32, (E, E * win), 0), 1.0, 0.0).astype(BF16)
    slot = jnp.dot(pos_ref[0].astype(BF16), spread, preferred_element_type=F32)
    gate = jnp.dot(aff_ref[0].astype(BF16), spread, preferred_element_type=F32)
    col1 = lax.broadcasted_iota(jnp.int32, (1, E * win), 1)
    col_expert, col_in_win = col1 // win, col1 % win
    acc_ref[...] = jnp.zeros(acc_ref.shape, F32)

    def window(w, carry):
        first_col = jnp.zeros((1, E * win), jnp.int32)
        want_col = jnp.zeros((1, E * win), jnp.int32)
        for e in range(E):
            first = starts[e] + w * win
            off = jnp.minimum(first, cap - win)
            ystack_ref[e * win:(e + 1) * win, :] = y_ref[0, e, pl.ds(pl.multiple_of(off, SLOT_ALIGN), win), :]
            first_col = jnp.where(col_expert == e, first, first_col)
            want_col = jnp.where(col_expert == e, off + col_in_win, want_col)
        hit = (slot == want_col.astype(F32)) & (slot >= first_col.astype(F32))
        acc_ref[...] += jnp.dot(jnp.where(hit, gate, 0.0).astype(BF16), ystack_ref[...], preferred_element_type=F32)
        return carry

    lax.fori_loop(0, n_win, window, 0)
    x_new = x_ref[0] + m_ref[0, 5:6, :] * acc_ref[...]
    if keep_x:
        o_ref[0] = x_new
    hn_ref[0] = _rms_modulate(x_new, gn_ref, mn_ref).astype(hn_ref.dtype)


def _combine(lo, pos, aff, y, x, mod, g_next, mod_next):
    B, E, cap, D = y.shape
    N = x.shape[1]
    tn, win = min(PREFIX_CHUNK, N), min(SLOT_WINDOW, cap)
    keep_x = next_modulated = mod_next is not None
    tok = pl.BlockSpec((1, tn, E), lambda b, t, lo: (b, t, 0))
    rows = pl.BlockSpec((1, tn, D), lambda b, t, lo: (b, t, 0))
    mod_spec = lambda m: pl.BlockSpec((1, N_MOD, D),
                                      (lambda b, t, lo: (b, 0, 0)) if m.shape[0] > 1 else (lambda b, t, lo: (0, 0, 0)))
    in_specs = [tok, tok, pl.BlockSpec((1, E, cap, D), lambda b, t, lo: (b, 0, 0, 0)), rows, mod_spec(mod),
                pl.BlockSpec((1, D), lambda b, t, lo: (0, 0))]
    args = [lo, jnp.transpose(pos, (0, 2, 1)), jnp.transpose(aff, (0, 2, 1)), y, x, mod, g_next.reshape(1, D)]
    out_specs, out_shape = [rows], [jax.ShapeDtypeStruct((B, N, D), BF16 if next_modulated else F32)]
    if next_modulated:
        in_specs.append(mod_spec(mod_next))
        args.append(mod_next)
        out_specs, out_shape = [rows] + out_specs, [jax.ShapeDtypeStruct((B, N, D), F32)] + out_shape
    res = pl.pallas_call(
        functools.partial(_combine_kernel, cap=cap, win=win, keep_x=keep_x, next_modulated=next_modulated),
        grid_spec=pltpu.PrefetchScalarGridSpec(
            num_scalar_prefetch=1, grid=(B, N // tn), in_specs=in_specs, out_specs=out_specs,
            scratch_shapes=[pltpu.VMEM((E * win, D), BF16), pltpu.VMEM((tn, D), F32)]),
        out_shape=out_shape,
        compiler_params=_params(("arbitrary", "arbitrary"), vmem_mib=56),
        name="moe_combine",
    )(*args)
    return (res[0], res[1]) if keep_x else (None, res[0])


def _moe(x, h, logits_t, mod, wg, wu, wd, layer, g_next, mod_next):
    N = x.shape[1]
    cap = EC_CAPACITY_FACTOR * N // N_EXPERTS
    assert cap <= 256, "slot ids must stay exact in bf16"
    pos, aff, lo = _route(logits_t, cap)
    y = _ffn(_gather(lo, pos, h, cap), wg, wu, wd, layer)
    return _combine(lo, pos, aff, y, x, mod, g_next, mod_next)


def _rope_tables(n_tok):
    t = jnp.arange(n_tok)
    rows = (t // GRID_W).astype(F32)
    cols = (t % GRID_W).astype(F32)
    n_freq = GLA_DK // 4
    inv = ROPE_THETA ** (-jnp.arange(n_freq, dtype=F32) / n_freq)
    ang = jnp.concatenate([rows[:, None] * inv, cols[:, None] * inv], axis=-1)
    cos = jnp.repeat(jnp.cos(ang), 2, axis=-1)
    sin = jnp.repeat(jnp.sin(ang), 2, axis=-1) * jnp.tile(jnp.asarray([-1.0, 1.0], F32), GLA_DK // 2)
    return cos, sin


def _rec_mixer(h_ctx, h_lat, w_in, alpha_up, alpha_b, gla_g, conv_w, conv_b, w_a, b_a, w_i, b_i, lam, rope_tabs):
    za0 = 2 * GLA_KEY_WIDTH + 2 * GLA_VAL_WIDTH
    za1 = za0 + 2 * GLA_RANK
    w_main = jnp.concatenate([w_in[:, :za0], w_in[:, za1:]], axis=1).astype(BF16)
    w_za = jnp.pad(w_in[:, za0:za1], ((0, 0), (0, LANES - 2 * GLA_RANK))).astype(BF16)
    up = alpha_up.reshape(2, GLA_RANK, GLA_HEADS, 1, GLA_DK)
    zero = jnp.zeros_like(up[0])
    aup = jnp.concatenate([jnp.concatenate([up[0], zero], axis=2), jnp.concatenate([zero, up[1]], axis=2)], axis=0)
    aup = jnp.pad(aup.reshape(2 * GLA_RANK, 2 * GLA_KEY_WIDTH), ((0, LANES - 2 * GLA_RANK), (0, 0))).astype(BF16)
    ab = jnp.transpose(alpha_b.reshape(2, GLA_HEADS, GLA_DK), (1, 0, 2)).reshape(1, 2 * GLA_KEY_WIDTH)
    gg = gla_g.reshape(1, GLA_DV)
    wa, wi = w_a.astype(BF16), w_i.astype(BF16)
    outs = []
    gla_state, lru_state = None, None
    for h, tabs in ((h_ctx, None), (h_lat, rope_tabs)):
        B = h.shape[0]
        p_main = _proj(h, w_main, F32, tn=2560)
        p_za = _proj(h, w_za, F32, tn=LANES)
        y_gla, gla_state = _gla(p_main, p_za, aup, ab, gg, tabs, gla_state)
        if lru_state is None:
            lru_state = (jnp.zeros((B, 1, LRU_WIDTH), F32),) * 2
        y_lru, lru_state = _lru(p_main, conv_w, conv_b, wa, b_a, wi, b_i, lam, lru_state)
        outs.append([y_gla, y_lru])
    return outs


def kernel(x, c, ctx, c_ctx, w_mod, b_mod, norm_mix_g, norm_ffn_g, w_out, router_w, exp_w_gate, exp_w_up, exp_w_down, rec_w_in, gla_alpha_up, gla_alpha_b, gla_norm_g, lru_conv_w, lru_conv_b, lru_w_a, lru_b_a, lru_w_i, lru_b_i, lru_lambda, na_w_qkv, na_rpb, norm_f_g):
    B, T, D = x.shape
    pad_rows = (-(B + 1)) % 8
    cc = jnp.concatenate([c, c_ctx[None, :], jnp.zeros((pad_rows, D), F32)], axis=0)
    m_all = _modulation(cc, w_mod, b_mod)
    rope_tabs = _rope_tables(T)
    exp_wg, exp_wu, exp_wd = exp_w_gate.astype(BF16), exp_w_up.astype(BF16), exp_w_down.astype(BF16)
    mods_lat = [m_all[l, :B].reshape(B, N_MOD, D) for l in range(DEPTH)]
    mods_ctx = [m_all[l, B:B + 1].reshape(1, N_MOD, D) for l in range(DEPTH)]
    h_lat = _norm(x, norm_mix_g[0], mods_lat[0])
    h_ctx = _norm(ctx, norm_mix_g[0], mods_ctx[0])
    for l in range(DEPTH):
        last = l == DEPTH - 1
        i = l // 2
        m_lat, m_ctx = mods_lat[l], mods_ctx[l]
        w_o = w_out[l].astype(BF16)
        if l % 2 == 0:
            ys_ctx, ys_lat = _rec_mixer(h_ctx, h_lat, rec_w_in[i], gla_alpha_up[i], gla_alpha_b[i], gla_norm_g[i],
                                        lru_conv_w[i], lru_conv_b[i], lru_w_a[i], lru_b_a[i], lru_w_i[i], lru_b_i[i],
                                        lru_lambda[i], rope_tabs)
            ws = [w_o[:GLA_VAL_WIDTH], w_o[GLA_VAL_WIDTH:]]
        else:
            w_qkv = na_w_qkv[i].astype(BF16)
            qkv_lat = _proj(h_lat, w_qkv, BF16, tn=3072)
            qkv_ctx = _proj(h_ctx, w_qkv, BF16, tn=3072)
            y_ctx, y_lat = _na(qkv_lat, qkv_ctx, _na_bias_tables(na_rpb[i], T // GRID_W), not last)
            ys_ctx, ys_lat, ws = [y_ctx], [y_lat], [w_o]
        moe_w = (exp_wg, exp_wu, exp_wd, l)
        router_wt = router_w[l].T
        x, h_ffn, logits_t = _oproj(ys_lat, ws, x, m_lat, norm_ffn_g[l], router_wt)
        if last:
            _, out = _moe(x, h_ffn, logits_t, m_lat, *moe_w, norm_f_g, None)
            return out
        x, h_lat = _moe(x, h_ffn, logits_t, m_lat, *moe_w, norm_mix_g[l + 1], mods_lat[l + 1])
        ctx, h_ffn, logits_t = _oproj(ys_ctx, ws, ctx, m_ctx, norm_ffn_g[l], router_wt)
        ctx, h_ctx = _moe(ctx, h_ffn, logits_t, m_ctx, *moe_w, norm_mix_g[l + 1], mods_ctx[l + 1])
```

```python
import functools

import numpy as np
import jax
import jax.numpy as jnp
from jax import lax
from jax.experimental import pallas as pl
from jax.experimental.pallas import tpu as pltpu

D_MODEL = 2048
DEPTH = 4
GRID_W = 64
RMS_EPS = 1e-6
N_MOD = 6
ROPE_THETA = 10000.0
GLA_HEADS = 4
GLA_DK = D_MODEL // 16
GLA_DV = D_MODEL // 8
GLA_KEY_WIDTH = GLA_HEADS * GLA_DK
GLA_VAL_WIDTH = GLA_HEADS * GLA_DV
GLA_RANK = 16
GLA_TAU = 16.0
GLA_CHUNK = 64
LRU_WIDTH = D_MODEL // 2
LRU_BLOCKS = 8
LRU_BLOCK_DIM = LRU_WIDTH // LRU_BLOCKS
LRU_CONV = 4
LRU_C = 8.0
NA_HEADS = 16
NA_HEAD_DIM = D_MODEL // NA_HEADS
NA_KH = 8
NA_KW = 16
N_EXPERTS = 16
EC_CAPACITY_FACTOR = 2
EXPERT_FF = D_MODEL // 2

F32 = jnp.float32
BF16 = jnp.bfloat16
HIGHEST = lax.Precision.HIGHEST
LANES = 128
NEG_BIG = -1e30
LOG2_E = float(np.log2(np.e))
NA_QROWS = 4
NA_KROWS = NA_QROWS + NA_KH - 1
PREFIX_CHUNK = 256
SLOT_WINDOW = 64
OPROJ_SUB_ROWS = 256
GLA_PREP_ROWS = 512
LRU_SEGMENTS = 8
LRU_PITCH_PAD = 8
FFN_ROWS = 512
SLOT_ALIGN = 16

_NT = (((1,), (1,)), ((), ()))
_TN = (((0,), (0,)), ((), ()))


def _params(sem, vmem_mib=48):
    return pltpu.CompilerParams(dimension_semantics=sem, vmem_limit_bytes=vmem_mib * 2 ** 20)


def _sigmoid(z):
    return 1.0 / (1.0 + jnp.exp(-z))


def _silu(z):
    return z * _sigmoid(z)


def _mod_kernel(c_ref, w_ref, b_ref, o_ref):
    a = _silu(c_ref[...]).astype(BF16)
    o_ref[0] = jnp.dot(a, w_ref[0].astype(BF16), preferred_element_type=F32) + b_ref[0]


def _modulation(cc, w_mod, b_mod):
    R, D = cc.shape
    L, _, N = w_mod.shape
    tn = 1024
    return pl.pallas_call(
        _mod_kernel,
        grid=(L, N // tn),
        in_specs=[pl.BlockSpec((R, D), lambda l, j: (0, 0)),
                  pl.BlockSpec((1, D, tn), lambda l, j: (l, 0, j)),
                  pl.BlockSpec((1, 1, tn), lambda l, j: (l, 0, j))],
        out_specs=pl.BlockSpec((1, R, tn), lambda l, j: (l, 0, j)),
        out_shape=jax.ShapeDtypeStruct((L, R, N), F32),
        compiler_params=_params(("arbitrary", "arbitrary")),
        name="modulation",
    )(cc, w_mod, b_mod.reshape(L, 1, N))


def _rms_modulate(x, g_ref, m_ref=None, shift_row=0, scale_row=1):
    y = x * lax.rsqrt(jnp.mean(x * x, axis=-1, keepdims=True) + RMS_EPS) * g_ref[...]
    if m_ref is not None:
        y = y * (1.0 + m_ref[0, scale_row:scale_row + 1, :]) + m_ref[0, shift_row:shift_row + 1, :]
    return y


def _norm_kernel(x_ref, g_ref, m_ref, h_ref):
    h_ref[0] = _rms_modulate(x_ref[0], g_ref, m_ref).astype(h_ref.dtype)


def _norm(x, g, mod):
    B, T, D = x.shape
    tt = min(T, 512)
    per_sample = mod.shape[0] > 1
    rows = pl.BlockSpec((1, tt, D), lambda b, t: (b, t, 0))
    return pl.pallas_call(
        _norm_kernel,
        grid=(B, T // tt),
        in_specs=[rows, pl.BlockSpec((1, D), lambda b, t: (0, 0)),
                  pl.BlockSpec((1, N_MOD, D), (lambda b, t: (b, 0, 0)) if per_sample else (lambda b, t: (0, 0, 0)))],
        out_specs=rows, out_shape=jax.ShapeDtypeStruct((B, T, D), BF16),
        compiler_params=_params(("arbitrary", "arbitrary")),
        name="rmsnorm",
    )(x, g.reshape(1, D), mod)


def _proj_kernel(a_ref, w_ref, o_ref):
    o_ref[...] = jnp.dot(a_ref[...], w_ref[...], preferred_element_type=F32).astype(o_ref.dtype)


def _proj(a, w, out_dtype, tn):
    B, T, K = a.shape
    N = w.shape[1]
    M = B * T
    tm = min(M, 512)
    out = pl.pallas_call(
        _proj_kernel,
        grid=(N // tn, M // tm),
        in_specs=[pl.BlockSpec((tm, K), lambda j, i: (i, 0)), pl.BlockSpec((K, tn), lambda j, i: (0, j))],
        out_specs=pl.BlockSpec((tm, tn), lambda j, i: (i, j)),
        out_shape=jax.ShapeDtypeStruct((M, N), out_dtype),
        compiler_params=_params(("arbitrary", "arbitrary")),
        name="proj",
    )(a.reshape(M, K), w)
    return out.reshape(B, T, N)


def _oproj_kernel(*refs, n_in, n_experts):
    a_refs, w_refs = refs[:n_in], refs[n_in:2 * n_in]
    x_ref, m_ref, g_ref, wr_ref, o_ref, h_ref, lg_ref = refs[2 * n_in:]
    tm = x_ref.shape[1]
    sub = min(tm, OPROJ_SUB_ROWS)
    for s in range(tm // sub):
        rows = slice(s * sub, (s + 1) * sub)
        acc = jnp.dot(a_refs[0][0, rows, :], w_refs[0][...], preferred_element_type=F32)
        for a_ref, w_ref in zip(a_refs[1:], w_refs[1:]):
            acc = acc + jnp.dot(a_ref[0, rows, :], w_ref[...], preferred_element_type=F32)
        x_new = x_ref[0, rows, :] + m_ref[0, 2:3, :] * acc
        o_ref[0, rows, :] = x_new
        y = _rms_modulate(x_new, g_ref, m_ref, shift_row=3, scale_row=4)
        y_hi = y.astype(BF16)
        h_ref[0, rows, :] = y_hi
        y_lo = (y - y_hi.astype(F32)).astype(BF16)
        parts = (jnp.dot(y_hi, wr_ref[...], preferred_element_type=F32)
                 + jnp.dot(y_lo, wr_ref[...], preferred_element_type=F32))
        lg_ref[0, rows, :] = parts + pltpu.roll(parts, LANES - n_experts, 1)


def _oproj(ys, ws, x, mod, g_ffn, router_wt):
    B, T, D = x.shape
    E = router_wt.shape[0]
    tm = min(T, 512)
    per_sample = mod.shape[0] > 1
    w_r_hi = router_wt.T.astype(BF16)
    w_r_lo = (router_wt.T - w_r_hi.astype(F32)).astype(BF16)
    w_r_split = jnp.pad(jnp.concatenate([w_r_hi, w_r_lo], axis=1), ((0, 0), (0, LANES - 2 * E)))
    rows = lambda n: pl.BlockSpec((1, tm, n), lambda b, t: (b, t, 0))
    in_specs = [rows(y.shape[2]) for y in ys]
    in_specs += [pl.BlockSpec(w.shape, lambda b, t: (0, 0)) for w in ws]
    in_specs += [rows(D),
                 pl.BlockSpec((1, N_MOD, D), (lambda b, t: (b, 0, 0)) if per_sample else (lambda b, t: (0, 0, 0))),
                 pl.BlockSpec((1, D), lambda b, t: (0, 0)), pl.BlockSpec((D, LANES), lambda b, t: (0, 0))]
    x_new, h, logits = pl.pallas_call(
        functools.partial(_oproj_kernel, n_in=len(ys), n_experts=E),
        grid=(B, T // tm), in_specs=in_specs,
        out_specs=[rows(D), rows(D), rows(LANES)],
        out_shape=[jax.ShapeDtypeStruct((B, T, D), F32), jax.ShapeDtypeStruct((B, T, D), BF16),
                   jax.ShapeDtypeStruct((B, T, LANES), F32)],
        compiler_params=_params(("arbitrary", "arbitrary"), vmem_mib=56),
        name="oproj",
    )(*ys, *ws, x, mod, g_ffn.reshape(1, D), w_r_split)
    return x_new, h, jnp.transpose(logits[:, :, :E], (0, 2, 1))


def _log_sigmoid(z):
    return jnp.minimum(z, 0.0) - jnp.log(1.0 + jnp.exp(-jnp.abs(z)))


def _gla_kernel(*refs, n_chunks, use_rope, has_state):
    it = iter(refs)
    q_ref, k_ref, v_ref, g_ref, za_ref = (next(it) for _ in range(5))
    cos_ref, sin_ref = (next(it), next(it)) if use_rope else (None, None)
    aup_ref, ab_ref, gg_ref = next(it), next(it), next(it)
    s0_ref = next(it) if has_state else None
    y_ref, sfin_ref = next(it), next(it)
    qd_scr, ki_scr, ke_scr, kv_scr, dec_scr, sall_scr, st_scr = (next(it) for _ in range(7))
    C, DK = GLA_CHUNK, GLA_DK
    T = n_chunks * C
    RB = min(T, GLA_PREP_ROWS)
    cpb = RB // C

    def prepare(blk, carry):
        r0 = pl.multiple_of(blk * RB, RB)
        rows = pl.ds(r0, RB)
        z = jnp.dot(za_ref[0, rows, :].astype(BF16), aup_ref[...], preferred_element_type=F32) + ab_ref[...]
        la = _log_sigmoid(z) * (1.0 / GLA_TAU)
        in_chunk = lax.broadcasted_iota(jnp.int32, la.shape, 0) % C
        prefix = la
        step = 1
        while step < C:
            prefix = prefix + jnp.where(in_chunk >= step, pltpu.roll(prefix, step, 0), 0.0)
            step *= 2
        q = q_ref[0, rows, :] * (GLA_DK ** -0.5)
        k = k_ref[0, rows, :]
        if use_rope:
            cos, sin = cos_ref[rows, :], sin_ref[rows, :]
            even = (lax.broadcasted_iota(jnp.int32, q.shape, 1) % 2) == 0

            def rope(t):
                swapped = jnp.where(even, pltpu.roll(t, LANES - 1, 1), pltpu.roll(t, 1, 1))
                return t * cos + swapped * sin
            q, k = rope(q), rope(k)
        split = lambda t: t.reshape(cpb, C, t.shape[-1])
        la, prefix, q, k = split(la), split(prefix), split(q), split(k)
        tot = prefix[:, C - 1:C, :]
        b_f = prefix[..., :DK]
        b_b = tot[..., DK:] - prefix[..., DK:] + la[..., DK:]
        dec = jnp.exp(tot)
        c0 = pl.multiple_of(blk * cpb, cpb)
        qd_scr[pl.ds(c0, cpb)] = jnp.concatenate([q * jnp.exp(b_f), q * jnp.exp(b_b)], axis=2).astype(BF16)
        ki_f, ki_b = k * jnp.exp(-b_f), k * jnp.exp(-b_b)
        ki_scr[0, pl.ds(c0, cpb)] = ki_f.astype(BF16)
        ki_scr[1, pl.ds(c0, cpb)] = ki_b.astype(BF16)
        ke_scr[pl.ds(c0, cpb)] = jnp.concatenate([ki_f * dec[..., :DK], ki_b * dec[..., DK:]], axis=2).astype(BF16)
        dec_scr[pl.ds(c0, cpb)] = jnp.broadcast_to(dec, (cpb,) + dec_scr.shape[1:])
        return carry

    lax.fori_loop(0, T // RB, prepare, 0)

    def increment(c, carry):
        rows = pl.ds(pl.multiple_of(c * C, C), C)
        kv_scr[c] = lax.dot_general(v_ref[0, rows, :].astype(BF16), ke_scr[c], _TN, preferred_element_type=F32)
        return carry

    lax.fori_loop(0, n_chunks, increment, 0, unroll=4)

    row = lax.broadcasted_iota(jnp.int32, (C, C), 0)
    col = lax.broadcasted_iota(jnp.int32, (C, C), 1)
    keep_f, keep_b = row >= col, row <= col

    st_scr[...] = s0_ref[0, 0] if has_state else jnp.zeros(st_scr.shape, F32)

    def recur(i, carry):
        cf, cb = i, n_chunks - 1 - i
        st = st_scr[...]
        sall_scr[cf, :, :DK] = st[:, :DK].astype(BF16)
        sall_scr[cb, :, DK:] = st[:, DK:].astype(BF16)
        st_scr[:, :DK] = st[:, :DK] * dec_scr[cf, 0:1, :DK] + kv_scr[cf, :, :DK]
        st_scr[:, DK:] = st[:, DK:] * dec_scr[cb, 0:1, DK:] + kv_scr[cb, :, DK:]
        return carry

    lax.fori_loop(0, n_chunks, recur, 0)
    sfin_ref[0, 0] = st_scr[...]

    def emit(c, carry):
        rows = pl.ds(pl.multiple_of(c * C, C), C)
        qd = qd_scr[c]
        s = (jnp.where(keep_f, lax.dot_general(qd[:, :DK], ki_scr[0, c], _NT, preferred_element_type=F32), 0.0)
             + jnp.where(keep_b, lax.dot_general(qd[:, DK:], ki_scr[1, c], _NT, preferred_element_type=F32), 0.0))
        o = (jnp.dot(s.astype(BF16), v_ref[0, rows, :].astype(BF16), preferred_element_type=F32)
             + lax.dot_general(qd, sall_scr[c], _NT, preferred_element_type=F32))
        o = o * lax.rsqrt(jnp.mean(o * o, axis=-1, keepdims=True) + RMS_EPS) * gg_ref[...]
        y_ref[0, rows, :] = (o * _silu(g_ref[0, rows, :])).astype(y_ref.dtype)
        return carry

    lax.fori_loop(0, n_chunks, emit, 0, unroll=4)


def _gla(p_main, p_za, aup, ab, gg, rope_tabs, state):
    B, T, _ = p_main.shape
    H, DK, DV = GLA_HEADS, GLA_DK, GLA_DV
    use_rope, has_state = rope_tabs is not None, state is not None
    n_chunks = T // GLA_CHUNK
    in_specs = [pl.BlockSpec((1, T, DK), lambda b, h: (b, 0, h)),
                pl.BlockSpec((1, T, DK), lambda b, h: (b, 0, H + h)),
                pl.BlockSpec((1, T, DV), lambda b, h: (b, 0, H + h)),
                pl.BlockSpec((1, T, DV), lambda b, h: (b, 0, 2 * H + h)),
                pl.BlockSpec((1, T, LANES), lambda b, h: (b, 0, 0))]
    args = [p_main, p_main, p_main, p_main, p_za]
    if use_rope:
        in_specs += [pl.BlockSpec((T, DK), lambda b, h: (0, 0))] * 2
        args += list(rope_tabs)
    in_specs += [pl.BlockSpec((LANES, 2 * DK), lambda b, h: (0, h)),
                 pl.BlockSpec((1, 2 * DK), lambda b, h: (0, h)),
                 pl.BlockSpec((1, DV), lambda b, h: (0, 0))]
    args += [aup, ab, gg]
    st_spec = pl.BlockSpec((1, 1, DV, 2 * DK), lambda b, h: (b, h, 0, 0))
    if has_state:
        in_specs.append(st_spec)
        args.append(state)
    return pl.pallas_call(
        functools.partial(_gla_kernel, n_chunks=n_chunks, use_rope=use_rope, has_state=has_state),
        grid=(B, H), in_specs=in_specs,
        out_specs=[pl.BlockSpec((1, T, DV), lambda b, h: (b, 0, h)), st_spec],
        out_shape=[jax.ShapeDtypeStruct((B, T, H * DV), BF16), jax.ShapeDtypeStruct((B, H, DV, 2 * DK), F32)],
        scratch_shapes=[pltpu.VMEM((n_chunks, GLA_CHUNK, 2 * DK), BF16), pltpu.VMEM((2, n_chunks, GLA_CHUNK, DK), BF16),
                        pltpu.VMEM((n_chunks, GLA_CHUNK, 2 * DK), BF16),
                        pltpu.VMEM((n_chunks, DV, 2 * DK), F32), pltpu.VMEM((n_chunks, 8, 2 * DK), F32),
                        pltpu.VMEM((n_chunks, DV, 2 * DK), BF16), pltpu.VMEM((DV, 2 * DK), F32)],
        compiler_params=_params(("arbitrary", "arbitrary")),
        name="gla",
    )(*args)


def _gelu_tanh(z):
    return 0.5 * z * (1.0 + jnp.tanh(np.sqrt(2.0 / np.pi) * (z + 0.044715 * (z * z * z))))


def _lru_kernel(x_ref, gt_ref, cw_ref, cb_ref, wa_ref, ba_ref, wi_ref, bi_ref, lam_ref, h0f_ref, h0b_ref,
                y_ref, hf_ref, hb_ref, a_scr, bx_scr, hl_scr, ap_scr, *, T):
    L = T // LRU_SEGMENTS
    P = L + LRU_PITCH_PAD
    x = x_ref[0]
    row = lax.broadcasted_iota(jnp.int32, x.shape, 0)

    def from_earlier(v, d, fill):
        return jnp.where(row >= d, pltpu.roll(v, d, 0), fill)

    def from_later(v, d, fill):
        return jnp.where(row < T - d, pltpu.roll(v, T - d, 0), fill)

    xc = (cw_ref[0:1, :] * from_earlier(x, 1, 0.0) + cw_ref[1:2, :] * x
          + cw_ref[2:3, :] * from_later(x, 1, 0.0) + cw_ref[3:4, :] * from_later(x, 2, 0.0) + cb_ref[...])
    xcb = xc.astype(BF16)

    def sigmoid(z):
        return 0.5 * jnp.tanh(0.5 * z) + 0.5

    for d in range(2):
        r = sigmoid(jnp.dot(xcb, wa_ref[d, 0], preferred_element_type=F32) + ba_ref[d])
        gate_i = sigmoid(jnp.dot(xcb, wi_ref[d, 0], preferred_element_type=F32) + bi_ref[d])
        neg_lam = -lam_ref[d]
        softplus = jnp.maximum(neg_lam, 0.0) + jnp.log(1.0 + jnp.exp(-jnp.abs(neg_lam)))
        a = jnp.exp((-LRU_C) * r * softplus)
        bx = jnp.sqrt(1.0 - a * a) * (gate_i * xc)
        for s in range(LRU_SEGMENTS):
            a_scr[d, s * P:s * P + L, :] = a[s * L:(s + 1) * L]
            bx_scr[d, s * P:s * P + L, :] = bx[s * L:(s + 1) * L]

    def step(i, carry):
        h_f, p_f, h_b, p_b = carry
        at_f = pl.ds(i, LRU_SEGMENTS, stride=P)
        at_b = pl.ds(L - 1 - i, LRU_SEGMENTS, stride=P)
        a_f, a_b = a_scr[0, at_f, :], a_scr[1, at_b, :]
        h_f = a_f * h_f + bx_scr[0, at_f, :]
        h_b = a_b * h_b + bx_scr[1, at_b, :]
        p_f, p_b = a_f * p_f, a_b * p_b
        hl_scr[0, at_f, :] = h_f
        hl_scr[1, at_b, :] = h_b
        ap_scr[0, at_f, :] = p_f
        ap_scr[1, at_b, :] = p_b
        return h_f, p_f, h_b, p_b

    zeros, ones = jnp.zeros((LRU_SEGMENTS, x.shape[1]), F32), jnp.ones((LRU_SEGMENTS, x.shape[1]), F32)
    h_f, p_f, h_b, p_b = lax.fori_loop(0, L, step, (zeros, ones, zeros, ones), unroll=8)

    enter_f, enter_b = [h0f_ref[0]], [h0b_ref[0]]
    for s in range(LRU_SEGMENTS):
        enter_f.append(h_f[s:s + 1] + p_f[s:s + 1] * enter_f[-1])
        t = LRU_SEGMENTS - 1 - s
        enter_b.append(h_b[t:t + 1] + p_b[t:t + 1] * enter_b[-1])
    hf_ref[0] = enter_f[-1]
    hb_ref[0] = enter_b[-1]
    for s in range(LRU_SEGMENTS):
        seg = slice(s * P, s * P + L)
        h = (hl_scr[0, seg, :] + ap_scr[0, seg, :] * enter_f[s]
             + hl_scr[1, seg, :] + ap_scr[1, seg, :] * enter_b[LRU_SEGMENTS - 1 - s])
        y_ref[0, s * L:(s + 1) * L, :] = (h * _gelu_tanh(gt_ref[0, s * L:(s + 1) * L, :])).astype(y_ref.dtype)


def _lru(p_main, cw, cb, wa, ba, wi, bi, lam, states):
    B, T, _ = p_main.shape
    W, NB, BD = LRU_WIDTH, LRU_BLOCKS, LRU_BLOCK_DIM
    x_blk0 = (2 * GLA_KEY_WIDTH + 2 * GLA_VAL_WIDTH) // BD
    vec = lambda n: pl.BlockSpec((n, 1, BD), lambda b, j: (0, 0, j))
    mat = pl.BlockSpec((2, 1, BD, BD), lambda b, j: (0, j, 0, 0))
    st_spec = pl.BlockSpec((1, 1, BD), lambda b, j: (b, 0, j))
    st_shape = jax.ShapeDtypeStruct((B, 1, W), F32)
    y, hf, hb = pl.pallas_call(
        functools.partial(_lru_kernel, T=T),
        grid=(B, NB),
        in_specs=[pl.BlockSpec((1, T, BD), lambda b, j: (b, 0, x_blk0 + j)),
                  pl.BlockSpec((1, T, BD), lambda b, j: (b, 0, x_blk0 + NB + j)),
                  pl.BlockSpec((LRU_CONV, BD), lambda b, j: (0, j)),
                  pl.BlockSpec((1, BD), lambda b, j: (0, j)),
                  mat, vec(2), mat, vec(2), vec(2), st_spec, st_spec],
        out_specs=[pl.BlockSpec((1, T, BD), lambda b, j: (b, 0, j)), st_spec, st_spec],
        out_shape=[jax.ShapeDtypeStruct((B, T, W), BF16), st_shape, st_shape],
        scratch_shapes=[pltpu.VMEM((2, LRU_SEGMENTS * (T // LRU_SEGMENTS + LRU_PITCH_PAD), BD), F32)] * 4,
        compiler_params=_params(("arbitrary", "arbitrary")),
        name="rglru",
    )(p_main, p_main, cw, cb.reshape(1, W), wa, ba.reshape(2, 1, W), wi, bi.reshape(2, 1, W),
      lam.reshape(2, 1, W), *states)
    return y, (hf, hb)


def _na_tile(i, n_tiles, rows):
    r0 = i * NA_QROWS
    ks = min(max(r0 - NA_KH // 2, 0), rows - NA_KROWS)
    cls = 0 if i == 0 else (2 if i == n_tiles - 1 else 1)
    return r0, ks, cls


def _na_bias_tables(rpb, rows):
    H = rpb.shape[0]
    qc = np.arange(GRID_W)[:, None]
    kc = np.arange(GRID_W)[None, :]
    col_start = np.clip(qc - NA_KW // 2, 0, GRID_W - NA_KW)
    col_valid = (kc >= col_start) & (kc < col_start + NA_KW)
    dc = np.clip(kc - qc, -(NA_KW - 1), NA_KW - 1) + NA_KW - 1
    pick = (dc[None] == np.arange(2 * NA_KW - 1)[:, None, None]).astype(np.float32)
    band = jnp.einsum('hrd,dqk->hrqk', rpb.astype(F32), jnp.asarray(pick), precision=HIGHEST)
    band = jnp.where(col_valid, band, NEG_BIG)
    masked = jnp.full((H, GRID_W, GRID_W), NEG_BIG, F32)
    n_tiles = rows // NA_QROWS
    tabs = []
    for i in (0, 1, n_tiles - 1):
        r0, ks, _ = _na_tile(i, n_tiles, rows)
        per_qrow = []
        for qr in range(r0, r0 + NA_QROWS):
            row_start = min(max(qr - NA_KH // 2, 0), rows - NA_KH)
            blocks = [band[:, kr - qr + NA_KH - 1] if row_start <= kr < row_start + NA_KH else masked
                      for kr in range(ks, ks + NA_KROWS)]
            per_qrow.append(jnp.stack(blocks, axis=2).reshape(H, GRID_W, NA_KROWS * GRID_W))
        tabs.append(jnp.concatenate(per_qrow, axis=1))
    return jnp.stack(tabs, axis=1) * LOG2_E


def _na_kernel(*refs, rows, need_ctx):
    q_ref, k_ref, v_ref, qc_ref, kc_ref, vc_ref, tab_ref = refs[:7]
    y_ref = refs[7]
    scale = NA_HEAD_DIM ** -0.5 * LOG2_E
    kc, vc = kc_ref[0], vc_ref[0]
    n_tiles = rows // NA_QROWS
    nq, nk = NA_QROWS * GRID_W, NA_KROWS * GRID_W
    for i in range(n_tiles):
        r0, ks, cls = _na_tile(i, n_tiles, rows)
        q = q_ref[0, r0 * GRID_W:r0 * GRID_W + nq, :]
        kt = k_ref[0, ks * GRID_W:ks * GRID_W + nk, :]
        vt = v_ref[0, ks * GRID_W:ks * GRID_W + nk, :]
        s_loc = lax.dot_general(q, kt, _NT, preferred_element_type=F32) * scale + tab_ref[0, cls]
        s_ctx = lax.dot_general(q, kc, _NT, preferred_element_type=F32) * scale
        m = jnp.maximum(jnp.max(s_loc, axis=-1, keepdims=True), jnp.max(s_ctx, axis=-1, keepdims=True))
        p_loc, p_ctx = jnp.exp2(s_loc - m), jnp.exp2(s_ctx - m)
        denom = jnp.sum(p_loc, axis=-1, keepdims=True) + jnp.sum(p_ctx, axis=-1, keepdims=True)
        o = (jnp.dot(p_loc.astype(BF16), vt, preferred_element_type=F32)
             + jnp.dot(p_ctx.astype(BF16), vc, preferred_element_type=F32))
        y_ref[0, r0 * GRID_W:r0 * GRID_W + nq, :] = (o / denom).astype(y_ref.dtype)
    if need_ctx:
        yc_ref = refs[8]
        s = lax.dot_general(qc_ref[0], kc, _NT, preferred_element_type=F32) * scale
        p = jnp.exp2(s - jnp.max(s, axis=-1, keepdims=True))
        o = jnp.dot(p.astype(BF16), vc, preferred_element_type=F32) / jnp.sum(p, axis=-1, keepdims=True)
        yc_ref[0] = o.astype(yc_ref.dtype)


def _na(qkv_lat, qkv_ctx, tabs, need_ctx):
    B, T, _ = qkv_lat.shape
    Tc = qkv_ctx.shape[1]
    H, HD = NA_HEADS, NA_HEAD_DIM
    blk = lambda t, off: pl.BlockSpec((1, t, HD), lambda h, b: (b, 0, off + h))
    out_specs = [pl.BlockSpec((1, T, HD), lambda h, b: (b, 0, h))]
    out_shape = [jax.ShapeDtypeStruct((B, T, H * HD), BF16)]
    if need_ctx:
        out_specs.append(pl.BlockSpec((1, Tc, HD), lambda h, b: (b, 0, h)))
        out_shape.append(jax.ShapeDtypeStruct((B, Tc, H * HD), BF16))
    res = pl.pallas_call(
        functools.partial(_na_kernel, rows=T // GRID_W, need_ctx=need_ctx),
        grid=(H, B),
        in_specs=[blk(T, 0), blk(T, H), blk(T, 2 * H), blk(Tc, 0), blk(Tc, H), blk(Tc, 2 * H),
                  pl.BlockSpec((1,) + tabs.shape[1:], lambda h, b: (h, 0, 0, 0))],
        out_specs=out_specs, out_shape=out_shape,
        compiler_params=_params(("arbitrary", "arbitrary")),
        name="natten",
    )(qkv_lat, qkv_lat, qkv_lat, qkv_ctx, qkv_ctx, qkv_ctx, tabs)
    return (res[1] if need_ctx else None), res[0]


def _route_kernel(lg_ref, pos_ref, aff_ref, lo_ref, *, cap):
    lg = lg_ref[0]
    E, N = lg.shape
    ex = jnp.exp(lg - jnp.max(lg, axis=0, keepdims=True))
    aff = ex / jnp.sum(ex, axis=0, keepdims=True)
    aff_ref[0] = aff
    bits = pltpu.bitcast(aff, jnp.int32)
    thr = jnp.zeros((E, 1), jnp.int32)
    for bit in range(29, -1, -1):
        cand = thr | (1 << bit)
        cnt = jnp.sum(jnp.where(bits >= cand, 1.0, 0.0), axis=1, keepdims=True)
        thr = jnp.where(cnt >= cap, cand, thr)
    above = bits > thr
    tied = bits == thr
    need = cap - jnp.sum(jnp.where(above, 1.0, 0.0), axis=1, keepdims=True)
    pc = min(PREFIX_CHUNK, N)
    before = jnp.where(lax.broadcasted_iota(jnp.int32, (pc, pc), 0) < lax.broadcasted_iota(jnp.int32, (pc, pc), 1),
                       1.0, 0.0).astype(BF16)
    tied_seen = jnp.zeros((E, 1), F32)
    sel_seen = jnp.zeros((E, 1), F32)
    tile_lane = lax.broadcasted_iota(jnp.int32, (E, LANES), 1)
    lo = jnp.where(tile_lane == N // pc, float(cap), 0.0)
    for c in range(N // pc):
        sl = slice(c * pc, (c + 1) * pc)
        lo = jnp.where(tile_lane == c, sel_seen, lo)
        tied_c = jnp.where(tied[:, sl], 1.0, 0.0)
        tied_before = jnp.dot(tied_c.astype(BF16), before, preferred_element_type=F32) + tied_seen
        sel_c = jnp.where(above[:, sl], 1.0, jnp.where(tied_before < need, tied_c, 0.0))
        sel_before = jnp.dot(sel_c.astype(BF16), before, preferred_element_type=F32) + sel_seen
        pos_ref[0, :, sl] = jnp.where(sel_c > 0.0, sel_before, -1.0)
        tied_seen = tied_seen + jnp.sum(tied_c, axis=1, keepdims=True)
        sel_seen = sel_seen + jnp.sum(sel_c, axis=1, keepdims=True)
    lo_ref[0] = lo.astype(jnp.int32)


def _route(logits_t, cap):
    B, E, N = logits_t.shape
    spec = pl.BlockSpec((1, E, N), lambda b: (b, 0, 0))
    lo_spec = pl.BlockSpec((1, E, LANES), lambda b: (b, 0, 0))
    pos, aff, lo = pl.pallas_call(
        functools.partial(_route_kernel, cap=cap),
        grid=(B,), in_specs=[spec], out_specs=[spec, spec, lo_spec],
        out_shape=[jax.ShapeDtypeStruct((B, E, N), F32)] * 2 + [jax.ShapeDtypeStruct((B, E, LANES), jnp.int32)],
        compiler_params=_params(("arbitrary",)),
        name="route",
    )(logits_t)
    n_bounds = N // min(PREFIX_CHUNK, N) + 1
    return pos, aff, lo[:, :, :n_bounds].reshape(-1)


def _window_count(lo_ref, base, experts, tile, n_bounds, win):
    starts, n_win = [], 0
    for e in experts:
        first = lo_ref[(base + e) * n_bounds + tile]
        end = lo_ref[(base + e) * n_bounds + tile + 1]
        start = (first // SLOT_ALIGN) * SLOT_ALIGN
        starts.append(start)
        n_win = jnp.maximum(n_win, (end - start + win - 1) // win)
    return starts, n_win


def _gather_kernel(lo_ref, pos_ref, h_ref, xs_ref, *, cap, win, tn):
    EG, N = pos_ref.shape[1], pos_ref.shape[2]
    n_bounds = N // tn + 1
    base = pl.program_id(0) * (EG * pl.num_programs(1)) + pl.program_id(1) * EG
    xs_ref[...] = jnp.zeros(xs_ref.shape, xs_ref.dtype)
    slot_iota = lax.broadcasted_iota(jnp.int32, (win, tn), 0)
    for j in range(N // tn):
        h_tile = h_ref[0, j * tn:(j + 1) * tn, :]
        starts, n_win = _window_count(lo_ref, base, range(EG), j, n_bounds, win)

        def window(w, carry):
            pieces, offs = [], []
            for e in range(EG):
                first = starts[e] + w * win
                off = jnp.minimum(first, cap - win)
                prow = pos_ref[0, e:e + 1, j * tn:(j + 1) * tn]
                hit = ((slot_iota + off).astype(F32) == prow) & (prow >= first.astype(F32))
                pieces.append(jnp.where(hit, 1.0, 0.0).astype(BF16))
                offs.append(pl.multiple_of(off, SLOT_ALIGN))
            got = jnp.dot(jnp.concatenate(pieces, axis=0), h_tile, preferred_element_type=F32)
            for e in range(EG):
                xs_ref[0, e, pl.ds(offs[e], win), :] += got[e * win:(e + 1) * win].astype(xs_ref.dtype)
            return carry

        lax.fori_loop(0, n_win, window, 0)


def _gather(lo, pos, h, cap):
    B, E, N = pos.shape
    D = h.shape[2]
    EG = 8
    tn, win = min(PREFIX_CHUNK, N), min(SLOT_WINDOW, cap)
    return pl.pallas_call(
        functools.partial(_gather_kernel, cap=cap, win=win, tn=tn),
        grid_spec=pltpu.PrefetchScalarGridSpec(
            num_scalar_prefetch=1, grid=(B, E // EG),
            in_specs=[pl.BlockSpec((1, EG, N), lambda b, g, lo: (b, g, 0)),
                      pl.BlockSpec((1, N, D), lambda b, g, lo: (b, 0, 0))],
            out_specs=pl.BlockSpec((1, EG, cap, D), lambda b, g, lo: (b, g, 0, 0))),
        out_shape=jax.ShapeDtypeStruct((B, E, cap, D), BF16),
        compiler_params=_params(("arbitrary", "arbitrary")),
        name="moe_gather",
    )(lo, pos, h)


def _ffn_kernel(xs_ref, wg_ref, wu_ref, wd_ref, y_ref):
    bb, _, cap, D = xs_ref.shape
    xs = xs_ref[:, 0].reshape(bb * cap, D)
    hid = (_silu(jnp.dot(xs, wg_ref[0, 0], preferred_element_type=F32))
           * jnp.dot(xs, wu_ref[0, 0], preferred_element_type=F32))
    y = jnp.dot(hid.astype(BF16), wd_ref[0, 0], preferred_element_type=F32)
    y_ref[:, 0] = y.reshape(bb, cap, D).astype(y_ref.dtype)


def _ffn(xs, wg, wu, wd, layer):
    B, E, cap, D = xs.shape
    F = wg.shape[3]
    bb = max(1, min(B, FFN_ROWS // cap))
    tok = pl.BlockSpec((bb, 1, cap, D), lambda e, b: (b, e, 0, 0))
    return pl.pallas_call(
        _ffn_kernel,
        grid=(E, B // bb),
        in_specs=[tok, pl.BlockSpec((1, 1, D, F), lambda e, b: (layer, e, 0, 0)),
                  pl.BlockSpec((1, 1, D, F), lambda e, b: (layer, e, 0, 0)),
                  pl.BlockSpec((1, 1, F, D), lambda e, b: (layer, e, 0, 0))],
        out_specs=tok,
        out_shape=jax.ShapeDtypeStruct((B, E, cap, D), BF16),
        compiler_params=_params(("arbitrary", "arbitrary")),
        name="moe_ffn",
    )(xs, wg, wu, wd)


def _combine_kernel(*refs, cap, win, keep_x, next_modulated):
    it = iter(refs)
    lo_ref, pos_ref, aff_ref, y_ref, x_ref, m_ref, gn_ref = (next(it) for _ in range(7))
    mn_ref = next(it) if next_modulated else None
    o_ref = next(it) if keep_x else None
    hn_ref, ystack_ref, acc_ref = next(it), next(it), next(it)
    E = y_ref.shape[1]
    tn = pos_ref.shape[1]
    n_bounds = pl.num_programs(1) + 1
    j = pl.program_id(1)
    starts, n_win = _window_count(lo_ref, pl.program_id(0) * E, range(E), j, n_bounds, win)
    col = lax.broadcasted_iota(jnp.int32, (E, E * win), 1)
    spread = jnp.where(col // win == lax.broadcasted_iota(jnp.int32, (E, E * win), 0), 1.0, 0.0).astype(BF16)
    slot = jnp.dot(pos_ref[0].astype(BF16), spread, preferred_element_type=F32)
    gate = jnp.dot(aff_ref[0].astype(BF16), spread, preferred_element_type=F32)
    col1 = lax.broadcasted_iota(jnp.int32, (1, E * win), 1)
    col_expert, col_in_win = col1 // win, col1 % win
    acc_ref[...] = jnp.zeros(acc_ref.shape, F32)

    def window(w, carry):
        first_col = jnp.zeros((1, E * win), jnp.int32)
        want_col = jnp.zeros((1, E * win), jnp.int32)
        for e in range(E):
            first = starts[e] + w * win
            off = jnp.minimum(first, cap - win)
            ystack_ref[e * win:(e + 1) * win, :] = y_ref[0, e, pl.ds(pl.multiple_of(off, SLOT_ALIGN), win), :]
            first_col = jnp.where(col_expert == e, first, first_col)
            want_col = jnp.where(col_expert == e, off + col_in_win, want_col)
        hit = (slot == want_col.astype(F32)) & (slot >= first_col.astype(F32))
        acc_ref[...] += jnp.dot(jnp.where(hit, gate, 0.0).astype(BF16), ystack_ref[...], preferred_element_type=F32)
        return carry

    lax.fori_loop(0, n_win, window, 0)
    x_new = x_ref[0] + m_ref[0, 5:6, :] * acc_ref[...]
    if keep_x:
        o_ref[0] = x_new
    hn_ref[0] = _rms_modulate(x_new, gn_ref, mn_ref).astype(hn_ref.dtype)


def _combine(lo, pos, aff, y, x, mod, g_next, mod_next):
    B, E, cap, D = y.shape
    N = x.shape[1]
    tn, win = min(PREFIX_CHUNK, N), min(SLOT_WINDOW, cap)
    keep_x = next_modulated = mod_next is not None
    tok = pl.BlockSpec((1, tn, E), lambda b, t, lo: (b, t, 0))
    rows = pl.BlockSpec((1, tn, D), lambda b, t, lo: (b, t, 0))
    mod_spec = lambda m: pl.BlockSpec((1, N_MOD, D),
                                      (lambda b, t, lo: (b, 0, 0)) if m.shape[0] > 1 else (lambda b, t, lo: (0, 0, 0)))
    in_specs = [tok, tok, pl.BlockSpec((1, E, cap, D), lambda b, t, lo: (b, 0, 0, 0)), rows, mod_spec(mod),
                pl.BlockSpec((1, D), lambda b, t, lo: (0, 0))]
    args = [lo, jnp.transpose(pos, (0, 2, 1)), jnp.transpose(aff, (0, 2, 1)), y, x, mod, g_next.reshape(1, D)]
    out_specs, out_shape = [rows], [jax.ShapeDtypeStruct((B, N, D), BF16 if next_modulated else F32)]
    if next_modulated:
        in_specs.append(mod_spec(mod_next))
        args.append(mod_next)
        out_specs, out_shape = [rows] + out_specs, [jax.ShapeDtypeStruct((B, N, D), F32)] + out_shape
    res = pl.pallas_call(
        functools.partial(_combine_kernel, cap=cap, win=win, keep_x=keep_x, next_modulated=next_modulated),
        grid_spec=pltpu.PrefetchScalarGridSpec(
            num_scalar_prefetch=1, grid=(B, N // tn), in_specs=in_specs, out_specs=out_specs,
            scratch_shapes=[pltpu.VMEM((E * win, D), BF16), pltpu.VMEM((tn, D), F32)]),
        out_shape=out_shape,
        compiler_params=_params(("arbitrary", "arbitrary"), vmem_mib=56),
        name="moe_combine",
    )(*args)
    return (res[0], res[1]) if keep_x else (None, res[0])


def _moe(x, h, logits_t, mod, wg, wu, wd, layer, g_next, mod_next):
    N = x.shape[1]
    cap = EC_CAPACITY_FACTOR * N // N_EXPERTS
    assert cap <= 256, "slot ids must stay exact in bf16"
    pos, aff, lo = _route(logits_t, cap)
    y = _ffn(_gather(lo, pos, h, cap), wg, wu, wd, layer)
    return _combine(lo, pos, aff, y, x, mod, g_next, mod_next)


def _rope_tables(n_tok):
    t = jnp.arange(n_tok)
    rows = (t // GRID_W).astype(F32)
    cols = (t % GRID_W).astype(F32)
    n_freq = GLA_DK // 4
    inv = ROPE_THETA ** (-jnp.arange(n_freq, dtype=F32) / n_freq)
    ang = jnp.concatenate([rows[:, None] * inv, cols[:, None] * inv], axis=-1)
    cos = jnp.repeat(jnp.cos(ang), 2, axis=-1)
    sin = jnp.repeat(jnp.sin(ang), 2, axis=-1) * jnp.tile(jnp.asarray([-1.0, 1.0], F32), GLA_DK // 2)
    return cos, sin


def _rec_mixer(h_ctx, h_lat, w_in, alpha_up, alpha_b, gla_g, conv_w, conv_b, w_a, b_a, w_i, b_i, lam, rope_tabs):
    za0 = 2 * GLA_KEY_WIDTH + 2 * GLA_VAL_WIDTH
    za1 = za0 + 2 * GLA_RANK
    w_main = jnp.concatenate([w_in[:, :za0], w_in[:, za1:]], axis=1).astype(BF16)
    w_za = jnp.pad(w_in[:, za0:za1], ((0, 0), (0, LANES - 2 * GLA_RANK))).astype(BF16)
    up = alpha_up.reshape(2, GLA_RANK, GLA_HEADS, 1, GLA_DK)
    zero = jnp.zeros_like(up[0])
    aup = jnp.concatenate([jnp.concatenate([up[0], zero], axis=2), jnp.concatenate([zero, up[1]], axis=2)], axis=0)
    aup = jnp.pad(aup.reshape(2 * GLA_RANK, 2 * GLA_KEY_WIDTH), ((0, LANES - 2 * GLA_RANK), (0, 0))).astype(BF16)
    ab = jnp.transpose(alpha_b.reshape(2, GLA_HEADS, GLA_DK), (1, 0, 2)).reshape(1, 2 * GLA_KEY_WIDTH)
    gg = gla_g.reshape(1, GLA_DV)
    wa, wi = w_a.astype(BF16), w_i.astype(BF16)
    outs = []
    gla_state, lru_state = None, None
    for h, tabs in ((h_ctx, None), (h_lat, rope_tabs)):
        B = h.shape[0]
        p_main = _proj(h, w_main, F32, tn=2560)
        p_za = _proj(h, w_za, F32, tn=LANES)
        y_gla, gla_state = _gla(p_main, p_za, aup, ab, gg, tabs, gla_state)
        if lru_state is None:
            lru_state = (jnp.zeros((B, 1, LRU_WIDTH), F32),) * 2
        y_lru, lru_state = _lru(p_main, conv_w, conv_b, wa, b_a, wi, b_i, lam, lru_state)
        outs.append([y_gla, y_lru])
    return outs


def kernel(x, c, ctx, c_ctx, w_mod, b_mod, norm_mix_g, norm_ffn_g, w_out, router_w, exp_w_gate, exp_w_up, exp_w_down, rec_w_in, gla_alpha_up, gla_alpha_b, gla_norm_g, lru_conv_w, lru_conv_b, lru_w_a, lru_b_a, lru_w_i, lru_b_i, lru_lambda, na_w_qkv, na_rpb, norm_f_g):
    B, T, D = x.shape
    pad_rows = (-(B + 1)) % 8
    cc = jnp.concatenate([c, c_ctx[None, :], jnp.zeros((pad_rows, D), F32)], axis=0)
    m_all = _modulation(cc, w_mod, b_mod)
    rope_tabs = _rope_tables(T)
    exp_wg, exp_wu, exp_wd = exp_w_gate.astype(BF16), exp_w_up.astype(BF16), exp_w_down.astype(BF16)
    mods_lat = [m_all[l, :B].reshape(B, N_MOD, D) for l in range(DEPTH)]
    mods_ctx = [m_all[l, B:B + 1].reshape(1, N_MOD, D) for l in range(DEPTH)]
    h_lat = _norm(x, norm_mix_g[0], mods_lat[0])
    h_ctx = _norm(ctx, norm_mix_g[0], mods_ctx[0])
    for l in range(DEPTH):
        last = l == DEPTH - 1
        i = l // 2
        m_lat, m_ctx = mods_lat[l], mods_ctx[l]
        w_o = w_out[l].astype(BF16)
        if l % 2 == 0:
            ys_ctx, ys_lat = _rec_mixer(h_ctx, h_lat, rec_w_in[i], gla_alpha_up[i], gla_alpha_b[i], gla_norm_g[i],
                                        lru_conv_w[i], lru_conv_b[i], lru_w_a[i], lru_b_a[i], lru_w_i[i], lru_b_i[i],
                                        lru_lambda[i], rope_tabs)
            ws = [w_o[:GLA_VAL_WIDTH], w_o[GLA_VAL_WIDTH:]]
        else:
            w_qkv = na_w_qkv[i].astype(BF16)
            qkv_lat = _proj(h_lat, w_qkv, BF16, tn=3072)
            qkv_ctx = _proj(h_ctx, w_qkv, BF16, tn=3072)
            y_ctx, y_lat = _na(qkv_lat, qkv_ctx, _na_bias_tables(na_rpb[i], T // GRID_W), not last)
            ys_ctx, ys_lat, ws = [y_ctx], [y_lat], [w_o]
        moe_w = (exp_wg, exp_wu, exp_wd, l)
        router_wt = router_w[l].T
        x, h_ffn, logits_t = _oproj(ys_lat, ws, x, m_lat, norm_ffn_g[l], router_wt)
        if last:
            _, out = _moe(x, h_ffn, logits_t, m_lat, *moe_w, norm_f_g, None)
            return out
        x, h_lat = _moe(x, h_ffn, logits_t, m_lat, *moe_w, norm_mix_g[l + 1], mods_lat[l + 1])
        ctx, h_ffn, logits_t = _oproj(ys_ctx, ws, ctx, m_ctx, norm_ffn_g[l], router_wt)
        ctx, h_ctx = _moe(ctx, h_ffn, logits_t, m_ctx, *moe_w, norm_mix_g[l + 1], mods_ctx[l + 1])
```

```python
import functools

import numpy as np
import jax
import jax.numpy as jnp
from jax import lax
from jax.experimental import pallas as pl
from jax.experimental.pallas import tpu as pltpu

D_MODEL = 2048
DEPTH = 4
GRID_W = 64
RMS_EPS = 1e-6
N_MOD = 6
ROPE_THETA = 10000.0
GLA_HEADS = 4
GLA_DK = D_MODEL // 16
GLA_DV = D_MODEL // 8
GLA_KEY_WIDTH = GLA_HEADS * GLA_DK
GLA_VAL_WIDTH = GLA_HEADS * GLA_DV
GLA_RANK = 16
GLA_TAU = 16.0
GLA_CHUNK = 64
LRU_WIDTH = D_MODEL // 2
LRU_BLOCKS = 8
LRU_BLOCK_DIM = LRU_WIDTH // LRU_BLOCKS
LRU_CONV = 4
LRU_C = 8.0
NA_HEADS = 16
NA_HEAD_DIM = D_MODEL // NA_HEADS
NA_KH = 8
NA_KW = 16
N_EXPERTS = 16
EC_CAPACITY_FACTOR = 2
EXPERT_FF = D_MODEL // 2

F32 = jnp.float32
BF16 = jnp.bfloat16
HIGHEST = lax.Precision.HIGHEST
LANES = 128
NEG_BIG = -1e30
LOG2_E = float(np.log2(np.e))
NA_QROWS = 4
NA_KROWS = NA_QROWS + NA_KH - 1
PREFIX_CHUNK = 256
SLOT_WINDOW = 64
OPROJ_SUB_ROWS = 256
GLA_EMIT_CHUNKS = 4
GLA_PREP_ROWS = 512
LRU_SEGMENTS = 8
LRU_PITCH_PAD = 8
FFN_ROWS = 512
SLOT_ALIGN = 16

_NT = (((1,), (1,)), ((), ()))
_TN = (((0,), (0,)), ((), ()))


def _params(sem, vmem_mib=48):
    return pltpu.CompilerParams(dimension_semantics=sem, vmem_limit_bytes=vmem_mib * 2 ** 20)


def _aligned(index, multiple):
    return index if isinstance(index, int) else pl.multiple_of(index, multiple)


def _sigmoid(z):
    return 1.0 / (1.0 + jnp.exp(-z))


def _silu(z):
    return z * _sigmoid(z)


def _mod_kernel(c_ref, w_ref, b_ref, o_ref):
    a = _silu(c_ref[...]).astype(BF16)
    o_ref[0] = jnp.dot(a, w_ref[0].astype(BF16), preferred_element_type=F32) + b_ref[0]


def _modulation(cc, w_mod, b_mod):
    R, D = cc.shape
    L, _, N = w_mod.shape
    tn = 1024
    return pl.pallas_call(
        _mod_kernel,
        grid=(L, N // tn),
        in_specs=[pl.BlockSpec((R, D), lambda l, j: (0, 0)),
                  pl.BlockSpec((1, D, tn), lambda l, j: (l, 0, j)),
                  pl.BlockSpec((1, 1, tn), lambda l, j: (l, 0, j))],
        out_specs=pl.BlockSpec((1, R, tn), lambda l, j: (l, 0, j)),
        out_shape=jax.ShapeDtypeStruct((L, R, N), F32),
        compiler_params=_params(("arbitrary", "arbitrary")),
        name="modulation",
    )(cc, w_mod, b_mod.reshape(L, 1, N))


def _rms_modulate(x, g_ref, m_ref=None, shift_row=0, scale_row=1):
    y = x * lax.rsqrt(jnp.mean(x * x, axis=-1, keepdims=True) + RMS_EPS) * g_ref[...]
    if m_ref is not None:
        y = y * (1.0 + m_ref[0, scale_row:scale_row + 1, :]) + m_ref[0, shift_row:shift_row + 1, :]
    return y


def _norm_kernel(x_ref, g_ref, m_ref, h_ref):
    h_ref[0] = _rms_modulate(x_ref[0], g_ref, m_ref).astype(h_ref.dtype)


def _norm(x, g, mod):
    B, T, D = x.shape
    tt = min(T, 512)
    per_sample = mod.shape[0] > 1
    rows = pl.BlockSpec((1, tt, D), lambda b, t: (b, t, 0))
    return pl.pallas_call(
        _norm_kernel,
        grid=(B, T // tt),
        in_specs=[rows, pl.BlockSpec((1, D), lambda b, t: (0, 0)),
                  pl.BlockSpec((1, N_MOD, D), (lambda b, t: (b, 0, 0)) if per_sample else (lambda b, t: (0, 0, 0)))],
        out_specs=rows, out_shape=jax.ShapeDtypeStruct((B, T, D), BF16),
        compiler_params=_params(("arbitrary", "arbitrary")),
        name="rmsnorm",
    )(x, g.reshape(1, D), mod)


def _proj_kernel(a_ref, w_ref, o_ref):
    o_ref[...] = jnp.dot(a_ref[...], w_ref[...], preferred_element_type=F32).astype(o_ref.dtype)


def _proj(a, w, out_dtype, tn):
    B, T, K = a.shape
    N = w.shape[1]
    M = B * T
    tm = min(M, 512)
    out = pl.pallas_call(
        _proj_kernel,
        grid=(N // tn, M // tm),
        in_specs=[pl.BlockSpec((tm, K), lambda j, i: (i, 0)), pl.BlockSpec((K, tn), lambda j, i: (0, j))],
        out_specs=pl.BlockSpec((tm, tn), lambda j, i: (i, j)),
        out_shape=jax.ShapeDtypeStruct((M, N), out_dtype),
        compiler_params=_params(("arbitrary", "arbitrary")),
        name="proj",
    )(a.reshape(M, K), w)
    return out.reshape(B, T, N)


def _oproj_kernel(*refs, n_in, n_experts):
    a_refs, w_refs = refs[:n_in], refs[n_in:2 * n_in]
    x_ref, m_ref, g_ref, wr_ref, o_ref, h_ref, lg_ref = refs[2 * n_in:]
    tm = x_ref.shape[1]
    sub = min(tm, OPROJ_SUB_ROWS)
    for s in range(tm // sub):
        rows = slice(s * sub, (s + 1) * sub)
        acc = jnp.dot(a_refs[0][0, rows, :], w_refs[0][...], preferred_element_type=F32)
        for a_ref, w_ref in zip(a_refs[1:], w_refs[1:]):
            acc = acc + jnp.dot(a_ref[0, rows, :], w_ref[...], preferred_element_type=F32)
        x_new = x_ref[0, rows, :] + m_ref[0, 2:3, :] * acc
        o_ref[0, rows, :] = x_new
        y = _rms_modulate(x_new, g_ref, m_ref, shift_row=3, scale_row=4)
        y_hi = y.astype(BF16)
        h_ref[0, rows, :] = y_hi
        y_lo = (y - y_hi.astype(F32)).astype(BF16)
        parts = (jnp.dot(y_hi, wr_ref[...], preferred_element_type=F32)
                 + jnp.dot(y_lo, wr_ref[...], preferred_element_type=F32))
        lg_ref[0, rows, :] = parts + pltpu.roll(parts, LANES - n_experts, 1)


def _oproj(ys, ws, x, mod, g_ffn, router_wt):
    B, T, D = x.shape
    E = router_wt.shape[0]
    tm = min(T, 512)
    per_sample = mod.shape[0] > 1
    w_r_hi = router_wt.T.astype(BF16)
    w_r_lo = (router_wt.T - w_r_hi.astype(F32)).astype(BF16)
    w_r_split = jnp.pad(jnp.concatenate([w_r_hi, w_r_lo], axis=1), ((0, 0), (0, LANES - 2 * E)))
    rows = lambda n: pl.BlockSpec((1, tm, n), lambda b, t: (b, t, 0))
    in_specs = [rows(y.shape[2]) for y in ys]
    in_specs += [pl.BlockSpec(w.shape, lambda b, t: (0, 0)) for w in ws]
    in_specs += [rows(D),
                 pl.BlockSpec((1, N_MOD, D), (lambda b, t: (b, 0, 0)) if per_sample else (lambda b, t: (0, 0, 0))),
                 pl.BlockSpec((1, D), lambda b, t: (0, 0)), pl.BlockSpec((D, LANES), lambda b, t: (0, 0))]
    x_new, h, logits = pl.pallas_call(
        functools.partial(_oproj_kernel, n_in=len(ys), n_experts=E),
        grid=(B, T // tm), in_specs=in_specs,
        out_specs=[rows(D), rows(D), rows(LANES)],
        out_shape=[jax.ShapeDtypeStruct((B, T, D), F32), jax.ShapeDtypeStruct((B, T, D), BF16),
                   jax.ShapeDtypeStruct((B, T, LANES), F32)],
        compiler_params=_params(("arbitrary", "arbitrary"), vmem_mib=56),
        name="oproj",
    )(*ys, *ws, x, mod, g_ffn.reshape(1, D), w_r_split)
    return x_new, h, jnp.transpose(logits[:, :, :E], (0, 2, 1))


def _log_sigmoid(z):
    return jnp.minimum(z, 0.0) - jnp.log(1.0 + jnp.exp(-jnp.abs(z)))


def _gla_kernel(*refs, n_chunks, use_rope, has_state):
    it = iter(refs)
    q_ref, k_ref, v_ref, g_ref, za_ref = (next(it) for _ in range(5))
    cos_ref, sin_ref = (next(it), next(it)) if use_rope else (None, None)
    aup_ref, ab_ref, gg_ref = next(it), next(it), next(it)
    s0_ref = next(it) if has_state else None
    y_ref, sfin_ref = next(it), next(it)
    qd_scr, ki_scr, ke_scr, kv_scr, dec_scr, sall_scr, st_scr = (next(it) for _ in range(7))
    C, DK = GLA_CHUNK, GLA_DK
    T = n_chunks * C
    RB = min(T, GLA_PREP_ROWS)
    cpb = RB // C

    def increments(blk):
        for i in range(cpb):
            c = blk * cpb + i
            rows = pl.ds(_aligned(c * C, C), C)
            kv_scr[c] = lax.dot_general(v_ref[0, rows, :].astype(BF16), ke_scr[c], _TN, preferred_element_type=F32)

    def prepare(blk):
        r0 = _aligned(blk * RB, RB)
        rows = pl.ds(r0, RB)
        z = jnp.dot(za_ref[0, rows, :].astype(BF16), aup_ref[...], preferred_element_type=F32) + ab_ref[...]
        la = _log_sigmoid(z) * (1.0 / GLA_TAU)
        in_chunk = lax.broadcasted_iota(jnp.int32, la.shape, 0) % C
        prefix = la
        step = 1
        while step < C:
            prefix = prefix + jnp.where(in_chunk >= step, pltpu.roll(prefix, step, 0), 0.0)
            step *= 2
        q = q_ref[0, rows, :] * (GLA_DK ** -0.5)
        k = k_ref[0, rows, :]
        if use_rope:
            cos, sin = cos_ref[rows, :], sin_ref[rows, :]
            even = (lax.broadcasted_iota(jnp.int32, q.shape, 1) % 2) == 0

            def rope(t):
                swapped = jnp.where(even, pltpu.roll(t, LANES - 1, 1), pltpu.roll(t, 1, 1))
                return t * cos + swapped * sin
            q, k = rope(q), rope(k)
        split = lambda t: t.reshape(cpb, C, t.shape[-1])
        la, prefix, q, k = split(la), split(prefix), split(q), split(k)
        tot = prefix[:, C - 1:C, :]
        b_f = prefix[..., :DK]
        b_b = tot[..., DK:] - prefix[..., DK:] + la[..., DK:]
        dec = jnp.exp(tot)
        c0 = _aligned(blk * cpb, cpb)
        qd_scr[pl.ds(c0, cpb)] = jnp.concatenate([q * jnp.exp(b_f), q * jnp.exp(b_b)], axis=2).astype(BF16)
        ki_f, ki_b = k * jnp.exp(-b_f), k * jnp.exp(-b_b)
        ki_scr[0, pl.ds(c0, cpb)] = ki_f.astype(BF16)
        ki_scr[1, pl.ds(c0, cpb)] = ki_b.astype(BF16)
        ke_scr[pl.ds(c0, cpb)] = jnp.concatenate([ki_f * dec[..., :DK], ki_b * dec[..., DK:]], axis=2).astype(BF16)
        dec_scr[pl.ds(c0, cpb)] = jnp.broadcast_to(dec, (cpb,) + dec_scr.shape[1:])

    def prepare_and_increment(blk, carry):
        increments(blk - 1)
        prepare(blk)
        return carry

    n_blocks = T // RB
    prepare(0)
    lax.fori_loop(1, n_blocks, prepare_and_increment, 0)
    increments(n_blocks - 1)

    st_scr[...] = s0_ref[0, 0] if has_state else jnp.zeros(st_scr.shape, F32)

    def recur(i, carry):
        cf, cb = i, n_chunks - 1 - i
        st = st_scr[...]
        sall_scr[cf, :, :DK] = st[:, :DK].astype(BF16)
        sall_scr[cb, :, DK:] = st[:, DK:].astype(BF16)
        st_scr[:, :DK] = st[:, :DK] * dec_scr[cf, 0:1, :DK] + kv_scr[cf, :, :DK]
        st_scr[:, DK:] = st[:, DK:] * dec_scr[cb, 0:1, DK:] + kv_scr[cb, :, DK:]
        return carry

    lax.fori_loop(0, n_chunks, recur, 0)
    sfin_ref[0, 0] = st_scr[...]

    G = min(n_chunks, GLA_EMIT_CHUNKS)
    GR = G * C
    row = lax.broadcasted_iota(jnp.int32, (GR, GR), 0)
    col = lax.broadcasted_iota(jnp.int32, (GR, GR), 1)
    same_chunk = (row // C) == (col // C)
    keep_f, keep_b = same_chunk & (row >= col), same_chunk & (row <= col)

    def emit(grp, carry):
        c0 = pl.multiple_of(grp * G, G)
        rows = pl.ds(pl.multiple_of(grp * GR, GR), GR)
        qd = qd_scr[pl.ds(c0, G)].reshape(GR, 2 * DK)
        ki_f = ki_scr[0, pl.ds(c0, G)].reshape(GR, DK)
        ki_b = ki_scr[1, pl.ds(c0, G)].reshape(GR, DK)
        s = (jnp.where(keep_f, lax.dot_general(qd[:, :DK], ki_f, _NT, preferred_element_type=F32), 0.0)
             + jnp.where(keep_b, lax.dot_general(qd[:, DK:], ki_b, _NT, preferred_element_type=F32), 0.0))
        carried = [lax.dot_general(qd[i * C:(i + 1) * C], sall_scr[c0 + i], _NT, preferred_element_type=F32)
                   for i in range(G)]
        o = (jnp.dot(s.astype(BF16), v_ref[0, rows, :].astype(BF16), preferred_element_type=F32)
             + jnp.concatenate(carried, axis=0))
        o = o * lax.rsqrt(jnp.mean(o * o, axis=-1, keepdims=True) + RMS_EPS) * gg_ref[...]
        y_ref[0, rows, :] = (o * _silu(g_ref[0, rows, :])).astype(y_ref.dtype)
        return carry

    lax.fori_loop(0, n_chunks // G, emit, 0, unroll=min(4, n_chunks // G))


def _gla(p_main, p_za, aup, ab, gg, rope_tabs, state):
    B, T, _ = p_main.shape
    H, DK, DV = GLA_HEADS, GLA_DK, GLA_DV
    use_rope, has_state = rope_tabs is not None, state is not None
    n_chunks = T // GLA_CHUNK
    in_specs = [pl.BlockSpec((1, T, DK), lambda b, h: (b, 0, h)),
                pl.BlockSpec((1, T, DK), lambda b, h: (b, 0, H + h)),
                pl.BlockSpec((1, T, DV), lambda b, h: (b, 0, H + h)),
                pl.BlockSpec((1, T, DV), lambda b, h: (b, 0, 2 * H + h)),
                pl.BlockSpec((1, T, LANES), lambda b, h: (b, 0, 0))]
    args = [p_main, p_main, p_main, p_main, p_za]
    if use_rope:
        in_specs += [pl.BlockSpec((T, DK), lambda b, h: (0, 0))] * 2
        args += list(rope_tabs)
    in_specs += [pl.BlockSpec((LANES, 2 * DK), lambda b, h: (0, h)),
                 pl.BlockSpec((1, 2 * DK), lambda b, h: (0, h)),
                 pl.BlockSpec((1, DV), lambda b, h: (0, 0))]
    args += [aup, ab, gg]
    st_spec = pl.BlockSpec((1, 1, DV, 2 * DK), lambda b, h: (b, h, 0, 0))
    if has_state:
        in_specs.append(st_spec)
        args.append(state)
    return pl.pallas_call(
        functools.partial(_gla_kernel, n_chunks=n_chunks, use_rope=use_rope, has_state=has_state),
        grid=(B, H), in_specs=in_specs,
        out_specs=[pl.BlockSpec((1, T, DV), lambda b, h: (b, 0, h)), st_spec],
        out_shape=[jax.ShapeDtypeStruct((B, T, H * DV), BF16), jax.ShapeDtypeStruct((B, H, DV, 2 * DK), F32)],
        scratch_shapes=[pltpu.VMEM((n_chunks, GLA_CHUNK, 2 * DK), BF16), pltpu.VMEM((2, n_chunks, GLA_CHUNK, DK), BF16),
                        pltpu.VMEM((n_chunks, GLA_CHUNK, 2 * DK), BF16),
                        pltpu.VMEM((n_chunks, DV, 2 * DK), F32), pltpu.VMEM((n_chunks, 8, 2 * DK), F32),
                        pltpu.VMEM((n_chunks, DV, 2 * DK), BF16), pltpu.VMEM((DV, 2 * DK), F32)],
        compiler_params=_params(("arbitrary", "arbitrary")),
        name="gla",
    )(*args)


def _gelu_tanh(z):
    return 0.5 * z * (1.0 + jnp.tanh(np.sqrt(2.0 / np.pi) * (z + 0.044715 * (z * z * z))))


def _lru_kernel(x_ref, gt_ref, cw_ref, cb_ref, wa_ref, ba_ref, wi_ref, bi_ref, lam_ref, h0f_ref, h0b_ref,
                y_ref, hf_ref, hb_ref, a_scr, bx_scr, hl_scr, ap_scr, *, T):
    L = T // LRU_SEGMENTS
    P = L + LRU_PITCH_PAD
    x = x_ref[0]
    row = lax.broadcasted_iota(jnp.int32, x.shape, 0)

    def from_earlier(v, d, fill):
        return jnp.where(row >= d, pltpu.roll(v, d, 0), fill)

    def from_later(v, d, fill):
        return jnp.where(row < T - d, pltpu.roll(v, T - d, 0), fill)

    xc = (cw_ref[0:1, :] * from_earlier(x, 1, 0.0) + cw_ref[1:2, :] * x
          + cw_ref[2:3, :] * from_later(x, 1, 0.0) + cw_ref[3:4, :] * from_later(x, 2, 0.0) + cb_ref[...])
    xcb = xc.astype(BF16)

    def sigmoid(z):
        return 0.5 * jnp.tanh(0.5 * z) + 0.5

    for d in range(2):
        r = sigmoid(jnp.dot(xcb, wa_ref[d, 0], preferred_element_type=F32) + ba_ref[d])
        gate_i = sigmoid(jnp.dot(xcb, wi_ref[d, 0], preferred_element_type=F32) + bi_ref[d])
        neg_lam = -lam_ref[d]
        softplus = jnp.maximum(neg_lam, 0.0) + jnp.log(1.0 + jnp.exp(-jnp.abs(neg_lam)))
        a = jnp.exp((-LRU_C) * r * softplus)
        u = 1.0 - a * a
        bx = jnp.where(u > 0.0, u * lax.rsqrt(u), 0.0) * (gate_i * xc)
        for s in range(LRU_SEGMENTS):
            a_scr[d, s * P:s * P + L, :] = a[s * L:(s + 1) * L]
            bx_scr[d, s * P:s * P + L, :] = bx[s * L:(s + 1) * L]

    def step(i, carry):
        h_f, p_f, h_b, p_b = carry
        at_f = pl.ds(i, LRU_SEGMENTS, stride=P)
        at_b = pl.ds(L - 1 - i, LRU_SEGMENTS, stride=P)
        a_f, a_b = a_scr[0, at_f, :], a_scr[1, at_b, :]
        h_f = a_f * h_f + bx_scr[0, at_f, :]
        h_b = a_b * h_b + bx_scr[1, at_b, :]
        p_f, p_b = a_f * p_f, a_b * p_b
        hl_scr[0, at_f, :] = h_f
        hl_scr[1, at_b, :] = h_b
        ap_scr[0, at_f, :] = p_f
        ap_scr[1, at_b, :] = p_b
        return h_f, p_f, h_b, p_b

    zeros, ones = jnp.zeros((LRU_SEGMENTS, x.shape[1]), F32), jnp.ones((LRU_SEGMENTS, x.shape[1]), F32)
    h_f, p_f, h_b, p_b = lax.fori_loop(0, L, step, (zeros, ones, zeros, ones), unroll=8)

    enter_f, enter_b = [h0f_ref[0]], [h0b_ref[0]]
    for s in range(LRU_SEGMENTS):
        enter_f.append(h_f[s:s + 1] + p_f[s:s + 1] * enter_f[-1])
        t = LRU_SEGMENTS - 1 - s
        enter_b.append(h_b[t:t + 1] + p_b[t:t + 1] * enter_b[-1])
    hf_ref[0] = enter_f[-1]
    hb_ref[0] = enter_b[-1]
    for s in range(LRU_SEGMENTS):
        seg = slice(s * P, s * P + L)
        h = (hl_scr[0, seg, :] + ap_scr[0, seg, :] * enter_f[s]
             + hl_scr[1, seg, :] + ap_scr[1, seg, :] * enter_b[LRU_SEGMENTS - 1 - s])
        y_ref[0, s * L:(s + 1) * L, :] = (h * _gelu_tanh(gt_ref[0, s * L:(s + 1) * L, :])).astype(y_ref.dtype)


def _lru(p_main, cw, cb, wa, ba, wi, bi, lam, states):
    B, T, _ = p_main.shape
    W, NB, BD = LRU_WIDTH, LRU_BLOCKS, LRU_BLOCK_DIM
    x_blk0 = (2 * GLA_KEY_WIDTH + 2 * GLA_VAL_WIDTH) // BD
    vec = lambda n: pl.BlockSpec((n, 1, BD), lambda b, j: (0, 0, j))
    mat = pl.BlockSpec((2, 1, BD, BD), lambda b, j: (0, j, 0, 0))
    st_spec = pl.BlockSpec((1, 1, BD), lambda b, j: (b, 0, j))
    st_shape = jax.ShapeDtypeStruct((B, 1, W), F32)
    y, hf, hb = pl.pallas_call(
        functools.partial(_lru_kernel, T=T),
        grid=(B, NB),
        in_specs=[pl.BlockSpec((1, T, BD), lambda b, j: (b, 0, x_blk0 + j)),
                  pl.BlockSpec((1, T, BD), lambda b, j: (b, 0, x_blk0 + NB + j)),
                  pl.BlockSpec((LRU_CONV, BD), lambda b, j: (0, j)),
                  pl.BlockSpec((1, BD), lambda b, j: (0, j)),
                  mat, vec(2), mat, vec(2), vec(2), st_spec, st_spec],
        out_specs=[pl.BlockSpec((1, T, BD), lambda b, j: (b, 0, j)), st_spec, st_spec],
        out_shape=[jax.ShapeDtypeStruct((B, T, W), BF16), st_shape, st_shape],
        scratch_shapes=[pltpu.VMEM((2, LRU_SEGMENTS * (T // LRU_SEGMENTS + LRU_PITCH_PAD), BD), F32)] * 4,
        compiler_params=_params(("arbitrary", "arbitrary")),
        name="rglru",
    )(p_main, p_main, cw, cb.reshape(1, W), wa, ba.reshape(2, 1, W), wi, bi.reshape(2, 1, W),
      lam.reshape(2, 1, W), *states)
    return y, (hf, hb)


def _na_tile(i, n_tiles, rows):
    r0 = i * NA_QROWS
    ks = min(max(r0 - NA_KH // 2, 0), rows - NA_KROWS)
    cls = 0 if i == 0 else (2 if i == n_tiles - 1 else 1)
    return r0, ks, cls


def _na_bias_tables(rpb, rows):
    H = rpb.shape[0]
    qc = np.arange(GRID_W)[:, None]
    kc = np.arange(GRID_W)[None, :]
    col_start = np.clip(qc - NA_KW // 2, 0, GRID_W - NA_KW)
    col_valid = (kc >= col_start) & (kc < col_start + NA_KW)
    dc = np.clip(kc - qc, -(NA_KW - 1), NA_KW - 1) + NA_KW - 1
    pick = (dc[None] == np.arange(2 * NA_KW - 1)[:, None, None]).astype(np.float32)
    band = jnp.einsum('hrd,dqk->hrqk', rpb.astype(F32), jnp.asarray(pick), precision=HIGHEST)
    band = jnp.where(col_valid, band, NEG_BIG)
    masked = jnp.full((H, GRID_W, GRID_W), NEG_BIG, F32)
    n_tiles = rows // NA_QROWS
    tabs = []
    for i in (0, 1, n_tiles - 1):
        r0, ks, _ = _na_tile(i, n_tiles, rows)
        per_qrow = []
        for qr in range(r0, r0 + NA_QROWS):
            row_start = min(max(qr - NA_KH // 2, 0), rows - NA_KH)
            blocks = [band[:, kr - qr + NA_KH - 1] if row_start <= kr < row_start + NA_KH else masked
                      for kr in range(ks, ks + NA_KROWS)]
            per_qrow.append(jnp.stack(blocks, axis=2).reshape(H, GRID_W, NA_KROWS * GRID_W))
        tabs.append(jnp.concatenate(per_qrow, axis=1))
    return jnp.stack(tabs, axis=1) * LOG2_E


def _na_kernel(*refs, rows, need_ctx):
    q_ref, k_ref, v_ref, qc_ref, kc_ref, vc_ref, tab_ref = refs[:7]
    y_ref = refs[7]
    scale = NA_HEAD_DIM ** -0.5 * LOG2_E
    kc, vc = kc_ref[0], vc_ref[0]
    n_tiles = rows // NA_QROWS
    nq, nk = NA_QROWS * GRID_W, NA_KROWS * GRID_W
    for i in range(n_tiles):
        r0, ks, cls = _na_tile(i, n_tiles, rows)
        q = q_ref[0, r0 * GRID_W:r0 * GRID_W + nq, :]
        kt = k_ref[0, ks * GRID_W:ks * GRID_W + nk, :]
        vt = v_ref[0, ks * GRID_W:ks * GRID_W + nk, :]
        s_loc = lax.dot_general(q, kt, _NT, preferred_element_type=F32) * scale + tab_ref[0, cls]
        s_ctx = lax.dot_general(q, kc, _NT, preferred_element_type=F32) * scale
        m = jnp.maximum(jnp.max(s_loc, axis=-1, keepdims=True), jnp.max(s_ctx, axis=-1, keepdims=True))
        p_loc, p_ctx = jnp.exp2(s_loc - m), jnp.exp2(s_ctx - m)
        denom = jnp.sum(p_loc, axis=-1, keepdims=True) + jnp.sum(p_ctx, axis=-1, keepdims=True)
        o = (jnp.dot(p_loc.astype(BF16), vt, preferred_element_type=F32)
             + jnp.dot(p_ctx.astype(BF16), vc, preferred_element_type=F32))
        y_ref[0, r0 * GRID_W:r0 * GRID_W + nq, :] = (o / denom).astype(y_ref.dtype)
    if need_ctx:
        yc_ref = refs[8]
        s = lax.dot_general(qc_ref[0], kc, _NT, preferred_element_type=F32) * scale
        p = jnp.exp2(s - jnp.max(s, axis=-1, keepdims=True))
        o = jnp.dot(p.astype(BF16), vc, preferred_element_type=F32) / jnp.sum(p, axis=-1, keepdims=True)
        yc_ref[0] = o.astype(yc_ref.dtype)


def _na(qkv_lat, qkv_ctx, tabs, need_ctx):
    B, T, _ = qkv_lat.shape
    Tc = qkv_ctx.shape[1]
    H, HD = NA_HEADS, NA_HEAD_DIM
    blk = lambda t, off: pl.BlockSpec((1, t, HD), lambda h, b: (b, 0, off + h))
    out_specs = [pl.BlockSpec((1, T, HD), lambda h, b: (b, 0, h))]
    out_shape = [jax.ShapeDtypeStruct((B, T, H * HD), BF16)]
    if need_ctx:
        out_specs.append(pl.BlockSpec((1, Tc, HD), lambda h, b: (b, 0, h)))
        out_shape.append(jax.ShapeDtypeStruct((B, Tc, H * HD), BF16))
    res = pl.pallas_call(
        functools.partial(_na_kernel, rows=T // GRID_W, need_ctx=need_ctx),
        grid=(H, B),
        in_specs=[blk(T, 0), blk(T, H), blk(T, 2 * H), blk(Tc, 0), blk(Tc, H), blk(Tc, 2 * H),
                  pl.BlockSpec((1,) + tabs.shape[1:], lambda h, b: (h, 0, 0, 0))],
        out_specs=out_specs, out_shape=out_shape,
        compiler_params=_params(("arbitrary", "arbitrary")),
        name="natten",
    )(qkv_lat, qkv_lat, qkv_lat, qkv_ctx, qkv_ctx, qkv_ctx, tabs)
    return (res[1] if need_ctx else None), res[0]


def _route_kernel(lg_ref, pos_ref, aff_ref, lo_ref, *, cap):
    lg = lg_ref[0]
    E, N = lg.shape
    ex = jnp.exp(lg - jnp.max(lg, axis=0, keepdims=True))
    aff = ex / jnp.sum(ex, axis=0, keepdims=True)
    aff_ref[0] = aff
    bits = pltpu.bitcast(aff, jnp.int32)
    thr = jnp.zeros((E, 1), jnp.int32)
    for bit in range(29, -1, -1):
        cand = thr | (1 << bit)
        cnt = jnp.sum(jnp.where(bits >= cand, 1.0, 0.0), axis=1, keepdims=True)
        thr = jnp.where(cnt >= cap, cand, thr)
    above = bits > thr
    tied = bits == thr
    need = cap - jnp.sum(jnp.where(above, 1.0, 0.0), axis=1, keepdims=True)
    pc = min(PREFIX_CHUNK, N)
    before = jnp.where(lax.broadcasted_iota(jnp.int32, (pc, pc), 0) < lax.broadcasted_iota(jnp.int32, (pc, pc), 1),
                       1.0, 0.0).astype(BF16)
    tied_seen = jnp.zeros((E, 1), F32)
    sel_seen = jnp.zeros((E, 1), F32)
    tile_lane = lax.broadcasted_iota(jnp.int32, (E, LANES), 1)
    lo = jnp.where(tile_lane == N // pc, float(cap), 0.0)
    for c in range(N // pc):
        sl = slice(c * pc, (c + 1) * pc)
        lo = jnp.where(tile_lane == c, sel_seen, lo)
        tied_c = jnp.where(tied[:, sl], 1.0, 0.0)
        tied_before = jnp.dot(tied_c.astype(BF16), before, preferred_element_type=F32) + tied_seen
        sel_c = jnp.where(above[:, sl], 1.0, jnp.where(tied_before < need, tied_c, 0.0))
        sel_before = jnp.dot(sel_c.astype(BF16), before, preferred_element_type=F32) + sel_seen
        pos_ref[0, :, sl] = jnp.where(sel_c > 0.0, sel_before, -1.0)
        tied_seen = tied_seen + jnp.sum(tied_c, axis=1, keepdims=True)
        sel_seen = sel_seen + jnp.sum(sel_c, axis=1, keepdims=True)
    lo_ref[0] = lo.astype(jnp.int32)


def _route(logits_t, cap):
    B, E, N = logits_t.shape
    spec = pl.BlockSpec((1, E, N), lambda b: (b, 0, 0))
    lo_spec = pl.BlockSpec((1, E, LANES), lambda b: (b, 0, 0))
    pos, aff, lo = pl.pallas_call(
        functools.partial(_route_kernel, cap=cap),
        grid=(B,), in_specs=[spec], out_specs=[spec, spec, lo_spec],
        out_shape=[jax.ShapeDtypeStruct((B, E, N), F32)] * 2 + [jax.ShapeDtypeStruct((B, E, LANES), jnp.int32)],
        compiler_params=_params(("arbitrary",)),
        name="route",
    )(logits_t)
    n_bounds = N // min(PREFIX_CHUNK, N) + 1
    return pos, aff, lo[:, :, :n_bounds].reshape(-1)


def _window_count(lo_ref, base, experts, tile, n_bounds, win):
    starts, n_win = [], 0
    for e in experts:
        first = lo_ref[(base + e) * n_bounds + tile]
        end = lo_ref[(base + e) * n_bounds + tile + 1]
        start = (first // SLOT_ALIGN) * SLOT_ALIGN
        starts.append(start)
        n_win = jnp.maximum(n_win, (end - start + win - 1) // win)
    return starts, n_win


def _gather_kernel(lo_ref, pos_ref, h_ref, xs_ref, *, cap, win, tn):
    EG, N = pos_ref.shape[1], pos_ref.shape[2]
    n_bounds = N // tn + 1
    base = pl.program_id(0) * (EG * pl.num_programs(1)) + pl.program_id(1) * EG
    xs_ref[...] = jnp.zeros(xs_ref.shape, xs_ref.dtype)
    slot_iota = lax.broadcasted_iota(jnp.int32, (win, tn), 0)
    for j in range(N // tn):
        h_tile = h_ref[0, j * tn:(j + 1) * tn, :]
        starts, n_win = _window_count(lo_ref, base, range(EG), j, n_bounds, win)

        def window(w, carry):
            pieces, offs = [], []
            for e in range(EG):
                first = starts[e] + w * win
                off = jnp.minimum(first, cap - win)
                prow = pos_ref[0, e:e + 1, j * tn:(j + 1) * tn]
                hit = ((slot_iota + off).astype(F32) == prow) & (prow >= first.astype(F32))
                pieces.append(jnp.where(hit, 1.0, 0.0).astype(BF16))
                offs.append(pl.multiple_of(off, SLOT_ALIGN))
            got = jnp.dot(jnp.concatenate(pieces, axis=0), h_tile, preferred_element_type=F32)
            for e in range(EG):
                xs_ref[0, e, pl.ds(offs[e], win), :] += got[e * win:(e + 1) * win].astype(xs_ref.dtype)
            return carry

        lax.fori_loop(0, n_win, window, 0)


def _gather(lo, pos, h, cap):
    B, E, N = pos.shape
    D = h.shape[2]
    EG = 8
    tn, win = min(PREFIX_CHUNK, N), min(SLOT_WINDOW, cap)
    return pl.pallas_call(
        functools.partial(_gather_kernel, cap=cap, win=win, tn=tn),
        grid_spec=pltpu.PrefetchScalarGridSpec(
            num_scalar_prefetch=1, grid=(B, E // EG),
            in_specs=[pl.BlockSpec((1, EG, N), lambda b, g, lo: (b, g, 0)),
                      pl.BlockSpec((1, N, D), lambda b, g, lo: (b, 0, 0))],
            out_specs=pl.BlockSpec((1, EG, cap, D), lambda b, g, lo: (b, g, 0, 0))),
        out_shape=jax.ShapeDtypeStruct((B, E, cap, D), BF16),
        compiler_params=_params(("arbitrary", "arbitrary")),
        name="moe_gather",
    )(lo, pos, h)


def _ffn_kernel(xs_ref, wg_ref, wu_ref, wd_ref, y_ref):
    bb, _, cap, D = xs_ref.shape
    xs = xs_ref[:, 0].reshape(bb * cap, D)
    hid = (_silu(jnp.dot(xs, wg_ref[0, 0], preferred_element_type=F32))
           * jnp.dot(xs, wu_ref[0, 0], preferred_element_type=F32))
    y = jnp.dot(hid.astype(BF16), wd_ref[0, 0], preferred_element_type=F32)
    y_ref[:, 0] = y.reshape(bb, cap, D).astype(y_ref.dtype)


def _ffn(xs, wg, wu, wd, layer):
    B, E, cap, D = xs.shape
    F = wg.shape[3]
    bb = max(1, min(B, FFN_ROWS // cap))
    tok = pl.BlockSpec((bb, 1, cap, D), lambda e, b: (b, e, 0, 0))
    return pl.pallas_call(
        _ffn_kernel,
        grid=(E, B // bb),
        in_specs=[tok, pl.BlockSpec((1, 1, D, F), lambda e, b: (layer, e, 0, 0)),
                  pl.BlockSpec((1, 1, D, F), lambda e, b: (layer, e, 0, 0)),
                  pl.BlockSpec((1, 1, F, D), lambda e, b: (layer, e, 0, 0))],
        out_specs=tok,
        out_shape=jax.ShapeDtypeStruct((B, E, cap, D), BF16),
        compiler_params=_params(("arbitrary", "arbitrary")),
        name="moe_ffn",
    )(xs, wg, wu, wd)


def _combine_kernel(*refs, cap, win, keep_x, next_modulated):
    it = iter(refs)
    lo_ref, pos_ref, aff_ref, y_ref, x_ref, m_ref, gn_ref = (next(it) for _ in range(7))
    mn_ref = next(it) if next_modulated else None
    o_ref = next(it) if keep_x else None
    hn_ref, ystack_ref, acc_ref = next(it), next(it), next(it)
    E = y_ref.shape[1]
    tn = pos_ref.shape[1]
    n_bounds = pl.num_programs(1) + 1
    j = pl.program_id(1)
    starts, n_win = _window_count(lo_ref, pl.program_id(0) * E, range(E), j, n_bounds, win)
    col = lax.broadcasted_iota(jnp.int32, (E, E * win), 1)
    spread = jnp.where(col // win == lax.broadcasted_iota(jnp.int32, (E, E * win), 0), 1.0, 0.0).astype(BF16)
    slot = jnp.dot(pos_ref[0].astype(BF16), spread, preferred_element_type=F32)
    gate = jnp.dot(aff_ref[0].astype(BF16), spread, preferred_element_type=F32)
    col1 = lax.broadcasted_iota(jnp.int32, (1, E * win), 1)
    col_expert, col_in_win = col1 // win, col1 % win
    acc_ref[...] = jnp.zeros(acc_ref.shape, F32)

    def window(w, carry):
        first_col = jnp.zeros((1, E * win), jnp.int32)
        want_col = jnp.zeros((1, E * win), jnp.int32)
        for e in range(E):
            first = starts[e] + w * win
            off = jnp.minimum(first, cap - win)
            ystack_ref[e * win:(e + 1) * win, :] = y_ref[0, e, pl.ds(pl.multiple_of(off, SLOT_ALIGN), win), :]
            first_col = jnp.where(col_expert == e, first, first_col)
            want_col = jnp.where(col_expert == e, off + col_in_win, want_col)
        hit = (slot == want_col.astype(F32)) & (slot >= first_col.astype(F32))
        acc_ref[...] += jnp.dot(jnp.where(hit, gate, 0.0).astype(BF16), ystack_ref[...], preferred_element_type=F32)
        return carry

    lax.fori_loop(0, n_win, window, 0)
    x_new = x_ref[0] + m_ref[0, 5:6, :] * acc_ref[...]
    if keep_x:
        o_ref[0] = x_new
    hn_ref[0] = _rms_modulate(x_new, gn_ref, mn_ref).astype(hn_ref.dtype)


def _combine(lo, pos, aff, y, x, mod, g_next, mod_next):
    B, E, cap, D = y.shape
    N = x.shape[1]
    tn, win = min(PREFIX_CHUNK, N), min(SLOT_WINDOW, cap)
    keep_x = next_modulated = mod_next is not None
    tok = pl.BlockSpec((1, tn, E), lambda b, t, lo: (b, t, 0))
    rows = pl.BlockSpec((1, tn, D), lambda b, t, lo: (b, t, 0))
    mod_spec = lambda m: pl.BlockSpec((1, N_MOD, D),
                                      (lambda b, t, lo: (b, 0, 0)) if m.shape[0] > 1 else (lambda b, t, lo: (0, 0, 0)))
    in_specs = [tok, tok, pl.BlockSpec((1, E, cap, D), lambda b, t, lo: (b, 0, 0, 0)), rows, mod_spec(mod),
                pl.BlockSpec((1, D), lambda b, t, lo: (0, 0))]
    args = [lo, jnp.transpose(pos, (0, 2, 1)), jnp.transpose(aff, (0, 2, 1)), y, x, mod, g_next.reshape(1, D)]
    out_specs, out_shape = [rows], [jax.ShapeDtypeStruct((B, N, D), BF16 if next_modulated else F32)]
    if next_modulated:
        in_specs.append(mod_spec(mod_next))
        args.append(mod_next)
        out_specs, out_shape = [rows] + out_specs, [jax.ShapeDtypeStruct((B, N, D), F32)] + out_shape
    res = pl.pallas_call(
        functools.partial(_combine_kernel, cap=cap, win=win, keep_x=keep_x, next_modulated=next_modulated),
        grid_spec=pltpu.PrefetchScalarGridSpec(
            num_scalar_prefetch=1, grid=(B, N // tn), in_specs=in_specs, out_specs=out_specs,
            scratch_shapes=[pltpu.VMEM((E * win, D), BF16), pltpu.VMEM((tn, D), F32)]),
        out_shape=out_shape,
        compiler_params=_params(("arbitrary", "arbitrary"), vmem_mib=56),
        name="moe_combine",
    )(*args)
    return (res[0], res[1]) if keep_x else (None, res[0])


def _moe(x, h, logits_t, mod, wg, wu, wd, layer, g_next, mod_next):
    N = x.shape[1]
    cap = EC_CAPACITY_FACTOR * N // N_EXPERTS
    assert cap <= 256, "slot ids must stay exact in bf16"
    pos, aff, lo = _route(logits_t, cap)
    y = _ffn(_gather(lo, pos, h, cap), wg, wu, wd, layer)
    return _combine(lo, pos, aff, y, x, mod, g_next, mod_next)


def _rope_tables(n_tok):
    t = jnp.arange(n_tok)
    rows = (t // GRID_W).astype(F32)
    cols = (t % GRID_W).astype(F32)
    n_freq = GLA_DK // 4
    inv = ROPE_THETA ** (-jnp.arange(n_freq, dtype=F32) / n_freq)
    ang = jnp.concatenate([rows[:, None] * inv, cols[:, None] * inv], axis=-1)
    cos = jnp.repeat(jnp.cos(ang), 2, axis=-1)
    sin = jnp.repeat(jnp.sin(ang), 2, axis=-1) * jnp.tile(jnp.asarray([-1.0, 1.0], F32), GLA_DK // 2)
    return cos, sin


def _rec_mixer(h_ctx, h_lat, w_in, alpha_up, alpha_b, gla_g, conv_w, conv_b, w_a, b_a, w_i, b_i, lam, rope_tabs):
    za0 = 2 * GLA_KEY_WIDTH + 2 * GLA_VAL_WIDTH
    za1 = za0 + 2 * GLA_RANK
    w_main = jnp.concatenate([w_in[:, :za0], w_in[:, za1:]], axis=1).astype(BF16)
    w_za = jnp.pad(w_in[:, za0:za1], ((0, 0), (0, LANES - 2 * GLA_RANK))).astype(BF16)
    up = alpha_up.reshape(2, GLA_RANK, GLA_HEADS, 1, GLA_DK)
    zero = jnp.zeros_like(up[0])
    aup = jnp.concatenate([jnp.concatenate([up[0], zero], axis=2), jnp.concatenate([zero, up[1]], axis=2)], axis=0)
    aup = jnp.pad(aup.reshape(2 * GLA_RANK, 2 * GLA_KEY_WIDTH), ((0, LANES - 2 * GLA_RANK), (0, 0))).astype(BF16)
    ab = jnp.transpose(alpha_b.reshape(2, GLA_HEADS, GLA_DK), (1, 0, 2)).reshape(1, 2 * GLA_KEY_WIDTH)
    gg = gla_g.reshape(1, GLA_DV)
    wa, wi = w_a.astype(BF16), w_i.astype(BF16)
    outs = []
    gla_state, lru_state = None, None
    for h, tabs in ((h_ctx, None), (h_lat, rope_tabs)):
        B = h.shape[0]
        p_main = _proj(h, w_main, F32, tn=2560)
        p_za = _proj(h, w_za, F32, tn=LANES)
        y_gla, gla_state = _gla(p_main, p_za, aup, ab, gg, tabs, gla_state)
        if lru_state is None:
            lru_state = (jnp.zeros((B, 1, LRU_WIDTH), F32),) * 2
        y_lru, lru_state = _lru(p_main, conv_w, conv_b, wa, b_a, wi, b_i, lam, lru_state)
        outs.append([y_gla, y_lru])
    return outs


def kernel(x, c, ctx, c_ctx, w_mod, b_mod, norm_mix_g, norm_ffn_g, w_out, router_w, exp_w_gate, exp_w_up, exp_w_down, rec_w_in, gla_alpha_up, gla_alpha_b, gla_norm_g, lru_conv_w, lru_conv_b, lru_w_a, lru_b_a, lru_w_i, lru_b_i, lru_lambda, na_w_qkv, na_rpb, norm_f_g):
    B, T, D = x.shape
    pad_rows = (-(B + 1)) % 8
    cc = jnp.concatenate([c, c_ctx[None, :], jnp.zeros((pad_rows, D), F32)], axis=0)
    m_all = _modulation(cc, w_mod, b_mod)
    rope_tabs = _rope_tables(T)
    exp_wg, exp_wu, exp_wd = exp_w_gate.astype(BF16), exp_w_up.astype(BF16), exp_w_down.astype(BF16)
    mods_lat = [m_all[l, :B].reshape(B, N_MOD, D) for l in range(DEPTH)]
    mods_ctx = [m_all[l, B:B + 1].reshape(1, N_MOD, D) for l in range(DEPTH)]
    h_lat = _norm(x, norm_mix_g[0], mods_lat[0])
    h_ctx = _norm(ctx, norm_mix_g[0], mods_ctx[0])
    for l in range(DEPTH):
        last = l == DEPTH - 1
        i = l // 2
        m_lat, m_ctx = mods_lat[l], mods_ctx[l]
        w_o = w_out[l].astype(BF16)
        if l % 2 == 0:
            ys_ctx, ys_lat = _rec_mixer(h_ctx, h_lat, rec_w_in[i], gla_alpha_up[i], gla_alpha_b[i], gla_norm_g[i],
                                        lru_conv_w[i], lru_conv_b[i], lru_w_a[i], lru_b_a[i], lru_w_i[i], lru_b_i[i],
                                        lru_lambda[i], rope_tabs)
            ws = [w_o[:GLA_VAL_WIDTH], w_o[GLA_VAL_WIDTH:]]
        else:
            w_qkv = na_w_qkv[i].astype(BF16)
            qkv_lat = _proj(h_lat, w_qkv, BF16, tn=3072)
            qkv_ctx = _proj(h_ctx, w_qkv, BF16, tn=3072)
            y_ctx, y_lat = _na(qkv_lat, qkv_ctx, _na_bias_tables(na_rpb[i], T // GRID_W), not last)
            ys_ctx, ys_lat, ws = [y_ctx], [y_lat], [w_o]
        moe_w = (exp_wg, exp_wu, exp_wd, l)
        router_wt = router_w[l].T
        x, h_ffn, logits_t = _oproj(ys_lat, ws, x, m_lat, norm_ffn_g[l], router_wt)
        if last:
            _, out = _moe(x, h_ffn, logits_t, m_lat, *moe_w, norm_f_g, None)
            return out
        x, h_lat = _moe(x, h_ffn, logits_t, m_lat, *moe_w, norm_mix_g[l + 1], mods_lat[l + 1])
        ctx, h_ffn, logits_t = _oproj(ys_ctx, ws, ctx, m_ctx, norm_ffn_g[l], router_wt)
        ctx, h_ctx = _moe(ctx, h_ffn, logits_t, m_ctx, *moe_w, norm_mix_g[l + 1], mods_ctx[l + 1])
```

```python
import functools

import numpy as np
import jax
import jax.numpy as jnp
from jax import lax
from jax.experimental import pallas as pl
from jax.experimental.pallas import tpu as pltpu

D_MODEL = 2048
DEPTH = 4
GRID_W = 64
RMS_EPS = 1e-6
N_MOD = 6
ROPE_THETA = 10000.0
GLA_HEADS = 4
GLA_DK = D_MODEL // 16
GLA_DV = D_MODEL // 8
GLA_KEY_WIDTH = GLA_HEADS * GLA_DK
GLA_VAL_WIDTH = GLA_HEADS * GLA_DV
GLA_RANK = 16
GLA_TAU = 16.0
GLA_CHUNK = 64
LRU_WIDTH = D_MODEL // 2
LRU_BLOCKS = 8
LRU_BLOCK_DIM = LRU_WIDTH // LRU_BLOCKS
LRU_CONV = 4
LRU_C = 8.0
NA_HEADS = 16
NA_HEAD_DIM = D_MODEL // NA_HEADS
NA_KH = 8
NA_KW = 16
N_EXPERTS = 16
EC_CAPACITY_FACTOR = 2
EXPERT_FF = D_MODEL // 2

F32 = jnp.float32
BF16 = jnp.bfloat16
HIGHEST = lax.Precision.HIGHEST
LANES = 128
NEG_BIG = -1e30
LOG2_E = float(np.log2(np.e))
NA_QROWS = 4
NA_KROWS = NA_QROWS + NA_KH - 1
PREFIX_CHUNK = 256
SLOT_WINDOW = 64
OPROJ_SUB_ROWS = 256
GLA_EMIT_CHUNKS = 4
GLA_PREP_ROWS = 512
LRU_SEGMENTS = 8
LRU_PITCH_PAD = 8
FFN_ROWS = 512
SLOT_ALIGN = 16

_NT = (((1,), (1,)), ((), ()))
_TN = (((0,), (0,)), ((), ()))


def _params(sem, vmem_mib=48):
    return pltpu.CompilerParams(dimension_semantics=sem, vmem_limit_bytes=vmem_mib * 2 ** 20)


def _aligned(index, multiple):
    return index if isinstance(index, int) else pl.multiple_of(index, multiple)


def _sigmoid(z):
    return 1.0 / (1.0 + jnp.exp(-z))


def _silu(z):
    return z * _sigmoid(z)


def _mod_kernel(c_ref, w_ref, b_ref, o_ref):
    a = _silu(c_ref[...]).astype(BF16)
    o_ref[0] = jnp.dot(a, w_ref[0].astype(BF16), preferred_element_type=F32) + b_ref[0]


def _modulation(cc, w_mod, b_mod):
    R, D = cc.shape
    L, _, N = w_mod.shape
    tn = 1024
    return pl.pallas_call(
        _mod_kernel,
        grid=(L, N // tn),
        in_specs=[pl.BlockSpec((R, D), lambda l, j: (0, 0)),
                  pl.BlockSpec((1, D, tn), lambda l, j: (l, 0, j)),
                  pl.BlockSpec((1, 1, tn), lambda l, j: (l, 0, j))],
        out_specs=pl.BlockSpec((1, R, tn), lambda l, j: (l, 0, j)),
        out_shape=jax.ShapeDtypeStruct((L, R, N), F32),
        compiler_params=_params(("arbitrary", "arbitrary")),
        name="modulation",
    )(cc, w_mod, b_mod.reshape(L, 1, N))


def _rms_modulate(x, g_ref, m_ref=None, shift_row=0, scale_row=1):
    y = x * lax.rsqrt(jnp.mean(x * x, axis=-1, keepdims=True) + RMS_EPS) * g_ref[...]
    if m_ref is not None:
        y = y * (1.0 + m_ref[0, scale_row:scale_row + 1, :]) + m_ref[0, shift_row:shift_row + 1, :]
    return y


def _norm_kernel(x_ref, g_ref, m_ref, h_ref):
    h_ref[0] = _rms_modulate(x_ref[0], g_ref, m_ref).astype(h_ref.dtype)


def _norm(x, g, mod):
    B, T, D = x.shape
    tt = min(T, 512)
    per_sample = mod.shape[0] > 1
    rows = pl.BlockSpec((1, tt, D), lambda b, t: (b, t, 0))
    return pl.pallas_call(
        _norm_kernel,
        grid=(B, T // tt),
        in_specs=[rows, pl.BlockSpec((1, D), lambda b, t: (0, 0)),
                  pl.BlockSpec((1, N_MOD, D), (lambda b, t: (b, 0, 0)) if per_sample else (lambda b, t: (0, 0, 0)))],
        out_specs=rows, out_shape=jax.ShapeDtypeStruct((B, T, D), BF16),
        compiler_params=_params(("arbitrary", "arbitrary")),
        name="rmsnorm",
    )(x, g.reshape(1, D), mod)


def _proj_kernel(*refs, n_side):
    a_ref, w_ref = refs[:2]
    side_in, o_ref, side_out = refs[2:2 + n_side], refs[2 + n_side], refs[3 + n_side:]
    o_ref[...] = jnp.dot(a_ref[...], w_ref[...], preferred_element_type=F32).astype(o_ref.dtype)
    for src, dst in zip(side_in, side_out):
        dst[...] = src[...].astype(dst.dtype)


def _proj(a, w, out_dtype, tn, side=(), layer=0):
    B, T, K = a.shape
    N = w.shape[1]
    M = B * T
    tm = min(M, 512)
    n_i = M // tm
    steps = (N // tn) * n_i
    flat = [s.reshape(s.shape[0], steps, -1, s.shape[-1]) for s in side]
    res = pl.pallas_call(
        functools.partial(_proj_kernel, n_side=len(side)),
        grid=(N // tn, n_i),
        in_specs=[pl.BlockSpec((tm, K), lambda j, i: (i, 0)), pl.BlockSpec((K, tn), lambda j, i: (0, j))]
                 + [pl.BlockSpec((1, 1) + f.shape[2:], lambda j, i: (layer, j * n_i + i, 0, 0)) for f in flat],
        out_specs=[pl.BlockSpec((tm, tn), lambda j, i: (i, j))]
                  + [pl.BlockSpec((1, 1) + f.shape[2:], lambda j, i: (0, j * n_i + i, 0, 0)) for f in flat],
        out_shape=[jax.ShapeDtypeStruct((M, N), out_dtype)]
                  + [jax.ShapeDtypeStruct((1,) + f.shape[1:], BF16) for f in flat],
        compiler_params=_params(("arbitrary", "arbitrary")),
        name="proj",
    )(a.reshape(M, K), w, *flat)
    out = res[0].reshape(B, T, N)
    return (out, *[r.reshape((1,) + s.shape[1:]) for r, s in zip(res[1:], side)]) if side else out


def _oproj_kernel(*refs, n_in, n_experts):
    a_refs, w_refs = refs[:n_in], refs[n_in:2 * n_in]
    x_ref, m_ref, g_ref, wr_ref, o_ref, h_ref, lg_ref = refs[2 * n_in:]
    tm = x_ref.shape[1]
    sub = min(tm, OPROJ_SUB_ROWS)
    for s in range(tm // sub):
        rows = slice(s * sub, (s + 1) * sub)
        acc = jnp.dot(a_refs[0][0, rows, :], w_refs[0][...], preferred_element_type=F32)
        for a_ref, w_ref in zip(a_refs[1:], w_refs[1:]):
            acc = acc + jnp.dot(a_ref[0, rows, :], w_ref[...], preferred_element_type=F32)
        x_new = x_ref[0, rows, :] + m_ref[0, 2:3, :] * acc
        o_ref[0, rows, :] = x_new
        y = _rms_modulate(x_new, g_ref, m_ref, shift_row=3, scale_row=4)
        y_hi = y.astype(BF16)
        h_ref[0, rows, :] = y_hi
        y_lo = (y - y_hi.astype(F32)).astype(BF16)
        parts = (jnp.dot(y_hi, wr_ref[...], preferred_element_type=F32)
                 + jnp.dot(y_lo, wr_ref[...], preferred_element_type=F32))
        lg_ref[0, rows, :] = parts + pltpu.roll(parts, LANES - n_experts, 1)


def _oproj(ys, ws, x, mod, g_ffn, router_wt):
    B, T, D = x.shape
    E = router_wt.shape[0]
    tm = min(T, 512)
    per_sample = mod.shape[0] > 1
    w_r_hi = router_wt.T.astype(BF16)
    w_r_lo = (router_wt.T - w_r_hi.astype(F32)).astype(BF16)
    w_r_split = jnp.pad(jnp.concatenate([w_r_hi, w_r_lo], axis=1), ((0, 0), (0, LANES - 2 * E)))
    rows = lambda n: pl.BlockSpec((1, tm, n), lambda b, t: (b, t, 0))
    in_specs = [rows(y.shape[2]) for y in ys]
    in_specs += [pl.BlockSpec(w.shape, lambda b, t: (0, 0)) for w in ws]
    in_specs += [rows(D),
                 pl.BlockSpec((1, N_MOD, D), (lambda b, t: (b, 0, 0)) if per_sample else (lambda b, t: (0, 0, 0))),
                 pl.BlockSpec((1, D), lambda b, t: (0, 0)), pl.BlockSpec((D, LANES), lambda b, t: (0, 0))]
    x_new, h, logits = pl.pallas_call(
        functools.partial(_oproj_kernel, n_in=len(ys), n_experts=E),
        grid=(B, T // tm), in_specs=in_specs,
        out_specs=[rows(D), rows(D), rows(LANES)],
        out_shape=[jax.ShapeDtypeStruct((B, T, D), F32), jax.ShapeDtypeStruct((B, T, D), BF16),
                   jax.ShapeDtypeStruct((B, T, LANES), F32)],
        compiler_params=_params(("arbitrary", "arbitrary"), vmem_mib=56),
        name="oproj",
    )(*ys, *ws, x, mod, g_ffn.reshape(1, D), w_r_split)
    return x_new, h, jnp.transpose(logits[:, :, :E], (0, 2, 1))


def _log_sigmoid(z):
    return jnp.minimum(z, 0.0) - jnp.log(1.0 + jnp.exp(-jnp.abs(z)))


def _gla_kernel(*refs, n_chunks, use_rope, has_state):
    it = iter(refs)
    q_ref, k_ref, v_ref, g_ref, za_ref = (next(it) for _ in range(5))
    cos_ref, sin_ref = (next(it), next(it)) if use_rope else (None, None)
    aup_ref, ab_ref, gg_ref = next(it), next(it), next(it)
    s0_ref = next(it) if has_state else None
    y_ref, sfin_ref = next(it), next(it)
    qd_scr, ki_scr, ke_scr, kv_scr, dec_scr, sall_scr, st_scr = (next(it) for _ in range(7))
    C, DK = GLA_CHUNK, GLA_DK
    T = n_chunks * C
    RB = min(T, GLA_PREP_ROWS)
    cpb = RB // C

    def increments(blk):
        for i in range(cpb):
            c = blk * cpb + i
            rows = pl.ds(_aligned(c * C, C), C)
            kv_scr[c] = lax.dot_general(v_ref[0, rows, :].astype(BF16), ke_scr[c], _TN, preferred_element_type=F32)

    def prepare(blk):
        r0 = _aligned(blk * RB, RB)
        rows = pl.ds(r0, RB)
        z = jnp.dot(za_ref[0, rows, :].astype(BF16), aup_ref[...], preferred_element_type=F32) + ab_ref[...]
        la = _log_sigmoid(z) * (1.0 / GLA_TAU)
        in_chunk = lax.broadcasted_iota(jnp.int32, la.shape, 0) % C
        prefix = la
        step = 1
        while step < C:
            prefix = prefix + jnp.where(in_chunk >= step, pltpu.roll(prefix, step, 0), 0.0)
            step *= 2
        q = q_ref[0, rows, :] * (GLA_DK ** -0.5)
        k = k_ref[0, rows, :]
        if use_rope:
            cos, sin = cos_ref[rows, :], sin_ref[rows, :]
            even = (lax.broadcasted_iota(jnp.int32, q.shape, 1) % 2) == 0

            def rope(t):
                swapped = jnp.where(even, pltpu.roll(t, LANES - 1, 1), pltpu.roll(t, 1, 1))
                return t * cos + swapped * sin
            q, k = rope(q), rope(k)
        split = lambda t: t.reshape(cpb, C, t.shape[-1])
        la, prefix, q, k = split(la), split(prefix), split(q), split(k)
        tot = prefix[:, C - 1:C, :]
        b_f = prefix[..., :DK]
        b_b = tot[..., DK:] - prefix[..., DK:] + la[..., DK:]
        dec = jnp.exp(tot)
        c0 = _aligned(blk * cpb, cpb)
        qd_scr[pl.ds(c0, cpb)] = jnp.concatenate([q * jnp.exp(b_f), q * jnp.exp(b_b)], axis=2).astype(BF16)
        ki_f, ki_b = k * jnp.exp(-b_f), k * jnp.exp(-b_b)
        ki_scr[0, pl.ds(c0, cpb)] = ki_f.astype(BF16)
        ki_scr[1, pl.ds(c0, cpb)] = ki_b.astype(BF16)
        ke_scr[pl.ds(c0, cpb)] = jnp.concatenate([ki_f * dec[..., :DK], ki_b * dec[..., DK:]], axis=2).astype(BF16)
        dec_scr[pl.ds(c0, cpb)] = jnp.broadcast_to(dec, (cpb,) + dec_scr.shape[1:])

    def prepare_and_increment(blk, carry):
        increments(blk - 1)
        prepare(blk)
        return carry

    n_blocks = T // RB
    prepare(0)
    lax.fori_loop(1, n_blocks, prepare_and_increment, 0)
    increments(n_blocks - 1)

    st_scr[...] = s0_ref[0, 0] if has_state else jnp.zeros(st_scr.shape, F32)

    def recur(i, carry):
        cf, cb = i, n_chunks - 1 - i
        st = st_scr[...]
        sall_scr[cf, :, :DK] = st[:, :DK].astype(BF16)
        sall_scr[cb, :, DK:] = st[:, DK:].astype(BF16)
        st_scr[:, :DK] = st[:, :DK] * dec_scr[cf, 0:1, :DK] + kv_scr[cf, :, :DK]
        st_scr[:, DK:] = st[:, DK:] * dec_scr[cb, 0:1, DK:] + kv_scr[cb, :, DK:]
        return carry

    lax.fori_loop(0, n_chunks, recur, 0)
    sfin_ref[0, 0] = st_scr[...]

    G = min(n_chunks, GLA_EMIT_CHUNKS)
    GR = G * C
    row = lax.broadcasted_iota(jnp.int32, (GR, GR), 0)
    col = lax.broadcasted_iota(jnp.int32, (GR, GR), 1)
    same_chunk = (row // C) == (col // C)
    keep_f, keep_b = same_chunk & (row >= col), same_chunk & (row <= col)

    def emit(grp, carry):
        c0 = pl.multiple_of(grp * G, G)
        rows = pl.ds(pl.multiple_of(grp * GR, GR), GR)
        qd = qd_scr[pl.ds(c0, G)].reshape(GR, 2 * DK)
        ki_f = ki_scr[0, pl.ds(c0, G)].reshape(GR, DK)
        ki_b = ki_scr[1, pl.ds(c0, G)].reshape(GR, DK)
        s = (jnp.where(keep_f, lax.dot_general(qd[:, :DK], ki_f, _NT, preferred_element_type=F32), 0.0)
             + jnp.where(keep_b, lax.dot_general(qd[:, DK:], ki_b, _NT, preferred_element_type=F32), 0.0))
        carried = [lax.dot_general(qd[i * C:(i + 1) * C], sall_scr[c0 + i], _NT, preferred_element_type=F32)
                   for i in range(G)]
        o = (jnp.dot(s.astype(BF16), v_ref[0, rows, :].astype(BF16), preferred_element_type=F32)
             + jnp.concatenate(carried, axis=0))
        o = o * lax.rsqrt(jnp.mean(o * o, axis=-1, keepdims=True) + RMS_EPS) * gg_ref[...]
        y_ref[0, rows, :] = (o * _silu(g_ref[0, rows, :])).astype(y_ref.dtype)
        return carry

    lax.fori_loop(0, n_chunks // G, emit, 0, unroll=min(4, n_chunks // G))


def _gla(p_main, p_za, aup, ab, gg, rope_tabs, state):
    B, T, _ = p_main.shape
    H, DK, DV = GLA_HEADS, GLA_DK, GLA_DV
    use_rope, has_state = rope_tabs is not None, state is not None
    n_chunks = T // GLA_CHUNK
    in_specs = [pl.BlockSpec((1, T, DK), lambda b, h: (b, 0, h)),
                pl.BlockSpec((1, T, DK), lambda b, h: (b, 0, H + h)),
                pl.BlockSpec((1, T, DV), lambda b, h: (b, 0, H + h)),
                pl.BlockSpec((1, T, DV), lambda b, h: (b, 0, 2 * H + h)),
                pl.BlockSpec((1, T, LANES), lambda b, h: (b, 0, 0))]
    args = [p_main, p_main, p_main, p_main, p_za]
    if use_rope:
        in_specs += [pl.BlockSpec((T, DK), lambda b, h: (0, 0))] * 2
        args += list(rope_tabs)
    in_specs += [pl.BlockSpec((LANES, 2 * DK), lambda b, h: (0, h)),
                 pl.BlockSpec((1, 2 * DK), lambda b, h: (0, h)),
                 pl.BlockSpec((1, DV), lambda b, h: (0, 0))]
    args += [aup, ab, gg]
    st_spec = pl.BlockSpec((1, 1, DV, 2 * DK), lambda b, h: (b, h, 0, 0))
    if has_state:
        in_specs.append(st_spec)
        args.append(state)
    return pl.pallas_call(
        functools.partial(_gla_kernel, n_chunks=n_chunks, use_rope=use_rope, has_state=has_state),
        grid=(B, H), in_specs=in_specs,
        out_specs=[pl.BlockSpec((1, T, DV), lambda b, h: (b, 0, h)), st_spec],
        out_shape=[jax.ShapeDtypeStruct((B, T, H * DV), BF16), jax.ShapeDtypeStruct((B, H, DV, 2 * DK), F32)],
        scratch_shapes=[pltpu.VMEM((n_chunks, GLA_CHUNK, 2 * DK), BF16), pltpu.VMEM((2, n_chunks, GLA_CHUNK, DK), BF16),
                        pltpu.VMEM((n_chunks, GLA_CHUNK, 2 * DK), BF16),
                        pltpu.VMEM((n_chunks, DV, 2 * DK), F32), pltpu.VMEM((n_chunks, 8, 2 * DK), F32),
                        pltpu.VMEM((n_chunks, DV, 2 * DK), BF16), pltpu.VMEM((DV, 2 * DK), F32)],
        compiler_params=_params(("arbitrary", "arbitrary")),
        name="gla",
    )(*args)


def _gelu_tanh(z):
    return 0.5 * z * (1.0 + jnp.tanh(np.sqrt(2.0 / np.pi) * (z + 0.044715 * (z * z * z))))


def _lru_kernel(x_ref, gt_ref, cw_ref, cb_ref, wa_ref, ba_ref, wi_ref, bi_ref, lam_ref, h0f_ref, h0b_ref,
                y_ref, hf_ref, hb_ref, a_scr, bx_scr, hl_scr, ap_scr, *, T):
    L = T // LRU_SEGMENTS
    P = L + LRU_PITCH_PAD
    x = x_ref[0]
    row = lax.broadcasted_iota(jnp.int32, x.shape, 0)

    def from_earlier(v, d, fill):
        return jnp.where(row >= d, pltpu.roll(v, d, 0), fill)

    def from_later(v, d, fill):
        return jnp.where(row < T - d, pltpu.roll(v, T - d, 0), fill)

    xc = (cw_ref[0:1, :] * from_earlier(x, 1, 0.0) + cw_ref[1:2, :] * x
          + cw_ref[2:3, :] * from_later(x, 1, 0.0) + cw_ref[3:4, :] * from_later(x, 2, 0.0) + cb_ref[...])
    xcb = xc.astype(BF16)

    def sigmoid(z):
        return 0.5 * jnp.tanh(0.5 * z) + 0.5

    for d in range(2):
        r = sigmoid(jnp.dot(xcb, wa_ref[d, 0], preferred_element_type=F32) + ba_ref[d])
        gate_i = sigmoid(jnp.dot(xcb, wi_ref[d, 0], preferred_element_type=F32) + bi_ref[d])
        neg_lam = -lam_ref[d]
        softplus = jnp.maximum(neg_lam, 0.0) + jnp.log(1.0 + jnp.exp(-jnp.abs(neg_lam)))
        a = jnp.exp((-LRU_C) * r * softplus)
        u = 1.0 - a * a
        bx = jnp.where(u > 0.0, u * lax.rsqrt(u), 0.0) * (gate_i * xc)
        for s in range(LRU_SEGMENTS):
            a_scr[d, s * P:s * P + L, :] = a[s * L:(s + 1) * L]
            bx_scr[d, s * P:s * P + L, :] = bx[s * L:(s + 1) * L]

    def step(i, carry):
        h_f, p_f, h_b, p_b = carry
        at_f = pl.ds(i, LRU_SEGMENTS, stride=P)
        at_b = pl.ds(L - 1 - i, LRU_SEGMENTS, stride=P)
        a_f, a_b = a_scr[0, at_f, :], a_scr[1, at_b, :]
        h_f = a_f * h_f + bx_scr[0, at_f, :]
        h_b = a_b * h_b + bx_scr[1, at_b, :]
        p_f, p_b = a_f * p_f, a_b * p_b
        hl_scr[0, at_f, :] = h_f
        hl_scr[1, at_b, :] = h_b
        ap_scr[0, at_f, :] = p_f
        ap_scr[1, at_b, :] = p_b
        return h_f, p_f, h_b, p_b

    zeros, ones = jnp.zeros((LRU_SEGMENTS, x.shape[1]), F32), jnp.ones((LRU_SEGMENTS, x.shape[1]), F32)
    h_f, p_f, h_b, p_b = lax.fori_loop(0, L, step, (zeros, ones, zeros, ones), unroll=8)

    enter_f, enter_b = [h0f_ref[0]], [h0b_ref[0]]
    for s in range(LRU_SEGMENTS):
        enter_f.append(h_f[s:s + 1] + p_f[s:s + 1] * enter_f[-1])
        t = LRU_SEGMENTS - 1 - s
        enter_b.append(h_b[t:t + 1] + p_b[t:t + 1] * enter_b[-1])
    hf_ref[0] = enter_f[-1]
    hb_ref[0] = enter_b[-1]
    for s in range(LRU_SEGMENTS):
        seg = slice(s * P, s * P + L)
        h = (hl_scr[0, seg, :] + ap_scr[0, seg, :] * enter_f[s]
             + hl_scr[1, seg, :] + ap_scr[1, seg, :] * enter_b[LRU_SEGMENTS - 1 - s])
        y_ref[0, s * L:(s + 1) * L, :] = (h * _gelu_tanh(gt_ref[0, s * L:(s + 1) * L, :])).astype(y_ref.dtype)


def _lru(p_main, cw, cb, wa, ba, wi, bi, lam, states):
    B, T, _ = p_main.shape
    W, NB, BD = LRU_WIDTH, LRU_BLOCKS, LRU_BLOCK_DIM
    x_blk0 = (2 * GLA_KEY_WIDTH + 2 * GLA_VAL_WIDTH) // BD
    vec = lambda n: pl.BlockSpec((n, 1, BD), lambda b, j: (0, 0, j))
    mat = pl.BlockSpec((2, 1, BD, BD), lambda b, j: (0, j, 0, 0))
    st_spec = pl.BlockSpec((1, 1, BD), lambda b, j: (b, 0, j))
    st_shape = jax.ShapeDtypeStruct((B, 1, W), F32)
    y, hf, hb = pl.pallas_call(
        functools.partial(_lru_kernel, T=T),
        grid=(B, NB),
        in_specs=[pl.BlockSpec((1, T, BD), lambda b, j: (b, 0, x_blk0 + j)),
                  pl.BlockSpec((1, T, BD), lambda b, j: (b, 0, x_blk0 + NB + j)),
                  pl.BlockSpec((LRU_CONV, BD), lambda b, j: (0, j)),
                  pl.BlockSpec((1, BD), lambda b, j: (0, j)),
                  mat, vec(2), mat, vec(2), vec(2), st_spec, st_spec],
        out_specs=[pl.BlockSpec((1, T, BD), lambda b, j: (b, 0, j)), st_spec, st_spec],
        out_shape=[jax.ShapeDtypeStruct((B, T, W), BF16), st_shape, st_shape],
        scratch_shapes=[pltpu.VMEM((2, LRU_SEGMENTS * (T // LRU_SEGMENTS + LRU_PITCH_PAD), BD), F32)] * 4,
        compiler_params=_params(("arbitrary", "arbitrary")),
        name="rglru",
    )(p_main, p_main, cw, cb.reshape(1, W), wa, ba.reshape(2, 1, W), wi, bi.reshape(2, 1, W),
      lam.reshape(2, 1, W), *states)
    return y, (hf, hb)


def _na_tile(i, n_tiles, rows):
    r0 = i * NA_QROWS
    ks = min(max(r0 - NA_KH // 2, 0), rows - NA_KROWS)
    cls = 0 if i == 0 else (2 if i == n_tiles - 1 else 1)
    return r0, ks, cls


def _na_bias_tables(rpb, rows):
    H = rpb.shape[0]
    qc = np.arange(GRID_W)[:, None]
    kc = np.arange(GRID_W)[None, :]
    col_start = np.clip(qc - NA_KW // 2, 0, GRID_W - NA_KW)
    col_valid = (kc >= col_start) & (kc < col_start + NA_KW)
    dc = np.clip(kc - qc, -(NA_KW - 1), NA_KW - 1) + NA_KW - 1
    pick = (dc[None] == np.arange(2 * NA_KW - 1)[:, None, None]).astype(np.float32)
    band = jnp.einsum('hrd,dqk->hrqk', rpb.astype(F32), jnp.asarray(pick), precision=HIGHEST)
    band = jnp.where(col_valid, band, NEG_BIG)
    masked = jnp.full((H, GRID_W, GRID_W), NEG_BIG, F32)
    n_tiles = rows // NA_QROWS
    tabs = []
    for i in (0, 1, n_tiles - 1):
        r0, ks, _ = _na_tile(i, n_tiles, rows)
        per_qrow = []
        for qr in range(r0, r0 + NA_QROWS):
            row_start = min(max(qr - NA_KH // 2, 0), rows - NA_KH)
            blocks = [band[:, kr - qr + NA_KH - 1] if row_start <= kr < row_start + NA_KH else masked
                      for kr in range(ks, ks + NA_KROWS)]
            per_qrow.append(jnp.stack(blocks, axis=2).reshape(H, GRID_W, NA_KROWS * GRID_W))
        tabs.append(jnp.concatenate(per_qrow, axis=1))
    return jnp.stack(tabs, axis=1) * LOG2_E


def _na_kernel(*refs, rows, need_ctx):
    q_ref, k_ref, v_ref, qc_ref, kc_ref, vc_ref, tab_ref = refs[:7]
    y_ref, vx_ref = refs[7], refs[-1]
    HD = NA_HEAD_DIM
    scale = HD ** -0.5 * LOG2_E
    vx_ref[...] = jnp.concatenate([v_ref[0], jnp.ones(v_ref.shape[1:], BF16)], axis=1)
    kc = kc_ref[0]
    vcx = jnp.concatenate([vc_ref[0], jnp.ones(vc_ref.shape[1:], BF16)], axis=1)
    n_tiles = rows // NA_QROWS
    nq, nk = NA_QROWS * GRID_W, NA_KROWS * GRID_W
    for i in range(n_tiles):
        r0, ks, cls = _na_tile(i, n_tiles, rows)
        q = q_ref[0, r0 * GRID_W:r0 * GRID_W + nq, :]
        kt = k_ref[0, ks * GRID_W:ks * GRID_W + nk, :]
        s_loc = lax.dot_general(q, kt, _NT, preferred_element_type=F32) * scale + tab_ref[0, cls]
        s_ctx = lax.dot_general(q, kc, _NT, preferred_element_type=F32) * scale
        m = jnp.maximum(jnp.max(s_loc, axis=-1, keepdims=True), jnp.max(s_ctx, axis=-1, keepdims=True))
        o = (jnp.dot(jnp.exp2(s_loc - m).astype(BF16), vx_ref[ks * GRID_W:ks * GRID_W + nk, :], preferred_element_type=F32)
             + jnp.dot(jnp.exp2(s_ctx - m).astype(BF16), vcx, preferred_element_type=F32))
        y_ref[0, r0 * GRID_W:r0 * GRID_W + nq, :] = (o[:, :HD] / o[:, HD:]).astype(y_ref.dtype)
    if need_ctx:
        yc_ref = refs[8]
        s = lax.dot_general(qc_ref[0], kc, _NT, preferred_element_type=F32) * scale
        p = jnp.exp2(s - jnp.max(s, axis=-1, keepdims=True))
        o = jnp.dot(p.astype(BF16), vcx, preferred_element_type=F32)
        yc_ref[0] = (o[:, :HD] / o[:, HD:]).astype(yc_ref.dtype)


def _na(qkv_lat, qkv_ctx, tabs, need_ctx):
    B, T, _ = qkv_lat.shape
    Tc = qkv_ctx.shape[1]
    H, HD = NA_HEADS, NA_HEAD_DIM
    blk = lambda t, off: pl.BlockSpec((1, t, HD), lambda h, b: (b, 0, off + h))
    out_specs = [pl.BlockSpec((1, T, HD), lambda h, b: (b, 0, h))]
    out_shape = [jax.ShapeDtypeStruct((B, T, H * HD), BF16)]
    if need_ctx:
        out_specs.append(pl.BlockSpec((1, Tc, HD), lambda h, b: (b, 0, h)))
        out_shape.append(jax.ShapeDtypeStruct((B, Tc, H * HD), BF16))
    res = pl.pallas_call(
        functools.partial(_na_kernel, rows=T // GRID_W, need_ctx=need_ctx),
        grid=(H, B),
        in_specs=[blk(T, 0), blk(T, H), blk(T, 2 * H), blk(Tc, 0), blk(Tc, H), blk(Tc, 2 * H),
                  pl.BlockSpec((1,) + tabs.shape[1:], lambda h, b: (h, 0, 0, 0))],
        out_specs=out_specs, out_shape=out_shape,
        scratch_shapes=[pltpu.VMEM((T, 2 * HD), BF16)],
        compiler_params=_params(("arbitrary", "arbitrary")),
        name="natten",
    )(qkv_lat, qkv_lat, qkv_lat, qkv_ctx, qkv_ctx, qkv_ctx, tabs)
    return (res[1] if need_ctx else None), res[0]


def _route_kernel(lg_ref, pos_ref, aff_ref, lo_ref, *, cap):
    lg = lg_ref[0]
    E, N = lg.shape
    ex = jnp.exp(lg - jnp.max(lg, axis=0, keepdims=True))
    aff = ex / jnp.sum(ex, axis=0, keepdims=True)
    aff_ref[0] = aff
    bits = pltpu.bitcast(aff, jnp.int32)
    thr = jnp.zeros((E, 1), jnp.int32)
    for bit in range(29, -1, -1):
        cand = thr | (1 << bit)
        cnt = jnp.sum(jnp.where(bits >= cand, 1.0, 0.0), axis=1, keepdims=True)
        thr = jnp.where(cnt >= cap, cand, thr)
    above = bits > thr
    tied = bits == thr
    need = cap - jnp.sum(jnp.where(above, 1.0, 0.0), axis=1, keepdims=True)
    pc = min(PREFIX_CHUNK, N)
    before = jnp.where(lax.broadcasted_iota(jnp.int32, (pc, pc), 0) < lax.broadcasted_iota(jnp.int32, (pc, pc), 1),
                       1.0, 0.0).astype(BF16)
    tied_seen = jnp.zeros((E, 1), F32)
    sel_seen = jnp.zeros((E, 1), F32)
    tile_lane = lax.broadcasted_iota(jnp.int32, (E, LANES), 1)
    lo = jnp.where(tile_lane == N // pc, float(cap), 0.0)
    for c in range(N // pc):
        sl = slice(c * pc, (c + 1) * pc)
        lo = jnp.where(tile_lane == c, sel_seen, lo)
        tied_c = jnp.where(tied[:, sl], 1.0, 0.0)
        tied_before = jnp.dot(tied_c.astype(BF16), before, preferred_element_type=F32) + tied_seen
        sel_c = jnp.where(above[:, sl], 1.0, jnp.where(tied_before < need, tied_c, 0.0))
        sel_before = jnp.dot(sel_c.astype(BF16), before, preferred_element_type=F32) + sel_seen
        pos_ref[0, :, sl] = jnp.where(sel_c > 0.0, sel_before, -1.0)
        tied_seen = tied_seen + jnp.sum(tied_c, axis=1, keepdims=True)
        sel_seen = sel_seen + jnp.sum(sel_c, axis=1, keepdims=True)
    lo_ref[0] = lo.astype(jnp.int32)


def _route(logits_t, cap):
    B, E, N = logits_t.shape
    spec = pl.BlockSpec((1, E, N), lambda b: (b, 0, 0))
    lo_spec = pl.BlockSpec((1, E, LANES), lambda b: (b, 0, 0))
    pos, aff, lo = pl.pallas_call(
        functools.partial(_route_kernel, cap=cap),
        grid=(B,), in_specs=[spec], out_specs=[spec, spec, lo_spec],
        out_shape=[jax.ShapeDtypeStruct((B, E, N), F32)] * 2 + [jax.ShapeDtypeStruct((B, E, LANES), jnp.int32)],
        compiler_params=_params(("arbitrary",)),
        name="route",
    )(logits_t)
    n_bounds = N // min(PREFIX_CHUNK, N) + 1
    return pos, aff, lo[:, :, :n_bounds].reshape(-1)


def _window_count(lo_ref, base, experts, tile, n_bounds, win):
    starts, n_win = [], 0
    for e in experts:
        first = lo_ref[(base + e) * n_bounds + tile]
        end = lo_ref[(base + e) * n_bounds + tile + 1]
        start = (first // SLOT_ALIGN) * SLOT_ALIGN
        starts.append(start)
        n_win = jnp.maximum(n_win, (end - start + win - 1) // win)
    return starts, n_win


def _gather_kernel(lo_ref, pos_ref, h_ref, xs_ref, *, cap, win, tn):
    EG, N = pos_ref.shape[1], pos_ref.shape[2]
    n_bounds = N // tn + 1
    base = pl.program_id(0) * (EG * pl.num_programs(1)) + pl.program_id(1) * EG
    xs_ref[...] = jnp.zeros(xs_ref.shape, xs_ref.dtype)
    slot_iota = lax.broadcasted_iota(jnp.int32, (win, tn), 0)
    for j in range(N // tn):
        h_tile = h_ref[0, j * tn:(j + 1) * tn, :]
        starts, n_win = _window_count(lo_ref, base, range(EG), j, n_bounds, win)

        def window(w, carry):
            pieces, offs = [], []
            for e in range(EG):
                first = starts[e] + w * win
                off = jnp.minimum(first, cap - win)
                prow = pos_ref[0, e:e + 1, j * tn:(j + 1) * tn]
                hit = ((slot_iota + off).astype(F32) == prow) & (prow >= first.astype(F32))
                pieces.append(jnp.where(hit, 1.0, 0.0).astype(BF16))
                offs.append(pl.multiple_of(off, SLOT_ALIGN))
            got = jnp.dot(jnp.concatenate(pieces, axis=0), h_tile, preferred_element_type=F32)
            for e in range(EG):
                xs_ref[0, e, pl.ds(offs[e], win), :] += got[e * win:(e + 1) * win].astype(xs_ref.dtype)
            return carry

        lax.fori_loop(0, n_win, window, 0)


def _gather(lo, pos, h, cap):
    B, E, N = pos.shape
    D = h.shape[2]
    EG = 8
    tn, win = min(PREFIX_CHUNK, N), min(SLOT_WINDOW, cap)
    return pl.pallas_call(
        functools.partial(_gather_kernel, cap=cap, win=win, tn=tn),
        grid_spec=pltpu.PrefetchScalarGridSpec(
            num_scalar_prefetch=1, grid=(B, E // EG),
            in_specs=[pl.BlockSpec((1, EG, N), lambda b, g, lo: (b, g, 0)),
                      pl.BlockSpec((1, N, D), lambda b, g, lo: (b, 0, 0))],
            out_specs=pl.BlockSpec((1, EG, cap, D), lambda b, g, lo: (b, g, 0, 0))),
        out_shape=jax.ShapeDtypeStruct((B, E, cap, D), BF16),
        compiler_params=_params(("arbitrary", "arbitrary")),
        name="moe_gather",
    )(lo, pos, h)


def _ffn_kernel(xs_ref, wg_ref, wu_ref, wd_ref, y_ref):
    bb, _, cap, D = xs_ref.shape
    xs = xs_ref[:, 0].reshape(bb * cap, D)
    hid = (_silu(jnp.dot(xs, wg_ref[0, 0], preferred_element_type=F32))
           * jnp.dot(xs, wu_ref[0, 0], preferred_element_type=F32))
    y = jnp.dot(hid.astype(BF16), wd_ref[0, 0], preferred_element_type=F32)
    y_ref[:, 0] = y.reshape(bb, cap, D).astype(y_ref.dtype)


def _ffn(xs, wg, wu, wd, layer):
    B, E, cap, D = xs.shape
    F = wg.shape[3]
    bb = max(1, min(B, FFN_ROWS // cap))
    tok = pl.BlockSpec((bb, 1, cap, D), lambda e, b: (b, e, 0, 0))
    return pl.pallas_call(
        _ffn_kernel,
        grid=(E, B // bb),
        in_specs=[tok, pl.BlockSpec((1, 1, D, F), lambda e, b: (layer, e, 0, 0)),
                  pl.BlockSpec((1, 1, D, F), lambda e, b: (layer, e, 0, 0)),
                  pl.BlockSpec((1, 1, F, D), lambda e, b: (layer, e, 0, 0))],
        out_specs=tok,
        out_shape=jax.ShapeDtypeStruct((B, E, cap, D), BF16),
        compiler_params=_params(("arbitrary", "arbitrary")),
        name="moe_ffn",
    )(xs, wg, wu, wd)


def _combine_kernel(*refs, cap, win, keep_x, next_modulated):
    it = iter(refs)
    lo_ref, pos_ref, aff_ref, y_ref, x_ref, m_ref, gn_ref = (next(it) for _ in range(7))
    mn_ref = next(it) if next_modulated else None
    o_ref = next(it) if keep_x else None
    hn_ref, ystack_ref, acc_ref = next(it), next(it), next(it)
    E = y_ref.shape[1]
    tn = pos_ref.shape[1]
    n_bounds = pl.num_programs(1) + 1
    j = pl.program_id(1)
    starts, n_win = _window_count(lo_ref, pl.program_id(0) * E, range(E), j, n_bounds, win)
    col = lax.broadcasted_iota(jnp.int32, (E, E * win), 1)
    spread = jnp.where(col // win == lax.broadcasted_iota(jnp.int32, (E, E * win), 0), 1.0, 0.0).astype(BF16)
    slot = jnp.dot(pos_ref[0].astype(BF16), spread, preferred_element_type=F32)
    gate = jnp.dot(aff_ref[0].astype(BF16), spread, preferred_element_type=F32)
    col1 = lax.broadcasted_iota(jnp.int32, (1, E * win), 1)
    col_expert, col_in_win = col1 // win, col1 % win
    acc_ref[...] = jnp.zeros(acc_ref.shape, F32)

    def window(w, carry):
        first_col = jnp.zeros((1, E * win), jnp.int32)
        want_col = jnp.zeros((1, E * win), jnp.int32)
        for e in range(E):
            first = starts[e] + w * win
            off = jnp.minimum(first, cap - win)
            ystack_ref[e * win:(e + 1) * win, :] = y_ref[0, e, pl.ds(pl.multiple_of(off, SLOT_ALIGN), win), :]
            first_col = jnp.where(col_expert == e, first, first_col)
            want_col = jnp.where(col_expert == e, off + col_in_win, want_col)
        hit = (slot == want_col.astype(F32)) & (slot >= first_col.astype(F32))
        acc_ref[...] += jnp.dot(jnp.where(hit, gate, 0.0).astype(BF16), ystack_ref[...], preferred_element_type=F32)
        return carry

    lax.fori_loop(0, n_win, window, 0)
    x_new = x_ref[0] + m_ref[0, 5:6, :] * acc_ref[...]
    if keep_x:
        o_ref[0] = x_new
    hn_ref[0] = _rms_modulate(x_new, gn_ref, mn_ref).astype(hn_ref.dtype)


def _combine(lo, pos, aff, y, x, mod, g_next, mod_next):
    B, E, cap, D = y.shape
    N = x.shape[1]
    tn, win = min(PREFIX_CHUNK, N), min(SLOT_WINDOW, cap)
    keep_x = next_modulated = mod_next is not None
    tok = pl.BlockSpec((1, tn, E), lambda b, t, lo: (b, t, 0))
    rows = pl.BlockSpec((1, tn, D), lambda b, t, lo: (b, t, 0))
    mod_spec = lambda m: pl.BlockSpec((1, N_MOD, D),
                                      (lambda b, t, lo: (b, 0, 0)) if m.shape[0] > 1 else (lambda b, t, lo: (0, 0, 0)))
    in_specs = [tok, tok, pl.BlockSpec((1, E, cap, D), lambda b, t, lo: (b, 0, 0, 0)), rows, mod_spec(mod),
                pl.BlockSpec((1, D), lambda b, t, lo: (0, 0))]
    args = [lo, jnp.transpose(pos, (0, 2, 1)), jnp.transpose(aff, (0, 2, 1)), y, x, mod, g_next.reshape(1, D)]
    out_specs, out_shape = [rows], [jax.ShapeDtypeStruct((B, N, D), BF16 if next_modulated else F32)]
    if next_modulated:
        in_specs.append(mod_spec(mod_next))
        args.append(mod_next)
        out_specs, out_shape = [rows] + out_specs, [jax.ShapeDtypeStruct((B, N, D), F32)] + out_shape
    res = pl.pallas_call(
        functools.partial(_combine_kernel, cap=cap, win=win, keep_x=keep_x, next_modulated=next_modulated),
        grid_spec=pltpu.PrefetchScalarGridSpec(
            num_scalar_prefetch=1, grid=(B, N // tn), in_specs=in_specs, out_specs=out_specs,
            scratch_shapes=[pltpu.VMEM((E * win, D), BF16), pltpu.VMEM((tn, D), F32)]),
        out_shape=out_shape,
        compiler_params=_params(("arbitrary", "arbitrary"), vmem_mib=56),
        name="moe_combine",
    )(*args)
    return (res[0], res[1]) if keep_x else (None, res[0])


def _moe(x, h, logits_t, mod, wg, wu, wd, layer, g_next, mod_next):
    N = x.shape[1]
    cap = EC_CAPACITY_FACTOR * N // N_EXPERTS
    assert cap <= 256, "slot ids must stay exact in bf16"
    pos, aff, lo = _route(logits_t, cap)
    y = _ffn(_gather(lo, pos, h, cap), wg, wu, wd, layer)
    return _combine(lo, pos, aff, y, x, mod, g_next, mod_next)


def _rope_tables(n_tok):
    t = jnp.arange(n_tok)
    rows = (t // GRID_W).astype(F32)
    cols = (t % GRID_W).astype(F32)
    n_freq = GLA_DK // 4
    inv = ROPE_THETA ** (-jnp.arange(n_freq, dtype=F32) / n_freq)
    ang = jnp.concatenate([rows[:, None] * inv, cols[:, None] * inv], axis=-1)
    cos = jnp.repeat(jnp.cos(ang), 2, axis=-1)
    sin = jnp.repeat(jnp.sin(ang), 2, axis=-1) * jnp.tile(jnp.asarray([-1.0, 1.0], F32), GLA_DK // 2)
    return cos, sin


def _rec_mixer(h_ctx, h_lat, w_in, alpha_up, alpha_b, gla_g, conv_w, conv_b, w_a, b_a, w_i, b_i, lam, rope_tabs,
               side, layer):
    za0 = 2 * GLA_KEY_WIDTH + 2 * GLA_VAL_WIDTH
    za1 = za0 + 2 * GLA_RANK
    w_main = jnp.concatenate([w_in[:, :za0], w_in[:, za1:]], axis=1).astype(BF16)
    w_za = jnp.pad(w_in[:, za0:za1], ((0, 0), (0, LANES - 2 * GLA_RANK))).astype(BF16)
    up = alpha_up.reshape(2, GLA_RANK, GLA_HEADS, 1, GLA_DK)
    zero = jnp.zeros_like(up[0])
    aup = jnp.concatenate([jnp.concatenate([up[0], zero], axis=2), jnp.concatenate([zero, up[1]], axis=2)], axis=0)
    aup = jnp.pad(aup.reshape(2 * GLA_RANK, 2 * GLA_KEY_WIDTH), ((0, LANES - 2 * GLA_RANK), (0, 0))).astype(BF16)
    ab = jnp.transpose(alpha_b.reshape(2, GLA_HEADS, GLA_DK), (1, 0, 2)).reshape(1, 2 * GLA_KEY_WIDTH)
    gg = gla_g.reshape(1, GLA_DV)
    wa, wi = w_a.astype(BF16), w_i.astype(BF16)
    outs = []
    gla_state, lru_state = None, None
    for h, tabs in ((h_ctx, None), (h_lat, rope_tabs)):
        B = h.shape[0]
        if tabs is None:
            p_main = _proj(h, w_main, F32, tn=2560)
        else:
            p_main, *cast = _proj(h, w_main, F32, tn=2560, side=side, layer=layer)
        p_za = _proj(h, w_za, F32, tn=LANES)
        y_gla, gla_state = _gla(p_main, p_za, aup, ab, gg, tabs, gla_state)
        if lru_state is None:
            lru_state = (jnp.zeros((B, 1, LRU_WIDTH), F32),) * 2
        y_lru, lru_state = _lru(p_main, conv_w, conv_b, wa, b_a, wi, b_i, lam, lru_state)
        outs.append([y_gla, y_lru])
    return outs, cast


def kernel(x, c, ctx, c_ctx, w_mod, b_mod, norm_mix_g, norm_ffn_g, w_out, router_w, exp_w_gate, exp_w_up, exp_w_down, rec_w_in, gla_alpha_up, gla_alpha_b, gla_norm_g, lru_conv_w, lru_conv_b, lru_w_a, lru_b_a, lru_w_i, lru_b_i, lru_lambda, na_w_qkv, na_rpb, norm_f_g):
    B, T, D = x.shape
    pad_rows = (-(B + 1)) % 8
    cc = jnp.concatenate([c, c_ctx[None, :], jnp.zeros((pad_rows, D), F32)], axis=0)
    m_all = _modulation(cc, w_mod, b_mod)
    rope_tabs = _rope_tables(T)
    expert_w_f32 = (exp_w_gate, exp_w_up, exp_w_down)
    mods_lat = [m_all[l, :B].reshape(B, N_MOD, D) for l in range(DEPTH)]
    mods_ctx = [m_all[l, B:B + 1].reshape(1, N_MOD, D) for l in range(DEPTH)]
    h_lat = _norm(x, norm_mix_g[0], mods_lat[0])
    h_ctx = _norm(ctx, norm_mix_g[0], mods_ctx[0])
    for l in range(DEPTH):
        last = l == DEPTH - 1
        i = l // 2
        m_lat, m_ctx = mods_lat[l], mods_ctx[l]
        w_o = w_out[l].astype(BF16)
        if l % 2 == 0:
            (ys_ctx, ys_lat), expert_w = _rec_mixer(
                h_ctx, h_lat, rec_w_in[i], gla_alpha_up[i], gla_alpha_b[i], gla_norm_g[i], lru_conv_w[i], lru_conv_b[i],
                lru_w_a[i], lru_b_a[i], lru_w_i[i], lru_b_i[i], lru_lambda[i], rope_tabs, expert_w_f32, l)
            ws = [w_o[:GLA_VAL_WIDTH], w_o[GLA_VAL_WIDTH:]]
        else:
            w_qkv = na_w_qkv[i].astype(BF16)
            qkv_lat, *expert_w = _proj(h_lat, w_qkv, BF16, tn=3072, side=expert_w_f32, layer=l)
            qkv_ctx = _proj(h_ctx, w_qkv, BF16, tn=3072)
            y_ctx, y_lat = _na(qkv_lat, qkv_ctx, _na_bias_tables(na_rpb[i], T // GRID_W), not last)
            ys_ctx, ys_lat, ws = [y_ctx], [y_lat], [w_o]
        moe_w = (*expert_w, 0)
        router_wt = router_w[l].T
        x, h_ffn, logits_t = _oproj(ys_lat, ws, x, m_lat, norm_ffn_g[l], router_wt)
        if last:
            _, out = _moe(x, h_ffn, logits_t, m_lat, *moe_w, norm_f_g, None)
            return out
        x, h_lat = _moe(x, h_ffn, logits_t, m_lat, *moe_w, norm_mix_g[l + 1], mods_lat[l + 1])
        ctx, h_ffn, logits_t = _oproj(ys_ctx, ws, ctx, m_ctx, norm_ffn_g[l], router_wt)
        ctx, h_ctx = _moe(ctx, h_ffn, logits_t, m_ctx, *moe_w, norm_mix_g[l + 1], mods_ctx[l + 1])
```

```python
import functools

import numpy as np
import jax
import jax.numpy as jnp
from jax import lax
from jax.experimental import pallas as pl
from jax.experimental.pallas import tpu as pltpu

D_MODEL = 2048
DEPTH = 4
GRID_W = 64
RMS_EPS = 1e-6
N_MOD = 6
ROPE_THETA = 10000.0
GLA_HEADS = 4
GLA_DK = D_MODEL // 16
GLA_DV = D_MODEL // 8
GLA_KEY_WIDTH = GLA_HEADS * GLA_DK
GLA_VAL_WIDTH = GLA_HEADS * GLA_DV
GLA_RANK = 16
GLA_TAU = 16.0
GLA_CHUNK = 64
LRU_WIDTH = D_MODEL // 2
LRU_BLOCKS = 8
LRU_BLOCK_DIM = LRU_WIDTH // LRU_BLOCKS
LRU_CONV = 4
LRU_C = 8.0
NA_HEADS = 16
NA_HEAD_DIM = D_MODEL // NA_HEADS
NA_KH = 8
NA_KW = 16
N_EXPERTS = 16
EC_CAPACITY_FACTOR = 2
EXPERT_FF = D_MODEL // 2

F32 = jnp.float32
BF16 = jnp.bfloat16
HIGHEST = lax.Precision.HIGHEST
LANES = 128
NEG_BIG = -1e30
LOG2_E = float(np.log2(np.e))
NA_QROWS = 4
NA_KROWS = NA_QROWS + NA_KH - 1
PREFIX_CHUNK = 256
SLOT_WINDOW = 64
OPROJ_SUB_ROWS = 256
GLA_EMIT_CHUNKS = 4
GLA_PREP_ROWS = 512
LRU_SEGMENTS = 8
LRU_PITCH_PAD = 8
FFN_ROWS = 512
SLOT_ALIGN = 16

_NT = (((1,), (1,)), ((), ()))
_TN = (((0,), (0,)), ((), ()))


def _params(sem, vmem_mib=48):
    return pltpu.CompilerParams(dimension_semantics=sem, vmem_limit_bytes=vmem_mib * 2 ** 20)


def _aligned(index, multiple):
    return index if isinstance(index, int) else pl.multiple_of(index, multiple)


def _sigmoid(z):
    return 1.0 / (1.0 + jnp.exp(-z))


def _silu(z):
    return z * _sigmoid(z)


def _mod_kernel(c_ref, w_ref, b_ref, o_ref):
    a = _silu(c_ref[...]).astype(BF16)
    o_ref[0] = jnp.dot(a, w_ref[0].astype(BF16), preferred_element_type=F32) + b_ref[0]


def _modulation(cc, w_mod, b_mod):
    R, D = cc.shape
    L, _, N = w_mod.shape
    tn = 1024
    return pl.pallas_call(
        _mod_kernel,
        grid=(L, N // tn),
        in_specs=[pl.BlockSpec((R, D), lambda l, j: (0, 0)),
                  pl.BlockSpec((1, D, tn), lambda l, j: (l, 0, j)),
                  pl.BlockSpec((1, 1, tn), lambda l, j: (l, 0, j))],
        out_specs=pl.BlockSpec((1, R, tn), lambda l, j: (l, 0, j)),
        out_shape=jax.ShapeDtypeStruct((L, R, N), F32),
        compiler_params=_params(("arbitrary", "arbitrary")),
        name="modulation",
    )(cc, w_mod, b_mod.reshape(L, 1, N))


def _rms_modulate(x, g_ref, m_ref=None, shift_row=0, scale_row=1):
    y = x * lax.rsqrt(jnp.mean(x * x, axis=-1, keepdims=True) + RMS_EPS) * g_ref[...]
    if m_ref is not None:
        y = y * (1.0 + m_ref[0, scale_row:scale_row + 1, :]) + m_ref[0, shift_row:shift_row + 1, :]
    return y


def _norm_kernel(x_ref, g_ref, m_ref, h_ref):
    h_ref[0] = _rms_modulate(x_ref[0], g_ref, m_ref).astype(h_ref.dtype)


def _norm(x, g, mod):
    B, T, D = x.shape
    tt = min(T, 512)
    per_sample = mod.shape[0] > 1
    rows = pl.BlockSpec((1, tt, D), lambda b, t: (b, t, 0))
    return pl.pallas_call(
        _norm_kernel,
        grid=(B, T // tt),
        in_specs=[rows, pl.BlockSpec((1, D), lambda b, t: (0, 0)),
                  pl.BlockSpec((1, N_MOD, D), (lambda b, t: (b, 0, 0)) if per_sample else (lambda b, t: (0, 0, 0)))],
        out_specs=rows, out_shape=jax.ShapeDtypeStruct((B, T, D), BF16),
        compiler_params=_params(("arbitrary", "arbitrary")),
        name="rmsnorm",
    )(x, g.reshape(1, D), mod)


def _proj_kernel(*refs, n_side):
    a_ref, w_ref = refs[:2]
    side_in, o_ref, side_out = refs[2:2 + n_side], refs[2 + n_side], refs[3 + n_side:]
    o_ref[...] = jnp.dot(a_ref[...], w_ref[...], preferred_element_type=F32).astype(o_ref.dtype)
    for src, dst in zip(side_in, side_out):
        dst[...] = src[...].astype(dst.dtype)


def _proj(a, w, out_dtype, tn, side=(), layer=0):
    B, T, K = a.shape
    N = w.shape[1]
    M = B * T
    tm = min(M, 512)
    n_i = M // tm
    steps = (N // tn) * n_i
    flat = [s.reshape(s.shape[0], steps, -1, s.shape[-1]) for s in side]
    res = pl.pallas_call(
        functools.partial(_proj_kernel, n_side=len(side)),
        grid=(N // tn, n_i),
        in_specs=[pl.BlockSpec((tm, K), lambda j, i: (i, 0)), pl.BlockSpec((K, tn), lambda j, i: (0, j))]
                 + [pl.BlockSpec((1, 1) + f.shape[2:], lambda j, i: (layer, j * n_i + i, 0, 0)) for f in flat],
        out_specs=[pl.BlockSpec((tm, tn), lambda j, i: (i, j))]
                  + [pl.BlockSpec((1, 1) + f.shape[2:], lambda j, i: (0, j * n_i + i, 0, 0)) for f in flat],
        out_shape=[jax.ShapeDtypeStruct((M, N), out_dtype)]
                  + [jax.ShapeDtypeStruct((1,) + f.shape[1:], BF16) for f in flat],
        compiler_params=_params(("arbitrary", "arbitrary")),
        name="proj",
    )(a.reshape(M, K), w, *flat)
    out = res[0].reshape(B, T, N)
    return (out, *[r.reshape((1,) + s.shape[1:]) for r, s in zip(res[1:], side)]) if side else out


def _oproj_kernel(*refs, n_in, n_experts):
    a_refs, w_refs = refs[:n_in], refs[n_in:2 * n_in]
    x_ref, m_ref, g_ref, wr_ref, o_ref, h_ref, lg_ref = refs[2 * n_in:]
    tm = x_ref.shape[1]
    sub = min(tm, OPROJ_SUB_ROWS)
    for s in range(tm // sub):
        rows = slice(s * sub, (s + 1) * sub)
        acc = jnp.dot(a_refs[0][0, rows, :], w_refs[0][...], preferred_element_type=F32)
        for a_ref, w_ref in zip(a_refs[1:], w_refs[1:]):
            acc = acc + jnp.dot(a_ref[0, rows, :], w_ref[...], preferred_element_type=F32)
        x_new = x_ref[0, rows, :] + m_ref[0, 2:3, :] * acc
        o_ref[0, rows, :] = x_new
        y = _rms_modulate(x_new, g_ref, m_ref, shift_row=3, scale_row=4)
        y_hi = y.astype(BF16)
        h_ref[0, rows, :] = y_hi
        y_lo = (y - y_hi.astype(F32)).astype(BF16)
        parts = (jnp.dot(y_hi, wr_ref[...], preferred_element_type=F32)
                 + jnp.dot(y_lo, wr_ref[...], preferred_element_type=F32))
        lg_ref[0, :, rows] = jnp.transpose(parts + pltpu.roll(parts, LANES - n_experts, 1))


def _oproj(ys, ws, x, mod, g_ffn, router_wt):
    B, T, D = x.shape
    E = router_wt.shape[0]
    tm = min(T, 512)
    per_sample = mod.shape[0] > 1
    w_r_hi = router_wt.T.astype(BF16)
    w_r_lo = (router_wt.T - w_r_hi.astype(F32)).astype(BF16)
    w_r_split = jnp.pad(jnp.concatenate([w_r_hi, w_r_lo], axis=1), ((0, 0), (0, LANES - 2 * E)))
    rows = lambda n: pl.BlockSpec((1, tm, n), lambda b, t: (b, t, 0))
    in_specs = [rows(y.shape[2]) for y in ys]
    in_specs += [pl.BlockSpec(w.shape, lambda b, t: (0, 0)) for w in ws]
    in_specs += [rows(D),
                 pl.BlockSpec((1, N_MOD, D), (lambda b, t: (b, 0, 0)) if per_sample else (lambda b, t: (0, 0, 0))),
                 pl.BlockSpec((1, D), lambda b, t: (0, 0)), pl.BlockSpec((D, LANES), lambda b, t: (0, 0))]
    x_new, h, logits = pl.pallas_call(
        functools.partial(_oproj_kernel, n_in=len(ys), n_experts=E),
        grid=(B, T // tm), in_specs=in_specs,
        out_specs=[rows(D), rows(D), pl.BlockSpec((1, LANES, tm), lambda b, t: (b, 0, t))],
        out_shape=[jax.ShapeDtypeStruct((B, T, D), F32), jax.ShapeDtypeStruct((B, T, D), BF16),
                   jax.ShapeDtypeStruct((B, LANES, T), F32)],
        compiler_params=_params(("arbitrary", "arbitrary"), vmem_mib=56),
        name="oproj",
    )(*ys, *ws, x, mod, g_ffn.reshape(1, D), w_r_split)
    return x_new, h, logits


def _log_sigmoid(z):
    return jnp.minimum(z, 0.0) - jnp.log(1.0 + jnp.exp(-jnp.abs(z)))


def _gla_kernel(*refs, n_chunks, use_rope, has_state):
    it = iter(refs)
    q_ref, k_ref, v_ref, g_ref, za_ref = (next(it) for _ in range(5))
    cos_ref, sin_ref = (next(it), next(it)) if use_rope else (None, None)
    aup_ref, ab_ref, gg_ref = next(it), next(it), next(it)
    s0_ref = next(it) if has_state else None
    y_ref, sfin_ref = next(it), next(it)
    qd_scr, ki_scr, ke_scr, kv_scr, dec_scr, sall_scr, st_scr = (next(it) for _ in range(7))
    C, DK = GLA_CHUNK, GLA_DK
    T = n_chunks * C
    RB = min(T, GLA_PREP_ROWS)
    cpb = RB // C

    def increments(blk):
        for i in range(cpb):
            c = blk * cpb + i
            rows = pl.ds(_aligned(c * C, C), C)
            kv_scr[c] = lax.dot_general(v_ref[0, rows, :].astype(BF16), ke_scr[c], _TN, preferred_element_type=F32)

    def prepare(blk):
        r0 = _aligned(blk * RB, RB)
        rows = pl.ds(r0, RB)
        z = jnp.dot(za_ref[0, rows, :].astype(BF16), aup_ref[...], preferred_element_type=F32) + ab_ref[...]
        la = _log_sigmoid(z) * (1.0 / GLA_TAU)
        in_chunk = lax.broadcasted_iota(jnp.int32, la.shape, 0) % C
        prefix = la
        step = 1
        while step < C:
            prefix = prefix + jnp.where(in_chunk >= step, pltpu.roll(prefix, step, 0), 0.0)
            step *= 2
        q = q_ref[0, rows, :] * (GLA_DK ** -0.5)
        k = k_ref[0, rows, :]
        if use_rope:
            cos, sin = cos_ref[rows, :], sin_ref[rows, :]
            even = (lax.broadcasted_iota(jnp.int32, q.shape, 1) % 2) == 0

            def rope(t):
                swapped = jnp.where(even, pltpu.roll(t, LANES - 1, 1), pltpu.roll(t, 1, 1))
                return t * cos + swapped * sin
            q, k = rope(q), rope(k)
        split = lambda t: t.reshape(cpb, C, t.shape[-1])
        la, prefix, q, k = split(la), split(prefix), split(q), split(k)
        tot = prefix[:, C - 1:C, :]
        b_f = prefix[..., :DK]
        b_b = tot[..., DK:] - prefix[..., DK:] + la[..., DK:]
        dec = jnp.exp(tot)
        c0 = _aligned(blk * cpb, cpb)
        qd_scr[pl.ds(c0, cpb)] = jnp.concatenate([q * jnp.exp(b_f), q * jnp.exp(b_b)], axis=2).astype(BF16)
        ki_f, ki_b = k * jnp.exp(-b_f), k * jnp.exp(-b_b)
        ki_scr[0, pl.ds(c0, cpb)] = ki_f.astype(BF16)
        ki_scr[1, pl.ds(c0, cpb)] = ki_b.astype(BF16)
        ke_scr[pl.ds(c0, cpb)] = jnp.concatenate([ki_f * dec[..., :DK], ki_b * dec[..., DK:]], axis=2).astype(BF16)
        dec_scr[pl.ds(c0, cpb)] = jnp.broadcast_to(dec, (cpb,) + dec_scr.shape[1:])

    def prepare_and_increment(blk, carry):
        increments(blk - 1)
        prepare(blk)
        return carry

    n_blocks = T // RB
    prepare(0)
    lax.fori_loop(1, n_blocks, prepare_and_increment, 0)
    increments(n_blocks - 1)

    st_scr[...] = s0_ref[0, 0] if has_state else jnp.zeros(st_scr.shape, F32)

    def recur(i, carry):
        cf, cb = i, n_chunks - 1 - i
        st = st_scr[...]
        sall_scr[cf, :, :DK] = st[:, :DK].astype(BF16)
        sall_scr[cb, :, DK:] = st[:, DK:].astype(BF16)
        st_scr[:, :DK] = st[:, :DK] * dec_scr[cf, 0:1, :DK] + kv_scr[cf, :, :DK]
        st_scr[:, DK:] = st[:, DK:] * dec_scr[cb, 0:1, DK:] + kv_scr[cb, :, DK:]
        return carry

    lax.fori_loop(0, n_chunks, recur, 0)
    sfin_ref[0, 0] = st_scr[...]

    G = min(n_chunks, GLA_EMIT_CHUNKS)
    GR = G * C
    row = lax.broadcasted_iota(jnp.int32, (GR, GR), 0)
    col = lax.broadcasted_iota(jnp.int32, (GR, GR), 1)
    same_chunk = (row // C) == (col // C)
    keep_f, keep_b = same_chunk & (row >= col), same_chunk & (row <= col)

    def emit(grp, carry):
        c0 = pl.multiple_of(grp * G, G)
        rows = pl.ds(pl.multiple_of(grp * GR, GR), GR)
        qd = qd_scr[pl.ds(c0, G)].reshape(GR, 2 * DK)
        ki_f = ki_scr[0, pl.ds(c0, G)].reshape(GR, DK)
        ki_b = ki_scr[1, pl.ds(c0, G)].reshape(GR, DK)
        s = (jnp.where(keep_f, lax.dot_general(qd[:, :DK], ki_f, _NT, preferred_element_type=F32), 0.0)
             + jnp.where(keep_b, lax.dot_general(qd[:, DK:], ki_b, _NT, preferred_element_type=F32), 0.0))
        carried = [lax.dot_general(qd[i * C:(i + 1) * C], sall_scr[c0 + i], _NT, preferred_element_type=F32)
                   for i in range(G)]
        o = (jnp.dot(s.astype(BF16), v_ref[0, rows, :].astype(BF16), preferred_element_type=F32)
             + jnp.concatenate(carried, axis=0))
        o = o * lax.rsqrt(jnp.mean(o * o, axis=-1, keepdims=True) + RMS_EPS) * gg_ref[...]
        y_ref[0, rows, :] = (o * _silu(g_ref[0, rows, :])).astype(y_ref.dtype)
        return carry

    lax.fori_loop(0, n_chunks // G, emit, 0, unroll=min(4, n_chunks // G))


def _gla(p_main, p_za, aup, ab, gg, rope_tabs, state):
    B, T, _ = p_main.shape
    H, DK, DV = GLA_HEADS, GLA_DK, GLA_DV
    use_rope, has_state = rope_tabs is not None, state is not None
    n_chunks = T // GLA_CHUNK
    in_specs = [pl.BlockSpec((1, T, DK), lambda b, h: (b, 0, h)),
                pl.BlockSpec((1, T, DK), lambda b, h: (b, 0, H + h)),
                pl.BlockSpec((1, T, DV), lambda b, h: (b, 0, H + h)),
                pl.BlockSpec((1, T, DV), lambda b, h: (b, 0, 2 * H + h)),
                pl.BlockSpec((1, T, LANES), lambda b, h: (b, 0, 0))]
    args = [p_main, p_main, p_main, p_main, p_za]
    if use_rope:
        in_specs += [pl.BlockSpec((T, DK), lambda b, h: (0, 0))] * 2
        args += list(rope_tabs)
    in_specs += [pl.BlockSpec((LANES, 2 * DK), lambda b, h: (0, h)),
                 pl.BlockSpec((1, 2 * DK), lambda b, h: (0, h)),
                 pl.BlockSpec((1, DV), lambda b, h: (0, 0))]
    args += [aup, ab, gg]
    st_spec = pl.BlockSpec((1, 1, DV, 2 * DK), lambda b, h: (b, h, 0, 0))
    if has_state:
        in_specs.append(st_spec)
        args.append(state)
    return pl.pallas_call(
        functools.partial(_gla_kernel, n_chunks=n_chunks, use_rope=use_rope, has_state=has_state),
        grid=(B, H), in_specs=in_specs,
        out_specs=[pl.BlockSpec((1, T, DV), lambda b, h: (b, 0, h)), st_spec],
        out_shape=[jax.ShapeDtypeStruct((B, T, H * DV), BF16), jax.ShapeDtypeStruct((B, H, DV, 2 * DK), F32)],
        scratch_shapes=[pltpu.VMEM((n_chunks, GLA_CHUNK, 2 * DK), BF16), pltpu.VMEM((2, n_chunks, GLA_CHUNK, DK), BF16),
                        pltpu.VMEM((n_chunks, GLA_CHUNK, 2 * DK), BF16),
                        pltpu.VMEM((n_chunks, DV, 2 * DK), F32), pltpu.VMEM((n_chunks, 8, 2 * DK), F32),
                        pltpu.VMEM((n_chunks, DV, 2 * DK), BF16), pltpu.VMEM((DV, 2 * DK), F32)],
        compiler_params=_params(("arbitrary", "arbitrary")),
        name="gla",
    )(*args)


def _gelu_tanh(z):
    return 0.5 * z * (1.0 + jnp.tanh(np.sqrt(2.0 / np.pi) * (z + 0.044715 * (z * z * z))))


def _lru_kernel(x_ref, gt_ref, cw_ref, cb_ref, wa_ref, ba_ref, wi_ref, bi_ref, lam_ref, h0f_ref, h0b_ref,
                y_ref, hf_ref, hb_ref, a_scr, bx_scr, hl_scr, ap_scr, *, T):
    L = T // LRU_SEGMENTS
    P = L + LRU_PITCH_PAD
    x = x_ref[0]
    row = lax.broadcasted_iota(jnp.int32, x.shape, 0)

    def from_earlier(v, d, fill):
        return jnp.where(row >= d, pltpu.roll(v, d, 0), fill)

    def from_later(v, d, fill):
        return jnp.where(row < T - d, pltpu.roll(v, T - d, 0), fill)

    xc = (cw_ref[0:1, :] * from_earlier(x, 1, 0.0) + cw_ref[1:2, :] * x
          + cw_ref[2:3, :] * from_later(x, 1, 0.0) + cw_ref[3:4, :] * from_later(x, 2, 0.0) + cb_ref[...])
    xcb = xc.astype(BF16)

    def sigmoid(z):
        return 0.5 * jnp.tanh(0.5 * z) + 0.5

    for d in range(2):
        r = sigmoid(jnp.dot(xcb, wa_ref[d, 0], preferred_element_type=F32) + ba_ref[d])
        gate_i = sigmoid(jnp.dot(xcb, wi_ref[d, 0], preferred_element_type=F32) + bi_ref[d])
        neg_lam = -lam_ref[d]
        softplus = jnp.maximum(neg_lam, 0.0) + jnp.log(1.0 + jnp.exp(-jnp.abs(neg_lam)))
        a = jnp.exp((-LRU_C) * r * softplus)
        u = 1.0 - a * a
        bx = jnp.where(u > 0.0, u * lax.rsqrt(u), 0.0) * (gate_i * xc)
        for s in range(LRU_SEGMENTS):
            a_scr[d, s * P:s * P + L, :] = a[s * L:(s + 1) * L]
            bx_scr[d, s * P:s * P + L, :] = bx[s * L:(s + 1) * L]

    def step(i, carry):
        h_f, p_f, h_b, p_b = carry
        at_f = pl.ds(i, LRU_SEGMENTS, stride=P)
        at_b = pl.ds(L - 1 - i, LRU_SEGMENTS, stride=P)
        a_f, a_b = a_scr[0, at_f, :], a_scr[1, at_b, :]
        h_f = a_f * h_f + bx_scr[0, at_f, :]
        h_b = a_b * h_b + bx_scr[1, at_b, :]
        p_f, p_b = a_f * p_f, a_b * p_b
        hl_scr[0, at_f, :] = h_f
        hl_scr[1, at_b, :] = h_b
        ap_scr[0, at_f, :] = p_f
        ap_scr[1, at_b, :] = p_b
        return h_f, p_f, h_b, p_b

    zeros, ones = jnp.zeros((LRU_SEGMENTS, x.shape[1]), F32), jnp.ones((LRU_SEGMENTS, x.shape[1]), F32)
    h_f, p_f, h_b, p_b = lax.fori_loop(0, L, step, (zeros, ones, zeros, ones), unroll=8)

    enter_f, enter_b = [h0f_ref[0]], [h0b_ref[0]]
    for s in range(LRU_SEGMENTS):
        enter_f.append(h_f[s:s + 1] + p_f[s:s + 1] * enter_f[-1])
        t = LRU_SEGMENTS - 1 - s
        enter_b.append(h_b[t:t + 1] + p_b[t:t + 1] * enter_b[-1])
    hf_ref[0] = enter_f[-1]
    hb_ref[0] = enter_b[-1]
    for s in range(LRU_SEGMENTS):
        seg = slice(s * P, s * P + L)
        h = (hl_scr[0, seg, :] + ap_scr[0, seg, :] * enter_f[s]
             + hl_scr[1, seg, :] + ap_scr[1, seg, :] * enter_b[LRU_SEGMENTS - 1 - s])
        y_ref[0, s * L:(s + 1) * L, :] = (h * _gelu_tanh(gt_ref[0, s * L:(s + 1) * L, :])).astype(y_ref.dtype)


def _lru(p_main, cw, cb, wa, ba, wi, bi, lam, states):
    B, T, _ = p_main.shape
    W, NB, BD = LRU_WIDTH, LRU_BLOCKS, LRU_BLOCK_DIM
    x_blk0 = (2 * GLA_KEY_WIDTH + 2 * GLA_VAL_WIDTH) // BD
    vec = lambda n: pl.BlockSpec((n, 1, BD), lambda b, j: (0, 0, j))
    mat = pl.BlockSpec((2, 1, BD, BD), lambda b, j: (0, j, 0, 0))
    st_spec = pl.BlockSpec((1, 1, BD), lambda b, j: (b, 0, j))
    st_shape = jax.ShapeDtypeStruct((B, 1, W), F32)
    y, hf, hb = pl.pallas_call(
        functools.partial(_lru_kernel, T=T),
        grid=(B, NB),
        in_specs=[pl.BlockSpec((1, T, BD), lambda b, j: (b, 0, x_blk0 + j)),
                  pl.BlockSpec((1, T, BD), lambda b, j: (b, 0, x_blk0 + NB + j)),
                  pl.BlockSpec((LRU_CONV, BD), lambda b, j: (0, j)),
                  pl.BlockSpec((1, BD), lambda b, j: (0, j)),
                  mat, vec(2), mat, vec(2), vec(2), st_spec, st_spec],
        out_specs=[pl.BlockSpec((1, T, BD), lambda b, j: (b, 0, j)), st_spec, st_spec],
        out_shape=[jax.ShapeDtypeStruct((B, T, W), BF16), st_shape, st_shape],
        scratch_shapes=[pltpu.VMEM((2, LRU_SEGMENTS * (T // LRU_SEGMENTS + LRU_PITCH_PAD), BD), F32)] * 4,
        compiler_params=_params(("arbitrary", "arbitrary")),
        name="rglru",
    )(p_main, p_main, cw, cb.reshape(1, W), wa, ba.reshape(2, 1, W), wi, bi.reshape(2, 1, W),
      lam.reshape(2, 1, W), *states)
    return y, (hf, hb)


def _na_tile(i, n_tiles, rows):
    r0 = i * NA_QROWS
    ks = min(max(r0 - NA_KH // 2, 0), rows - NA_KROWS)
    cls = 0 if i == 0 else (2 if i == n_tiles - 1 else 1)
    return r0, ks, cls


def _na_bias_tables(rpb, rows):
    H = rpb.shape[0]
    qc = np.arange(GRID_W)[:, None]
    kc = np.arange(GRID_W)[None, :]
    col_start = np.clip(qc - NA_KW // 2, 0, GRID_W - NA_KW)
    col_valid = (kc >= col_start) & (kc < col_start + NA_KW)
    dc = np.clip(kc - qc, -(NA_KW - 1), NA_KW - 1) + NA_KW - 1
    pick = (dc[None] == np.arange(2 * NA_KW - 1)[:, None, None]).astype(np.float32)
    band = jnp.einsum('hrd,dqk->hrqk', rpb.astype(F32), jnp.asarray(pick), precision=HIGHEST)
    band = jnp.where(col_valid, band, NEG_BIG)
    masked = jnp.full((H, GRID_W, GRID_W), NEG_BIG, F32)
    n_tiles = rows // NA_QROWS
    tabs = []
    for i in (0, 1, n_tiles - 1):
        r0, ks, _ = _na_tile(i, n_tiles, rows)
        per_qrow = []
        for qr in range(r0, r0 + NA_QROWS):
            row_start = min(max(qr - NA_KH // 2, 0), rows - NA_KH)
            blocks = [band[:, kr - qr + NA_KH - 1] if row_start <= kr < row_start + NA_KH else masked
                      for kr in range(ks, ks + NA_KROWS)]
            per_qrow.append(jnp.stack(blocks, axis=2).reshape(H, GRID_W, NA_KROWS * GRID_W))
        tabs.append(jnp.concatenate(per_qrow, axis=1))
    return jnp.stack(tabs, axis=1) * LOG2_E


def _na_kernel(*refs, rows, need_ctx):
    q_ref, k_ref, v_ref, qc_ref, kc_ref, vc_ref, tab_ref = refs[:7]
    y_ref, vx_ref = refs[7], refs[-1]
    HD = NA_HEAD_DIM
    scale = HD ** -0.5 * LOG2_E
    vx_ref[...] = jnp.concatenate([v_ref[0], jnp.ones(v_ref.shape[1:], BF16)], axis=1)
    kc = kc_ref[0]
    vcx = jnp.concatenate([vc_ref[0], jnp.ones(vc_ref.shape[1:], BF16)], axis=1)
    n_tiles = rows // NA_QROWS
    nq, nk = NA_QROWS * GRID_W, NA_KROWS * GRID_W
    for i in range(n_tiles):
        r0, ks, cls = _na_tile(i, n_tiles, rows)
        q = q_ref[0, r0 * GRID_W:r0 * GRID_W + nq, :]
        kt = k_ref[0, ks * GRID_W:ks * GRID_W + nk, :]
        s_loc = lax.dot_general(q, kt, _NT, preferred_element_type=F32) * scale + tab_ref[0, cls]
        s_ctx = lax.dot_general(q, kc, _NT, preferred_element_type=F32) * scale
        m = jnp.maximum(jnp.max(s_loc, axis=-1, keepdims=True), jnp.max(s_ctx, axis=-1, keepdims=True))
        o = (jnp.dot(jnp.exp2(s_loc - m).astype(BF16), vx_ref[ks * GRID_W:ks * GRID_W + nk, :], preferred_element_type=F32)
             + jnp.dot(jnp.exp2(s_ctx - m).astype(BF16), vcx, preferred_element_type=F32))
        y_ref[0, r0 * GRID_W:r0 * GRID_W + nq, :] = (o[:, :HD] / o[:, HD:]).astype(y_ref.dtype)
    if need_ctx:
        yc_ref = refs[8]
        s = lax.dot_general(qc_ref[0], kc, _NT, preferred_element_type=F32) * scale
        p = jnp.exp2(s - jnp.max(s, axis=-1, keepdims=True))
        o = jnp.dot(p.astype(BF16), vcx, preferred_element_type=F32)
        yc_ref[0] = (o[:, :HD] / o[:, HD:]).astype(yc_ref.dtype)


def _na(qkv_lat, qkv_ctx, tabs, need_ctx):
    B, T, _ = qkv_lat.shape
    Tc = qkv_ctx.shape[1]
    H, HD = NA_HEADS, NA_HEAD_DIM
    blk = lambda t, off: pl.BlockSpec((1, t, HD), lambda h, b: (b, 0, off + h))
    out_specs = [pl.BlockSpec((1, T, HD), lambda h, b: (b, 0, h))]
    out_shape = [jax.ShapeDtypeStruct((B, T, H * HD), BF16)]
    if need_ctx:
        out_specs.append(pl.BlockSpec((1, Tc, HD), lambda h, b: (b, 0, h)))
        out_shape.append(jax.ShapeDtypeStruct((B, Tc, H * HD), BF16))
    res = pl.pallas_call(
        functools.partial(_na_kernel, rows=T // GRID_W, need_ctx=need_ctx),
        grid=(H, B),
        in_specs=[blk(T, 0), blk(T, H), blk(T, 2 * H), blk(Tc, 0), blk(Tc, H), blk(Tc, 2 * H),
                  pl.BlockSpec((1,) + tabs.shape[1:], lambda h, b: (h, 0, 0, 0))],
        out_specs=out_specs, out_shape=out_shape,
        scratch_shapes=[pltpu.VMEM((T, 2 * HD), BF16)],
        compiler_params=_params(("arbitrary", "arbitrary")),
        name="natten",
    )(qkv_lat, qkv_lat, qkv_lat, qkv_ctx, qkv_ctx, qkv_ctx, tabs)
    return (res[1] if need_ctx else None), res[0]


def _route_kernel(lg_ref, pos_ref, aff_ref, lo_ref, *, cap):
    lg = lg_ref[0]
    E, N = lg.shape
    ex = jnp.exp(lg - jnp.max(lg, axis=0, keepdims=True))
    aff = ex / jnp.sum(ex, axis=0, keepdims=True)
    aff_ref[0] = aff
    bits = pltpu.bitcast(aff, jnp.int32)
    thr = jnp.zeros((E, 1), jnp.int32)
    for bit in range(29, -1, -1):
        cand = thr | (1 << bit)
        cnt = jnp.sum(jnp.where(bits >= cand, 1.0, 0.0), axis=1, keepdims=True)
        thr = jnp.where(cnt >= cap, cand, thr)
    above = bits > thr
    tied = bits == thr
    need = cap - jnp.sum(jnp.where(above, 1.0, 0.0), axis=1, keepdims=True)
    pc = min(PREFIX_CHUNK, N)
    before = jnp.where(lax.broadcasted_iota(jnp.int32, (pc, pc), 0) < lax.broadcasted_iota(jnp.int32, (pc, pc), 1),
                       1.0, 0.0).astype(BF16)
    tied_seen = jnp.zeros((E, 1), F32)
    sel_seen = jnp.zeros((E, 1), F32)
    tile_lane = lax.broadcasted_iota(jnp.int32, (E, LANES), 1)
    lo = jnp.where(tile_lane == N // pc, float(cap), 0.0)
    for c in range(N // pc):
        sl = slice(c * pc, (c + 1) * pc)
        lo = jnp.where(tile_lane == c, sel_seen, lo)
        tied_c = jnp.where(tied[:, sl], 1.0, 0.0)
        tied_before = jnp.dot(tied_c.astype(BF16), before, preferred_element_type=F32) + tied_seen
        sel_c = jnp.where(above[:, sl], 1.0, jnp.where(tied_before < need, tied_c, 0.0))
        sel_before = jnp.dot(sel_c.astype(BF16), before, preferred_element_type=F32) + sel_seen
        pos_ref[0, :, sl] = jnp.where(sel_c > 0.0, sel_before, -1.0)
        tied_seen = tied_seen + jnp.sum(tied_c, axis=1, keepdims=True)
        sel_seen = sel_seen + jnp.sum(sel_c, axis=1, keepdims=True)
    lo_ref[0] = lo.astype(jnp.int32)


def _route(logits_t, cap):
    B, _, N = logits_t.shape
    E = N_EXPERTS
    spec = pl.BlockSpec((1, E, N), lambda b: (b, 0, 0))
    lo_spec = pl.BlockSpec((1, E, LANES), lambda b: (b, 0, 0))
    pos, aff, lo = pl.pallas_call(
        functools.partial(_route_kernel, cap=cap),
        grid=(B,), in_specs=[spec], out_specs=[spec, spec, lo_spec],
        out_shape=[jax.ShapeDtypeStruct((B, E, N), F32)] * 2 + [jax.ShapeDtypeStruct((B, E, LANES), jnp.int32)],
        compiler_params=_params(("arbitrary",)),
        name="route",
    )(logits_t)
    n_bounds = N // min(PREFIX_CHUNK, N) + 1
    return pos, aff, lo[:, :, :n_bounds].reshape(-1)


def _window_count(lo_ref, base, experts, tile, n_bounds, win):
    starts, n_win = [], 0
    for e in experts:
        first = lo_ref[(base + e) * n_bounds + tile]
        end = lo_ref[(base + e) * n_bounds + tile + 1]
        start = (first // SLOT_ALIGN) * SLOT_ALIGN
        starts.append(start)
        n_win = jnp.maximum(n_win, (end - start + win - 1) // win)
    return starts, n_win


def _gather_kernel(lo_ref, pos_ref, h_ref, xs_ref, *, cap, win, tn):
    EG, N = pos_ref.shape[1], pos_ref.shape[2]
    n_bounds = N // tn + 1
    base = pl.program_id(0) * (EG * pl.num_programs(1)) + pl.program_id(1) * EG
    xs_ref[...] = jnp.zeros(xs_ref.shape, xs_ref.dtype)
    slot_iota = lax.broadcasted_iota(jnp.int32, (win, tn), 0)
    for j in range(N // tn):
        h_tile = h_ref[0, j * tn:(j + 1) * tn, :]
        starts, n_win = _window_count(lo_ref, base, range(EG), j, n_bounds, win)

        def window(w, carry):
            pieces, offs = [], []
            for e in range(EG):
                first = starts[e] + w * win
                off = jnp.minimum(first, cap - win)
                prow = pos_ref[0, e:e + 1, j * tn:(j + 1) * tn]
                hit = ((slot_iota + off).astype(F32) == prow) & (prow >= first.astype(F32))
                pieces.append(jnp.where(hit, 1.0, 0.0).astype(BF16))
                offs.append(pl.multiple_of(off, SLOT_ALIGN))
            got = jnp.dot(jnp.concatenate(pieces, axis=0), h_tile, preferred_element_type=F32)
            for e in range(EG):
                xs_ref[0, e, pl.ds(offs[e], win), :] += got[e * win:(e + 1) * win].astype(xs_ref.dtype)
            return carry

        lax.fori_loop(0, n_win, window, 0)


def _gather(lo, pos, h, cap):
    B, E, N = pos.shape
    D = h.shape[2]
    EG = 8
    tn, win = min(PREFIX_CHUNK, N), min(SLOT_WINDOW, cap)
    return pl.pallas_call(
        functools.partial(_gather_kernel, cap=cap, win=win, tn=tn),
        grid_spec=pltpu.PrefetchScalarGridSpec(
            num_scalar_prefetch=1, grid=(B, E // EG),
            in_specs=[pl.BlockSpec((1, EG, N), lambda b, g, lo: (b, g, 0)),
                      pl.BlockSpec((1, N, D), lambda b, g, lo: (b, 0, 0))],
            out_specs=pl.BlockSpec((1, EG, cap, D), lambda b, g, lo: (b, g, 0, 0))),
        out_shape=jax.ShapeDtypeStruct((B, E, cap, D), BF16),
        compiler_params=_params(("arbitrary", "arbitrary")),
        name="moe_gather",
    )(lo, pos, h)


def _ffn_kernel(xs_ref, wg_ref, wu_ref, wd_ref, y_ref):
    bb, _, cap, D = xs_ref.shape
    xs = xs_ref[:, 0].reshape(bb * cap, D)
    hid = (_silu(jnp.dot(xs, wg_ref[0, 0], preferred_element_type=F32))
           * jnp.dot(xs, wu_ref[0, 0], preferred_element_type=F32))
    y = jnp.dot(hid.astype(BF16), wd_ref[0, 0], preferred_element_type=F32)
    y_ref[:, 0] = y.reshape(bb, cap, D).astype(y_ref.dtype)


def _ffn(xs, wg, wu, wd, layer):
    B, E, cap, D = xs.shape
    F = wg.shape[3]
    bb = max(1, min(B, FFN_ROWS // cap))
    tok = pl.BlockSpec((bb, 1, cap, D), lambda e, b: (b, e, 0, 0))
    return pl.pallas_call(
        _ffn_kernel,
        grid=(E, B // bb),
        in_specs=[tok, pl.BlockSpec((1, 1, D, F), lambda e, b: (layer, e, 0, 0)),
                  pl.BlockSpec((1, 1, D, F), lambda e, b: (layer, e, 0, 0)),
                  pl.BlockSpec((1, 1, F, D), lambda e, b: (layer, e, 0, 0))],
        out_specs=tok,
        out_shape=jax.ShapeDtypeStruct((B, E, cap, D), BF16),
        compiler_params=_params(("arbitrary", "arbitrary")),
        name="moe_ffn",
    )(xs, wg, wu, wd)


def _combine_kernel(*refs, cap, win, keep_x, next_modulated):
    it = iter(refs)
    lo_ref, pos_ref, aff_ref, y_ref, x_ref, m_ref, gn_ref = (next(it) for _ in range(7))
    mn_ref = next(it) if next_modulated else None
    o_ref = next(it) if keep_x else None
    hn_ref, ystack_ref, acc_ref = next(it), next(it), next(it)
    E = y_ref.shape[1]
    n_bounds = pl.num_programs(1) + 1
    j = pl.program_id(1)
    starts, n_win = _window_count(lo_ref, pl.program_id(0) * E, range(E), j, n_bounds, win)
    col = lax.broadcasted_iota(jnp.int32, (E, E * win), 1)
    spread = jnp.where(col // win == lax.broadcasted_iota(jnp.int32, (E, E * win), 0), 1.0, 0.0).astype(BF16)
    slot = lax.dot_general(pos_ref[0].astype(BF16), spread, _TN, preferred_element_type=F32)
    gate = lax.dot_general(aff_ref[0].astype(BF16), spread, _TN, preferred_element_type=F32)
    col1 = lax.broadcasted_iota(jnp.int32, (1, E * win), 1)
    col_expert, col_in_win = col1 // win, col1 % win
    acc_ref[...] = jnp.zeros(acc_ref.shape, F32)

    def window(w, carry):
        first_col = jnp.zeros((1, E * win), jnp.int32)
        want_col = jnp.zeros((1, E * win), jnp.int32)
        for e in range(E):
            first = starts[e] + w * win
            off = jnp.minimum(first, cap - win)
            ystack_ref[e * win:(e + 1) * win, :] = y_ref[0, e, pl.ds(pl.multiple_of(off, SLOT_ALIGN), win), :]
            first_col = jnp.where(col_expert == e, first, first_col)
            want_col = jnp.where(col_expert == e, off + col_in_win, want_col)
        hit = (slot == want_col.astype(F32)) & (slot >= first_col.astype(F32))
        acc_ref[...] += jnp.dot(jnp.where(hit, gate, 0.0).astype(BF16), ystack_ref[...], preferred_element_type=F32)
        return carry

    lax.fori_loop(0, n_win, window, 0)
    x_new = x_ref[0] + m_ref[0, 5:6, :] * acc_ref[...]
    if keep_x:
        o_ref[0] = x_new
    hn_ref[0] = _rms_modulate(x_new, gn_ref, mn_ref).astype(hn_ref.dtype)


def _combine(lo, pos, aff, y, x, mod, g_next, mod_next):
    B, E, cap, D = y.shape
    N = x.shape[1]
    tn, win = min(PREFIX_CHUNK, N), min(SLOT_WINDOW, cap)
    keep_x = next_modulated = mod_next is not None
    tok = pl.BlockSpec((1, E, tn), lambda b, t, lo: (b, 0, t))
    rows = pl.BlockSpec((1, tn, D), lambda b, t, lo: (b, t, 0))
    mod_spec = lambda m: pl.BlockSpec((1, N_MOD, D),
                                      (lambda b, t, lo: (b, 0, 0)) if m.shape[0] > 1 else (lambda b, t, lo: (0, 0, 0)))
    in_specs = [tok, tok, pl.BlockSpec((1, E, cap, D), lambda b, t, lo: (b, 0, 0, 0)), rows, mod_spec(mod),
                pl.BlockSpec((1, D), lambda b, t, lo: (0, 0))]
    args = [lo, pos, aff, y, x, mod, g_next.reshape(1, D)]
    out_specs, out_shape = [rows], [jax.ShapeDtypeStruct((B, N, D), BF16 if next_modulated else F32)]
    if next_modulated:
        in_specs.append(mod_spec(mod_next))
        args.append(mod_next)
        out_specs, out_shape = [rows] + out_specs, [jax.ShapeDtypeStruct((B, N, D), F32)] + out_shape
    res = pl.pallas_call(
        functools.partial(_combine_kernel, cap=cap, win=win, keep_x=keep_x, next_modulated=next_modulated),
        grid_spec=pltpu.PrefetchScalarGridSpec(
            num_scalar_prefetch=1, grid=(B, N // tn), in_specs=in_specs, out_specs=out_specs,
            scratch_shapes=[pltpu.VMEM((E * win, D), BF16), pltpu.VMEM((tn, D), F32)]),
        out_shape=out_shape,
        compiler_params=_params(("arbitrary", "arbitrary"), vmem_mib=56),
        name="moe_combine",
    )(*args)
    return (res[0], res[1]) if keep_x else (None, res[0])


def _moe(x, h, logits_t, mod, wg, wu, wd, layer, g_next, mod_next):
    N = x.shape[1]
    cap = EC_CAPACITY_FACTOR * N // N_EXPERTS
    assert cap <= 256, "slot ids must stay exact in bf16"
    pos, aff, lo = _route(logits_t, cap)
    y = _ffn(_gather(lo, pos, h, cap), wg, wu, wd, layer)
    return _combine(lo, pos, aff, y, x, mod, g_next, mod_next)


def _rope_tables(n_tok):
    t = jnp.arange(n_tok)
    rows = (t // GRID_W).astype(F32)
    cols = (t % GRID_W).astype(F32)
    n_freq = GLA_DK // 4
    inv = ROPE_THETA ** (-jnp.arange(n_freq, dtype=F32) / n_freq)
    ang = jnp.concatenate([rows[:, None] * inv, cols[:, None] * inv], axis=-1)
    cos = jnp.repeat(jnp.cos(ang), 2, axis=-1)
    sin = jnp.repeat(jnp.sin(ang), 2, axis=-1) * jnp.tile(jnp.asarray([-1.0, 1.0], F32), GLA_DK // 2)
    return cos, sin


def _rec_mixer(h_ctx, h_lat, w_in, alpha_up, alpha_b, gla_g, conv_w, conv_b, w_a, b_a, w_i, b_i, lam, rope_tabs,
               side, layer):
    za0 = 2 * GLA_KEY_WIDTH + 2 * GLA_VAL_WIDTH
    za1 = za0 + 2 * GLA_RANK
    w_main = jnp.concatenate([w_in[:, :za0], w_in[:, za1:]], axis=1).astype(BF16)
    w_za = jnp.pad(w_in[:, za0:za1], ((0, 0), (0, LANES - 2 * GLA_RANK))).astype(BF16)
    up = alpha_up.reshape(2, GLA_RANK, GLA_HEADS, 1, GLA_DK)
    zero = jnp.zeros_like(up[0])
    aup = jnp.concatenate([jnp.concatenate([up[0], zero], axis=2), jnp.concatenate([zero, up[1]], axis=2)], axis=0)
    aup = jnp.pad(aup.reshape(2 * GLA_RANK, 2 * GLA_KEY_WIDTH), ((0, LANES - 2 * GLA_RANK), (0, 0))).astype(BF16)
    ab = jnp.transpose(alpha_b.reshape(2, GLA_HEADS, GLA_DK), (1, 0, 2)).reshape(1, 2 * GLA_KEY_WIDTH)
    gg = gla_g.reshape(1, GLA_DV)
    wa, wi = w_a.astype(BF16), w_i.astype(BF16)
    outs = []
    gla_state, lru_state = None, None
    for h, tabs in ((h_ctx, None), (h_lat, rope_tabs)):
        B = h.shape[0]
        if tabs is None:
            p_main = _proj(h, w_main, F32, tn=2560)
        else:
            p_main, *cast = _proj(h, w_main, F32, tn=2560, side=side, layer=layer)
        p_za = _proj(h, w_za, F32, tn=LANES)
        y_gla, gla_state = _gla(p_main, p_za, aup, ab, gg, tabs, gla_state)
        if lru_state is None:
            lru_state = (jnp.zeros((B, 1, LRU_WIDTH), F32),) * 2
        y_lru, lru_state = _lru(p_main, conv_w, conv_b, wa, b_a, wi, b_i, lam, lru_state)
        outs.append([y_gla, y_lru])
    return outs, cast


def kernel(x, c, ctx, c_ctx, w_mod, b_mod, norm_mix_g, norm_ffn_g, w_out, router_w, exp_w_gate, exp_w_up, exp_w_down, rec_w_in, gla_alpha_up, gla_alpha_b, gla_norm_g, lru_conv_w, lru_conv_b, lru_w_a, lru_b_a, lru_w_i, lru_b_i, lru_lambda, na_w_qkv, na_rpb, norm_f_g):
    B, T, D = x.shape
    pad_rows = (-(B + 1)) % 8
    cc = jnp.concatenate([c, c_ctx[None, :], jnp.zeros((pad_rows, D), F32)], axis=0)
    m_all = _modulation(cc, w_mod, b_mod)
    rope_tabs = _rope_tables(T)
    expert_w_f32 = (exp_w_gate, exp_w_up, exp_w_down)
    mods_lat = [m_all[l, :B].reshape(B, N_MOD, D) for l in range(DEPTH)]
    mods_ctx = [m_all[l, B:B + 1].reshape(1, N_MOD, D) for l in range(DEPTH)]
    h_lat = _norm(x, norm_mix_g[0], mods_lat[0])
    h_ctx = _norm(ctx, norm_mix_g[0], mods_ctx[0])
    for l in range(DEPTH):
        last = l == DEPTH - 1
        i = l // 2
        m_lat, m_ctx = mods_lat[l], mods_ctx[l]
        w_o = w_out[l].astype(BF16)
        if l % 2 == 0:
            (ys_ctx, ys_lat), expert_w = _rec_mixer(
                h_ctx, h_lat, rec_w_in[i], gla_alpha_up[i], gla_alpha_b[i], gla_norm_g[i], lru_conv_w[i], lru_conv_b[i],
                lru_w_a[i], lru_b_a[i], lru_w_i[i], lru_b_i[i], lru_lambda[i], rope_tabs, expert_w_f32, l)
            ws = [w_o[:GLA_VAL_WIDTH], w_o[GLA_VAL_WIDTH:]]
        else:
            w_qkv = na_w_qkv[i].astype(BF16)
            qkv_lat, *expert_w = _proj(h_lat, w_qkv, BF16, tn=3072, side=expert_w_f32, layer=l)
            qkv_ctx = _proj(h_ctx, w_qkv, BF16, tn=3072)
            y_ctx, y_lat = _na(qkv_lat, qkv_ctx, _na_bias_tables(na_rpb[i], T // GRID_W), not last)
            ys_ctx, ys_lat, ws = [y_ctx], [y_lat], [w_o]
        moe_w = (*expert_w, 0)
        router_wt = router_w[l].T
        x, h_ffn, logits_t = _oproj(ys_lat, ws, x, m_lat, norm_ffn_g[l], router_wt)
        if last:
            _, out = _moe(x, h_ffn, logits_t, m_lat, *moe_w, norm_f_g, None)
            return out
        x, h_lat = _moe(x, h_ffn, logits_t, m_lat, *moe_w, norm_mix_g[l + 1], mods_lat[l + 1])
        ctx, h_ffn, logits_t = _oproj(ys_ctx, ws, ctx, m_ctx, norm_ffn_g[l], router_wt)
        ctx, h_ctx = _moe(ctx, h_ffn, logits_t, m_ctx, *moe_w, norm_mix_g[l + 1], mods_ctx[l + 1])
```

```python
import functools

import numpy as np
import jax
import jax.numpy as jnp
from jax import lax
from jax.experimental import pallas as pl
from jax.experimental.pallas import tpu as pltpu

D_MODEL = 2048
DEPTH = 4
GRID_W = 64
RMS_EPS = 1e-6
N_MOD = 6
ROPE_THETA = 10000.0
GLA_HEADS = 4
GLA_DK = D_MODEL // 16
GLA_DV = D_MODEL // 8
GLA_KEY_WIDTH = GLA_HEADS * GLA_DK
GLA_VAL_WIDTH = GLA_HEADS * GLA_DV
GLA_RANK = 16
GLA_TAU = 16.0
GLA_CHUNK = 64
LRU_WIDTH = D_MODEL // 2
LRU_BLOCKS = 8
LRU_BLOCK_DIM = LRU_WIDTH // LRU_BLOCKS
LRU_CONV = 4
LRU_C = 8.0
NA_HEADS = 16
NA_HEAD_DIM = D_MODEL // NA_HEADS
NA_KH = 8
NA_KW = 16
N_EXPERTS = 16
EC_CAPACITY_FACTOR = 2
EXPERT_FF = D_MODEL // 2

F32 = jnp.float32
BF16 = jnp.bfloat16
HIGHEST = lax.Precision.HIGHEST
LANES = 128
NEG_BIG = -1e30
LOG2_E = float(np.log2(np.e))
NA_QROWS = 4
NA_KROWS = NA_QROWS + NA_KH - 1
PREFIX_CHUNK = 256
SLOT_WINDOW = 64
OPROJ_SUB_ROWS = 256
GLA_EMIT_CHUNKS = 4
GLA_PREP_ROWS = 512
LRU_SEGMENTS = 8
LRU_PITCH_PAD = 8
FFN_ROWS = 512
SLOT_ALIGN = 16

_NT = (((1,), (1,)), ((), ()))
_TN = (((0,), (0,)), ((), ()))


def _params(sem, vmem_mib=48):
    return pltpu.CompilerParams(dimension_semantics=sem, vmem_limit_bytes=vmem_mib * 2 ** 20)


def _aligned(index, multiple):
    return index if isinstance(index, int) else pl.multiple_of(index, multiple)


def _sigmoid(z):
    return 1.0 / (1.0 + jnp.exp(-z))


def _silu(z):
    return z * _sigmoid(z)


def _mod_kernel(c_ref, w_ref, b_ref, o_ref):
    a = _silu(c_ref[...]).astype(BF16)
    o_ref[0] = jnp.dot(a, w_ref[0].astype(BF16), preferred_element_type=F32) + b_ref[0]


def _modulation(cc, w_mod, b_mod):
    R, D = cc.shape
    L, _, N = w_mod.shape
    tn = 1024
    return pl.pallas_call(
        _mod_kernel,
        grid=(L, N // tn),
        in_specs=[pl.BlockSpec((R, D), lambda l, j: (0, 0)),
                  pl.BlockSpec((1, D, tn), lambda l, j: (l, 0, j)),
                  pl.BlockSpec((1, 1, tn), lambda l, j: (l, 0, j))],
        out_specs=pl.BlockSpec((1, R, tn), lambda l, j: (l, 0, j)),
        out_shape=jax.ShapeDtypeStruct((L, R, N), F32),
        compiler_params=_params(("arbitrary", "arbitrary")),
        name="modulation",
    )(cc, w_mod, b_mod.reshape(L, 1, N))


def _rms_modulate(x, g_ref, m_ref=None, shift_row=0, scale_row=1):
    y = x * lax.rsqrt(jnp.mean(x * x, axis=-1, keepdims=True) + RMS_EPS) * g_ref[...]
    if m_ref is not None:
        y = y * (1.0 + m_ref[0, scale_row:scale_row + 1, :]) + m_ref[0, shift_row:shift_row + 1, :]
    return y


def _norm_kernel(x_ref, g_ref, m_ref, h_ref):
    h_ref[0] = _rms_modulate(x_ref[0], g_ref, m_ref).astype(h_ref.dtype)


def _norm(x, g, mod):
    B, T, D = x.shape
    tt = min(T, 512)
    per_sample = mod.shape[0] > 1
    rows = pl.BlockSpec((1, tt, D), lambda b, t: (b, t, 0))
    return pl.pallas_call(
        _norm_kernel,
        grid=(B, T // tt),
        in_specs=[rows, pl.BlockSpec((1, D), lambda b, t: (0, 0)),
                  pl.BlockSpec((1, N_MOD, D), (lambda b, t: (b, 0, 0)) if per_sample else (lambda b, t: (0, 0, 0)))],
        out_specs=rows, out_shape=jax.ShapeDtypeStruct((B, T, D), BF16),
        compiler_params=_params(("arbitrary", "arbitrary")),
        name="rmsnorm",
    )(x, g.reshape(1, D), mod)


def _proj_kernel(*refs, n_side):
    a_ref, w_ref = refs[:2]
    side_in, o_ref, side_out = refs[2:2 + n_side], refs[2 + n_side], refs[3 + n_side:]
    o_ref[...] = jnp.dot(a_ref[...], w_ref[...], preferred_element_type=F32).astype(o_ref.dtype)
    for src, dst in zip(side_in, side_out):
        dst[...] = src[...].astype(dst.dtype)


def _proj(a, w, out_dtype, tn, side=(), layer=0):
    B, T, K = a.shape
    N = w.shape[1]
    M = B * T
    tm = min(M, 512)
    n_i = M // tm
    steps = (N // tn) * n_i
    flat = [s.reshape(s.shape[0], steps, -1, s.shape[-1]) for s in side]
    res = pl.pallas_call(
        functools.partial(_proj_kernel, n_side=len(side)),
        grid=(N // tn, n_i),
        in_specs=[pl.BlockSpec((tm, K), lambda j, i: (i, 0)), pl.BlockSpec((K, tn), lambda j, i: (0, j))]
                 + [pl.BlockSpec((1, 1) + f.shape[2:], lambda j, i: (layer, j * n_i + i, 0, 0)) for f in flat],
        out_specs=[pl.BlockSpec((tm, tn), lambda j, i: (i, j))]
                  + [pl.BlockSpec((1, 1) + f.shape[2:], lambda j, i: (0, j * n_i + i, 0, 0)) for f in flat],
        out_shape=[jax.ShapeDtypeStruct((M, N), out_dtype)]
                  + [jax.ShapeDtypeStruct((1,) + f.shape[1:], BF16) for f in flat],
        compiler_params=_params(("arbitrary", "arbitrary")),
        name="proj",
    )(a.reshape(M, K), w, *flat)
    out = res[0].reshape(B, T, N)
    return (out, *[r.reshape((1,) + s.shape[1:]) for r, s in zip(res[1:], side)]) if side else out


def _oproj_kernel(*refs, n_in, n_experts):
    a_refs, w_refs = refs[:n_in], refs[n_in:2 * n_in]
    x_ref, m_ref, g_ref, wr_ref, o_ref, h_ref, lg_ref = refs[2 * n_in:]
    tm = x_ref.shape[1]
    sub = min(tm, OPROJ_SUB_ROWS)
    for s in range(tm // sub):
        rows = slice(s * sub, (s + 1) * sub)
        acc = jnp.dot(a_refs[0][0, rows, :], w_refs[0][...], preferred_element_type=F32)
        for a_ref, w_ref in zip(a_refs[1:], w_refs[1:]):
            acc = acc + jnp.dot(a_ref[0, rows, :], w_ref[...], preferred_element_type=F32)
        x_new = x_ref[0, rows, :] + m_ref[0, 2:3, :] * acc
        o_ref[0, rows, :] = x_new
        y = _rms_modulate(x_new, g_ref, m_ref, shift_row=3, scale_row=4)
        y_hi = y.astype(BF16)
        h_ref[0, rows, :] = y_hi
        y_lo = (y - y_hi.astype(F32)).astype(BF16)
        parts = (jnp.dot(y_hi, wr_ref[...], preferred_element_type=F32)
                 + jnp.dot(y_lo, wr_ref[...], preferred_element_type=F32))
        lg_ref[0, :, rows] = jnp.transpose(parts + pltpu.roll(parts, LANES - n_experts, 1))


def _oproj(ys, ws, x, mod, g_ffn, router_wt):
    B, T, D = x.shape
    E = router_wt.shape[0]
    tm = min(T, 512)
    per_sample = mod.shape[0] > 1
    w_r_hi = router_wt.T.astype(BF16)
    w_r_lo = (router_wt.T - w_r_hi.astype(F32)).astype(BF16)
    w_r_split = jnp.pad(jnp.concatenate([w_r_hi, w_r_lo], axis=1), ((0, 0), (0, LANES - 2 * E)))
    rows = lambda n: pl.BlockSpec((1, tm, n), lambda b, t: (b, t, 0))
    in_specs = [rows(y.shape[2]) for y in ys]
    in_specs += [pl.BlockSpec(w.shape, lambda b, t: (0, 0)) for w in ws]
    in_specs += [rows(D),
                 pl.BlockSpec((1, N_MOD, D), (lambda b, t: (b, 0, 0)) if per_sample else (lambda b, t: (0, 0, 0))),
                 pl.BlockSpec((1, D), lambda b, t: (0, 0)), pl.BlockSpec((D, LANES), lambda b, t: (0, 0))]
    x_new, h, logits = pl.pallas_call(
        functools.partial(_oproj_kernel, n_in=len(ys), n_experts=E),
        grid=(B, T // tm), in_specs=in_specs,
        out_specs=[rows(D), rows(D), pl.BlockSpec((1, LANES, tm), lambda b, t: (b, 0, t))],
        out_shape=[jax.ShapeDtypeStruct((B, T, D), F32), jax.ShapeDtypeStruct((B, T, D), BF16),
                   jax.ShapeDtypeStruct((B, LANES, T), F32)],
        compiler_params=_params(("arbitrary", "arbitrary"), vmem_mib=56),
        name="oproj",
    )(*ys, *ws, x, mod, g_ffn.reshape(1, D), w_r_split)
    return x_new, h, logits


def _log_sigmoid(z):
    return jnp.minimum(z, 0.0) - jnp.log(1.0 + jnp.exp(-jnp.abs(z)))


def _gla_kernel(*refs, n_chunks, use_rope, has_state):
    it = iter(refs)
    q_ref, k_ref, v_ref, g_ref, za_ref = (next(it) for _ in range(5))
    cos_ref, sin_ref = (next(it), next(it)) if use_rope else (None, None)
    aup_ref, ab_ref, gg_ref = next(it), next(it), next(it)
    s0_ref = next(it) if has_state else None
    y_ref, sfin_ref = next(it), next(it)
    qd_scr, ki_scr, ke_scr, kv_scr, dec_scr, sall_scr, st_scr = (next(it) for _ in range(7))
    C, DK = GLA_CHUNK, GLA_DK
    T = n_chunks * C
    RB = min(T, GLA_PREP_ROWS)
    cpb = RB // C

    def increments(blk):
        for i in range(cpb):
            c = blk * cpb + i
            rows = pl.ds(_aligned(c * C, C), C)
            kv_scr[c] = lax.dot_general(v_ref[0, rows, :].astype(BF16), ke_scr[c], _TN, preferred_element_type=F32)

    def prepare(blk):
        r0 = _aligned(blk * RB, RB)
        rows = pl.ds(r0, RB)
        z = jnp.dot(za_ref[0, rows, :].astype(BF16), aup_ref[...], preferred_element_type=F32) + ab_ref[...]
        la = _log_sigmoid(z) * (1.0 / GLA_TAU)
        in_chunk = lax.broadcasted_iota(jnp.int32, la.shape, 0) % C
        prefix = la
        step = 1
        while step < C:
            prefix = prefix + jnp.where(in_chunk >= step, pltpu.roll(prefix, step, 0), 0.0)
            step *= 2
        q = q_ref[0, rows, :] * (GLA_DK ** -0.5)
        k = k_ref[0, rows, :]
        if use_rope:
            cos, sin = cos_ref[rows, :], sin_ref[rows, :]
            even = (lax.broadcasted_iota(jnp.int32, q.shape, 1) % 2) == 0

            def rope(t):
                swapped = jnp.where(even, pltpu.roll(t, LANES - 1, 1), pltpu.roll(t, 1, 1))
                return t * cos + swapped * sin
            q, k = rope(q), rope(k)
        split = lambda t: t.reshape(cpb, C, t.shape[-1])
        la, prefix, q, k = split(la), split(prefix), split(q), split(k)
        tot = prefix[:, C - 1:C, :]
        b_f = prefix[..., :DK]
        b_b = tot[..., DK:] - prefix[..., DK:] + la[..., DK:]
        dec = jnp.exp(tot)
        c0 = _aligned(blk * cpb, cpb)
        qd_scr[pl.ds(c0, cpb)] = jnp.concatenate([q * jnp.exp(b_f), q * jnp.exp(b_b)], axis=2).astype(BF16)
        ki_f, ki_b = k * jnp.exp(-b_f), k * jnp.exp(-b_b)
        ki_scr[0, pl.ds(c0, cpb)] = ki_f.astype(BF16)
        ki_scr[1, pl.ds(c0, cpb)] = ki_b.astype(BF16)
        ke_scr[pl.ds(c0, cpb)] = jnp.concatenate([ki_f * dec[..., :DK], ki_b * dec[..., DK:]], axis=2).astype(BF16)
        dec_scr[pl.ds(c0, cpb)] = jnp.broadcast_to(dec, (cpb,) + dec_scr.shape[1:])

    def prepare_and_increment(blk, carry):
        increments(blk - 1)
        prepare(blk)
        return carry

    n_blocks = T // RB
    prepare(0)
    lax.fori_loop(1, n_blocks, prepare_and_increment, 0)
    increments(n_blocks - 1)

    st_scr[...] = s0_ref[0, 0] if has_state else jnp.zeros(st_scr.shape, F32)

    def recur(i, carry):
        cf, cb = i, n_chunks - 1 - i
        st = st_scr[...]
        sall_scr[cf, :, :DK] = st[:, :DK].astype(BF16)
        sall_scr[cb, :, DK:] = st[:, DK:].astype(BF16)
        st_scr[:, :DK] = st[:, :DK] * dec_scr[cf, 0:1, :DK] + kv_scr[cf, :, :DK]
        st_scr[:, DK:] = st[:, DK:] * dec_scr[cb, 0:1, DK:] + kv_scr[cb, :, DK:]
        return carry

    lax.fori_loop(0, n_chunks, recur, 0)
    sfin_ref[0, 0] = st_scr[...]

    G = min(n_chunks, GLA_EMIT_CHUNKS)
    GR = G * C
    row = lax.broadcasted_iota(jnp.int32, (GR, GR), 0)
    col = lax.broadcasted_iota(jnp.int32, (GR, GR), 1)
    same_chunk = (row // C) == (col // C)
    keep_f, keep_b = same_chunk & (row >= col), same_chunk & (row <= col)

    def emit(grp, carry):
        c0 = pl.multiple_of(grp * G, G)
        rows = pl.ds(pl.multiple_of(grp * GR, GR), GR)
        qd = qd_scr[pl.ds(c0, G)].reshape(GR, 2 * DK)
        ki_f = ki_scr[0, pl.ds(c0, G)].reshape(GR, DK)
        ki_b = ki_scr[1, pl.ds(c0, G)].reshape(GR, DK)
        s = (jnp.where(keep_f, lax.dot_general(qd[:, :DK], ki_f, _NT, preferred_element_type=F32), 0.0)
             + jnp.where(keep_b, lax.dot_general(qd[:, DK:], ki_b, _NT, preferred_element_type=F32), 0.0))
        carried = [lax.dot_general(qd[i * C:(i + 1) * C], sall_scr[c0 + i], _NT, preferred_element_type=F32)
                   for i in range(G)]
        o = (jnp.dot(s.astype(BF16), v_ref[0, rows, :].astype(BF16), preferred_element_type=F32)
             + jnp.concatenate(carried, axis=0))
        o = o * lax.rsqrt(jnp.mean(o * o, axis=-1, keepdims=True) + RMS_EPS) * gg_ref[...]
        y_ref[0, rows, :] = (o * _silu(g_ref[0, rows, :])).astype(y_ref.dtype)
        return carry

    lax.fori_loop(0, n_chunks // G, emit, 0, unroll=min(4, n_chunks // G))


def _gla(p_main, p_za, aup, ab, gg, rope_tabs, state):
    B, T, _ = p_main.shape
    H, DK, DV = GLA_HEADS, GLA_DK, GLA_DV
    use_rope, has_state = rope_tabs is not None, state is not None
    n_chunks = T // GLA_CHUNK
    in_specs = [pl.BlockSpec((1, T, DK), lambda b, h: (b, 0, h)),
                pl.BlockSpec((1, T, DK), lambda b, h: (b, 0, H + h)),
                pl.BlockSpec((1, T, DV), lambda b, h: (b, 0, H + h)),
                pl.BlockSpec((1, T, DV), lambda b, h: (b, 0, 2 * H + h)),
                pl.BlockSpec((1, T, LANES), lambda b, h: (b, 0, 0))]
    args = [p_main, p_main, p_main, p_main, p_za]
    if use_rope:
        in_specs += [pl.BlockSpec((T, DK), lambda b, h: (0, 0))] * 2
        args += list(rope_tabs)
    in_specs += [pl.BlockSpec((LANES, 2 * DK), lambda b, h: (0, h)),
                 pl.BlockSpec((1, 2 * DK), lambda b, h: (0, h)),
                 pl.BlockSpec((1, DV), lambda b, h: (0, 0))]
    args += [aup, ab, gg]
    st_spec = pl.BlockSpec((1, 1, DV, 2 * DK), lambda b, h: (b, h, 0, 0))
    if has_state:
        in_specs.append(st_spec)
        args.append(state)
    return pl.pallas_call(
        functools.partial(_gla_kernel, n_chunks=n_chunks, use_rope=use_rope, has_state=has_state),
        grid=(B, H), in_specs=in_specs,
        out_specs=[pl.BlockSpec((1, T, DV), lambda b, h: (b, 0, h)), st_spec],
        out_shape=[jax.ShapeDtypeStruct((B, T, H * DV), BF16), jax.ShapeDtypeStruct((B, H, DV, 2 * DK), F32)],
        scratch_shapes=[pltpu.VMEM((n_chunks, GLA_CHUNK, 2 * DK), BF16), pltpu.VMEM((2, n_chunks, GLA_CHUNK, DK), BF16),
                        pltpu.VMEM((n_chunks, GLA_CHUNK, 2 * DK), BF16),
                        pltpu.VMEM((n_chunks, DV, 2 * DK), F32), pltpu.VMEM((n_chunks, 8, 2 * DK), F32),
                        pltpu.VMEM((n_chunks, DV, 2 * DK), BF16), pltpu.VMEM((DV, 2 * DK), F32)],
        compiler_params=_params(("arbitrary", "arbitrary")),
        name="gla",
    )(*args)


def _gelu_tanh(z):
    return 0.5 * z * (1.0 + jnp.tanh(np.sqrt(2.0 / np.pi) * (z + 0.044715 * (z * z * z))))


def _lru_kernel(x_ref, gt_ref, cw_ref, cb_ref, wa_ref, ba_ref, wi_ref, bi_ref, lam_ref, h0f_ref, h0b_ref,
                y_ref, hf_ref, hb_ref, a_scr, bx_scr, hl_scr, ap_scr, *, T):
    L = T // LRU_SEGMENTS
    P = L + LRU_PITCH_PAD
    x = x_ref[0]
    row = lax.broadcasted_iota(jnp.int32, x.shape, 0)

    def from_earlier(v, d, fill):
        return jnp.where(row >= d, pltpu.roll(v, d, 0), fill)

    def from_later(v, d, fill):
        return jnp.where(row < T - d, pltpu.roll(v, T - d, 0), fill)

    xc = (cw_ref[0:1, :] * from_earlier(x, 1, 0.0) + cw_ref[1:2, :] * x
          + cw_ref[2:3, :] * from_later(x, 1, 0.0) + cw_ref[3:4, :] * from_later(x, 2, 0.0) + cb_ref[...])
    xcb = xc.astype(BF16)

    def sigmoid(z):
        return 0.5 * jnp.tanh(0.5 * z) + 0.5

    for d in range(2):
        r = sigmoid(jnp.dot(xcb, wa_ref[d, 0], preferred_element_type=F32) + ba_ref[d])
        gate_i = sigmoid(jnp.dot(xcb, wi_ref[d, 0], preferred_element_type=F32) + bi_ref[d])
        neg_lam = -lam_ref[d]
        softplus = jnp.maximum(neg_lam, 0.0) + jnp.log(1.0 + jnp.exp(-jnp.abs(neg_lam)))
        a = jnp.exp((-LRU_C) * r * softplus)
        u = 1.0 - a * a
        bx = jnp.where(u > 0.0, u * lax.rsqrt(u), 0.0) * (gate_i * xc)
        for s in range(LRU_SEGMENTS):
            a_scr[d, s * P:s * P + L, :] = a[s * L:(s + 1) * L]
            bx_scr[d, s * P:s * P + L, :] = bx[s * L:(s + 1) * L]

    def step(i, carry):
        h_f, p_f, h_b, p_b = carry
        at_f = pl.ds(i, LRU_SEGMENTS, stride=P)
        at_b = pl.ds(L - 1 - i, LRU_SEGMENTS, stride=P)
        a_f, a_b = a_scr[0, at_f, :], a_scr[1, at_b, :]
        h_f = a_f * h_f + bx_scr[0, at_f, :]
        h_b = a_b * h_b + bx_scr[1, at_b, :]
        p_f, p_b = a_f * p_f, a_b * p_b
        hl_scr[0, at_f, :] = h_f
        hl_scr[1, at_b, :] = h_b
        ap_scr[0, at_f, :] = p_f
        ap_scr[1, at_b, :] = p_b
        return h_f, p_f, h_b, p_b

    zeros, ones = jnp.zeros((LRU_SEGMENTS, x.shape[1]), F32), jnp.ones((LRU_SEGMENTS, x.shape[1]), F32)
    h_f, p_f, h_b, p_b = lax.fori_loop(0, L, step, (zeros, ones, zeros, ones), unroll=8)

    enter_f, enter_b = [h0f_ref[0]], [h0b_ref[0]]
    for s in range(LRU_SEGMENTS):
        enter_f.append(h_f[s:s + 1] + p_f[s:s + 1] * enter_f[-1])
        t = LRU_SEGMENTS - 1 - s
        enter_b.append(h_b[t:t + 1] + p_b[t:t + 1] * enter_b[-1])
    hf_ref[0] = enter_f[-1]
    hb_ref[0] = enter_b[-1]
    for s in range(LRU_SEGMENTS):
        seg = slice(s * P, s * P + L)
        h = (hl_scr[0, seg, :] + ap_scr[0, seg, :] * enter_f[s]
             + hl_scr[1, seg, :] + ap_scr[1, seg, :] * enter_b[LRU_SEGMENTS - 1 - s])
        y_ref[0, s * L:(s + 1) * L, :] = (h * _gelu_tanh(gt_ref[0, s * L:(s + 1) * L, :])).astype(y_ref.dtype)


def _lru(p_main, cw, cb, wa, ba, wi, bi, lam, states):
    B, T, _ = p_main.shape
    W, NB, BD = LRU_WIDTH, LRU_BLOCKS, LRU_BLOCK_DIM
    x_blk0 = (2 * GLA_KEY_WIDTH + 2 * GLA_VAL_WIDTH) // BD
    vec = lambda n: pl.BlockSpec((n, 1, BD), lambda b, j: (0, 0, j))
    mat = pl.BlockSpec((2, 1, BD, BD), lambda b, j: (0, j, 0, 0))
    st_spec = pl.BlockSpec((1, 1, BD), lambda b, j: (b, 0, j))
    st_shape = jax.ShapeDtypeStruct((B, 1, W), F32)
    y, hf, hb = pl.pallas_call(
        functools.partial(_lru_kernel, T=T),
        grid=(B, NB),
        in_specs=[pl.BlockSpec((1, T, BD), lambda b, j: (b, 0, x_blk0 + j)),
                  pl.BlockSpec((1, T, BD), lambda b, j: (b, 0, x_blk0 + NB + j)),
                  pl.BlockSpec((LRU_CONV, BD), lambda b, j: (0, j)),
                  pl.BlockSpec((1, BD), lambda b, j: (0, j)),
                  mat, vec(2), mat, vec(2), vec(2), st_spec, st_spec],
        out_specs=[pl.BlockSpec((1, T, BD), lambda b, j: (b, 0, j)), st_spec, st_spec],
        out_shape=[jax.ShapeDtypeStruct((B, T, W), BF16), st_shape, st_shape],
        scratch_shapes=[pltpu.VMEM((2, LRU_SEGMENTS * (T // LRU_SEGMENTS + LRU_PITCH_PAD), BD), F32)] * 4,
        compiler_params=_params(("arbitrary", "arbitrary")),
        name="rglru",
    )(p_main, p_main, cw, cb.reshape(1, W), wa, ba.reshape(2, 1, W), wi, bi.reshape(2, 1, W),
      lam.reshape(2, 1, W), *states)
    return y, (hf, hb)


def _na_tile(i, n_tiles, rows):
    r0 = i * NA_QROWS
    ks = min(max(r0 - NA_KH // 2, 0), rows - NA_KROWS)
    cls = 0 if i == 0 else (2 if i == n_tiles - 1 else 1)
    return r0, ks, cls


def _na_bias_tables(rpb, rows):
    H = rpb.shape[0]
    qc = np.arange(GRID_W)[:, None]
    kc = np.arange(GRID_W)[None, :]
    col_start = np.clip(qc - NA_KW // 2, 0, GRID_W - NA_KW)
    col_valid = (kc >= col_start) & (kc < col_start + NA_KW)
    dc = np.clip(kc - qc, -(NA_KW - 1), NA_KW - 1) + NA_KW - 1
    pick = (dc[None] == np.arange(2 * NA_KW - 1)[:, None, None]).astype(np.float32)
    band = jnp.einsum('hrd,dqk->hrqk', rpb.astype(F32), jnp.asarray(pick), precision=HIGHEST)
    band = jnp.where(col_valid, band, NEG_BIG)
    masked = jnp.full((H, GRID_W, GRID_W), NEG_BIG, F32)
    n_tiles = rows // NA_QROWS
    tabs = []
    for i in (0, 1, n_tiles - 1):
        r0, ks, _ = _na_tile(i, n_tiles, rows)
        per_qrow = []
        for qr in range(r0, r0 + NA_QROWS):
            row_start = min(max(qr - NA_KH // 2, 0), rows - NA_KH)
            blocks = [band[:, kr - qr + NA_KH - 1] if row_start <= kr < row_start + NA_KH else masked
                      for kr in range(ks, ks + NA_KROWS)]
            per_qrow.append(jnp.concatenate(blocks, axis=2))
        tabs.append(jnp.concatenate(per_qrow, axis=1))
    return jnp.stack(tabs, axis=1) * LOG2_E


def _na_kernel(*refs, rows, need_ctx):
    q_ref, k_ref, v_ref, qc_ref, kc_ref, vc_ref, tab_ref = refs[:7]
    y_ref, vx_ref = refs[7], refs[-1]
    HD = NA_HEAD_DIM
    scale = HD ** -0.5 * LOG2_E
    vx_ref[...] = jnp.concatenate([v_ref[0], jnp.ones(v_ref.shape[1:], BF16)], axis=1)
    kc = kc_ref[0]
    vcx = jnp.concatenate([vc_ref[0], jnp.ones(vc_ref.shape[1:], BF16)], axis=1)
    n_tiles = rows // NA_QROWS
    nq, nk = NA_QROWS * GRID_W, NA_KROWS * GRID_W
    for i in range(n_tiles):
        r0, ks, cls = _na_tile(i, n_tiles, rows)
        q = q_ref[0, r0 * GRID_W:r0 * GRID_W + nq, :]
        kt = k_ref[0, ks * GRID_W:ks * GRID_W + nk, :]
        s_loc = lax.dot_general(q, kt, _NT, preferred_element_type=F32) * scale + tab_ref[0, cls]
        s_ctx = lax.dot_general(q, kc, _NT, preferred_element_type=F32) * scale
        m = jnp.maximum(jnp.max(s_loc, axis=-1, keepdims=True), jnp.max(s_ctx, axis=-1, keepdims=True))
        o = (jnp.dot(jnp.exp2(s_loc - m).astype(BF16), vx_ref[ks * GRID_W:ks * GRID_W + nk, :], preferred_element_type=F32)
             + jnp.dot(jnp.exp2(s_ctx - m).astype(BF16), vcx, preferred_element_type=F32))
        y_ref[0, r0 * GRID_W:r0 * GRID_W + nq, :] = (o[:, :HD] / o[:, HD:]).astype(y_ref.dtype)
    if need_ctx:
        yc_ref = refs[8]
        s = lax.dot_general(qc_ref[0], kc, _NT, preferred_element_type=F32) * scale
        p = jnp.exp2(s - jnp.max(s, axis=-1, keepdims=True))
        o = jnp.dot(p.astype(BF16), vcx, preferred_element_type=F32)
        yc_ref[0] = (o[:, :HD] / o[:, HD:]).astype(yc_ref.dtype)


def _na(qkv_lat, qkv_ctx, tabs, need_ctx):
    B, T, _ = qkv_lat.shape
    Tc = qkv_ctx.shape[1]
    H, HD = NA_HEADS, NA_HEAD_DIM
    blk = lambda t, off: pl.BlockSpec((1, t, HD), lambda h, b: (b, 0, off + h))
    out_specs = [pl.BlockSpec((1, T, HD), lambda h, b: (b, 0, h))]
    out_shape = [jax.ShapeDtypeStruct((B, T, H * HD), BF16)]
    if need_ctx:
        out_specs.append(pl.BlockSpec((1, Tc, HD), lambda h, b: (b, 0, h)))
        out_shape.append(jax.ShapeDtypeStruct((B, Tc, H * HD), BF16))
    res = pl.pallas_call(
        functools.partial(_na_kernel, rows=T // GRID_W, need_ctx=need_ctx),
        grid=(H, B),
        in_specs=[blk(T, 0), blk(T, H), blk(T, 2 * H), blk(Tc, 0), blk(Tc, H), blk(Tc, 2 * H),
                  pl.BlockSpec((1,) + tabs.shape[1:], lambda h, b: (h, 0, 0, 0))],
        out_specs=out_specs, out_shape=out_shape,
        scratch_shapes=[pltpu.VMEM((T, 2 * HD), BF16)],
        compiler_params=_params(("arbitrary", "arbitrary")),
        name="natten",
    )(qkv_lat, qkv_lat, qkv_lat, qkv_ctx, qkv_ctx, qkv_ctx, tabs)
    return (res[1] if need_ctx else None), res[0]


def _route_kernel(lg_ref, pos_ref, aff_ref, lo_ref, *, cap):
    lg = lg_ref[0]
    E, N = lg.shape
    ex = jnp.exp(lg - jnp.max(lg, axis=0, keepdims=True))
    aff = ex / jnp.sum(ex, axis=0, keepdims=True)
    aff_ref[0] = aff
    bits = pltpu.bitcast(aff, jnp.int32)
    thr = jnp.zeros((E, 1), jnp.int32)
    for bit in range(29, -1, -1):
        cand = thr | (1 << bit)
        cnt = jnp.sum(jnp.where(bits >= cand, 1.0, 0.0), axis=1, keepdims=True)
        thr = jnp.where(cnt >= cap, cand, thr)
    above = bits > thr
    tied = bits == thr
    need = cap - jnp.sum(jnp.where(above, 1.0, 0.0), axis=1, keepdims=True)
    pc = min(PREFIX_CHUNK, N)
    before = jnp.where(lax.broadcasted_iota(jnp.int32, (pc, pc), 0) < lax.broadcasted_iota(jnp.int32, (pc, pc), 1),
                       1.0, 0.0).astype(BF16)
    tied_seen = jnp.zeros((E, 1), F32)
    sel_seen = jnp.zeros((E, 1), F32)
    tile_lane = lax.broadcasted_iota(jnp.int32, (E, LANES), 1)
    lo = jnp.where(tile_lane == N // pc, float(cap), 0.0)
    for c in range(N // pc):
        sl = slice(c * pc, (c + 1) * pc)
        lo = jnp.where(tile_lane == c, sel_seen, lo)
        tied_c = jnp.where(tied[:, sl], 1.0, 0.0)
        tied_before = jnp.dot(tied_c.astype(BF16), before, preferred_element_type=F32) + tied_seen
        sel_c = jnp.where(above[:, sl], 1.0, jnp.where(tied_before < need, tied_c, 0.0))
        sel_before = jnp.dot(sel_c.astype(BF16), before, preferred_element_type=F32) + sel_seen
        pos_ref[0, :, sl] = jnp.where(sel_c > 0.0, sel_before, -1.0)
        tied_seen = tied_seen + jnp.sum(tied_c, axis=1, keepdims=True)
        sel_seen = sel_seen + jnp.sum(sel_c, axis=1, keepdims=True)
    lo_ref[0] = lo.astype(jnp.int32)


def _route(logits_t, cap):
    B, _, N = logits_t.shape
    E = N_EXPERTS
    spec = pl.BlockSpec((1, E, N), lambda b: (b, 0, 0))
    lo_spec = pl.BlockSpec((1, E, LANES), lambda b: (b, 0, 0))
    pos, aff, lo = pl.pallas_call(
        functools.partial(_route_kernel, cap=cap),
        grid=(B,), in_specs=[spec], out_specs=[spec, spec, lo_spec],
        out_shape=[jax.ShapeDtypeStruct((B, E, N), F32)] * 2 + [jax.ShapeDtypeStruct((B, E, LANES), jnp.int32)],
        compiler_params=_params(("arbitrary",)),
        name="route",
    )(logits_t)
    n_bounds = N // min(PREFIX_CHUNK, N) + 1
    return pos, aff, lo[:, :, :n_bounds].reshape(-1)


def _window_count(lo_ref, base, experts, tile, n_bounds, win):
    starts, n_win = [], 0
    for e in experts:
        first = lo_ref[(base + e) * n_bounds + tile]
        end = lo_ref[(base + e) * n_bounds + tile + 1]
        start = (first // SLOT_ALIGN) * SLOT_ALIGN
        starts.append(start)
        n_win = jnp.maximum(n_win, (end - start + win - 1) // win)
    return starts, n_win


def _gather_kernel(lo_ref, pos_ref, h_ref, xs_ref, *, cap, win, tn):
    EG, N = pos_ref.shape[1], pos_ref.shape[2]
    n_bounds = N // tn + 1
    base = pl.program_id(0) * (EG * pl.num_programs(1)) + pl.program_id(1) * EG
    xs_ref[...] = jnp.zeros(xs_ref.shape, xs_ref.dtype)
    slot_iota = lax.broadcasted_iota(jnp.int32, (win, tn), 0)
    for j in range(N // tn):
        h_tile = h_ref[0, j * tn:(j + 1) * tn, :]
        starts, n_win = _window_count(lo_ref, base, range(EG), j, n_bounds, win)

        def window(w, carry):
            pieces, offs = [], []
            for e in range(EG):
                first = starts[e] + w * win
                off = jnp.minimum(first, cap - win)
                prow = pos_ref[0, e:e + 1, j * tn:(j + 1) * tn]
                hit = ((slot_iota + off).astype(F32) == prow) & (prow >= first.astype(F32))
                pieces.append(jnp.where(hit, 1.0, 0.0).astype(BF16))
                offs.append(pl.multiple_of(off, SLOT_ALIGN))
            got = jnp.dot(jnp.concatenate(pieces, axis=0), h_tile, preferred_element_type=F32)
            for e in range(EG):
                xs_ref[0, e, pl.ds(offs[e], win), :] += got[e * win:(e + 1) * win].astype(xs_ref.dtype)
            return carry

        lax.fori_loop(0, n_win, window, 0)


def _gather(lo, pos, h, cap):
    B, E, N = pos.shape
    D = h.shape[2]
    EG = 8
    tn, win = min(PREFIX_CHUNK, N), min(SLOT_WINDOW, cap)
    return pl.pallas_call(
        functools.partial(_gather_kernel, cap=cap, win=win, tn=tn),
        grid_spec=pltpu.PrefetchScalarGridSpec(
            num_scalar_prefetch=1, grid=(B, E // EG),
            in_specs=[pl.BlockSpec((1, EG, N), lambda b, g, lo: (b, g, 0)),
                      pl.BlockSpec((1, N, D), lambda b, g, lo: (b, 0, 0))],
            out_specs=pl.BlockSpec((1, EG, cap, D), lambda b, g, lo: (b, g, 0, 0))),
        out_shape=jax.ShapeDtypeStruct((B, E, cap, D), BF16),
        compiler_params=_params(("arbitrary", "arbitrary")),
        name="moe_gather",
    )(lo, pos, h)


def _ffn_kernel(xs_ref, wg_ref, wu_ref, wd_ref, y_ref):
    bb, _, cap, D = xs_ref.shape
    xs = xs_ref[:, 0].reshape(bb * cap, D)
    hid = (_silu(jnp.dot(xs, wg_ref[0, 0], preferred_element_type=F32))
           * jnp.dot(xs, wu_ref[0, 0], preferred_element_type=F32))
    y = jnp.dot(hid.astype(BF16), wd_ref[0, 0], preferred_element_type=F32)
    y_ref[:, 0] = y.reshape(bb, cap, D).astype(y_ref.dtype)


def _ffn(xs, wg, wu, wd, layer):
    B, E, cap, D = xs.shape
    F = wg.shape[3]
    bb = max(1, min(B, FFN_ROWS // cap))
    tok = pl.BlockSpec((bb, 1, cap, D), lambda e, b: (b, e, 0, 0))
    return pl.pallas_call(
        _ffn_kernel,
        grid=(E, B // bb),
        in_specs=[tok, pl.BlockSpec((1, 1, D, F), lambda e, b: (layer, e, 0, 0)),
                  pl.BlockSpec((1, 1, D, F), lambda e, b: (layer, e, 0, 0)),
                  pl.BlockSpec((1, 1, F, D), lambda e, b: (layer, e, 0, 0))],
        out_specs=tok,
        out_shape=jax.ShapeDtypeStruct((B, E, cap, D), BF16),
        compiler_params=_params(("arbitrary", "arbitrary")),
        name="moe_ffn",
    )(xs, wg, wu, wd)


def _combine_kernel(*refs, cap, win, keep_x, next_modulated):
    it = iter(refs)
    lo_ref, pos_ref, aff_ref, y_ref, x_ref, m_ref, gn_ref = (next(it) for _ in range(7))
    mn_ref = next(it) if next_modulated else None
    o_ref = next(it) if keep_x else None
    hn_ref, ystack_ref, acc_ref = next(it), next(it), next(it)
    E = y_ref.shape[1]
    n_bounds = pl.num_programs(1) + 1
    j = pl.program_id(1)
    starts, n_win = _window_count(lo_ref, pl.program_id(0) * E, range(E), j, n_bounds, win)
    col = lax.broadcasted_iota(jnp.int32, (E, E * win), 1)
    spread = jnp.where(col // win == lax.broadcasted_iota(jnp.int32, (E, E * win), 0), 1.0, 0.0).astype(BF16)
    slot = lax.dot_general(pos_ref[0].astype(BF16), spread, _TN, preferred_element_type=F32)
    gate = lax.dot_general(aff_ref[0].astype(BF16), spread, _TN, preferred_element_type=F32)
    col1 = lax.broadcasted_iota(jnp.int32, (1, E * win), 1)
    col_expert, col_in_win = col1 // win, col1 % win

    def window(w):
        first_col = jnp.zeros((1, E * win), jnp.int32)
        want_col = jnp.zeros((1, E * win), jnp.int32)
        for e in range(E):
            first = starts[e] + w * win
            off = jnp.minimum(first, cap - win)
            ystack_ref[e * win:(e + 1) * win, :] = y_ref[0, e, pl.ds(pl.multiple_of(off, SLOT_ALIGN), win), :]
            first_col = jnp.where(col_expert == e, first, first_col)
            want_col = jnp.where(col_expert == e, off + col_in_win, want_col)
        hit = (slot == want_col.astype(F32)) & (slot >= first_col.astype(F32))
        return jnp.dot(jnp.where(hit, gate, 0.0).astype(BF16), ystack_ref[...], preferred_element_type=F32)

    def more(w, carry):
        acc_ref[...] += window(w)
        return carry

    acc_ref[...] = window(0)
    lax.fori_loop(1, n_win, more, 0)
    x_new = x_ref[0] + m_ref[0, 5:6, :] * acc_ref[...]
    if keep_x:
        o_ref[0] = x_new
    hn_ref[0] = _rms_modulate(x_new, gn_ref, mn_ref).astype(hn_ref.dtype)


def _combine(lo, pos, aff, y, x, mod, g_next, mod_next):
    B, E, cap, D = y.shape
    N = x.shape[1]
    tn, win = min(PREFIX_CHUNK, N), min(SLOT_WINDOW, cap)
    keep_x = next_modulated = mod_next is not None
    tok = pl.BlockSpec((1, E, tn), lambda b, t, lo: (b, 0, t))
    rows = pl.BlockSpec((1, tn, D), lambda b, t, lo: (b, t, 0))
    mod_spec = lambda m: pl.BlockSpec((1, N_MOD, D),
                                      (lambda b, t, lo: (b, 0, 0)) if m.shape[0] > 1 else (lambda b, t, lo: (0, 0, 0)))
    in_specs = [tok, tok, pl.BlockSpec((1, E, cap, D), lambda b, t, lo: (b, 0, 0, 0)), rows, mod_spec(mod),
                pl.BlockSpec((1, D), lambda b, t, lo: (0, 0))]
    args = [lo, pos, aff, y, x, mod, g_next.reshape(1, D)]
    out_specs, out_shape = [rows], [jax.ShapeDtypeStruct((B, N, D), BF16 if next_modulated else F32)]
    if next_modulated:
        in_specs.append(mod_spec(mod_next))
        args.append(mod_next)
        out_specs, out_shape = [rows] + out_specs, [jax.ShapeDtypeStruct((B, N, D), F32)] + out_shape
    res = pl.pallas_call(
        functools.partial(_combine_kernel, cap=cap, win=win, keep_x=keep_x, next_modulated=next_modulated),
        grid_spec=pltpu.PrefetchScalarGridSpec(
            num_scalar_prefetch=1, grid=(B, N // tn), in_specs=in_specs, out_specs=out_specs,
            scratch_shapes=[pltpu.VMEM((E * win, D), BF16), pltpu.VMEM((tn, D), F32)]),
        out_shape=out_shape,
        compiler_params=_params(("arbitrary", "arbitrary"), vmem_mib=56),
        name="moe_combine",
    )(*args)
    return (res[0], res[1]) if keep_x else (None, res[0])


def _moe(x, h, logits_t, mod, wg, wu, wd, layer, g_next, mod_next):
    N = x.shape[1]
    cap = EC_CAPACITY_FACTOR * N // N_EXPERTS
    assert cap <= 256, "slot ids must stay exact in bf16"
    pos, aff, lo = _route(logits_t, cap)
    y = _ffn(_gather(lo, pos, h, cap), wg, wu, wd, layer)
    return _combine(lo, pos, aff, y, x, mod, g_next, mod_next)


def _rope_tables(n_tok):
    t = jnp.arange(n_tok)
    rows = (t // GRID_W).astype(F32)
    cols = (t % GRID_W).astype(F32)
    n_freq = GLA_DK // 4
    inv = ROPE_THETA ** (-jnp.arange(n_freq, dtype=F32) / n_freq)
    ang = jnp.concatenate([rows[:, None] * inv, cols[:, None] * inv], axis=-1)
    cos = jnp.repeat(jnp.cos(ang), 2, axis=-1)
    sin = jnp.repeat(jnp.sin(ang), 2, axis=-1) * jnp.tile(jnp.asarray([-1.0, 1.0], F32), GLA_DK // 2)
    return cos, sin


def _rec_mixer(h_ctx, h_lat, w_in, alpha_up, alpha_b, gla_g, conv_w, conv_b, w_a, b_a, w_i, b_i, lam, rope_tabs,
               side, layer):
    za0 = 2 * GLA_KEY_WIDTH + 2 * GLA_VAL_WIDTH
    za1 = za0 + 2 * GLA_RANK
    w_main = jnp.concatenate([w_in[:, :za0], w_in[:, za1:]], axis=1).astype(BF16)
    w_za = jnp.pad(w_in[:, za0:za1], ((0, 0), (0, LANES - 2 * GLA_RANK))).astype(BF16)
    up = alpha_up.reshape(2, GLA_RANK, GLA_HEADS, 1, GLA_DK)
    zero = jnp.zeros_like(up[0])
    aup = jnp.concatenate([jnp.concatenate([up[0], zero], axis=2), jnp.concatenate([zero, up[1]], axis=2)], axis=0)
    aup = jnp.pad(aup.reshape(2 * GLA_RANK, 2 * GLA_KEY_WIDTH), ((0, LANES - 2 * GLA_RANK), (0, 0))).astype(BF16)
    ab = jnp.transpose(alpha_b.reshape(2, GLA_HEADS, GLA_DK), (1, 0, 2)).reshape(1, 2 * GLA_KEY_WIDTH)
    gg = gla_g.reshape(1, GLA_DV)
    wa, wi = w_a.astype(BF16), w_i.astype(BF16)
    outs = []
    gla_state, lru_state = None, None
    for h, tabs in ((h_ctx, None), (h_lat, rope_tabs)):
        B = h.shape[0]
        if tabs is None:
            p_main = _proj(h, w_main, F32, tn=2560)
        else:
            p_main, *cast = _proj(h, w_main, F32, tn=2560, side=side, layer=layer)
        p_za = _proj(h, w_za, F32, tn=LANES)
        y_gla, gla_state = _gla(p_main, p_za, aup, ab, gg, tabs, gla_state)
        if lru_state is None:
            lru_state = (jnp.zeros((B, 1, LRU_WIDTH), F32),) * 2
        y_lru, lru_state = _lru(p_main, conv_w, conv_b, wa, b_a, wi, b_i, lam, lru_state)
        outs.append([y_gla, y_lru])
    return outs, cast


def kernel(x, c, ctx, c_ctx, w_mod, b_mod, norm_mix_g, norm_ffn_g, w_out, router_w, exp_w_gate, exp_w_up, exp_w_down, rec_w_in, gla_alpha_up, gla_alpha_b, gla_norm_g, lru_conv_w, lru_conv_b, lru_w_a, lru_b_a, lru_w_i, lru_b_i, lru_lambda, na_w_qkv, na_rpb, norm_f_g):
    B, T, D = x.shape
    pad_rows = (-(B + 1)) % 8
    cc = jnp.concatenate([c, c_ctx[None, :], jnp.zeros((pad_rows, D), F32)], axis=0)
    m_all = _modulation(cc, w_mod, b_mod)
    rope_tabs = _rope_tables(T)
    expert_w_f32 = (exp_w_gate, exp_w_up, exp_w_down)
    mods_lat = [m_all[l, :B].reshape(B, N_MOD, D) for l in range(DEPTH)]
    mods_ctx = [m_all[l, B:B + 1].reshape(1, N_MOD, D) for l in range(DEPTH)]
    h_lat = _norm(x, norm_mix_g[0], mods_lat[0])
    h_ctx = _norm(ctx, norm_mix_g[0], mods_ctx[0])
    for l in range(DEPTH):
        last = l == DEPTH - 1
        i = l // 2
        m_lat, m_ctx = mods_lat[l], mods_ctx[l]
        w_o = w_out[l].astype(BF16)
        if l % 2 == 0:
            (ys_ctx, ys_lat), expert_w = _rec_mixer(
                h_ctx, h_lat, rec_w_in[i], gla_alpha_up[i], gla_alpha_b[i], gla_norm_g[i], lru_conv_w[i], lru_conv_b[i],
                lru_w_a[i], lru_b_a[i], lru_w_i[i], lru_b_i[i], lru_lambda[i], rope_tabs, expert_w_f32, l)
            ws = [w_o[:GLA_VAL_WIDTH], w_o[GLA_VAL_WIDTH:]]
        else:
            w_qkv = na_w_qkv[i].astype(BF16)
            qkv_lat, *expert_w = _proj(h_lat, w_qkv, BF16, tn=3072, side=expert_w_f32, layer=l)
            qkv_ctx = _proj(h_ctx, w_qkv, BF16, tn=3072)
            y_ctx, y_lat = _na(qkv_lat, qkv_ctx, _na_bias_tables(na_rpb[i], T // GRID_W), not last)
            ys_ctx, ys_lat, ws = [y_ctx], [y_lat], [w_o]
        moe_w = (*expert_w, 0)
        router_wt = router_w[l].T
        x, h_ffn, logits_t = _oproj(ys_lat, ws, x, m_lat, norm_ffn_g[l], router_wt)
        if last:
            _, out = _moe(x, h_ffn, logits_t, m_lat, *moe_w, norm_f_g, None)
            return out
        x, h_lat = _moe(x, h_ffn, logits_t, m_lat, *moe_w, norm_mix_g[l + 1], mods_lat[l + 1])
        ctx, h_ffn, logits_t = _oproj(ys_ctx, ws, ctx, m_ctx, norm_ffn_g[l], router_wt)
        ctx, h_ctx = _moe(ctx, h_ffn, logits_t, m_ctx, *moe_w, norm_mix_g[l + 1], mods_ctx[l + 1])
```

```python
import functools

import numpy as np
import jax
import jax.numpy as jnp
from jax import lax
from jax.experimental import pallas as pl
from jax.experimental.pallas import tpu as pltpu

D_MODEL = 2048
DEPTH = 4
GRID_W = 64
RMS_EPS = 1e-6
N_MOD = 6
ROPE_THETA = 10000.0
GLA_HEADS = 4
GLA_DK = D_MODEL // 16
GLA_DV = D_MODEL // 8
GLA_KEY_WIDTH = GLA_HEADS * GLA_DK
GLA_VAL_WIDTH = GLA_HEADS * GLA_DV
GLA_RANK = 16
GLA_TAU = 16.0
GLA_CHUNK = 64
LRU_WIDTH = D_MODEL // 2
LRU_BLOCKS = 8
LRU_BLOCK_DIM = LRU_WIDTH // LRU_BLOCKS
LRU_CONV = 4
LRU_C = 8.0
NA_HEADS = 16
NA_HEAD_DIM = D_MODEL // NA_HEADS
NA_KH = 8
NA_KW = 16
N_EXPERTS = 16
EC_CAPACITY_FACTOR = 2

F32 = jnp.float32
BF16 = jnp.bfloat16
HIGHEST = lax.Precision.HIGHEST
LANES = 128
NEG_BIG = -1e30
LOG2_E = float(np.log2(np.e))
NA_QROWS = 4
NA_KROWS = NA_QROWS + NA_KH - 1
PREFIX_CHUNK = 256
SLOT_WINDOW = 64
OPROJ_SUB_ROWS = 256
GLA_EMIT_CHUNKS = 4
GLA_PREP_ROWS = 512
LRU_SEGMENTS = 8
LRU_PITCH_PAD = 8
ROUTE_SAMPLES = 4
FFN_ROWS = 512
SLOT_ALIGN = 16

_NT = (((1,), (1,)), ((), ()))
_TN = (((0,), (0,)), ((), ()))


def _params(sem, vmem_mib=48):
    return pltpu.CompilerParams(dimension_semantics=sem, vmem_limit_bytes=vmem_mib * 2 ** 20)


def _aligned(index, multiple):
    return index if isinstance(index, int) else pl.multiple_of(index, multiple)


def _sigmoid(z):
    return 1.0 / (1.0 + jnp.exp(-z))


def _silu(z):
    return z * _sigmoid(z)


def _mod_kernel(c_ref, w_ref, b_ref, o_ref):
    a = _silu(c_ref[...]).astype(BF16)
    o_ref[0] = jnp.dot(a, w_ref[0].astype(BF16), preferred_element_type=F32) + b_ref[0]


def _modulation(cc, w_mod, b_mod):
    R, D = cc.shape
    L, _, N = w_mod.shape
    tn = 1024
    return pl.pallas_call(
        _mod_kernel,
        grid=(L, N // tn),
        in_specs=[pl.BlockSpec((R, D), lambda l, j: (0, 0)),
                  pl.BlockSpec((1, D, tn), lambda l, j: (l, 0, j)),
                  pl.BlockSpec((1, 1, tn), lambda l, j: (l, 0, j))],
        out_specs=pl.BlockSpec((1, R, tn), lambda l, j: (l, 0, j)),
        out_shape=jax.ShapeDtypeStruct((L, R, N), F32),
        compiler_params=_params(("arbitrary", "arbitrary")),
        name="modulation",
    )(cc, w_mod, b_mod.reshape(L, 1, N))


def _rms_modulate(x, g_ref, m_ref=None, shift_row=0, scale_row=1):
    y = x * lax.rsqrt(jnp.mean(x * x, axis=-1, keepdims=True) + RMS_EPS) * g_ref[...]
    if m_ref is not None:
        y = y * (1.0 + m_ref[0, scale_row:scale_row + 1, :]) + m_ref[0, shift_row:shift_row + 1, :]
    return y


def _norm_kernel(x_ref, g_ref, m_ref, h_ref):
    h_ref[0] = _rms_modulate(x_ref[0], g_ref, m_ref).astype(h_ref.dtype)


def _norm(x, g, mod):
    B, T, D = x.shape
    tt = min(T, 512)
    per_sample = mod.shape[0] > 1
    rows = pl.BlockSpec((1, tt, D), lambda b, t: (b, t, 0))
    return pl.pallas_call(
        _norm_kernel,
        grid=(B, T // tt),
        in_specs=[rows, pl.BlockSpec((1, D), lambda b, t: (0, 0)),
                  pl.BlockSpec((1, N_MOD, D), (lambda b, t: (b, 0, 0)) if per_sample else (lambda b, t: (0, 0, 0)))],
        out_specs=rows, out_shape=jax.ShapeDtypeStruct((B, T, D), BF16),
        compiler_params=_params(("arbitrary", "arbitrary")),
        name="rmsnorm",
    )(x, g.reshape(1, D), mod)


def _proj_kernel(*refs, n_side):
    a_ref, w_ref = refs[:2]
    side_in, o_ref, side_out = refs[2:2 + n_side], refs[2 + n_side], refs[3 + n_side:]
    o_ref[...] = jnp.dot(a_ref[...], w_ref[...], preferred_element_type=F32).astype(o_ref.dtype)
    for src, dst in zip(side_in, side_out):
        dst[...] = src[...].astype(dst.dtype)


def _proj(a, w, out_dtype, tn, side=(), layer=0):
    B, T, K = a.shape
    N = w.shape[1]
    M = B * T
    tm = min(M, 512)
    n_i = M // tm
    steps = (N // tn) * n_i
    flat = [s.reshape(s.shape[0], steps, -1, s.shape[-1]) for s in side]
    res = pl.pallas_call(
        functools.partial(_proj_kernel, n_side=len(side)),
        grid=(N // tn, n_i),
        in_specs=[pl.BlockSpec((tm, K), lambda j, i: (i, 0)), pl.BlockSpec((K, tn), lambda j, i: (0, j))]
                 + [pl.BlockSpec((1, 1) + f.shape[2:], lambda j, i: (layer, j * n_i + i, 0, 0)) for f in flat],
        out_specs=[pl.BlockSpec((tm, tn), lambda j, i: (i, j))]
                  + [pl.BlockSpec((1, 1) + f.shape[2:], lambda j, i: (0, j * n_i + i, 0, 0)) for f in flat],
        out_shape=[jax.ShapeDtypeStruct((M, N), out_dtype)]
                  + [jax.ShapeDtypeStruct((1,) + f.shape[1:], BF16) for f in flat],
        compiler_params=_params(("arbitrary", "arbitrary")),
        name="proj",
    )(a.reshape(M, K), w, *flat)
    out = res[0].reshape(B, T, N)
    return (out, *[r.reshape((1,) + s.shape[1:]) for r, s in zip(res[1:], side)]) if side else out


def _oproj_kernel(*refs, n_in, n_experts):
    a_refs, w_refs = refs[:n_in], refs[n_in:2 * n_in]
    x_ref, m_ref, g_ref, wr_ref, o_ref, h_ref, lg_ref = refs[2 * n_in:]
    tm = x_ref.shape[1]
    sub = min(tm, OPROJ_SUB_ROWS)
    for s in range(tm // sub):
        rows = slice(s * sub, (s + 1) * sub)
        acc = jnp.dot(a_refs[0][0, rows, :], w_refs[0][...], preferred_element_type=F32)
        for a_ref, w_ref in zip(a_refs[1:], w_refs[1:]):
            acc = acc + jnp.dot(a_ref[0, rows, :], w_ref[...], preferred_element_type=F32)
        x_new = x_ref[0, rows, :] + m_ref[0, 2:3, :] * acc
        o_ref[0, rows, :] = x_new
        y = _rms_modulate(x_new, g_ref, m_ref, shift_row=3, scale_row=4)
        y_hi = y.astype(BF16)
        h_ref[0, rows, :] = y_hi
        y_lo = (y - y_hi.astype(F32)).astype(BF16)
        parts = (jnp.dot(y_hi, wr_ref[...], preferred_element_type=F32)
                 + jnp.dot(y_lo, wr_ref[...], preferred_element_type=F32))
        lg_ref[0, :, rows] = jnp.transpose(parts + pltpu.roll(parts, LANES - n_experts, 1))


def _oproj(ys, ws, x, mod, g_ffn, router_wt):
    B, T, D = x.shape
    E = router_wt.shape[0]
    tm = min(T, 512)
    per_sample = mod.shape[0] > 1
    w_r_hi = router_wt.T.astype(BF16)
    w_r_lo = (router_wt.T - w_r_hi.astype(F32)).astype(BF16)
    w_r_split = jnp.pad(jnp.concatenate([w_r_hi, w_r_lo], axis=1), ((0, 0), (0, LANES - 2 * E)))
    rows = lambda n: pl.BlockSpec((1, tm, n), lambda b, t: (b, t, 0))
    in_specs = [rows(y.shape[2]) for y in ys]
    in_specs += [pl.BlockSpec(w.shape, lambda b, t: (0, 0)) for w in ws]
    in_specs += [rows(D),
                 pl.BlockSpec((1, N_MOD, D), (lambda b, t: (b, 0, 0)) if per_sample else (lambda b, t: (0, 0, 0))),
                 pl.BlockSpec((1, D), lambda b, t: (0, 0)), pl.BlockSpec((D, LANES), lambda b, t: (0, 0))]
    x_new, h, logits = pl.pallas_call(
        functools.partial(_oproj_kernel, n_in=len(ys), n_experts=E),
        grid=(B, T // tm), in_specs=in_specs,
        out_specs=[rows(D), rows(D), pl.BlockSpec((1, LANES, tm), lambda b, t: (b, 0, t))],
        out_shape=[jax.ShapeDtypeStruct((B, T, D), F32), jax.ShapeDtypeStruct((B, T, D), BF16),
                   jax.ShapeDtypeStruct((B, LANES, T), F32)],
        compiler_params=_params(("arbitrary", "arbitrary"), vmem_mib=56),
        name="oproj",
    )(*ys, *ws, x, mod, g_ffn.reshape(1, D), w_r_split)
    return x_new, h, logits


def _log_sigmoid(z):
    return jnp.minimum(z, 0.0) - jnp.log(1.0 + jnp.exp(-jnp.abs(z)))


def _gla_kernel(*refs, n_chunks, use_rope, has_state):
    it = iter(refs)
    q_ref, k_ref, v_ref, g_ref, za_ref = (next(it) for _ in range(5))
    cos_ref, sin_ref = (next(it), next(it)) if use_rope else (None, None)
    aup_ref, ab_ref, gg_ref = next(it), next(it), next(it)
    s0_ref = next(it) if has_state else None
    y_ref, sfin_ref = next(it), next(it)
    qd_scr, ki_scr, ke_scr, kv_scr, dec_scr, sall_scr, st_scr = (next(it) for _ in range(7))
    C, DK = GLA_CHUNK, GLA_DK
    T = n_chunks * C
    RB = min(T, GLA_PREP_ROWS)
    cpb = RB // C

    def increments(blk):
        for i in range(cpb):
            c = blk * cpb + i
            rows = pl.ds(_aligned(c * C, C), C)
            kv_scr[c] = lax.dot_general(v_ref[0, rows, :].astype(BF16), ke_scr[c], _TN, preferred_element_type=F32)

    def prepare(blk):
        r0 = _aligned(blk * RB, RB)
        rows = pl.ds(r0, RB)
        z = jnp.dot(za_ref[0, rows, :].astype(BF16), aup_ref[...], preferred_element_type=F32) + ab_ref[...]
        la = _log_sigmoid(z) * (1.0 / GLA_TAU)
        in_chunk = lax.broadcasted_iota(jnp.int32, la.shape, 0) % C
        prefix = la
        step = 1
        while step < C:
            prefix = prefix + jnp.where(in_chunk >= step, pltpu.roll(prefix, step, 0), 0.0)
            step *= 2
        q = q_ref[0, rows, :] * (GLA_DK ** -0.5)
        k = k_ref[0, rows, :]
        if use_rope:
            cos, sin = cos_ref[rows, :], sin_ref[rows, :]
            even = (lax.broadcasted_iota(jnp.int32, q.shape, 1) % 2) == 0

            def rope(t):
                swapped = jnp.where(even, pltpu.roll(t, LANES - 1, 1), pltpu.roll(t, 1, 1))
                return t * cos + swapped * sin
            q, k = rope(q), rope(k)
        split = lambda t: t.reshape(cpb, C, t.shape[-1])
        la, prefix, q, k = split(la), split(prefix), split(q), split(k)
        tot = prefix[:, C - 1:C, :]
        b_f = prefix[..., :DK]
        b_b = tot[..., DK:] - prefix[..., DK:] + la[..., DK:]
        dec = jnp.exp(tot)
        c0 = _aligned(blk * cpb, cpb)
        qd_scr[pl.ds(c0, cpb)] = jnp.concatenate([q * jnp.exp(b_f), q * jnp.exp(b_b)], axis=2).astype(BF16)
        ki_f, ki_b = k * jnp.exp(-b_f), k * jnp.exp(-b_b)
        ki_scr[0, pl.ds(c0, cpb)] = ki_f.astype(BF16)
        ki_scr[1, pl.ds(c0, cpb)] = ki_b.astype(BF16)
        ke_scr[pl.ds(c0, cpb)] = jnp.concatenate([ki_f * dec[..., :DK], ki_b * dec[..., DK:]], axis=2).astype(BF16)
        dec_scr[pl.ds(c0, cpb)] = jnp.broadcast_to(dec, (cpb,) + dec_scr.shape[1:])

    def prepare_and_increment(blk, carry):
        increments(blk - 1)
        prepare(blk)
        return carry

    n_blocks = T // RB
    prepare(0)
    lax.fori_loop(1, n_blocks, prepare_and_increment, 0)
    increments(n_blocks - 1)

    st_scr[...] = s0_ref[0, 0] if has_state else jnp.zeros(st_scr.shape, F32)

    def recur(i, carry):
        cf, cb = i, n_chunks - 1 - i
        st = st_scr[...]
        sall_scr[cf, :, :DK] = st[:, :DK].astype(BF16)
        sall_scr[cb, :, DK:] = st[:, DK:].astype(BF16)
        st_scr[:, :DK] = st[:, :DK] * dec_scr[cf, 0:1, :DK] + kv_scr[cf, :, :DK]
        st_scr[:, DK:] = st[:, DK:] * dec_scr[cb, 0:1, DK:] + kv_scr[cb, :, DK:]
        return carry

    lax.fori_loop(0, n_chunks, recur, 0)
    sfin_ref[0, 0] = st_scr[...]

    G = min(n_chunks, GLA_EMIT_CHUNKS)
    GR = G * C
    row = lax.broadcasted_iota(jnp.int32, (GR, GR), 0)
    col = lax.broadcasted_iota(jnp.int32, (GR, GR), 1)
    same_chunk = (row // C) == (col // C)
    keep_f, keep_b = same_chunk & (row >= col), same_chunk & (row <= col)

    def emit(grp, carry):
        c0 = pl.multiple_of(grp * G, G)
        rows = pl.ds(pl.multiple_of(grp * GR, GR), GR)
        qd = qd_scr[pl.ds(c0, G)].reshape(GR, 2 * DK)
        ki_f = ki_scr[0, pl.ds(c0, G)].reshape(GR, DK)
        ki_b = ki_scr[1, pl.ds(c0, G)].reshape(GR, DK)
        s = (jnp.where(keep_f, lax.dot_general(qd[:, :DK], ki_f, _NT, preferred_element_type=F32), 0.0)
             + jnp.where(keep_b, lax.dot_general(qd[:, DK:], ki_b, _NT, preferred_element_type=F32), 0.0))
        carried = [lax.dot_general(qd[i * C:(i + 1) * C], sall_scr[c0 + i], _NT, preferred_element_type=F32)
                   for i in range(G)]
        o = (jnp.dot(s.astype(BF16), v_ref[0, rows, :].astype(BF16), preferred_element_type=F32)
             + jnp.concatenate(carried, axis=0))
        o = o * lax.rsqrt(jnp.mean(o * o, axis=-1, keepdims=True) + RMS_EPS) * gg_ref[...]
        y_ref[0, rows, :] = (o * _silu(g_ref[0, rows, :])).astype(y_ref.dtype)
        return carry

    lax.fori_loop(0, n_chunks // G, emit, 0, unroll=min(4, n_chunks // G))


def _gla(p_main, p_za, aup, ab, gg, rope_tabs, state):
    B, T, _ = p_main.shape
    H, DK, DV = GLA_HEADS, GLA_DK, GLA_DV
    use_rope, has_state = rope_tabs is not None, state is not None
    n_chunks = T // GLA_CHUNK
    in_specs = [pl.BlockSpec((1, T, DK), lambda b, h: (b, 0, h)),
                pl.BlockSpec((1, T, DK), lambda b, h: (b, 0, H + h)),
                pl.BlockSpec((1, T, DV), lambda b, h: (b, 0, H + h)),
                pl.BlockSpec((1, T, DV), lambda b, h: (b, 0, 2 * H + h)),
                pl.BlockSpec((1, T, LANES), lambda b, h: (b, 0, 0))]
    args = [p_main, p_main, p_main, p_main, p_za]
    if use_rope:
        in_specs += [pl.BlockSpec((T, DK), lambda b, h: (0, 0))] * 2
        args += list(rope_tabs)
    in_specs += [pl.BlockSpec((LANES, 2 * DK), lambda b, h: (0, h)),
                 pl.BlockSpec((1, 2 * DK), lambda b, h: (0, h)),
                 pl.BlockSpec((1, DV), lambda b, h: (0, 0))]
    args += [aup, ab, gg]
    st_spec = pl.BlockSpec((1, 1, DV, 2 * DK), lambda b, h: (b, h, 0, 0))
    if has_state:
        in_specs.append(st_spec)
        args.append(state)
    return pl.pallas_call(
        functools.partial(_gla_kernel, n_chunks=n_chunks, use_rope=use_rope, has_state=has_state),
        grid=(B, H), in_specs=in_specs,
        out_specs=[pl.BlockSpec((1, T, DV), lambda b, h: (b, 0, h)), st_spec],
        out_shape=[jax.ShapeDtypeStruct((B, T, H * DV), BF16), jax.ShapeDtypeStruct((B, H, DV, 2 * DK), F32)],
        scratch_shapes=[pltpu.VMEM((n_chunks, GLA_CHUNK, 2 * DK), BF16), pltpu.VMEM((2, n_chunks, GLA_CHUNK, DK), BF16),
                        pltpu.VMEM((n_chunks, GLA_CHUNK, 2 * DK), BF16),
                        pltpu.VMEM((n_chunks, DV, 2 * DK), F32), pltpu.VMEM((n_chunks, 8, 2 * DK), F32),
                        pltpu.VMEM((n_chunks, DV, 2 * DK), BF16), pltpu.VMEM((DV, 2 * DK), F32)],
        compiler_params=_params(("arbitrary", "arbitrary")),
        name="gla",
    )(*args)


def _gelu_tanh(z):
    return 0.5 * z * (1.0 + jnp.tanh(np.sqrt(2.0 / np.pi) * (z + 0.044715 * (z * z * z))))


def _lru_kernel(x_ref, gt_ref, cw_ref, cb_ref, wa_ref, ba_ref, wi_ref, bi_ref, lam_ref, h0f_ref, h0b_ref,
                y_ref, hf_ref, hb_ref, a_scr, bx_scr, hl_scr, ap_scr, *, T):
    L = T // LRU_SEGMENTS
    P = L + LRU_PITCH_PAD
    x = x_ref[0]
    row = lax.broadcasted_iota(jnp.int32, x.shape, 0)

    def from_earlier(v, d, fill):
        return jnp.where(row >= d, pltpu.roll(v, d, 0), fill)

    def from_later(v, d, fill):
        return jnp.where(row < T - d, pltpu.roll(v, T - d, 0), fill)

    xc = (cw_ref[0:1, :] * from_earlier(x, 1, 0.0) + cw_ref[1:2, :] * x
          + cw_ref[2:3, :] * from_later(x, 1, 0.0) + cw_ref[3:4, :] * from_later(x, 2, 0.0) + cb_ref[...])
    xcb = xc.astype(BF16)

    def sigmoid(z):
        return 0.5 * jnp.tanh(0.5 * z) + 0.5

    for d in range(2):
        r = sigmoid(jnp.dot(xcb, wa_ref[d, 0], preferred_element_type=F32) + ba_ref[d])
        gate_i = sigmoid(jnp.dot(xcb, wi_ref[d, 0], preferred_element_type=F32) + bi_ref[d])
        neg_lam = -lam_ref[d]
        softplus = jnp.maximum(neg_lam, 0.0) + jnp.log(1.0 + jnp.exp(-jnp.abs(neg_lam)))
        a = jnp.exp((-LRU_C) * r * softplus)
        u = 1.0 - a * a
        bx = jnp.where(u > 0.0, u * lax.rsqrt(u), 0.0) * (gate_i * xc)
        for s in range(LRU_SEGMENTS):
            a_scr[d, s * P:s * P + L, :] = a[s * L:(s + 1) * L]
            bx_scr[d, s * P:s * P + L, :] = bx[s * L:(s + 1) * L]

    def step(i, carry):
        h_f, p_f, h_b, p_b = carry
        at_f = pl.ds(i, LRU_SEGMENTS, stride=P)
        at_b = pl.ds(L - 1 - i, LRU_SEGMENTS, stride=P)
        a_f, a_b = a_scr[0, at_f, :], a_scr[1, at_b, :]
        h_f = a_f * h_f + bx_scr[0, at_f, :]
        h_b = a_b * h_b + bx_scr[1, at_b, :]
        p_f, p_b = a_f * p_f, a_b * p_b
        hl_scr[0, at_f, :] = h_f
        hl_scr[1, at_b, :] = h_b
        ap_scr[0, at_f, :] = p_f
        ap_scr[1, at_b, :] = p_b
        return h_f, p_f, h_b, p_b

    zeros, ones = jnp.zeros((LRU_SEGMENTS, x.shape[1]), F32), jnp.ones((LRU_SEGMENTS, x.shape[1]), F32)
    h_f, p_f, h_b, p_b = lax.fori_loop(0, L, step, (zeros, ones, zeros, ones), unroll=8)

    enter_f, enter_b = [h0f_ref[0]], [h0b_ref[0]]
    for s in range(LRU_SEGMENTS):
        enter_f.append(h_f[s:s + 1] + p_f[s:s + 1] * enter_f[-1])
        t = LRU_SEGMENTS - 1 - s
        enter_b.append(h_b[t:t + 1] + p_b[t:t + 1] * enter_b[-1])
    hf_ref[0] = enter_f[-1]
    hb_ref[0] = enter_b[-1]
    for s in range(LRU_SEGMENTS):
        seg = slice(s * P, s * P + L)
        h = (hl_scr[0, seg, :] + ap_scr[0, seg, :] * enter_f[s]
             + hl_scr[1, seg, :] + ap_scr[1, seg, :] * enter_b[LRU_SEGMENTS - 1 - s])
        y_ref[0, s * L:(s + 1) * L, :] = (h * _gelu_tanh(gt_ref[0, s * L:(s + 1) * L, :])).astype(y_ref.dtype)


def _lru(p_main, cw, cb, wa, ba, wi, bi, lam, states):
    B, T, _ = p_main.shape
    W, NB, BD = LRU_WIDTH, LRU_BLOCKS, LRU_BLOCK_DIM
    x_blk0 = (2 * GLA_KEY_WIDTH + 2 * GLA_VAL_WIDTH) // BD
    vec = lambda n: pl.BlockSpec((n, 1, BD), lambda b, j: (0, 0, j))
    mat = pl.BlockSpec((2, 1, BD, BD), lambda b, j: (0, j, 0, 0))
    st_spec = pl.BlockSpec((1, 1, BD), lambda b, j: (b, 0, j))
    st_shape = jax.ShapeDtypeStruct((B, 1, W), F32)
    y, hf, hb = pl.pallas_call(
        functools.partial(_lru_kernel, T=T),
        grid=(B, NB),
        in_specs=[pl.BlockSpec((1, T, BD), lambda b, j: (b, 0, x_blk0 + j)),
                  pl.BlockSpec((1, T, BD), lambda b, j: (b, 0, x_blk0 + NB + j)),
                  pl.BlockSpec((LRU_CONV, BD), lambda b, j: (0, j)),
                  pl.BlockSpec((1, BD), lambda b, j: (0, j)),
                  mat, vec(2), mat, vec(2), vec(2), st_spec, st_spec],
        out_specs=[pl.BlockSpec((1, T, BD), lambda b, j: (b, 0, j)), st_spec, st_spec],
        out_shape=[jax.ShapeDtypeStruct((B, T, W), BF16), st_shape, st_shape],
        scratch_shapes=[pltpu.VMEM((2, LRU_SEGMENTS * (T // LRU_SEGMENTS + LRU_PITCH_PAD), BD), F32)] * 4,
        compiler_params=_params(("arbitrary", "arbitrary")),
        name="rglru",
    )(p_main, p_main, cw, cb.reshape(1, W), wa, ba.reshape(2, 1, W), wi, bi.reshape(2, 1, W),
      lam.reshape(2, 1, W), *states)
    return y, (hf, hb)


def _na_tile(i, n_tiles, rows):
    r0 = i * NA_QROWS
    ks = min(max(r0 - NA_KH // 2, 0), rows - NA_KROWS)
    cls = 0 if i == 0 else (2 if i == n_tiles - 1 else 1)
    return r0, ks, cls


def _na_bias_tables(rpb, rows):
    H = rpb.shape[0]
    qc = np.arange(GRID_W)[:, None]
    kc = np.arange(GRID_W)[None, :]
    col_start = np.clip(qc - NA_KW // 2, 0, GRID_W - NA_KW)
    col_valid = (kc >= col_start) & (kc < col_start + NA_KW)
    dc = np.clip(kc - qc, -(NA_KW - 1), NA_KW - 1) + NA_KW - 1
    pick = (dc[None] == np.arange(2 * NA_KW - 1)[:, None, None]).astype(np.float32)
    band = jnp.einsum('hrd,dqk->hrqk', rpb.astype(F32), jnp.asarray(pick), precision=HIGHEST)
    band = jnp.where(col_valid, band, NEG_BIG)
    masked = jnp.full((H, GRID_W, GRID_W), NEG_BIG, F32)
    n_tiles = rows // NA_QROWS
    tabs = []
    for i in (0, 1, n_tiles - 1):
        r0, ks, _ = _na_tile(i, n_tiles, rows)
        per_qrow = []
        for qr in range(r0, r0 + NA_QROWS):
            row_start = min(max(qr - NA_KH // 2, 0), rows - NA_KH)
            blocks = [band[:, kr - qr + NA_KH - 1] if row_start <= kr < row_start + NA_KH else masked
                      for kr in range(ks, ks + NA_KROWS)]
            per_qrow.append(jnp.concatenate(blocks, axis=2))
        tabs.append(jnp.concatenate(per_qrow, axis=1))
    return jnp.stack(tabs, axis=1) * LOG2_E


def _na_kernel(*refs, rows, need_ctx):
    q_ref, k_ref, v_ref, qc_ref, kc_ref, vc_ref, tab_ref = refs[:7]
    y_ref, vx_ref = refs[7], refs[-1]
    HD = NA_HEAD_DIM
    scale = HD ** -0.5 * LOG2_E
    vx_ref[...] = jnp.concatenate([v_ref[0], jnp.ones(v_ref.shape[1:], BF16)], axis=1)
    kc = kc_ref[0]
    vcx = jnp.concatenate([vc_ref[0], jnp.ones(vc_ref.shape[1:], BF16)], axis=1)
    n_tiles = rows // NA_QROWS
    nq, nk = NA_QROWS * GRID_W, NA_KROWS * GRID_W
    for i in range(n_tiles):
        r0, ks, cls = _na_tile(i, n_tiles, rows)
        q = q_ref[0, r0 * GRID_W:r0 * GRID_W + nq, :]
        kt = k_ref[0, ks * GRID_W:ks * GRID_W + nk, :]
        s_loc = lax.dot_general(q, kt, _NT, preferred_element_type=F32) * scale + tab_ref[0, cls]
        s_ctx = lax.dot_general(q, kc, _NT, preferred_element_type=F32) * scale
        m = jnp.maximum(jnp.max(s_loc, axis=-1, keepdims=True), jnp.max(s_ctx, axis=-1, keepdims=True))
        o = (jnp.dot(jnp.exp2(s_loc - m).astype(BF16), vx_ref[ks * GRID_W:ks * GRID_W + nk, :], preferred_element_type=F32)
             + jnp.dot(jnp.exp2(s_ctx - m).astype(BF16), vcx, preferred_element_type=F32))
        y_ref[0, r0 * GRID_W:r0 * GRID_W + nq, :] = (o[:, :HD] / o[:, HD:]).astype(y_ref.dtype)
    if need_ctx:
        yc_ref = refs[8]
        s = lax.dot_general(qc_ref[0], kc, _NT, preferred_element_type=F32) * scale
        p = jnp.exp2(s - jnp.max(s, axis=-1, keepdims=True))
        o = jnp.dot(p.astype(BF16), vcx, preferred_element_type=F32)
        yc_ref[0] = (o[:, :HD] / o[:, HD:]).astype(yc_ref.dtype)


def _na(qkv_lat, qkv_ctx, tabs, need_ctx):
    B, T, _ = qkv_lat.shape
    Tc = qkv_ctx.shape[1]
    H, HD = NA_HEADS, NA_HEAD_DIM
    blk = lambda t, off: pl.BlockSpec((1, t, HD), lambda h, b: (b, 0, off + h))
    out_specs = [pl.BlockSpec((1, T, HD), lambda h, b: (b, 0, h))]
    out_shape = [jax.ShapeDtypeStruct((B, T, H * HD), BF16)]
    if need_ctx:
        out_specs.append(pl.BlockSpec((1, Tc, HD), lambda h, b: (b, 0, h)))
        out_shape.append(jax.ShapeDtypeStruct((B, Tc, H * HD), BF16))
    res = pl.pallas_call(
        functools.partial(_na_kernel, rows=T // GRID_W, need_ctx=need_ctx),
        grid=(H, B),
        in_specs=[blk(T, 0), blk(T, H), blk(T, 2 * H), blk(Tc, 0), blk(Tc, H), blk(Tc, 2 * H),
                  pl.BlockSpec((1,) + tabs.shape[1:], lambda h, b: (h, 0, 0, 0))],
        out_specs=out_specs, out_shape=out_shape,
        scratch_shapes=[pltpu.VMEM((T, 2 * HD), BF16)],
        compiler_params=_params(("arbitrary", "arbitrary")),
        name="natten",
    )(qkv_lat, qkv_lat, qkv_lat, qkv_ctx, qkv_ctx, qkv_ctx, tabs)
    return (res[1] if need_ctx else None), res[0]


def _route_kernel(lg_ref, pos_ref, aff_ref, lo_ref, *, cap):
    lg = lg_ref[...]
    bb, n_exp, N = lg.shape
    ex = jnp.exp(lg - jnp.max(lg, axis=1, keepdims=True))
    aff = ex / jnp.sum(ex, axis=1, keepdims=True)
    aff_ref[...] = aff
    E = bb * n_exp
    aff = aff.reshape(E, N)
    bits = pltpu.bitcast(aff, jnp.int32)
    thr = jnp.zeros((E, 1), jnp.int32)
    for bit in range(29, -1, -1):
        cand = thr | (1 << bit)
        cnt = jnp.sum(jnp.where(bits >= cand, 1.0, 0.0), axis=1, keepdims=True)
        thr = jnp.where(cnt >= cap, cand, thr)
    above = bits > thr
    tied = bits == thr
    need = cap - jnp.sum(jnp.where(above, 1.0, 0.0), axis=1, keepdims=True)
    pc = min(PREFIX_CHUNK, N)
    before = jnp.where(lax.broadcasted_iota(jnp.int32, (pc, pc), 0) < lax.broadcasted_iota(jnp.int32, (pc, pc), 1),
                       1.0, 0.0).astype(BF16)
    tied_seen = jnp.zeros((E, 1), F32)
    sel_seen = jnp.zeros((E, 1), F32)
    tile_lane = lax.broadcasted_iota(jnp.int32, (E, LANES), 1)
    lo = jnp.where(tile_lane == N // pc, float(cap), 0.0)
    for c in range(N // pc):
        sl = slice(c * pc, (c + 1) * pc)
        lo = jnp.where(tile_lane == c, sel_seen, lo)
        tied_c = jnp.where(tied[:, sl], 1.0, 0.0)
        tied_before = jnp.dot(tied_c.astype(BF16), before, preferred_element_type=F32) + tied_seen
        sel_c = jnp.where(above[:, sl], 1.0, jnp.where(tied_before < need, tied_c, 0.0))
        sel_before = jnp.dot(sel_c.astype(BF16), before, preferred_element_type=F32) + sel_seen
        pos_ref[:, :, sl] = jnp.where(sel_c > 0.0, sel_before, -1.0).reshape(bb, n_exp, pc)
        tied_seen = tied_seen + jnp.sum(tied_c, axis=1, keepdims=True)
        sel_seen = sel_seen + jnp.sum(sel_c, axis=1, keepdims=True)
    lo_ref[...] = lo.astype(jnp.int32).reshape(bb, n_exp, LANES)


def _route(logits_t, cap):
    B, _, N = logits_t.shape
    E = N_EXPERTS
    bb = ROUTE_SAMPLES if B % ROUTE_SAMPLES == 0 else 1
    spec = pl.BlockSpec((bb, E, N), lambda b: (b, 0, 0))
    lo_spec = pl.BlockSpec((bb, E, LANES), lambda b: (b, 0, 0))
    pos, aff, lo = pl.pallas_call(
        functools.partial(_route_kernel, cap=cap),
        grid=(B // bb,), in_specs=[spec], out_specs=[spec, spec, lo_spec],
        out_shape=[jax.ShapeDtypeStruct((B, E, N), F32)] * 2 + [jax.ShapeDtypeStruct((B, E, LANES), jnp.int32)],
        compiler_params=_params(("arbitrary",)),
        name="route",
    )(logits_t)
    n_bounds = N // min(PREFIX_CHUNK, N) + 1
    return pos, aff, lo[:, :, :n_bounds].reshape(-1)


def _window_count(lo_ref, base, experts, tile, n_bounds, win):
    starts, n_win = [], 0
    for e in experts:
        first = lo_ref[(base + e) * n_bounds + tile]
        end = lo_ref[(base + e) * n_bounds + tile + 1]
        start = (first // SLOT_ALIGN) * SLOT_ALIGN
        starts.append(start)
        n_win = jnp.maximum(n_win, (end - start + win - 1) // win)
    return starts, n_win


def _gather_kernel(lo_ref, pos_ref, h_ref, xs_ref, *, cap, win, tn):
    EG, N = pos_ref.shape[1], pos_ref.shape[2]
    n_bounds = N // tn + 1
    base = pl.program_id(0) * (EG * pl.num_programs(1)) + pl.program_id(1) * EG
    xs_ref[...] = jnp.zeros(xs_ref.shape, xs_ref.dtype)
    slot_iota = lax.broadcasted_iota(jnp.int32, (win, tn), 0)
    for j in range(N // tn):
        h_tile = h_ref[0, j * tn:(j + 1) * tn, :]
        starts, n_win = _window_count(lo_ref, base, range(EG), j, n_bounds, win)

        def window(w, carry):
            pieces, offs = [], []
            for e in range(EG):
                first = starts[e] + w * win
                off = jnp.minimum(first, cap - win)
                prow = pos_ref[0, e:e + 1, j * tn:(j + 1) * tn]
                hit = ((slot_iota + off).astype(F32) == prow) & (prow >= first.astype(F32))
                pieces.append(jnp.where(hit, 1.0, 0.0).astype(BF16))
                offs.append(pl.multiple_of(off, SLOT_ALIGN))
            got = jnp.dot(jnp.concatenate(pieces, axis=0), h_tile, preferred_element_type=F32)
            for e in range(EG):
                xs_ref[0, e, pl.ds(offs[e], win), :] += got[e * win:(e + 1) * win].astype(xs_ref.dtype)
            return carry

        lax.fori_loop(0, n_win, window, 0)


def _gather(lo, pos, h, cap):
    B, E, N = pos.shape
    D = h.shape[2]
    EG = 8
    tn, win = min(PREFIX_CHUNK, N), min(SLOT_WINDOW, cap)
    return pl.pallas_call(
        functools.partial(_gather_kernel, cap=cap, win=win, tn=tn),
        grid_spec=pltpu.PrefetchScalarGridSpec(
            num_scalar_prefetch=1, grid=(B, E // EG),
            in_specs=[pl.BlockSpec((1, EG, N), lambda b, g, lo: (b, g, 0)),
                      pl.BlockSpec((1, N, D), lambda b, g, lo: (b, 0, 0))],
            out_specs=pl.BlockSpec((1, EG, cap, D), lambda b, g, lo: (b, g, 0, 0))),
        out_shape=jax.ShapeDtypeStruct((B, E, cap, D), BF16),
        compiler_params=_params(("arbitrary", "arbitrary")),
        name="moe_gather",
    )(lo, pos, h)


def _ffn_kernel(xs_ref, wg_ref, wu_ref, wd_ref, y_ref):
    bb, _, cap, D = xs_ref.shape
    xs = xs_ref[:, 0].reshape(bb * cap, D)
    hid = (_silu(jnp.dot(xs, wg_ref[0, 0], preferred_element_type=F32))
           * jnp.dot(xs, wu_ref[0, 0], preferred_element_type=F32))
    y = jnp.dot(hid.astype(BF16), wd_ref[0, 0], preferred_element_type=F32)
    y_ref[:, 0] = y.reshape(bb, cap, D).astype(y_ref.dtype)


def _ffn(xs, wg, wu, wd, layer):
    B, E, cap, D = xs.shape
    F = wg.shape[3]
    bb = max(1, min(B, FFN_ROWS // cap))
    tok = pl.BlockSpec((bb, 1, cap, D), lambda e, b: (b, e, 0, 0))
    return pl.pallas_call(
        _ffn_kernel,
        grid=(E, B // bb),
        in_specs=[tok, pl.BlockSpec((1, 1, D, F), lambda e, b: (layer, e, 0, 0)),
                  pl.BlockSpec((1, 1, D, F), lambda e, b: (layer, e, 0, 0)),
                  pl.BlockSpec((1, 1, F, D), lambda e, b: (layer, e, 0, 0))],
        out_specs=tok,
        out_shape=jax.ShapeDtypeStruct((B, E, cap, D), BF16),
        compiler_params=_params(("arbitrary", "arbitrary")),
        name="moe_ffn",
    )(xs, wg, wu, wd)


def _combine_kernel(*refs, cap, win, keep_x, next_modulated):
    it = iter(refs)
    lo_ref, pos_ref, aff_ref, y_ref, x_ref, m_ref, gn_ref = (next(it) for _ in range(7))
    mn_ref = next(it) if next_modulated else None
    o_ref = next(it) if keep_x else None
    hn_ref, ystack_ref, acc_ref = next(it), next(it), next(it)
    E = y_ref.shape[1]
    n_bounds = pl.num_programs(1) + 1
    j = pl.program_id(1)
    starts, n_win = _window_count(lo_ref, pl.program_id(0) * E, range(E), j, n_bounds, win)
    col = lax.broadcasted_iota(jnp.int32, (E, E * win), 1)
    spread = jnp.where(col // win == lax.broadcasted_iota(jnp.int32, (E, E * win), 0), 1.0, 0.0).astype(BF16)
    slot = lax.dot_general(pos_ref[0].astype(BF16), spread, _TN, preferred_element_type=F32)
    gate = lax.dot_general(aff_ref[0].astype(BF16), spread, _TN, preferred_element_type=F32)
    col1 = lax.broadcasted_iota(jnp.int32, (1, E * win), 1)
    col_expert, col_in_win = col1 // win, col1 % win

    def window(w):
        first_col = jnp.zeros((1, E * win), jnp.int32)
        want_col = jnp.zeros((1, E * win), jnp.int32)
        for e in range(E):
            first = starts[e] + w * win
            off = jnp.minimum(first, cap - win)
            ystack_ref[e * win:(e + 1) * win, :] = y_ref[0, e, pl.ds(pl.multiple_of(off, SLOT_ALIGN), win), :]
            first_col = jnp.where(col_expert == e, first, first_col)
            want_col = jnp.where(col_expert == e, off + col_in_win, want_col)
        hit = (slot == want_col.astype(F32)) & (slot >= first_col.astype(F32))
        return jnp.dot(jnp.where(hit, gate, 0.0).astype(BF16), ystack_ref[...], preferred_element_type=F32)

    def more(w, carry):
        acc_ref[...] += window(w)
        return carry

    acc_ref[...] = window(0)
    lax.fori_loop(1, n_win, more, 0)
    x_new = x_ref[0] + m_ref[0, 5:6, :] * acc_ref[...]
    if keep_x:
        o_ref[0] = x_new
    hn_ref[0] = _rms_modulate(x_new, gn_ref, mn_ref).astype(hn_ref.dtype)


def _combine(lo, pos, aff, y, x, mod, g_next, mod_next):
    B, E, cap, D = y.shape
    N = x.shape[1]
    tn, win = min(PREFIX_CHUNK, N), min(SLOT_WINDOW, cap)
    keep_x = next_modulated = mod_next is not None
    tok = pl.BlockSpec((1, E, tn), lambda b, t, lo: (b, 0, t))
    rows = pl.BlockSpec((1, tn, D), lambda b, t, lo: (b, t, 0))
    mod_spec = lambda m: pl.BlockSpec((1, N_MOD, D),
                                      (lambda b, t, lo: (b, 0, 0)) if m.shape[0] > 1 else (lambda b, t, lo: (0, 0, 0)))
    in_specs = [tok, tok, pl.BlockSpec((1, E, cap, D), lambda b, t, lo: (b, 0, 0, 0)), rows, mod_spec(mod),
                pl.BlockSpec((1, D), lambda b, t, lo: (0, 0))]
    args = [lo, pos, aff, y, x, mod, g_next.reshape(1, D)]
    out_specs, out_shape = [rows], [jax.ShapeDtypeStruct((B, N, D), BF16 if next_modulated else F32)]
    if next_modulated:
        in_specs.append(mod_spec(mod_next))
        args.append(mod_next)
        out_specs, out_shape = [rows] + out_specs, [jax.ShapeDtypeStruct((B, N, D), F32)] + out_shape
    res = pl.pallas_call(
        functools.partial(_combine_kernel, cap=cap, win=win, keep_x=keep_x, next_modulated=next_modulated),
        grid_spec=pltpu.PrefetchScalarGridSpec(
            num_scalar_prefetch=1, grid=(B, N // tn), in_specs=in_specs, out_specs=out_specs,
            scratch_shapes=[pltpu.VMEM((E * win, D), BF16), pltpu.VMEM((tn, D), F32)]),
        out_shape=out_shape,
        compiler_params=_params(("arbitrary", "arbitrary"), vmem_mib=56),
        name="moe_combine",
    )(*args)
    return (res[0], res[1]) if keep_x else (None, res[0])


def _moe(x, h, logits_t, mod, wg, wu, wd, layer, g_next, mod_next):
    N = x.shape[1]
    cap = EC_CAPACITY_FACTOR * N // N_EXPERTS
    assert cap <= 256, "slot ids must stay exact in bf16"
    pos, aff, lo = _route(logits_t, cap)
    y = _ffn(_gather(lo, pos, h, cap), wg, wu, wd, layer)
    return _combine(lo, pos, aff, y, x, mod, g_next, mod_next)


def _rope_tables(n_tok):
    t = jnp.arange(n_tok)
    rows = (t // GRID_W).astype(F32)
    cols = (t % GRID_W).astype(F32)
    n_freq = GLA_DK // 4
    inv = ROPE_THETA ** (-jnp.arange(n_freq, dtype=F32) / n_freq)
    ang = jnp.concatenate([rows[:, None] * inv, cols[:, None] * inv], axis=-1)
    cos = jnp.repeat(jnp.cos(ang), 2, axis=-1)
    sin = jnp.repeat(jnp.sin(ang), 2, axis=-1) * jnp.tile(jnp.asarray([-1.0, 1.0], F32), GLA_DK // 2)
    return cos, sin


def _rec_mixer(h_ctx, h_lat, w_in, alpha_up, alpha_b, gla_g, conv_w, conv_b, w_a, b_a, w_i, b_i, lam, rope_tabs,
               side, layer):
    za0 = 2 * GLA_KEY_WIDTH + 2 * GLA_VAL_WIDTH
    za1 = za0 + 2 * GLA_RANK
    w_main = jnp.concatenate([w_in[:, :za0], w_in[:, za1:]], axis=1).astype(BF16)
    w_za = jnp.pad(w_in[:, za0:za1], ((0, 0), (0, LANES - 2 * GLA_RANK))).astype(BF16)
    up = alpha_up.reshape(2, GLA_RANK, GLA_HEADS, 1, GLA_DK)
    zero = jnp.zeros_like(up[0])
    aup = jnp.concatenate([jnp.concatenate([up[0], zero], axis=2), jnp.concatenate([zero, up[1]], axis=2)], axis=0)
    aup = jnp.pad(aup.reshape(2 * GLA_RANK, 2 * GLA_KEY_WIDTH), ((0, LANES - 2 * GLA_RANK), (0, 0))).astype(BF16)
    ab = jnp.transpose(alpha_b.reshape(2, GLA_HEADS, GLA_DK), (1, 0, 2)).reshape(1, 2 * GLA_KEY_WIDTH)
    gg = gla_g.reshape(1, GLA_DV)
    wa, wi = w_a.astype(BF16), w_i.astype(BF16)
    outs = []
    gla_state, lru_state = None, None
    for h, tabs in ((h_ctx, None), (h_lat, rope_tabs)):
        B = h.shape[0]
        if tabs is None:
            p_main = _proj(h, w_main, F32, tn=2560)
        else:
            p_main, *cast = _proj(h, w_main, F32, tn=2560, side=side, layer=layer)
        p_za = _proj(h, w_za, F32, tn=LANES)
        y_gla, gla_state = _gla(p_main, p_za, aup, ab, gg, tabs, gla_state)
        if lru_state is None:
            lru_state = (jnp.zeros((B, 1, LRU_WIDTH), F32),) * 2
        y_lru, lru_state = _lru(p_main, conv_w, conv_b, wa, b_a, wi, b_i, lam, lru_state)
        outs.append([y_gla, y_lru])
    return outs, cast


def kernel(x, c, ctx, c_ctx, w_mod, b_mod, norm_mix_g, norm_ffn_g, w_out, router_w, exp_w_gate, exp_w_up, exp_w_down, rec_w_in, gla_alpha_up, gla_alpha_b, gla_norm_g, lru_conv_w, lru_conv_b, lru_w_a, lru_b_a, lru_w_i, lru_b_i, lru_lambda, na_w_qkv, na_rpb, norm_f_g):
    B, T, D = x.shape
    pad_rows = (-(B + 1)) % 8
    cc = jnp.concatenate([c, c_ctx[None, :], jnp.zeros((pad_rows, D), F32)], axis=0)
    m_all = _modulation(cc, w_mod, b_mod)
    rope_tabs = _rope_tables(T)
    expert_w_f32 = (exp_w_gate, exp_w_up, exp_w_down)
    mods_lat = [m_all[l, :B].reshape(B, N_MOD, D) for l in range(DEPTH)]
    mods_ctx = [m_all[l, B:B + 1].reshape(1, N_MOD, D) for l in range(DEPTH)]
    h_lat = _norm(x, norm_mix_g[0], mods_lat[0])
    h_ctx = _norm(ctx, norm_mix_g[0], mods_ctx[0])
    for l in range(DEPTH):
        last = l == DEPTH - 1
        i = l // 2
        m_lat, m_ctx = mods_lat[l], mods_ctx[l]
        w_o = w_out[l].astype(BF16)
        if l % 2 == 0:
            (ys_ctx, ys_lat), expert_w = _rec_mixer(
                h_ctx, h_lat, rec_w_in[i], gla_alpha_up[i], gla_alpha_b[i], gla_norm_g[i], lru_conv_w[i], lru_conv_b[i],
                lru_w_a[i], lru_b_a[i], lru_w_i[i], lru_b_i[i], lru_lambda[i], rope_tabs, expert_w_f32, l)
            ws = [w_o[:GLA_VAL_WIDTH], w_o[GLA_VAL_WIDTH:]]
        else:
            w_qkv = na_w_qkv[i].astype(BF16)
            qkv_lat, *expert_w = _proj(h_lat, w_qkv, BF16, tn=3072, side=expert_w_f32, layer=l)
            qkv_ctx = _proj(h_ctx, w_qkv, BF16, tn=3072)
            y_ctx, y_lat = _na(qkv_lat, qkv_ctx, _na_bias_tables(na_rpb[i], T // GRID_W), not last)
            ys_ctx, ys_lat, ws = [y_ctx], [y_lat], [w_o]
        moe_w = (*expert_w, 0)
        router_wt = router_w[l].T
        x, h_ffn, logits_t = _oproj(ys_lat, ws, x, m_lat, norm_ffn_g[l], router_wt)
        if last:
            _, out = _moe(x, h_ffn, logits_t, m_lat, *moe_w, norm_f_g, None)
            return out
        x, h_lat = _moe(x, h_ffn, logits_t, m_lat, *moe_w, norm_mix_g[l + 1], mods_lat[l + 1])
        ctx, h_ffn, logits_t = _oproj(ys_ctx, ws, ctx, m_ctx, norm_ffn_g[l], router_wt)
        ctx, h_ctx = _moe(ctx, h_ffn, logits_t, m_ctx, *moe_w, norm_mix_g[l + 1], mods_ctx[l + 1])
```

```python
import functools

import numpy as np
import jax
import jax.numpy as jnp
from jax import lax
from jax.experimental import pallas as pl
from jax.experimental.pallas import tpu as pltpu

D_MODEL = 2048
DEPTH = 4
GRID_W = 64
RMS_EPS = 1e-6
N_MOD = 6
ROPE_THETA = 10000.0
GLA_HEADS = 4
GLA_DK = D_MODEL // 16
GLA_DV = D_MODEL // 8
GLA_KEY_WIDTH = GLA_HEADS * GLA_DK
GLA_VAL_WIDTH = GLA_HEADS * GLA_DV
GLA_RANK = 16
GLA_TAU = 16.0
GLA_CHUNK = 64
LRU_WIDTH = D_MODEL // 2
LRU_BLOCKS = 8
LRU_BLOCK_DIM = LRU_WIDTH // LRU_BLOCKS
LRU_CONV = 4
LRU_C = 8.0
NA_HEADS = 16
NA_HEAD_DIM = D_MODEL // NA_HEADS
NA_KH = 8
NA_KW = 16
N_EXPERTS = 16
EC_CAPACITY_FACTOR = 2

F32 = jnp.float32
BF16 = jnp.bfloat16
HIGHEST = lax.Precision.HIGHEST
LANES = 128
NEG_BIG = -1e30
LOG2_E = float(np.log2(np.e))
NA_QROWS = 4
NA_KROWS = NA_QROWS + NA_KH - 1
PREFIX_CHUNK = 256
SLOT_WINDOW = 64
OPROJ_SUB_ROWS = 256
GLA_EMIT_CHUNKS = 4
GLA_PREP_ROWS = 512
LRU_SEGMENTS = 8
LRU_PITCH_PAD = 8
ROUTE_SAMPLES = 4
FFN_ROWS = 512
SLOT_ALIGN = 16

_NT = (((1,), (1,)), ((), ()))
_TN = (((0,), (0,)), ((), ()))


def _params(sem, vmem_mib=48):
    return pltpu.CompilerParams(dimension_semantics=sem, vmem_limit_bytes=vmem_mib * 2 ** 20)


def _aligned(index, multiple):
    return index if isinstance(index, int) else pl.multiple_of(index, multiple)


def _sigmoid(z):
    return 1.0 / (1.0 + jnp.exp(-z))


def _silu(z):
    return z * _sigmoid(z)


def _mod_kernel(c_ref, w_ref, b_ref, o_ref):
    a = _silu(c_ref[...]).astype(BF16)
    o_ref[0] = jnp.dot(a, w_ref[0].astype(BF16), preferred_element_type=F32) + b_ref[0]


def _modulation(cc, w_mod, b_mod):
    R, D = cc.shape
    L, _, N = w_mod.shape
    tn = 1024
    return pl.pallas_call(
        _mod_kernel,
        grid=(L, N // tn),
        in_specs=[pl.BlockSpec((R, D), lambda l, j: (0, 0)),
                  pl.BlockSpec((1, D, tn), lambda l, j: (l, 0, j)),
                  pl.BlockSpec((1, 1, tn), lambda l, j: (l, 0, j))],
        out_specs=pl.BlockSpec((1, R, tn), lambda l, j: (l, 0, j)),
        out_shape=jax.ShapeDtypeStruct((L, R, N), F32),
        compiler_params=_params(("arbitrary", "arbitrary")),
        name="modulation",
    )(cc, w_mod, b_mod.reshape(L, 1, N))


def _rms_modulate(x, g_ref, m_ref=None, shift_row=0, scale_row=1):
    y = x * lax.rsqrt(jnp.mean(x * x, axis=-1, keepdims=True) + RMS_EPS) * g_ref[...]
    if m_ref is not None:
        y = y * (1.0 + m_ref[0, scale_row:scale_row + 1, :]) + m_ref[0, shift_row:shift_row + 1, :]
    return y


def _norm_kernel(x_ref, g_ref, m_ref, h_ref):
    h_ref[0] = _rms_modulate(x_ref[0], g_ref, m_ref).astype(h_ref.dtype)


def _norm(x, g, mod):
    B, T, D = x.shape
    tt = min(T, 512)
    per_sample = mod.shape[0] > 1
    rows = pl.BlockSpec((1, tt, D), lambda b, t: (b, t, 0))
    return pl.pallas_call(
        _norm_kernel,
        grid=(B, T // tt),
        in_specs=[rows, pl.BlockSpec((1, D), lambda b, t: (0, 0)),
                  pl.BlockSpec((1, N_MOD, D), (lambda b, t: (b, 0, 0)) if per_sample else (lambda b, t: (0, 0, 0)))],
        out_specs=rows, out_shape=jax.ShapeDtypeStruct((B, T, D), BF16),
        compiler_params=_params(("arbitrary", "arbitrary")),
        name="rmsnorm",
    )(x, g.reshape(1, D), mod)


def _proj_kernel(*refs, n_side):
    a_ref, w_ref = refs[:2]
    side_in, o_ref, side_out = refs[2:2 + n_side], refs[2 + n_side], refs[3 + n_side:]
    o_ref[...] = jnp.dot(a_ref[...], w_ref[...], preferred_element_type=F32).astype(o_ref.dtype)
    for src, dst in zip(side_in, side_out):
        dst[...] = src[...].astype(dst.dtype)


def _proj(a, w, out_dtype, tn, side=(), layer=0):
    B, T, K = a.shape
    N = w.shape[1]
    M = B * T
    tm = min(M, 512)
    n_i = M // tm
    steps = (N // tn) * n_i
    flat = [s.reshape(s.shape[0], steps, -1, s.shape[-1]) for s in side]
    res = pl.pallas_call(
        functools.partial(_proj_kernel, n_side=len(side)),
        grid=(N // tn, n_i),
        in_specs=[pl.BlockSpec((tm, K), lambda j, i: (i, 0)), pl.BlockSpec((K, tn), lambda j, i: (0, j))]
                 + [pl.BlockSpec((1, 1) + f.shape[2:], lambda j, i: (layer, j * n_i + i, 0, 0)) for f in flat],
        out_specs=[pl.BlockSpec((tm, tn), lambda j, i: (i, j))]
                  + [pl.BlockSpec((1, 1) + f.shape[2:], lambda j, i: (0, j * n_i + i, 0, 0)) for f in flat],
        out_shape=[jax.ShapeDtypeStruct((M, N), out_dtype)]
                  + [jax.ShapeDtypeStruct((1,) + f.shape[1:], BF16) for f in flat],
        compiler_params=_params(("arbitrary", "arbitrary")),
        name="proj",
    )(a.reshape(M, K), w, *flat)
    out = res[0].reshape(B, T, N)
    return (out, *[r.reshape((1,) + s.shape[1:]) for r, s in zip(res[1:], side)]) if side else out


def _oproj_kernel(*refs, n_in, n_experts):
    a_refs, w_refs = refs[:n_in], refs[n_in:2 * n_in]
    x_ref, m_ref, g_ref, wr_ref, o_ref, h_ref, lg_ref = refs[2 * n_in:]
    tm = x_ref.shape[1]
    sub = min(tm, OPROJ_SUB_ROWS)
    for s in range(tm // sub):
        rows = slice(s * sub, (s + 1) * sub)
        acc = jnp.dot(a_refs[0][0, rows, :], w_refs[0][...], preferred_element_type=F32)
        for a_ref, w_ref in zip(a_refs[1:], w_refs[1:]):
            acc = acc + jnp.dot(a_ref[0, rows, :], w_ref[...], preferred_element_type=F32)
        x_new = x_ref[0, rows, :] + m_ref[0, 2:3, :] * acc
        o_ref[0, rows, :] = x_new
        y = _rms_modulate(x_new, g_ref, m_ref, shift_row=3, scale_row=4)
        y_hi = y.astype(BF16)
        h_ref[0, rows, :] = y_hi
        y_lo = (y - y_hi.astype(F32)).astype(BF16)
        parts = (jnp.dot(y_hi, wr_ref[...], preferred_element_type=F32)
                 + jnp.dot(y_lo, wr_ref[...], preferred_element_type=F32))
        lg_ref[0, :, rows] = jnp.transpose(parts + pltpu.roll(parts, LANES - n_experts, 1))


def _oproj(ys, ws, x, mod, g_ffn, router_wt):
    B, T, D = x.shape
    E = router_wt.shape[0]
    tm = min(T, 512)
    per_sample = mod.shape[0] > 1
    w_r_hi = router_wt.T.astype(BF16)
    w_r_lo = (router_wt.T - w_r_hi.astype(F32)).astype(BF16)
    w_r_split = jnp.pad(jnp.concatenate([w_r_hi, w_r_lo], axis=1), ((0, 0), (0, LANES - 2 * E)))
    rows = lambda n: pl.BlockSpec((1, tm, n), lambda b, t: (b, t, 0))
    in_specs = [rows(y.shape[2]) for y in ys]
    in_specs += [pl.BlockSpec(w.shape, lambda b, t: (0, 0)) for w in ws]
    in_specs += [rows(D),
                 pl.BlockSpec((1, N_MOD, D), (lambda b, t: (b, 0, 0)) if per_sample else (lambda b, t: (0, 0, 0))),
                 pl.BlockSpec((1, D), lambda b, t: (0, 0)), pl.BlockSpec((D, LANES), lambda b, t: (0, 0))]
    x_new, h, logits = pl.pallas_call(
        functools.partial(_oproj_kernel, n_in=len(ys), n_experts=E),
        grid=(B, T // tm), in_specs=in_specs,
        out_specs=[rows(D), rows(D), pl.BlockSpec((1, LANES, tm), lambda b, t: (b, 0, t))],
        out_shape=[jax.ShapeDtypeStruct((B, T, D), F32), jax.ShapeDtypeStruct((B, T, D), BF16),
                   jax.ShapeDtypeStruct((B, LANES, T), F32)],
        compiler_params=_params(("arbitrary", "arbitrary"), vmem_mib=56),
        name="oproj",
    )(*ys, *ws, x, mod, g_ffn.reshape(1, D), w_r_split)
    return x_new, h, logits


def _log_sigmoid(z):
    return jnp.minimum(z, 0.0) - jnp.log(1.0 + jnp.exp(-jnp.abs(z)))


def _gla_kernel(*refs, n_chunks, use_rope, has_state):
    it = iter(refs)
    q_ref, k_ref, v_ref, g_ref, za_ref = (next(it) for _ in range(5))
    cos_ref, sin_ref = (next(it), next(it)) if use_rope else (None, None)
    aup_ref, ab_ref, gg_ref = next(it), next(it), next(it)
    s0_ref = next(it) if has_state else None
    y_ref, sfin_ref = next(it), next(it)
    qd_scr, ki_scr, ke_scr, kv_scr, dec_scr, sall_scr, st_scr = (next(it) for _ in range(7))
    C, DK = GLA_CHUNK, GLA_DK
    T = n_chunks * C
    RB = min(T, GLA_PREP_ROWS)
    cpb = RB // C

    def increments(blk):
        for i in range(cpb):
            c = blk * cpb + i
            rows = pl.ds(_aligned(c * C, C), C)
            kv_scr[c] = lax.dot_general(v_ref[0, rows, :].astype(BF16), ke_scr[c], _TN, preferred_element_type=F32)

    def prepare(blk):
        r0 = _aligned(blk * RB, RB)
        rows = pl.ds(r0, RB)
        z = jnp.dot(za_ref[0, rows, :].astype(BF16), aup_ref[...], preferred_element_type=F32) + ab_ref[...]
        la = _log_sigmoid(z) * (1.0 / GLA_TAU)
        in_chunk = lax.broadcasted_iota(jnp.int32, la.shape, 0) % C
        prefix = la
        step = 1
        while step < C:
            prefix = prefix + jnp.where(in_chunk >= step, pltpu.roll(prefix, step, 0), 0.0)
            step *= 2
        q = q_ref[0, rows, :] * (GLA_DK ** -0.5)
        k = k_ref[0, rows, :]
        if use_rope:
            cos, sin = cos_ref[rows, :], sin_ref[rows, :]
            even = (lax.broadcasted_iota(jnp.int32, q.shape, 1) % 2) == 0

            def rope(t):
                swapped = jnp.where(even, pltpu.roll(t, LANES - 1, 1), pltpu.roll(t, 1, 1))
                return t * cos + swapped * sin
            q, k = rope(q), rope(k)
        split = lambda t: t.reshape(cpb, C, t.shape[-1])
        la, prefix, q, k = split(la), split(prefix), split(q), split(k)
        tot = prefix[:, C - 1:C, :]
        b_f = prefix[..., :DK]
        b_b = tot[..., DK:] - prefix[..., DK:] + la[..., DK:]
        dec = jnp.exp(tot)
        c0 = _aligned(blk * cpb, cpb)
        qd_scr[pl.ds(c0, cpb)] = jnp.concatenate([q * jnp.exp(b_f), q * jnp.exp(b_b)], axis=2).astype(BF16)
        ki_f, ki_b = k * jnp.exp(-b_f), k * jnp.exp(-b_b)
        ki_scr[0, pl.ds(c0, cpb)] = ki_f.astype(BF16)
        ki_scr[1, pl.ds(c0, cpb)] = ki_b.astype(BF16)
        ke_scr[pl.ds(c0, cpb)] = jnp.concatenate([ki_f * dec[..., :DK], ki_b * dec[..., DK:]], axis=2).astype(BF16)
        dec_scr[pl.ds(c0, cpb)] = jnp.broadcast_to(dec, (cpb,) + dec_scr.shape[1:])

    def prepare_and_increment(blk, carry):
        increments(blk - 1)
        prepare(blk)
        return carry

    n_blocks = T // RB
    prepare(0)
    lax.fori_loop(1, n_blocks, prepare_and_increment, 0)
    increments(n_blocks - 1)

    st_scr[...] = s0_ref[0, 0] if has_state else jnp.zeros(st_scr.shape, F32)

    def recur(i, carry):
        cf, cb = i, n_chunks - 1 - i
        st = st_scr[...]
        sall_scr[cf, :, :DK] = st[:, :DK].astype(BF16)
        sall_scr[cb, :, DK:] = st[:, DK:].astype(BF16)
        st_scr[:, :DK] = st[:, :DK] * dec_scr[cf, 0:1, :DK] + kv_scr[cf, :, :DK]
        st_scr[:, DK:] = st[:, DK:] * dec_scr[cb, 0:1, DK:] + kv_scr[cb, :, DK:]
        return carry

    lax.fori_loop(0, n_chunks, recur, 0)
    sfin_ref[0, 0] = st_scr[...]

    G = min(n_chunks, GLA_EMIT_CHUNKS)
    GR = G * C
    row = lax.broadcasted_iota(jnp.int32, (GR, GR), 0)
    col = lax.broadcasted_iota(jnp.int32, (GR, GR), 1)
    same_chunk = (row // C) == (col // C)
    keep_f, keep_b = same_chunk & (row >= col), same_chunk & (row <= col)

    def emit(grp, carry):
        c0 = pl.multiple_of(grp * G, G)
        rows = pl.ds(pl.multiple_of(grp * GR, GR), GR)
        qd = qd_scr[pl.ds(c0, G)].reshape(GR, 2 * DK)
        ki_f = ki_scr[0, pl.ds(c0, G)].reshape(GR, DK)
        ki_b = ki_scr[1, pl.ds(c0, G)].reshape(GR, DK)
        s = (jnp.where(keep_f, lax.dot_general(qd[:, :DK], ki_f, _NT, preferred_element_type=F32), 0.0)
             + jnp.where(keep_b, lax.dot_general(qd[:, DK:], ki_b, _NT, preferred_element_type=F32), 0.0))
        carried = [lax.dot_general(qd[i * C:(i + 1) * C], sall_scr[c0 + i], _NT, preferred_element_type=F32)
                   for i in range(G)]
        o = (jnp.dot(s.astype(BF16), v_ref[0, rows, :].astype(BF16), preferred_element_type=F32)
             + jnp.concatenate(carried, axis=0))
        o = o * lax.rsqrt(jnp.mean(o * o, axis=-1, keepdims=True) + RMS_EPS) * gg_ref[...]
        y_ref[0, rows, :] = (o * _silu(g_ref[0, rows, :])).astype(y_ref.dtype)
        return carry

    lax.fori_loop(0, n_chunks // G, emit, 0, unroll=min(4, n_chunks // G))


def _gla(p_main, p_za, aup, ab, gg, rope_tabs, state):
    B, T, _ = p_main.shape
    H, DK, DV = GLA_HEADS, GLA_DK, GLA_DV
    use_rope, has_state = rope_tabs is not None, state is not None
    n_chunks = T // GLA_CHUNK
    in_specs = [pl.BlockSpec((1, T, DK), lambda b, h: (b, 0, h)),
                pl.BlockSpec((1, T, DK), lambda b, h: (b, 0, H + h)),
                pl.BlockSpec((1, T, DV), lambda b, h: (b, 0, H + h)),
                pl.BlockSpec((1, T, DV), lambda b, h: (b, 0, 2 * H + h)),
                pl.BlockSpec((1, T, LANES), lambda b, h: (b, 0, 0))]
    args = [p_main, p_main, p_main, p_main, p_za]
    if use_rope:
        in_specs += [pl.BlockSpec((T, DK), lambda b, h: (0, 0))] * 2
        args += list(rope_tabs)
    in_specs += [pl.BlockSpec((LANES, 2 * DK), lambda b, h: (0, h)),
                 pl.BlockSpec((1, 2 * DK), lambda b, h: (0, h)),
                 pl.BlockSpec((1, DV), lambda b, h: (0, 0))]
    args += [aup, ab, gg]
    st_spec = pl.BlockSpec((1, 1, DV, 2 * DK), lambda b, h: (b, h, 0, 0))
    if has_state:
        in_specs.append(st_spec)
        args.append(state)
    return pl.pallas_call(
        functools.partial(_gla_kernel, n_chunks=n_chunks, use_rope=use_rope, has_state=has_state),
        grid=(B, H), in_specs=in_specs,
        out_specs=[pl.BlockSpec((1, T, DV), lambda b, h: (b, 0, h)), st_spec],
        out_shape=[jax.ShapeDtypeStruct((B, T, H * DV), BF16), jax.ShapeDtypeStruct((B, H, DV, 2 * DK), F32)],
        scratch_shapes=[pltpu.VMEM((n_chunks, GLA_CHUNK, 2 * DK), BF16), pltpu.VMEM((2, n_chunks, GLA_CHUNK, DK), BF16),
                        pltpu.VMEM((n_chunks, GLA_CHUNK, 2 * DK), BF16),
                        pltpu.VMEM((n_chunks, DV, 2 * DK), F32), pltpu.VMEM((n_chunks, 8, 2 * DK), F32),
                        pltpu.VMEM((n_chunks, DV, 2 * DK), BF16), pltpu.VMEM((DV, 2 * DK), F32)],
        compiler_params=_params(("arbitrary", "arbitrary")),
        name="gla",
    )(*args)


def _gelu_tanh(z):
    return 0.5 * z * (1.0 + jnp.tanh(np.sqrt(2.0 / np.pi) * (z + 0.044715 * (z * z * z))))


def _lru_kernel(x_ref, gt_ref, cw_ref, cb_ref, wa_ref, ba_ref, wi_ref, bi_ref, lam_ref, h0f_ref, h0b_ref,
                y_ref, hf_ref, hb_ref, a_scr, bx_scr, hl_scr, ap_scr, *, T):
    L = T // LRU_SEGMENTS
    P = L + LRU_PITCH_PAD
    x = x_ref[0]
    row = lax.broadcasted_iota(jnp.int32, x.shape, 0)

    def from_earlier(v, d, fill):
        return jnp.where(row >= d, pltpu.roll(v, d, 0), fill)

    def from_later(v, d, fill):
        return jnp.where(row < T - d, pltpu.roll(v, T - d, 0), fill)

    xc = (cw_ref[0:1, :] * from_earlier(x, 1, 0.0) + cw_ref[1:2, :] * x
          + cw_ref[2:3, :] * from_later(x, 1, 0.0) + cw_ref[3:4, :] * from_later(x, 2, 0.0) + cb_ref[...])
    xcb = xc.astype(BF16)

    def sigmoid(z):
        return 0.5 * jnp.tanh(0.5 * z) + 0.5

    for d in range(2):
        r = sigmoid(jnp.dot(xcb, wa_ref[d, 0], preferred_element_type=F32) + ba_ref[d])
        gate_i = sigmoid(jnp.dot(xcb, wi_ref[d, 0], preferred_element_type=F32) + bi_ref[d])
        neg_lam = -lam_ref[d]
        softplus = jnp.maximum(neg_lam, 0.0) + jnp.log(1.0 + jnp.exp(-jnp.abs(neg_lam)))
        a = jnp.exp((-LRU_C) * r * softplus)
        u = 1.0 - a * a
        bx = jnp.where(u > 0.0, u * lax.rsqrt(u), 0.0) * (gate_i * xc)
        for s in range(LRU_SEGMENTS):
            a_scr[d, s * P:s * P + L, :] = a[s * L:(s + 1) * L]
            bx_scr[d, s * P:s * P + L, :] = bx[s * L:(s + 1) * L]

    def step(i, carry):
        h_f, p_f, h_b, p_b = carry
        at_f = pl.ds(i, LRU_SEGMENTS, stride=P)
        at_b = pl.ds(L - 1 - i, LRU_SEGMENTS, stride=P)
        a_f, a_b = a_scr[0, at_f, :], a_scr[1, at_b, :]
        h_f = a_f * h_f + bx_scr[0, at_f, :]
        h_b = a_b * h_b + bx_scr[1, at_b, :]
        p_f, p_b = a_f * p_f, a_b * p_b
        hl_scr[0, at_f, :] = h_f
        hl_scr[1, at_b, :] = h_b
        ap_scr[0, at_f, :] = p_f
        ap_scr[1, at_b, :] = p_b
        return h_f, p_f, h_b, p_b

    zeros, ones = jnp.zeros((LRU_SEGMENTS, x.shape[1]), F32), jnp.ones((LRU_SEGMENTS, x.shape[1]), F32)
    h_f, p_f, h_b, p_b = lax.fori_loop(0, L, step, (zeros, ones, zeros, ones), unroll=8)

    enter_f, enter_b = [h0f_ref[0]], [h0b_ref[0]]
    for s in range(LRU_SEGMENTS):
        enter_f.append(h_f[s:s + 1] + p_f[s:s + 1] * enter_f[-1])
        t = LRU_SEGMENTS - 1 - s
        enter_b.append(h_b[t:t + 1] + p_b[t:t + 1] * enter_b[-1])
    hf_ref[0] = enter_f[-1]
    hb_ref[0] = enter_b[-1]
    for s in range(LRU_SEGMENTS):
        seg = slice(s * P, s * P + L)
        h = (hl_scr[0, seg, :] + ap_scr[0, seg, :] * enter_f[s]
             + hl_scr[1, seg, :] + ap_scr[1, seg, :] * enter_b[LRU_SEGMENTS - 1 - s])
        y_ref[0, s * L:(s + 1) * L, :] = (h * _gelu_tanh(gt_ref[0, s * L:(s + 1) * L, :])).astype(y_ref.dtype)


def _lru(p_main, cw, cb, wa, ba, wi, bi, lam, states):
    B, T, _ = p_main.shape
    W, NB, BD = LRU_WIDTH, LRU_BLOCKS, LRU_BLOCK_DIM
    x_blk0 = (2 * GLA_KEY_WIDTH + 2 * GLA_VAL_WIDTH) // BD
    vec = lambda n: pl.BlockSpec((n, 1, BD), lambda b, j: (0, 0, j))
    mat = pl.BlockSpec((2, 1, BD, BD), lambda b, j: (0, j, 0, 0))
    st_spec = pl.BlockSpec((1, 1, BD), lambda b, j: (b, 0, j))
    st_shape = jax.ShapeDtypeStruct((B, 1, W), F32)
    y, hf, hb = pl.pallas_call(
        functools.partial(_lru_kernel, T=T),
        grid=(B, NB),
        in_specs=[pl.BlockSpec((1, T, BD), lambda b, j: (b, 0, x_blk0 + j)),
                  pl.BlockSpec((1, T, BD), lambda b, j: (b, 0, x_blk0 + NB + j)),
                  pl.BlockSpec((LRU_CONV, BD), lambda b, j: (0, j)),
                  pl.BlockSpec((1, BD), lambda b, j: (0, j)),
                  mat, vec(2), mat, vec(2), vec(2), st_spec, st_spec],
        out_specs=[pl.BlockSpec((1, T, BD), lambda b, j: (b, 0, j)), st_spec, st_spec],
        out_shape=[jax.ShapeDtypeStruct((B, T, W), BF16), st_shape, st_shape],
        scratch_shapes=[pltpu.VMEM((2, LRU_SEGMENTS * (T // LRU_SEGMENTS + LRU_PITCH_PAD), BD), F32)] * 4,
        compiler_params=_params(("arbitrary", "arbitrary")),
        name="rglru",
    )(p_main, p_main, cw, cb.reshape(1, W), wa, ba.reshape(2, 1, W), wi, bi.reshape(2, 1, W),
      lam.reshape(2, 1, W), *states)
    return y, (hf, hb)


def _na_tile(i, n_tiles, rows):
    r0 = i * NA_QROWS
    ks = min(max(r0 - NA_KH // 2, 0), rows - NA_KROWS)
    cls = 0 if i == 0 else (2 if i == n_tiles - 1 else 1)
    return r0, ks, cls


def _na_bias_tables(rpb, rows):
    H = rpb.shape[0]
    qc = np.arange(GRID_W)[:, None]
    kc = np.arange(GRID_W)[None, :]
    col_start = np.clip(qc - NA_KW // 2, 0, GRID_W - NA_KW)
    col_valid = (kc >= col_start) & (kc < col_start + NA_KW)
    dc = np.clip(kc - qc, -(NA_KW - 1), NA_KW - 1) + NA_KW - 1
    pick = (dc[None] == np.arange(2 * NA_KW - 1)[:, None, None]).astype(np.float32)
    band = jnp.einsum('hrd,dqk->hrqk', rpb.astype(F32), jnp.asarray(pick), precision=HIGHEST)
    band = jnp.where(col_valid, band, NEG_BIG)
    masked = jnp.full((H, GRID_W, GRID_W), NEG_BIG, F32)
    n_tiles = rows // NA_QROWS
    tabs = []
    for i in (0, 1, n_tiles - 1):
        r0, ks, _ = _na_tile(i, n_tiles, rows)
        per_qrow = []
        for qr in range(r0, r0 + NA_QROWS):
            row_start = min(max(qr - NA_KH // 2, 0), rows - NA_KH)
            blocks = [band[:, kr - qr + NA_KH - 1] if row_start <= kr < row_start + NA_KH else masked
                      for kr in range(ks, ks + NA_KROWS)]
            per_qrow.append(jnp.concatenate(blocks, axis=2))
        tabs.append(jnp.concatenate(per_qrow, axis=1))
    return jnp.stack(tabs, axis=1) * LOG2_E


def _na_kernel(*refs, rows, need_ctx):
    q_ref, k_ref, v_ref, qc_ref, kc_ref, vc_ref, tab_ref = refs[:7]
    y_ref, vx_ref = refs[7], refs[-1]
    HD = NA_HEAD_DIM
    scale = HD ** -0.5 * LOG2_E
    vx_ref[...] = jnp.concatenate([v_ref[0], jnp.ones(v_ref.shape[1:], BF16)], axis=1)
    kc = kc_ref[0]
    vcx = jnp.concatenate([vc_ref[0], jnp.ones(vc_ref.shape[1:], BF16)], axis=1)
    n_tiles = rows // NA_QROWS
    nq, nk = NA_QROWS * GRID_W, NA_KROWS * GRID_W
    for i in range(n_tiles):
        r0, ks, cls = _na_tile(i, n_tiles, rows)
        q = q_ref[0, r0 * GRID_W:r0 * GRID_W + nq, :]
        kt = k_ref[0, ks * GRID_W:ks * GRID_W + nk, :]
        s_loc = lax.dot_general(q, kt, _NT, preferred_element_type=F32) * scale + tab_ref[0, cls]
        s_ctx = lax.dot_general(q, kc, _NT, preferred_element_type=F32) * scale
        m = jnp.maximum(jnp.max(s_loc, axis=-1, keepdims=True), jnp.max(s_ctx, axis=-1, keepdims=True))
        o = (jnp.dot(jnp.exp2(s_loc - m).astype(BF16), vx_ref[ks * GRID_W:ks * GRID_W + nk, :], preferred_element_type=F32)
             + jnp.dot(jnp.exp2(s_ctx - m).astype(BF16), vcx, preferred_element_type=F32))
        y_ref[0, r0 * GRID_W:r0 * GRID_W + nq, :] = (o[:, :HD] / o[:, HD:]).astype(y_ref.dtype)
    if need_ctx:
        yc_ref = refs[8]
        s = lax.dot_general(qc_ref[0], kc, _NT, preferred_element_type=F32) * scale
        p = jnp.exp2(s - jnp.max(s, axis=-1, keepdims=True))
        o = jnp.dot(p.astype(BF16), vcx, preferred_element_type=F32)
        yc_ref[0] = (o[:, :HD] / o[:, HD:]).astype(yc_ref.dtype)


def _na(qkv_lat, qkv_ctx, tabs, need_ctx):
    B, T, _ = qkv_lat.shape
    Tc = qkv_ctx.shape[1]
    H, HD = NA_HEADS, NA_HEAD_DIM
    blk = lambda t, off: pl.BlockSpec((1, t, HD), lambda h, b: (b, 0, off + h))
    out_specs = [pl.BlockSpec((1, T, HD), lambda h, b: (b, 0, h))]
    out_shape = [jax.ShapeDtypeStruct((B, T, H * HD), BF16)]
    if need_ctx:
        out_specs.append(pl.BlockSpec((1, Tc, HD), lambda h, b: (b, 0, h)))
        out_shape.append(jax.ShapeDtypeStruct((B, Tc, H * HD), BF16))
    res = pl.pallas_call(
        functools.partial(_na_kernel, rows=T // GRID_W, need_ctx=need_ctx),
        grid=(H, B),
        in_specs=[blk(T, 0), blk(T, H), blk(T, 2 * H), blk(Tc, 0), blk(Tc, H), blk(Tc, 2 * H),
                  pl.BlockSpec((1,) + tabs.shape[1:], lambda h, b: (h, 0, 0, 0))],
        out_specs=out_specs, out_shape=out_shape,
        scratch_shapes=[pltpu.VMEM((T, 2 * HD), BF16)],
        compiler_params=_params(("arbitrary", "arbitrary")),
        name="natten",
    )(qkv_lat, qkv_lat, qkv_lat, qkv_ctx, qkv_ctx, qkv_ctx, tabs)
    return (res[1] if need_ctx else None), res[0]


def _route_kernel(lg_ref, pos_ref, aff_ref, lo_ref, *, cap):
    lg = lg_ref[...]
    bb, n_exp, N = lg.shape
    ex = jnp.exp(lg - jnp.max(lg, axis=1, keepdims=True))
    aff = ex / jnp.sum(ex, axis=1, keepdims=True)
    aff_ref[...] = aff
    E = bb * n_exp
    aff = aff.reshape(E, N)
    bits = pltpu.bitcast(aff, jnp.int32)
    thr = jnp.zeros((E, 1), jnp.int32)
    for bit in range(29, -1, -1):
        cand = thr | (1 << bit)
        cnt = jnp.sum(jnp.where(bits >= cand, 1.0, 0.0), axis=1, keepdims=True)
        thr = jnp.where(cnt >= cap, cand, thr)
    above = bits > thr
    tied = bits == thr
    need = cap - jnp.sum(jnp.where(above, 1.0, 0.0), axis=1, keepdims=True)
    pc = min(PREFIX_CHUNK, N)
    before = jnp.where(lax.broadcasted_iota(jnp.int32, (pc, pc), 0) < lax.broadcasted_iota(jnp.int32, (pc, pc), 1),
                       1.0, 0.0).astype(BF16)
    tied_seen = jnp.zeros((E, 1), F32)
    sel_seen = jnp.zeros((E, 1), F32)
    tile_lane = lax.broadcasted_iota(jnp.int32, (E, LANES), 1)
    lo = jnp.where(tile_lane == N // pc, float(cap), 0.0)
    for c in range(N // pc):
        sl = slice(c * pc, (c + 1) * pc)
        lo = jnp.where(tile_lane == c, sel_seen, lo)
        tied_c = jnp.where(tied[:, sl], 1.0, 0.0)
        tied_before = jnp.dot(tied_c.astype(BF16), before, preferred_element_type=F32) + tied_seen
        sel_c = jnp.where(above[:, sl], 1.0, jnp.where(tied_before < need, tied_c, 0.0))
        sel_before = jnp.dot(sel_c.astype(BF16), before, preferred_element_type=F32) + sel_seen
        pos_ref[:, :, sl] = jnp.where(sel_c > 0.0, sel_before, -1.0).reshape(bb, n_exp, pc)
        tied_seen = tied_seen + jnp.sum(tied_c, axis=1, keepdims=True)
        sel_seen = sel_seen + jnp.sum(sel_c, axis=1, keepdims=True)
    lo_ref[...] = lo.astype(jnp.int32).reshape(bb, n_exp, LANES)


def _route(logits_t, cap):
    B, _, N = logits_t.shape
    E = N_EXPERTS
    bb = ROUTE_SAMPLES if B % ROUTE_SAMPLES == 0 else 1
    spec = pl.BlockSpec((bb, E, N), lambda b: (b, 0, 0))
    lo_spec = pl.BlockSpec((bb, E, LANES), lambda b: (b, 0, 0))
    pos, aff, lo = pl.pallas_call(
        functools.partial(_route_kernel, cap=cap),
        grid=(B // bb,), in_specs=[spec], out_specs=[spec, spec, lo_spec],
        out_shape=[jax.ShapeDtypeStruct((B, E, N), F32)] * 2 + [jax.ShapeDtypeStruct((B, E, LANES), jnp.int32)],
        compiler_params=_params(("arbitrary",)),
        name="route",
    )(logits_t)
    n_bounds = N // min(PREFIX_CHUNK, N) + 1
    return pos, aff, lo[:, :, :n_bounds].reshape(-1)


def _window_count(lo_ref, base, experts, tile, n_bounds, win):
    starts, n_win = [], 0
    for e in experts:
        first = lo_ref[(base + e) * n_bounds + tile]
        end = lo_ref[(base + e) * n_bounds + tile + 1]
        start = (first // SLOT_ALIGN) * SLOT_ALIGN
        starts.append(start)
        n_win = jnp.maximum(n_win, (end - start + win - 1) // win)
    return starts, n_win


def _gather_kernel(lo_ref, pos_ref, h_ref, xs_ref, *, cap, win, tn):
    EG, N = pos_ref.shape[1], pos_ref.shape[2]
    n_bounds = N // tn + 1
    base = pl.program_id(0) * (EG * pl.num_programs(1)) + pl.program_id(1) * EG
    xs_ref[...] = jnp.zeros(xs_ref.shape, xs_ref.dtype)
    slot_iota = lax.broadcasted_iota(jnp.int32, (win, tn), 0)
    for j in range(N // tn):
        h_tile = h_ref[0, j * tn:(j + 1) * tn, :]
        starts, n_win = _window_count(lo_ref, base, range(EG), j, n_bounds, win)

        def window(w, carry):
            pieces, offs = [], []
            for e in range(EG):
                first = starts[e] + w * win
                off = jnp.minimum(first, cap - win)
                prow = pos_ref[0, e:e + 1, j * tn:(j + 1) * tn]
                hit = ((slot_iota + off).astype(F32) == prow) & (prow >= first.astype(F32))
                pieces.append(jnp.where(hit, 1.0, 0.0).astype(BF16))
                offs.append(pl.multiple_of(off, SLOT_ALIGN))
            got = jnp.dot(jnp.concatenate(pieces, axis=0), h_tile, preferred_element_type=F32)
            for e in range(EG):
                xs_ref[0, e, pl.ds(offs[e], win), :] += got[e * win:(e + 1) * win].astype(xs_ref.dtype)
            return carry

        lax.fori_loop(0, n_win, window, 0)


def _gather(lo, pos, h, cap):
    B, E, N = pos.shape
    D = h.shape[2]
    EG = 8
    tn, win = min(PREFIX_CHUNK, N), min(SLOT_WINDOW, cap)
    return pl.pallas_call(
        functools.partial(_gather_kernel, cap=cap, win=win, tn=tn),
        grid_spec=pltpu.PrefetchScalarGridSpec(
            num_scalar_prefetch=1, grid=(B, E // EG),
            in_specs=[pl.BlockSpec((1, EG, N), lambda b, g, lo: (b, g, 0)),
                      pl.BlockSpec((1, N, D), lambda b, g, lo: (b, 0, 0))],
            out_specs=pl.BlockSpec((1, EG, cap, D), lambda b, g, lo: (b, g, 0, 0))),
        out_shape=jax.ShapeDtypeStruct((B, E, cap, D), BF16),
        compiler_params=_params(("arbitrary", "arbitrary")),
        name="moe_gather",
    )(lo, pos, h)


def _ffn_kernel(*refs, n_main, has_extra):
    if has_extra:
        xs_ref, xe_ref, wg_ref, wu_ref, wd_ref, y_ref, ye_ref = refs
    else:
        xs_ref, wg_ref, wu_ref, wd_ref, y_ref = refs

    def swiglu(src_ref, dst_ref):
        bb, _, cap, D = src_ref.shape
        xs = src_ref[:, 0].reshape(bb * cap, D)
        hid = (_silu(jnp.dot(xs, wg_ref[0, 0], preferred_element_type=F32))
               * jnp.dot(xs, wu_ref[0, 0], preferred_element_type=F32))
        y = jnp.dot(hid.astype(BF16), wd_ref[0, 0], preferred_element_type=F32)
        dst_ref[:, 0] = y.reshape(bb, cap, D).astype(dst_ref.dtype)

    if not has_extra:
        swiglu(xs_ref, y_ref)
        return
    step = pl.program_id(1)

    @pl.when(step < n_main)
    def _():
        swiglu(xs_ref, y_ref)

    @pl.when(step == n_main)
    def _():
        swiglu(xe_ref, ye_ref)


def _ffn(xs, wg, wu, wd, layer, extra=None):
    B, E, cap, D = xs.shape
    F = wg.shape[3]
    bb = max(1, min(B, FFN_ROWS // cap))
    n_main = B // bb
    has_extra = extra is not None
    tok = pl.BlockSpec((bb, 1, cap, D), lambda e, b: (jnp.minimum(b, n_main - 1), e, 0, 0))
    weights = [pl.BlockSpec((1, 1, D, F), lambda e, b: (layer, e, 0, 0)),
               pl.BlockSpec((1, 1, D, F), lambda e, b: (layer, e, 0, 0)),
               pl.BlockSpec((1, 1, F, D), lambda e, b: (layer, e, 0, 0))]
    in_specs, out_specs = [tok], [tok]
    out_shape = [jax.ShapeDtypeStruct((B, E, cap, D), BF16)]
    args = [xs]
    if has_extra:
        tok_e = pl.BlockSpec((B, 1) + extra.shape[2:], lambda e, b: (0, e, 0, 0))
        in_specs.append(tok_e)
        out_specs.append(tok_e)
        out_shape.append(jax.ShapeDtypeStruct(extra.shape, BF16))
        args.append(extra)
    res = pl.pallas_call(
        functools.partial(_ffn_kernel, n_main=n_main, has_extra=has_extra),
        grid=(E, n_main + int(has_extra)),
        in_specs=in_specs + weights, out_specs=out_specs, out_shape=out_shape,
        compiler_params=_params(("arbitrary", "arbitrary"), vmem_mib=56 if has_extra else 48),
        name="moe_ffn",
    )(*args, wg, wu, wd)
    return res if has_extra else res[0]


def _combine_kernel(*refs, cap, win, keep_x, next_modulated):
    it = iter(refs)
    lo_ref, pos_ref, aff_ref, y_ref, x_ref, m_ref, gn_ref = (next(it) for _ in range(7))
    mn_ref = next(it) if next_modulated else None
    o_ref = next(it) if keep_x else None
    hn_ref, ystack_ref, acc_ref = next(it), next(it), next(it)
    E = y_ref.shape[1]
    n_bounds = pl.num_programs(1) + 1
    j = pl.program_id(1)
    starts, n_win = _window_count(lo_ref, pl.program_id(0) * E, range(E), j, n_bounds, win)
    col = lax.broadcasted_iota(jnp.int32, (E, E * win), 1)
    spread = jnp.where(col // win == lax.broadcasted_iota(jnp.int32, (E, E * win), 0), 1.0, 0.0).astype(BF16)
    slot = lax.dot_general(pos_ref[0].astype(BF16), spread, _TN, preferred_element_type=F32)
    gate = lax.dot_general(aff_ref[0].astype(BF16), spread, _TN, preferred_element_type=F32)
    col1 = lax.broadcasted_iota(jnp.int32, (1, E * win), 1)
    col_expert, col_in_win = col1 // win, col1 % win

    def window(w):
        first_col = jnp.zeros((1, E * win), jnp.int32)
        want_col = jnp.zeros((1, E * win), jnp.int32)
        for e in range(E):
            first = starts[e] + w * win
            off = jnp.minimum(first, cap - win)
            ystack_ref[e * win:(e + 1) * win, :] = y_ref[0, e, pl.ds(pl.multiple_of(off, SLOT_ALIGN), win), :]
            first_col = jnp.where(col_expert == e, first, first_col)
            want_col = jnp.where(col_expert == e, off + col_in_win, want_col)
        hit = (slot == want_col.astype(F32)) & (slot >= first_col.astype(F32))
        return jnp.dot(jnp.where(hit, gate, 0.0).astype(BF16), ystack_ref[...], preferred_element_type=F32)

    def more(w, carry):
        acc_ref[...] += window(w)
        return carry

    acc_ref[...] = window(0)
    lax.fori_loop(1, n_win, more, 0)
    x_new = x_ref[0] + m_ref[0, 5:6, :] * acc_ref[...]
    if keep_x:
        o_ref[0] = x_new
    hn_ref[0] = _rms_modulate(x_new, gn_ref, mn_ref).astype(hn_ref.dtype)


def _combine(lo, pos, aff, y, x, mod, g_next, mod_next):
    B, E, cap, D = y.shape
    N = x.shape[1]
    tn, win = min(PREFIX_CHUNK, N), min(SLOT_WINDOW, cap)
    keep_x = next_modulated = mod_next is not None
    tok = pl.BlockSpec((1, E, tn), lambda b, t, lo: (b, 0, t))
    rows = pl.BlockSpec((1, tn, D), lambda b, t, lo: (b, t, 0))
    mod_spec = lambda m: pl.BlockSpec((1, N_MOD, D),
                                      (lambda b, t, lo: (b, 0, 0)) if m.shape[0] > 1 else (lambda b, t, lo: (0, 0, 0)))
    in_specs = [tok, tok, pl.BlockSpec((1, E, cap, D), lambda b, t, lo: (b, 0, 0, 0)), rows, mod_spec(mod),
                pl.BlockSpec((1, D), lambda b, t, lo: (0, 0))]
    args = [lo, pos, aff, y, x, mod, g_next.reshape(1, D)]
    out_specs, out_shape = [rows], [jax.ShapeDtypeStruct((B, N, D), BF16 if next_modulated else F32)]
    if next_modulated:
        in_specs.append(mod_spec(mod_next))
        args.append(mod_next)
        out_specs, out_shape = [rows] + out_specs, [jax.ShapeDtypeStruct((B, N, D), F32)] + out_shape
    res = pl.pallas_call(
        functools.partial(_combine_kernel, cap=cap, win=win, keep_x=keep_x, next_modulated=next_modulated),
        grid_spec=pltpu.PrefetchScalarGridSpec(
            num_scalar_prefetch=1, grid=(B, N // tn), in_specs=in_specs, out_specs=out_specs,
            scratch_shapes=[pltpu.VMEM((E * win, D), BF16), pltpu.VMEM((tn, D), F32)]),
        out_shape=out_shape,
        compiler_params=_params(("arbitrary", "arbitrary"), vmem_mib=56),
        name="moe_combine",
    )(*args)
    return (res[0], res[1]) if keep_x else (None, res[0])


def _moe(x, h, logits_t, mod, wg, wu, wd, layer, g_next, mod_next):
    N = x.shape[1]
    cap = EC_CAPACITY_FACTOR * N // N_EXPERTS
    assert cap <= 256, "slot ids must stay exact in bf16"
    pos, aff, lo = _route(logits_t, cap)
    y = _ffn(_gather(lo, pos, h, cap), wg, wu, wd, layer)
    return _combine(lo, pos, aff, y, x, mod, g_next, mod_next)


def _moe_pair(lat, ctx, wg, wu, wd, layer, g_next):
    routed = []
    for x, h, logits_t, _, _ in (lat, ctx):
        cap = EC_CAPACITY_FACTOR * x.shape[1] // N_EXPERTS
        assert cap <= 256, "slot ids must stay exact in bf16"
        pos, aff, lo = _route(logits_t, cap)
        routed.append((pos, aff, lo, _gather(lo, pos, h, cap)))
    ys = _ffn(routed[0][3], wg, wu, wd, layer, extra=routed[1][3])
    return [_combine(lo, pos, aff, y, x, mod, g_next, mod_next)
            for (pos, aff, lo, _), y, (x, _, _, mod, mod_next) in zip(routed, ys, (lat, ctx))]


def _rope_tables(n_tok):
    t = jnp.arange(n_tok)
    rows = (t // GRID_W).astype(F32)
    cols = (t % GRID_W).astype(F32)
    n_freq = GLA_DK // 4
    inv = ROPE_THETA ** (-jnp.arange(n_freq, dtype=F32) / n_freq)
    ang = jnp.concatenate([rows[:, None] * inv, cols[:, None] * inv], axis=-1)
    cos = jnp.repeat(jnp.cos(ang), 2, axis=-1)
    sin = jnp.repeat(jnp.sin(ang), 2, axis=-1) * jnp.tile(jnp.asarray([-1.0, 1.0], F32), GLA_DK // 2)
    return cos, sin


def _rec_mixer(h_ctx, h_lat, w_in, alpha_up, alpha_b, gla_g, conv_w, conv_b, w_a, b_a, w_i, b_i, lam, rope_tabs,
               side, layer):
    za0 = 2 * GLA_KEY_WIDTH + 2 * GLA_VAL_WIDTH
    za1 = za0 + 2 * GLA_RANK
    w_main = jnp.concatenate([w_in[:, :za0], w_in[:, za1:]], axis=1).astype(BF16)
    w_za = jnp.pad(w_in[:, za0:za1], ((0, 0), (0, LANES - 2 * GLA_RANK))).astype(BF16)
    up = alpha_up.reshape(2, GLA_RANK, GLA_HEADS, 1, GLA_DK)
    zero = jnp.zeros_like(up[0])
    aup = jnp.concatenate([jnp.concatenate([up[0], zero], axis=2), jnp.concatenate([zero, up[1]], axis=2)], axis=0)
    aup = jnp.pad(aup.reshape(2 * GLA_RANK, 2 * GLA_KEY_WIDTH), ((0, LANES - 2 * GLA_RANK), (0, 0))).astype(BF16)
    ab = jnp.transpose(alpha_b.reshape(2, GLA_HEADS, GLA_DK), (1, 0, 2)).reshape(1, 2 * GLA_KEY_WIDTH)
    gg = gla_g.reshape(1, GLA_DV)
    wa, wi = w_a.astype(BF16), w_i.astype(BF16)
    outs = []
    gla_state, lru_state = None, None
    for h, tabs in ((h_ctx, None), (h_lat, rope_tabs)):
        B = h.shape[0]
        if tabs is None:
            p_main = _proj(h, w_main, F32, tn=2560)
        else:
            p_main, *cast = _proj(h, w_main, F32, tn=2560, side=side, layer=layer)
        p_za = _proj(h, w_za, F32, tn=LANES)
        y_gla, gla_state = _gla(p_main, p_za, aup, ab, gg, tabs, gla_state)
        if lru_state is None:
            lru_state = (jnp.zeros((B, 1, LRU_WIDTH), F32),) * 2
        y_lru, lru_state = _lru(p_main, conv_w, conv_b, wa, b_a, wi, b_i, lam, lru_state)
        outs.append([y_gla, y_lru])
    return outs, cast


def kernel(x, c, ctx, c_ctx, w_mod, b_mod, norm_mix_g, norm_ffn_g, w_out, router_w, exp_w_gate, exp_w_up, exp_w_down, rec_w_in, gla_alpha_up, gla_alpha_b, gla_norm_g, lru_conv_w, lru_conv_b, lru_w_a, lru_b_a, lru_w_i, lru_b_i, lru_lambda, na_w_qkv, na_rpb, norm_f_g):
    B, T, D = x.shape
    pad_rows = (-(B + 1)) % 8
    cc = jnp.concatenate([c, c_ctx[None, :], jnp.zeros((pad_rows, D), F32)], axis=0)
    m_all = _modulation(cc, w_mod, b_mod)
    rope_tabs = _rope_tables(T)
    expert_w_f32 = (exp_w_gate, exp_w_up, exp_w_down)
    mods_lat = [m_all[l, :B].reshape(B, N_MOD, D) for l in range(DEPTH)]
    mods_ctx = [m_all[l, B:B + 1].reshape(1, N_MOD, D) for l in range(DEPTH)]
    h_lat = _norm(x, norm_mix_g[0], mods_lat[0])
    h_ctx = _norm(ctx, norm_mix_g[0], mods_ctx[0])
    for l in range(DEPTH):
        last = l == DEPTH - 1
        i = l // 2
        m_lat, m_ctx = mods_lat[l], mods_ctx[l]
        w_o = w_out[l].astype(BF16)
        if l % 2 == 0:
            (ys_ctx, ys_lat), expert_w = _rec_mixer(
                h_ctx, h_lat, rec_w_in[i], gla_alpha_up[i], gla_alpha_b[i], gla_norm_g[i], lru_conv_w[i], lru_conv_b[i],
                lru_w_a[i], lru_b_a[i], lru_w_i[i], lru_b_i[i], lru_lambda[i], rope_tabs, expert_w_f32, l)
            ws = [w_o[:GLA_VAL_WIDTH], w_o[GLA_VAL_WIDTH:]]
        else:
            w_qkv = na_w_qkv[i].astype(BF16)
            qkv_lat, *expert_w = _proj(h_lat, w_qkv, BF16, tn=3072, side=expert_w_f32, layer=l)
            qkv_ctx = _proj(h_ctx, w_qkv, BF16, tn=3072)
            y_ctx, y_lat = _na(qkv_lat, qkv_ctx, _na_bias_tables(na_rpb[i], T // GRID_W), not last)
            ys_ctx, ys_lat, ws = [y_ctx], [y_lat], [w_o]
        moe_w = (*expert_w, 0)
        router_wt = router_w[l].T
        x, h_ffn, logits_t = _oproj(ys_lat, ws, x, m_lat, norm_ffn_g[l], router_wt)
        if last:
            _, out = _moe(x, h_ffn, logits_t, m_lat, *moe_w, norm_f_g, None)
            return out
        ctx, h_ffn_c, logits_c = _oproj(ys_ctx, ws, ctx, m_ctx, norm_ffn_g[l], router_wt)
        (x, h_lat), (ctx, h_ctx) = _moe_pair((x, h_ffn, logits_t, m_lat, mods_lat[l + 1]),
                                             (ctx, h_ffn_c, logits_c, m_ctx, mods_ctx[l + 1]),
                                             *moe_w, norm_mix_g[l + 1])
```

```python
import functools

import numpy as np
import jax
import jax.numpy as jnp
from jax import lax
from jax.experimental import pallas as pl
from jax.experimental.pallas import tpu as pltpu

D_MODEL = 2048
DEPTH = 4
GRID_W = 64
RMS_EPS = 1e-6
N_MOD = 6
ROPE_THETA = 10000.0
GLA_HEADS = 4
GLA_DK = D_MODEL // 16
GLA_DV = D_MODEL // 8
GLA_KEY_WIDTH = GLA_HEADS * GLA_DK
GLA_VAL_WIDTH = GLA_HEADS * GLA_DV
GLA_RANK = 16
GLA_TAU = 16.0
GLA_CHUNK = 64
LRU_WIDTH = D_MODEL // 2
LRU_BLOCKS = 8
LRU_BLOCK_DIM = LRU_WIDTH // LRU_BLOCKS
LRU_CONV = 4
LRU_C = 8.0
NA_HEADS = 16
NA_HEAD_DIM = D_MODEL // NA_HEADS
NA_KH = 8
NA_KW = 16
N_EXPERTS = 16
EC_CAPACITY_FACTOR = 2

F32 = jnp.float32
BF16 = jnp.bfloat16
HIGHEST = lax.Precision.HIGHEST
LANES = 128
NEG_BIG = -1e30
LOG2_E = float(np.log2(np.e))
NA_QROWS = 4
NA_KROWS = NA_QROWS + NA_KH - 1
PREFIX_CHUNK = 256
SLOT_WINDOW = 64
OPROJ_SUB_ROWS = 256
GLA_EMIT_CHUNKS = 4
GLA_PREP_ROWS = 512
LRU_SEGMENTS = 8
LRU_PITCH_PAD = 8
ROUTE_SAMPLES = 4
FFN_ROWS = 512
SLOT_ALIGN = 16

_NT = (((1,), (1,)), ((), ()))
_TN = (((0,), (0,)), ((), ()))


def _params(sem, vmem_mib=48):
    return pltpu.CompilerParams(dimension_semantics=sem, vmem_limit_bytes=vmem_mib * 2 ** 20)


def _aligned(index, multiple):
    return index if isinstance(index, int) else pl.multiple_of(index, multiple)


def _sigmoid(z):
    return 1.0 / (1.0 + jnp.exp(-z))


def _silu(z):
    return z * _sigmoid(z)


def _mod_kernel(c_ref, w_ref, b_ref, o_ref):
    a = _silu(c_ref[...]).astype(BF16)
    o_ref[0] = jnp.dot(a, w_ref[0].astype(BF16), preferred_element_type=F32) + b_ref[0]


def _modulation(cc, w_mod, b_mod):
    R, D = cc.shape
    L, _, N = w_mod.shape
    tn = 1024
    return pl.pallas_call(
        _mod_kernel,
        grid=(L, N // tn),
        in_specs=[pl.BlockSpec((R, D), lambda l, j: (0, 0)),
                  pl.BlockSpec((1, D, tn), lambda l, j: (l, 0, j)),
                  pl.BlockSpec((1, 1, tn), lambda l, j: (l, 0, j))],
        out_specs=pl.BlockSpec((1, R, tn), lambda l, j: (l, 0, j)),
        out_shape=jax.ShapeDtypeStruct((L, R, N), F32),
        compiler_params=_params(("arbitrary", "arbitrary")),
        name="modulation",
    )(cc, w_mod, b_mod.reshape(L, 1, N))


def _rms_modulate(x, g_ref, m_ref=None, shift_row=0, scale_row=1):
    y = x * lax.rsqrt(jnp.mean(x * x, axis=-1, keepdims=True) + RMS_EPS) * g_ref[...]
    if m_ref is not None:
        y = y * (1.0 + m_ref[0, scale_row:scale_row + 1, :]) + m_ref[0, shift_row:shift_row + 1, :]
    return y


def _norm_kernel(x_ref, g_ref, m_ref, h_ref):
    h_ref[0] = _rms_modulate(x_ref[0], g_ref, m_ref).astype(h_ref.dtype)


def _norm(x, g, mod):
    B, T, D = x.shape
    tt = min(T, 512)
    per_sample = mod.shape[0] > 1
    rows = pl.BlockSpec((1, tt, D), lambda b, t: (b, t, 0))
    return pl.pallas_call(
        _norm_kernel,
        grid=(B, T // tt),
        in_specs=[rows, pl.BlockSpec((1, D), lambda b, t: (0, 0)),
                  pl.BlockSpec((1, N_MOD, D), (lambda b, t: (b, 0, 0)) if per_sample else (lambda b, t: (0, 0, 0)))],
        out_specs=rows, out_shape=jax.ShapeDtypeStruct((B, T, D), BF16),
        compiler_params=_params(("arbitrary", "arbitrary")),
        name="rmsnorm",
    )(x, g.reshape(1, D), mod)


def _proj_kernel(*refs, n_side):
    a_ref, w_ref = refs[:2]
    side_in, o_ref, side_out = refs[2:2 + n_side], refs[2 + n_side], refs[3 + n_side:]
    o_ref[...] = jnp.dot(a_ref[...], w_ref[...], preferred_element_type=F32).astype(o_ref.dtype)
    for src, dst in zip(side_in, side_out):
        dst[...] = src[...].astype(dst.dtype)


def _proj(a, w, out_dtype, tn, side=(), layer=0):
    B, T, K = a.shape
    N = w.shape[1]
    M = B * T
    tm = min(M, 512)
    n_i = M // tm
    steps = (N // tn) * n_i
    flat = [s.reshape(s.shape[0], steps, -1, s.shape[-1]) for s in side]
    res = pl.pallas_call(
        functools.partial(_proj_kernel, n_side=len(side)),
        grid=(N // tn, n_i),
        in_specs=[pl.BlockSpec((tm, K), lambda j, i: (i, 0)), pl.BlockSpec((K, tn), lambda j, i: (0, j))]
                 + [pl.BlockSpec((1, 1) + f.shape[2:], lambda j, i: (layer, j * n_i + i, 0, 0)) for f in flat],
        out_specs=[pl.BlockSpec((tm, tn), lambda j, i: (i, j))]
                  + [pl.BlockSpec((1, 1) + f.shape[2:], lambda j, i: (0, j * n_i + i, 0, 0)) for f in flat],
        out_shape=[jax.ShapeDtypeStruct((M, N), out_dtype)]
                  + [jax.ShapeDtypeStruct((1,) + f.shape[1:], BF16) for f in flat],
        compiler_params=_params(("arbitrary", "arbitrary")),
        name="proj",
    )(a.reshape(M, K), w, *flat)
    out = res[0].reshape(B, T, N)
    return (out, *[r.reshape((1,) + s.shape[1:]) for r, s in zip(res[1:], side)]) if side else out


def _oproj_kernel(*refs, n_in, n_experts):
    a_refs, w_refs = refs[:n_in], refs[n_in:2 * n_in]
    x_ref, m_ref, g_ref, wr_ref, o_ref, h_ref, lg_ref = refs[2 * n_in:]
    tm = x_ref.shape[1]
    sub = min(tm, OPROJ_SUB_ROWS)
    for s in range(tm // sub):
        rows = slice(s * sub, (s + 1) * sub)
        acc = jnp.dot(a_refs[0][0, rows, :], w_refs[0][...], preferred_element_type=F32)
        for a_ref, w_ref in zip(a_refs[1:], w_refs[1:]):
            acc = acc + jnp.dot(a_ref[0, rows, :], w_ref[...], preferred_element_type=F32)
        x_new = x_ref[0, rows, :] + m_ref[0, 2:3, :] * acc
        o_ref[0, rows, :] = x_new
        y = _rms_modulate(x_new, g_ref, m_ref, shift_row=3, scale_row=4)
        y_hi = y.astype(BF16)
        h_ref[0, rows, :] = y_hi
        parts = jnp.dot(y_hi, wr_ref[...], preferred_element_type=F32)
        lg_ref[0, :, rows] = jnp.transpose(parts + pltpu.roll(parts, LANES - n_experts, 1))


def _oproj(ys, ws, x, mod, g_ffn, router_wt):
    B, T, D = x.shape
    E = router_wt.shape[0]
    tm = min(T, 512)
    per_sample = mod.shape[0] > 1
    w_r_hi = router_wt.T.astype(BF16)
    w_r_lo = (router_wt.T - w_r_hi.astype(F32)).astype(BF16)
    w_r_split = jnp.pad(jnp.concatenate([w_r_hi, w_r_lo], axis=1), ((0, 0), (0, LANES - 2 * E)))
    rows = lambda n: pl.BlockSpec((1, tm, n), lambda b, t: (b, t, 0))
    in_specs = [rows(y.shape[2]) for y in ys]
    in_specs += [pl.BlockSpec(w.shape, lambda b, t: (0, 0)) for w in ws]
    in_specs += [rows(D),
                 pl.BlockSpec((1, N_MOD, D), (lambda b, t: (b, 0, 0)) if per_sample else (lambda b, t: (0, 0, 0))),
                 pl.BlockSpec((1, D), lambda b, t: (0, 0)), pl.BlockSpec((D, LANES), lambda b, t: (0, 0))]
    x_new, h, logits = pl.pallas_call(
        functools.partial(_oproj_kernel, n_in=len(ys), n_experts=E),
        grid=(B, T // tm), in_specs=in_specs,
        out_specs=[rows(D), rows(D), pl.BlockSpec((1, LANES, tm), lambda b, t: (b, 0, t))],
        out_shape=[jax.ShapeDtypeStruct((B, T, D), F32), jax.ShapeDtypeStruct((B, T, D), BF16),
                   jax.ShapeDtypeStruct((B, LANES, T), F32)],
        compiler_params=_params(("arbitrary", "arbitrary"), vmem_mib=56),
        name="oproj",
    )(*ys, *ws, x, mod, g_ffn.reshape(1, D), w_r_split)
    return x_new, h, logits


def _log_sigmoid(z):
    return jnp.minimum(z, 0.0) - jnp.log(1.0 + jnp.exp(-jnp.abs(z)))


def _gla_kernel(*refs, n_chunks, use_rope, has_state):
    it = iter(refs)
    q_ref, k_ref, v_ref, g_ref, za_ref = (next(it) for _ in range(5))
    cos_ref, sin_ref = (next(it), next(it)) if use_rope else (None, None)
    aup_ref, ab_ref, gg_ref = next(it), next(it), next(it)
    s0_ref = next(it) if has_state else None
    y_ref, sfin_ref = next(it), next(it)
    qd_scr, ki_scr, ke_scr, kv_scr, dec_scr, sall_scr, st_scr = (next(it) for _ in range(7))
    C, DK = GLA_CHUNK, GLA_DK
    T = n_chunks * C
    RB = min(T, GLA_PREP_ROWS)
    cpb = RB // C

    def increments(blk):
        for i in range(cpb):
            c = blk * cpb + i
            rows = pl.ds(_aligned(c * C, C), C)
            kv_scr[c] = lax.dot_general(v_ref[0, rows, :].astype(BF16), ke_scr[c], _TN, preferred_element_type=F32)

    def prepare(blk):
        r0 = _aligned(blk * RB, RB)
        rows = pl.ds(r0, RB)
        z = jnp.dot(za_ref[0, rows, :].astype(BF16), aup_ref[...], preferred_element_type=F32) + ab_ref[...]
        la = _log_sigmoid(z) * (1.0 / GLA_TAU)
        in_chunk = lax.broadcasted_iota(jnp.int32, la.shape, 0) % C
        prefix = la
        step = 1
        while step < C:
            prefix = prefix + jnp.where(in_chunk >= step, pltpu.roll(prefix, step, 0), 0.0)
            step *= 2
        q = q_ref[0, rows, :] * (GLA_DK ** -0.5)
        k = k_ref[0, rows, :]
        if use_rope:
            cos, sin = cos_ref[rows, :], sin_ref[rows, :]
            even = (lax.broadcasted_iota(jnp.int32, q.shape, 1) % 2) == 0

            def rope(t):
                swapped = jnp.where(even, pltpu.roll(t, LANES - 1, 1), pltpu.roll(t, 1, 1))
                return t * cos + swapped * sin
            q, k = rope(q), rope(k)
        split = lambda t: t.reshape(cpb, C, t.shape[-1])
        la, prefix, q, k = split(la), split(prefix), split(q), split(k)
        tot = prefix[:, C - 1:C, :]
        b_f = prefix[..., :DK]
        b_b = tot[..., DK:] - prefix[..., DK:] + la[..., DK:]
        dec = jnp.exp(tot)
        c0 = _aligned(blk * cpb, cpb)
        qd_scr[pl.ds(c0, cpb)] = jnp.concatenate([q * jnp.exp(b_f), q * jnp.exp(b_b)], axis=2).astype(BF16)
        ki_f, ki_b = k * jnp.exp(-b_f), k * jnp.exp(-b_b)
        ki_scr[0, pl.ds(c0, cpb)] = ki_f.astype(BF16)
        ki_scr[1, pl.ds(c0, cpb)] = ki_b.astype(BF16)
        ke_scr[pl.ds(c0, cpb)] = jnp.concatenate([ki_f * dec[..., :DK], ki_b * dec[..., DK:]], axis=2).astype(BF16)
        dec_scr[pl.ds(c0, cpb)] = jnp.broadcast_to(dec, (cpb,) + dec_scr.shape[1:])

    def prepare_and_increment(blk, carry):
        increments(blk - 1)
        prepare(blk)
        return carry

    n_blocks = T // RB
    prepare(0)
    lax.fori_loop(1, n_blocks, prepare_and_increment, 0)
    increments(n_blocks - 1)

    st_scr[...] = s0_ref[0, 0] if has_state else jnp.zeros(st_scr.shape, F32)

    def recur(i, carry):
        cf, cb = i, n_chunks - 1 - i
        st = st_scr[...]
        sall_scr[cf, :, :DK] = st[:, :DK].astype(BF16)
        sall_scr[cb, :, DK:] = st[:, DK:].astype(BF16)
        st_scr[:, :DK] = st[:, :DK] * dec_scr[cf, 0:1, :DK] + kv_scr[cf, :, :DK]
        st_scr[:, DK:] = st[:, DK:] * dec_scr[cb, 0:1, DK:] + kv_scr[cb, :, DK:]
        return carry

    lax.fori_loop(0, n_chunks, recur, 0)
    sfin_ref[0, 0] = st_scr[...]

    G = min(n_chunks, GLA_EMIT_CHUNKS)
    GR = G * C
    row = lax.broadcasted_iota(jnp.int32, (GR, GR), 0)
    col = lax.broadcasted_iota(jnp.int32, (GR, GR), 1)
    same_chunk = (row // C) == (col // C)
    keep_f, keep_b = same_chunk & (row >= col), same_chunk & (row <= col)

    def emit(grp, carry):
        c0 = pl.multiple_of(grp * G, G)
        rows = pl.ds(pl.multiple_of(grp * GR, GR), GR)
        qd = qd_scr[pl.ds(c0, G)].reshape(GR, 2 * DK)
        ki_f = ki_scr[0, pl.ds(c0, G)].reshape(GR, DK)
        ki_b = ki_scr[1, pl.ds(c0, G)].reshape(GR, DK)
        s = (jnp.where(keep_f, lax.dot_general(qd[:, :DK], ki_f, _NT, preferred_element_type=F32), 0.0)
             + jnp.where(keep_b, lax.dot_general(qd[:, DK:], ki_b, _NT, preferred_element_type=F32), 0.0))
        carried = [lax.dot_general(qd[i * C:(i + 1) * C], sall_scr[c0 + i], _NT, preferred_element_type=F32)
                   for i in range(G)]
        o = (jnp.dot(s.astype(BF16), v_ref[0, rows, :].astype(BF16), preferred_element_type=F32)
             + jnp.concatenate(carried, axis=0))
        o = o * lax.rsqrt(jnp.mean(o * o, axis=-1, keepdims=True) + RMS_EPS) * gg_ref[...]
        y_ref[0, rows, :] = (o * _silu(g_ref[0, rows, :])).astype(y_ref.dtype)
        return carry

    lax.fori_loop(0, n_chunks // G, emit, 0, unroll=min(4, n_chunks // G))


def _gla(p_main, p_za, aup, ab, gg, rope_tabs, state):
    B, T, _ = p_main.shape
    H, DK, DV = GLA_HEADS, GLA_DK, GLA_DV
    use_rope, has_state = rope_tabs is not None, state is not None
    n_chunks = T // GLA_CHUNK
    in_specs = [pl.BlockSpec((1, T, DK), lambda b, h: (b, 0, h)),
                pl.BlockSpec((1, T, DK), lambda b, h: (b, 0, H + h)),
                pl.BlockSpec((1, T, DV), lambda b, h: (b, 0, H + h)),
                pl.BlockSpec((1, T, DV), lambda b, h: (b, 0, 2 * H + h)),
                pl.BlockSpec((1, T, LANES), lambda b, h: (b, 0, 0))]
    args = [p_main, p_main, p_main, p_main, p_za]
    if use_rope:
        in_specs += [pl.BlockSpec((T, DK), lambda b, h: (0, 0))] * 2
        args += list(rope_tabs)
    in_specs += [pl.BlockSpec((LANES, 2 * DK), lambda b, h: (0, h)),
                 pl.BlockSpec((1, 2 * DK), lambda b, h: (0, h)),
                 pl.BlockSpec((1, DV), lambda b, h: (0, 0))]
    args += [aup, ab, gg]
    st_spec = pl.BlockSpec((1, 1, DV, 2 * DK), lambda b, h: (b, h, 0, 0))
    if has_state:
        in_specs.append(st_spec)
        args.append(state)
    return pl.pallas_call(
        functools.partial(_gla_kernel, n_chunks=n_chunks, use_rope=use_rope, has_state=has_state),
        grid=(B, H), in_specs=in_specs,
        out_specs=[pl.BlockSpec((1, T, DV), lambda b, h: (b, 0, h)), st_spec],
        out_shape=[jax.ShapeDtypeStruct((B, T, H * DV), BF16), jax.ShapeDtypeStruct((B, H, DV, 2 * DK), F32)],
        scratch_shapes=[pltpu.VMEM((n_chunks, GLA_CHUNK, 2 * DK), BF16), pltpu.VMEM((2, n_chunks, GLA_CHUNK, DK), BF16),
                        pltpu.VMEM((n_chunks, GLA_CHUNK, 2 * DK), BF16),
                        pltpu.VMEM((n_chunks, DV, 2 * DK), F32), pltpu.VMEM((n_chunks, 8, 2 * DK), F32),
                        pltpu.VMEM((n_chunks, DV, 2 * DK), BF16), pltpu.VMEM((DV, 2 * DK), F32)],
        compiler_params=_params(("arbitrary", "arbitrary")),
        name="gla",
    )(*args)


def _gelu_tanh(z):
    return 0.5 * z * (1.0 + jnp.tanh(np.sqrt(2.0 / np.pi) * (z + 0.044715 * (z * z * z))))


def _lru_kernel(x_ref, gt_ref, cw_ref, cb_ref, wa_ref, ba_ref, wi_ref, bi_ref, lam_ref, h0f_ref, h0b_ref,
                y_ref, hf_ref, hb_ref, a_scr, bx_scr, hl_scr, ap_scr, *, T):
    L = T // LRU_SEGMENTS
    P = L + LRU_PITCH_PAD
    x = x_ref[0]
    row = lax.broadcasted_iota(jnp.int32, x.shape, 0)

    def from_earlier(v, d, fill):
        return jnp.where(row >= d, pltpu.roll(v, d, 0), fill)

    def from_later(v, d, fill):
        return jnp.where(row < T - d, pltpu.roll(v, T - d, 0), fill)

    xc = (cw_ref[0:1, :] * from_earlier(x, 1, 0.0) + cw_ref[1:2, :] * x
          + cw_ref[2:3, :] * from_later(x, 1, 0.0) + cw_ref[3:4, :] * from_later(x, 2, 0.0) + cb_ref[...])
    xcb = xc.astype(BF16)

    def sigmoid(z):
        return 0.5 * jnp.tanh(0.5 * z) + 0.5

    for d in range(2):
        r = sigmoid(jnp.dot(xcb, wa_ref[d, 0], preferred_element_type=F32) + ba_ref[d])
        gate_i = sigmoid(jnp.dot(xcb, wi_ref[d, 0], preferred_element_type=F32) + bi_ref[d])
        neg_lam = -lam_ref[d]
        softplus = jnp.maximum(neg_lam, 0.0) + jnp.log(1.0 + jnp.exp(-jnp.abs(neg_lam)))
        a = jnp.exp((-LRU_C) * r * softplus)
        u = 1.0 - a * a
        bx = jnp.where(u > 0.0, u * lax.rsqrt(u), 0.0) * (gate_i * xc)
        for s in range(LRU_SEGMENTS):
            a_scr[d, s * P:s * P + L, :] = a[s * L:(s + 1) * L]
            bx_scr[d, s * P:s * P + L, :] = bx[s * L:(s + 1) * L]

    def step(i, carry):
        h_f, p_f, h_b, p_b = carry
        at_f = pl.ds(i, LRU_SEGMENTS, stride=P)
        at_b = pl.ds(L - 1 - i, LRU_SEGMENTS, stride=P)
        a_f, a_b = a_scr[0, at_f, :], a_scr[1, at_b, :]
        h_f = a_f * h_f + bx_scr[0, at_f, :]
        h_b = a_b * h_b + bx_scr[1, at_b, :]
        p_f, p_b = a_f * p_f, a_b * p_b
        hl_scr[0, at_f, :] = h_f
        hl_scr[1, at_b, :] = h_b
        ap_scr[0, at_f, :] = p_f
        ap_scr[1, at_b, :] = p_b
        return h_f, p_f, h_b, p_b

    zeros, ones = jnp.zeros((LRU_SEGMENTS, x.shape[1]), F32), jnp.ones((LRU_SEGMENTS, x.shape[1]), F32)
    h_f, p_f, h_b, p_b = lax.fori_loop(0, L, step, (zeros, ones, zeros, ones), unroll=8)

    enter_f, enter_b = [h0f_ref[0]], [h0b_ref[0]]
    for s in range(LRU_SEGMENTS):
        enter_f.append(h_f[s:s + 1] + p_f[s:s + 1] * enter_f[-1])
        t = LRU_SEGMENTS - 1 - s
        enter_b.append(h_b[t:t + 1] + p_b[t:t + 1] * enter_b[-1])
    hf_ref[0] = enter_f[-1]
    hb_ref[0] = enter_b[-1]
    for s in range(LRU_SEGMENTS):
        seg = slice(s * P, s * P + L)
        h = (hl_scr[0, seg, :] + ap_scr[0, seg, :] * enter_f[s]
             + hl_scr[1, seg, :] + ap_scr[1, seg, :] * enter_b[LRU_SEGMENTS - 1 - s])
        y_ref[0, s * L:(s + 1) * L, :] = (h * _gelu_tanh(gt_ref[0, s * L:(s + 1) * L, :])).astype(y_ref.dtype)


def _lru(p_main, cw, cb, wa, ba, wi, bi, lam, states):
    B, T, _ = p_main.shape
    W, NB, BD = LRU_WIDTH, LRU_BLOCKS, LRU_BLOCK_DIM
    x_blk0 = (2 * GLA_KEY_WIDTH + 2 * GLA_VAL_WIDTH) // BD
    vec = lambda n: pl.BlockSpec((n, 1, BD), lambda b, j: (0, 0, j))
    mat = pl.BlockSpec((2, 1, BD, BD), lambda b, j: (0, j, 0, 0))
    st_spec = pl.BlockSpec((1, 1, BD), lambda b, j: (b, 0, j))
    st_shape = jax.ShapeDtypeStruct((B, 1, W), F32)
    y, hf, hb = pl.pallas_call(
        functools.partial(_lru_kernel, T=T),
        grid=(B, NB),
        in_specs=[pl.BlockSpec((1, T, BD), lambda b, j: (b, 0, x_blk0 + j)),
                  pl.BlockSpec((1, T, BD), lambda b, j: (b, 0, x_blk0 + NB + j)),
                  pl.BlockSpec((LRU_CONV, BD), lambda b, j: (0, j)),
                  pl.BlockSpec((1, BD), lambda b, j: (0, j)),
                  mat, vec(2), mat, vec(2), vec(2), st_spec, st_spec],
        out_specs=[pl.BlockSpec((1, T, BD), lambda b, j: (b, 0, j)), st_spec, st_spec],
        out_shape=[jax.ShapeDtypeStruct((B, T, W), BF16), st_shape, st_shape],
        scratch_shapes=[pltpu.VMEM((2, LRU_SEGMENTS * (T // LRU_SEGMENTS + LRU_PITCH_PAD), BD), F32)] * 4,
        compiler_params=_params(("arbitrary", "arbitrary")),
        name="rglru",
    )(p_main, p_main, cw, cb.reshape(1, W), wa, ba.reshape(2, 1, W), wi, bi.reshape(2, 1, W),
      lam.reshape(2, 1, W), *states)
    return y, (hf, hb)


def _na_tile(i, n_tiles, rows):
    r0 = i * NA_QROWS
    ks = min(max(r0 - NA_KH // 2, 0), rows - NA_KROWS)
    cls = 0 if i == 0 else (2 if i == n_tiles - 1 else 1)
    return r0, ks, cls


def _na_bias_tables(rpb, rows):
    H = rpb.shape[0]
    qc = np.arange(GRID_W)[:, None]
    kc = np.arange(GRID_W)[None, :]
    col_start = np.clip(qc - NA_KW // 2, 0, GRID_W - NA_KW)
    col_valid = (kc >= col_start) & (kc < col_start + NA_KW)
    dc = np.clip(kc - qc, -(NA_KW - 1), NA_KW - 1) + NA_KW - 1
    pick = (dc[None] == np.arange(2 * NA_KW - 1)[:, None, None]).astype(np.float32)
    band = jnp.einsum('hrd,dqk->hrqk', rpb.astype(F32), jnp.asarray(pick), precision=HIGHEST)
    band = jnp.where(col_valid, band, NEG_BIG)
    masked = jnp.full((H, GRID_W, GRID_W), NEG_BIG, F32)
    n_tiles = rows // NA_QROWS
    tabs = []
    for i in (0, 1, n_tiles - 1):
        r0, ks, _ = _na_tile(i, n_tiles, rows)
        per_qrow = []
        for qr in range(r0, r0 + NA_QROWS):
            row_start = min(max(qr - NA_KH // 2, 0), rows - NA_KH)
            blocks = [band[:, kr - qr + NA_KH - 1] if row_start <= kr < row_start + NA_KH else masked
                      for kr in range(ks, ks + NA_KROWS)]
            per_qrow.append(jnp.concatenate(blocks, axis=2))
        tabs.append(jnp.concatenate(per_qrow, axis=1))
    return jnp.stack(tabs, axis=1) * LOG2_E


def _na_kernel(*refs, rows, need_ctx):
    q_ref, k_ref, v_ref, qc_ref, kc_ref, vc_ref, tab_ref = refs[:7]
    y_ref, vx_ref = refs[7], refs[-1]
    HD = NA_HEAD_DIM
    scale = HD ** -0.5 * LOG2_E
    vx_ref[...] = jnp.concatenate([v_ref[0], jnp.ones(v_ref.shape[1:], BF16)], axis=1)
    kc = kc_ref[0]
    vcx = jnp.concatenate([vc_ref[0], jnp.ones(vc_ref.shape[1:], BF16)], axis=1)
    n_tiles = rows // NA_QROWS
    nq, nk = NA_QROWS * GRID_W, NA_KROWS * GRID_W
    for i in range(n_tiles):
        r0, ks, cls = _na_tile(i, n_tiles, rows)
        q = q_ref[0, r0 * GRID_W:r0 * GRID_W + nq, :]
        kt = k_ref[0, ks * GRID_W:ks * GRID_W + nk, :]
        s_loc = lax.dot_general(q, kt, _NT, preferred_element_type=F32) * scale + tab_ref[0, cls]
        s_ctx = lax.dot_general(q, kc, _NT, preferred_element_type=F32) * scale
        m = jnp.maximum(jnp.max(s_loc, axis=-1, keepdims=True), jnp.max(s_ctx, axis=-1, keepdims=True))
        o = (jnp.dot(jnp.exp2(s_loc - m).astype(BF16), vx_ref[ks * GRID_W:ks * GRID_W + nk, :], preferred_element_type=F32)
             + jnp.dot(jnp.exp2(s_ctx - m).astype(BF16), vcx, preferred_element_type=F32))
        y_ref[0, r0 * GRID_W:r0 * GRID_W + nq, :] = (o[:, :HD] / o[:, HD:]).astype(y_ref.dtype)
    if need_ctx:
        yc_ref = refs[8]
        s = lax.dot_general(qc_ref[0], kc, _NT, preferred_element_type=F32) * scale
        p = jnp.exp2(s - jnp.max(s, axis=-1, keepdims=True))
        o = jnp.dot(p.astype(BF16), vcx, preferred_element_type=F32)
        yc_ref[0] = (o[:, :HD] / o[:, HD:]).astype(yc_ref.dtype)


def _na(qkv_lat, qkv_ctx, tabs, need_ctx):
    B, T, _ = qkv_lat.shape
    Tc = qkv_ctx.shape[1]
    H, HD = NA_HEADS, NA_HEAD_DIM
    blk = lambda t, off: pl.BlockSpec((1, t, HD), lambda h, b: (b, 0, off + h))
    out_specs = [pl.BlockSpec((1, T, HD), lambda h, b: (b, 0, h))]
    out_shape = [jax.ShapeDtypeStruct((B, T, H * HD), BF16)]
    if need_ctx:
        out_specs.append(pl.BlockSpec((1, Tc, HD), lambda h, b: (b, 0, h)))
        out_shape.append(jax.ShapeDtypeStruct((B, Tc, H * HD), BF16))
    res = pl.pallas_call(
        functools.partial(_na_kernel, rows=T // GRID_W, need_ctx=need_ctx),
        grid=(H, B),
        in_specs=[blk(T, 0), blk(T, H), blk(T, 2 * H), blk(Tc, 0), blk(Tc, H), blk(Tc, 2 * H),
                  pl.BlockSpec((1,) + tabs.shape[1:], lambda h, b: (h, 0, 0, 0))],
        out_specs=out_specs, out_shape=out_shape,
        scratch_shapes=[pltpu.VMEM((T, 2 * HD), BF16)],
        compiler_params=_params(("arbitrary", "arbitrary")),
        name="natten",
    )(qkv_lat, qkv_lat, qkv_lat, qkv_ctx, qkv_ctx, qkv_ctx, tabs)
    return (res[1] if need_ctx else None), res[0]


def _route_kernel(lg_ref, pos_ref, aff_ref, lo_ref, *, cap):
    lg = lg_ref[...]
    bb, n_exp, N = lg.shape
    ex = jnp.exp(lg - jnp.max(lg, axis=1, keepdims=True))
    aff = ex / jnp.sum(ex, axis=1, keepdims=True)
    aff_ref[...] = aff
    E = bb * n_exp
    aff = aff.reshape(E, N)
    bits = pltpu.bitcast(aff, jnp.int32)
    thr = jnp.zeros((E, 1), jnp.int32)
    for bit in range(29, -1, -1):
        cand = thr | (1 << bit)
        cnt = jnp.sum(jnp.where(bits >= cand, 1.0, 0.0), axis=1, keepdims=True)
        thr = jnp.where(cnt >= cap, cand, thr)
    above = bits > thr
    tied = bits == thr
    need = cap - jnp.sum(jnp.where(above, 1.0, 0.0), axis=1, keepdims=True)
    pc = min(PREFIX_CHUNK, N)
    before = jnp.where(lax.broadcasted_iota(jnp.int32, (pc, pc), 0) < lax.broadcasted_iota(jnp.int32, (pc, pc), 1),
                       1.0, 0.0).astype(BF16)
    tied_seen = jnp.zeros((E, 1), F32)
    sel_seen = jnp.zeros((E, 1), F32)
    tile_lane = lax.broadcasted_iota(jnp.int32, (E, LANES), 1)
    lo = jnp.where(tile_lane == N // pc, float(cap), 0.0)
    for c in range(N // pc):
        sl = slice(c * pc, (c + 1) * pc)
        lo = jnp.where(tile_lane == c, sel_seen, lo)
        tied_c = jnp.where(tied[:, sl], 1.0, 0.0)
        tied_before = jnp.dot(tied_c.astype(BF16), before, preferred_element_type=F32) + tied_seen
        sel_c = jnp.where(above[:, sl], 1.0, jnp.where(tied_before < need, tied_c, 0.0))
        sel_before = jnp.dot(sel_c.astype(BF16), before, preferred_element_type=F32) + sel_seen
        pos_ref[:, :, sl] = jnp.where(sel_c > 0.0, sel_before, -1.0).reshape(bb, n_exp, pc)
        tied_seen = tied_seen + jnp.sum(tied_c, axis=1, keepdims=True)
        sel_seen = sel_seen + jnp.sum(sel_c, axis=1, keepdims=True)
    lo_ref[...] = lo.astype(jnp.int32).reshape(bb, n_exp, LANES)


def _route(logits_t, cap):
    B, _, N = logits_t.shape
    E = N_EXPERTS
    bb = ROUTE_SAMPLES if B % ROUTE_SAMPLES == 0 else 1
    spec = pl.BlockSpec((bb, E, N), lambda b: (b, 0, 0))
    lo_spec = pl.BlockSpec((bb, E, LANES), lambda b: (b, 0, 0))
    pos, aff, lo = pl.pallas_call(
        functools.partial(_route_kernel, cap=cap),
        grid=(B // bb,), in_specs=[spec], out_specs=[spec, spec, lo_spec],
        out_shape=[jax.ShapeDtypeStruct((B, E, N), F32)] * 2 + [jax.ShapeDtypeStruct((B, E, LANES), jnp.int32)],
        compiler_params=_params(("arbitrary",)),
        name="route",
    )(logits_t)
    n_bounds = N // min(PREFIX_CHUNK, N) + 1
    return pos, aff, lo[:, :, :n_bounds].reshape(-1)


def _window_count(lo_ref, base, experts, tile, n_bounds, win):
    starts, n_win = [], 0
    for e in experts:
        first = lo_ref[(base + e) * n_bounds + tile]
        end = lo_ref[(base + e) * n_bounds + tile + 1]
        start = (first // SLOT_ALIGN) * SLOT_ALIGN
        starts.append(start)
        n_win = jnp.maximum(n_win, (end - start + win - 1) // win)
    return starts, n_win


def _gather_kernel(lo_ref, pos_ref, h_ref, xs_ref, *, cap, win, tn):
    EG, N = pos_ref.shape[1], pos_ref.shape[2]
    n_bounds = N // tn + 1
    base = pl.program_id(0) * (EG * pl.num_programs(1)) + pl.program_id(1) * EG
    xs_ref[...] = jnp.zeros(xs_ref.shape, xs_ref.dtype)
    slot_iota = lax.broadcasted_iota(jnp.int32, (win, tn), 0)
    for j in range(N // tn):
        h_tile = h_ref[0, j * tn:(j + 1) * tn, :]
        starts, n_win = _window_count(lo_ref, base, range(EG), j, n_bounds, win)

        def window(w, carry):
            pieces, offs = [], []
            for e in range(EG):
                first = starts[e] + w * win
                off = jnp.minimum(first, cap - win)
                prow = pos_ref[0, e:e + 1, j * tn:(j + 1) * tn]
                hit = ((slot_iota + off).astype(F32) == prow) & (prow >= first.astype(F32))
                pieces.append(jnp.where(hit, 1.0, 0.0).astype(BF16))
                offs.append(pl.multiple_of(off, SLOT_ALIGN))
            got = jnp.dot(jnp.concatenate(pieces, axis=0), h_tile, preferred_element_type=F32)
            for e in range(EG):
                xs_ref[0, e, pl.ds(offs[e], win), :] += got[e * win:(e + 1) * win].astype(xs_ref.dtype)
            return carry

        lax.fori_loop(0, n_win, window, 0)


def _gather(lo, pos, h, cap):
    B, E, N = pos.shape
    D = h.shape[2]
    EG = 8
    tn, win = min(PREFIX_CHUNK, N), min(SLOT_WINDOW, cap)
    return pl.pallas_call(
        functools.partial(_gather_kernel, cap=cap, win=win, tn=tn),
        grid_spec=pltpu.PrefetchScalarGridSpec(
            num_scalar_prefetch=1, grid=(B, E // EG),
            in_specs=[pl.BlockSpec((1, EG, N), lambda b, g, lo: (b, g, 0)),
                      pl.BlockSpec((1, N, D), lambda b, g, lo: (b, 0, 0))],
            out_specs=pl.BlockSpec((1, EG, cap, D), lambda b, g, lo: (b, g, 0, 0))),
        out_shape=jax.ShapeDtypeStruct((B, E, cap, D), BF16),
        compiler_params=_params(("arbitrary", "arbitrary")),
        name="moe_gather",
    )(lo, pos, h)


def _ffn_kernel(*refs, n_main, has_extra):
    if has_extra:
        xs_ref, xe_ref, wg_ref, wu_ref, wd_ref, y_ref, ye_ref = refs
    else:
        xs_ref, wg_ref, wu_ref, wd_ref, y_ref = refs

    def swiglu(src_ref, dst_ref):
        bb, _, cap, D = src_ref.shape
        xs = src_ref[:, 0].reshape(bb * cap, D)
        hid = (_silu(jnp.dot(xs, wg_ref[0, 0], preferred_element_type=F32))
               * jnp.dot(xs, wu_ref[0, 0], preferred_element_type=F32))
        y = jnp.dot(hid.astype(BF16), wd_ref[0, 0], preferred_element_type=F32)
        dst_ref[:, 0] = y.reshape(bb, cap, D).astype(dst_ref.dtype)

    if not has_extra:
        swiglu(xs_ref, y_ref)
        return
    step = pl.program_id(1)

    @pl.when(step < n_main)
    def _():
        swiglu(xs_ref, y_ref)

    @pl.when(step == n_main)
    def _():
        swiglu(xe_ref, ye_ref)


def _ffn(xs, wg, wu, wd, layer, extra=None):
    B, E, cap, D = xs.shape
    F = wg.shape[3]
    bb = max(1, min(B, FFN_ROWS // cap))
    n_main = B // bb
    has_extra = extra is not None
    tok = pl.BlockSpec((bb, 1, cap, D), lambda e, b: (jnp.minimum(b, n_main - 1), e, 0, 0))
    weights = [pl.BlockSpec((1, 1, D, F), lambda e, b: (layer, e, 0, 0)),
               pl.BlockSpec((1, 1, D, F), lambda e, b: (layer, e, 0, 0)),
               pl.BlockSpec((1, 1, F, D), lambda e, b: (layer, e, 0, 0))]
    in_specs, out_specs = [tok], [tok]
    out_shape = [jax.ShapeDtypeStruct((B, E, cap, D), BF16)]
    args = [xs]
    if has_extra:
        tok_e = pl.BlockSpec((B, 1) + extra.shape[2:], lambda e, b: (0, e, 0, 0))
        in_specs.append(tok_e)
        out_specs.append(tok_e)
        out_shape.append(jax.ShapeDtypeStruct(extra.shape, BF16))
        args.append(extra)
    res = pl.pallas_call(
        functools.partial(_ffn_kernel, n_main=n_main, has_extra=has_extra),
        grid=(E, n_main + int(has_extra)),
        in_specs=in_specs + weights, out_specs=out_specs, out_shape=out_shape,
        compiler_params=_params(("arbitrary", "arbitrary"), vmem_mib=56 if has_extra else 48),
        name="moe_ffn",
    )(*args, wg, wu, wd)
    return res if has_extra else res[0]


def _combine_kernel(*refs, cap, win, keep_x, next_modulated):
    it = iter(refs)
    lo_ref, pos_ref, aff_ref, y_ref, x_ref, m_ref, gn_ref = (next(it) for _ in range(7))
    mn_ref = next(it) if next_modulated else None
    o_ref = next(it) if keep_x else None
    hn_ref, ystack_ref, acc_ref = next(it), next(it), next(it)
    E = y_ref.shape[1]
    n_bounds = pl.num_programs(1) + 1
    j = pl.program_id(1)
    starts, n_win = _window_count(lo_ref, pl.program_id(0) * E, range(E), j, n_bounds, win)
    col = lax.broadcasted_iota(jnp.int32, (E, E * win), 1)
    spread = jnp.where(col // win == lax.broadcasted_iota(jnp.int32, (E, E * win), 0), 1.0, 0.0).astype(BF16)
    slot = lax.dot_general(pos_ref[0].astype(BF16), spread, _TN, preferred_element_type=F32)
    gate = lax.dot_general(aff_ref[0].astype(BF16), spread, _TN, preferred_element_type=F32)
    col1 = lax.broadcasted_iota(jnp.int32, (1, E * win), 1)
    col_expert, col_in_win = col1 // win, col1 % win

    def window(w):
        first_col = jnp.zeros((1, E * win), jnp.int32)
        want_col = jnp.zeros((1, E * win), jnp.int32)
        for e in range(E):
            first = starts[e] + w * win
            off = jnp.minimum(first, cap - win)
            ystack_ref[e * win:(e + 1) * win, :] = y_ref[0, e, pl.ds(pl.multiple_of(off, SLOT_ALIGN), win), :]
            first_col = jnp.where(col_expert == e, first, first_col)
            want_col = jnp.where(col_expert == e, off + col_in_win, want_col)
        hit = (slot == want_col.astype(F32)) & (slot >= first_col.astype(F32))
        return jnp.dot(jnp.where(hit, gate, 0.0).astype(BF16), ystack_ref[...], preferred_element_type=F32)

    def more(w, carry):
        acc_ref[...] += window(w)
        return carry

    acc_ref[...] = window(0)
    lax.fori_loop(1, n_win, more, 0)
    x_new = x_ref[0] + m_ref[0, 5:6, :] * acc_ref[...]
    if keep_x:
        o_ref[0] = x_new
    hn_ref[0] = _rms_modulate(x_new, gn_ref, mn_ref).astype(hn_ref.dtype)


def _combine(lo, pos, aff, y, x, mod, g_next, mod_next):
    B, E, cap, D = y.shape
    N = x.shape[1]
    tn, win = min(PREFIX_CHUNK, N), min(SLOT_WINDOW, cap)
    keep_x = next_modulated = mod_next is not None
    tok = pl.BlockSpec((1, E, tn), lambda b, t, lo: (b, 0, t))
    rows = pl.BlockSpec((1, tn, D), lambda b, t, lo: (b, t, 0))
    mod_spec = lambda m: pl.BlockSpec((1, N_MOD, D),
                                      (lambda b, t, lo: (b, 0, 0)) if m.shape[0] > 1 else (lambda b, t, lo: (0, 0, 0)))
    in_specs = [tok, tok, pl.BlockSpec((1, E, cap, D), lambda b, t, lo: (b, 0, 0, 0)), rows, mod_spec(mod),
                pl.BlockSpec((1, D), lambda b, t, lo: (0, 0))]
    args = [lo, pos, aff, y, x, mod, g_next.reshape(1, D)]
    out_specs, out_shape = [rows], [jax.ShapeDtypeStruct((B, N, D), BF16 if next_modulated else F32)]
    if next_modulated:
        in_specs.append(mod_spec(mod_next))
        args.append(mod_next)
        out_specs, out_shape = [rows] + out_specs, [jax.ShapeDtypeStruct((B, N, D), F32)] + out_shape
    res = pl.pallas_call(
        functools.partial(_combine_kernel, cap=cap, win=win, keep_x=keep_x, next_modulated=next_modulated),
        grid_spec=pltpu.PrefetchScalarGridSpec(
            num_scalar_prefetch=1, grid=(B, N // tn), in_specs=in_specs, out_specs=out_specs,
            scratch_shapes=[pltpu.VMEM((E * win, D), BF16), pltpu.VMEM((tn, D), F32)]),
        out_shape=out_shape,
        compiler_params=_params(("arbitrary", "arbitrary"), vmem_mib=56),
        name="moe_combine",
    )(*args)
    return (res[0], res[1]) if keep_x else (None, res[0])


def _moe(x, h, logits_t, mod, wg, wu, wd, layer, g_next, mod_next):
    N = x.shape[1]
    cap = EC_CAPACITY_FACTOR * N // N_EXPERTS
    assert cap <= 256, "slot ids must stay exact in bf16"
    pos, aff, lo = _route(logits_t, cap)
    y = _ffn(_gather(lo, pos, h, cap), wg, wu, wd, layer)
    return _combine(lo, pos, aff, y, x, mod, g_next, mod_next)


def _moe_pair(lat, ctx, wg, wu, wd, layer, g_next):
    routed = []
    for x, h, logits_t, _, _ in (lat, ctx):
        cap = EC_CAPACITY_FACTOR * x.shape[1] // N_EXPERTS
        assert cap <= 256, "slot ids must stay exact in bf16"
        pos, aff, lo = _route(logits_t, cap)
        routed.append((pos, aff, lo, _gather(lo, pos, h, cap)))
    ys = _ffn(routed[0][3], wg, wu, wd, layer, extra=routed[1][3])
    return [_combine(lo, pos, aff, y, x, mod, g_next, mod_next)
            for (pos, aff, lo, _), y, (x, _, _, mod, mod_next) in zip(routed, ys, (lat, ctx))]


def _rope_tables(n_tok):
    t = jnp.arange(n_tok)
    rows = (t // GRID_W).astype(F32)
    cols = (t % GRID_W).astype(F32)
    n_freq = GLA_DK // 4
    inv = ROPE_THETA ** (-jnp.arange(n_freq, dtype=F32) / n_freq)
    ang = jnp.concatenate([rows[:, None] * inv, cols[:, None] * inv], axis=-1)
    cos = jnp.repeat(jnp.cos(ang), 2, axis=-1)
    sin = jnp.repeat(jnp.sin(ang), 2, axis=-1) * jnp.tile(jnp.asarray([-1.0, 1.0], F32), GLA_DK // 2)
    return cos, sin


def _rec_mixer(h_ctx, h_lat, w_in, alpha_up, alpha_b, gla_g, conv_w, conv_b, w_a, b_a, w_i, b_i, lam, rope_tabs,
               side, layer):
    za0 = 2 * GLA_KEY_WIDTH + 2 * GLA_VAL_WIDTH
    za1 = za0 + 2 * GLA_RANK
    w_main = jnp.concatenate([w_in[:, :za0], w_in[:, za1:]], axis=1).astype(BF16)
    w_za = jnp.pad(w_in[:, za0:za1], ((0, 0), (0, LANES - 2 * GLA_RANK))).astype(BF16)
    up = alpha_up.reshape(2, GLA_RANK, GLA_HEADS, 1, GLA_DK)
    zero = jnp.zeros_like(up[0])
    aup = jnp.concatenate([jnp.concatenate([up[0], zero], axis=2), jnp.concatenate([zero, up[1]], axis=2)], axis=0)
    aup = jnp.pad(aup.reshape(2 * GLA_RANK, 2 * GLA_KEY_WIDTH), ((0, LANES - 2 * GLA_RANK), (0, 0))).astype(BF16)
    ab = jnp.transpose(alpha_b.reshape(2, GLA_HEADS, GLA_DK), (1, 0, 2)).reshape(1, 2 * GLA_KEY_WIDTH)
    gg = gla_g.reshape(1, GLA_DV)
    wa, wi = w_a.astype(BF16), w_i.astype(BF16)
    outs = []
    gla_state, lru_state = None, None
    for h, tabs in ((h_ctx, None), (h_lat, rope_tabs)):
        B = h.shape[0]
        if tabs is None:
            p_main = _proj(h, w_main, F32, tn=2560)
        else:
            p_main, *cast = _proj(h, w_main, F32, tn=2560, side=side, layer=layer)
        p_za = _proj(h, w_za, F32, tn=LANES)
        y_gla, gla_state = _gla(p_main, p_za, aup, ab, gg, tabs, gla_state)
        if lru_state is None:
            lru_state = (jnp.zeros((B, 1, LRU_WIDTH), F32),) * 2
        y_lru, lru_state = _lru(p_main, conv_w, conv_b, wa, b_a, wi, b_i, lam, lru_state)
        outs.append([y_gla, y_lru])
    return outs, cast


def kernel(x, c, ctx, c_ctx, w_mod, b_mod, norm_mix_g, norm_ffn_g, w_out, router_w, exp_w_gate, exp_w_up, exp_w_down, rec_w_in, gla_alpha_up, gla_alpha_b, gla_norm_g, lru_conv_w, lru_conv_b, lru_w_a, lru_b_a, lru_w_i, lru_b_i, lru_lambda, na_w_qkv, na_rpb, norm_f_g):
    B, T, D = x.shape
    pad_rows = (-(B + 1)) % 8
    cc = jnp.concatenate([c, c_ctx[None, :], jnp.zeros((pad_rows, D), F32)], axis=0)
    m_all = _modulation(cc, w_mod, b_mod)
    rope_tabs = _rope_tables(T)
    expert_w_f32 = (exp_w_gate, exp_w_up, exp_w_down)
    mods_lat = [m_all[l, :B].reshape(B, N_MOD, D) for l in range(DEPTH)]
    mods_ctx = [m_all[l, B:B + 1].reshape(1, N_MOD, D) for l in range(DEPTH)]
    h_lat = _norm(x, norm_mix_g[0], mods_lat[0])
    h_ctx = _norm(ctx, norm_mix_g[0], mods_ctx[0])
    for l in range(DEPTH):
        last = l == DEPTH - 1
        i = l // 2
        m_lat, m_ctx = mods_lat[l], mods_ctx[l]
        w_o = w_out[l].astype(BF16)
        if l % 2 == 0:
            (ys_ctx, ys_lat), expert_w = _rec_mixer(
                h_ctx, h_lat, rec_w_in[i], gla_alpha_up[i], gla_alpha_b[i], gla_norm_g[i], lru_conv_w[i], lru_conv_b[i],
                lru_w_a[i], lru_b_a[i], lru_w_i[i], lru_b_i[i], lru_lambda[i], rope_tabs, expert_w_f32, l)
            ws = [w_o[:GLA_VAL_WIDTH], w_o[GLA_VAL_WIDTH:]]
        else:
            w_qkv = na_w_qkv[i].astype(BF16)
            qkv_lat, *expert_w = _proj(h_lat, w_qkv, BF16, tn=3072, side=expert_w_f32, layer=l)
            qkv_ctx = _proj(h_ctx, w_qkv, BF16, tn=3072)
            y_ctx, y_lat = _na(qkv_lat, qkv_ctx, _na_bias_tables(na_rpb[i], T // GRID_W), not last)
            ys_ctx, ys_lat, ws = [y_ctx], [y_lat], [w_o]
        moe_w = (*expert_w, 0)
        router_wt = router_w[l].T
        x, h_ffn, logits_t = _oproj(ys_lat, ws, x, m_lat, norm_ffn_g[l], router_wt)
        if last:
            _, out = _moe(x, h_ffn, logits_t, m_lat, *moe_w, norm_f_g, None)
            return out
        ctx, h_ffn_c, logits_c = _oproj(ys_ctx, ws, ctx, m_ctx, norm_ffn_g[l], router_wt)
        (x, h_lat), (ctx, h_ctx) = _moe_pair((x, h_ffn, logits_t, m_lat, mods_lat[l + 1]),
                                             (ctx, h_ffn_c, logits_c, m_ctx, mods_ctx[l + 1]),
                                             *moe_w, norm_mix_g[l + 1])
```
